```python
import jax, jax.numpy as jnp
from jax import lax
import numpy as np

D_MODEL = 1024
BATCH = 8
SEQ = 4096
DEPTH = 1

CTX_LEN = 256
GRID_W = 64

RW_HEADS = 16
RW_HEAD_DIM = 64
RW_WIDTH = RW_HEADS * RW_HEAD_DIM
RW_DECAY_RANK = 64
RW_ICL_RANK = 64
RW_GATE_RANK = 128
RW_GN_EPS = 64e-5

HG_HEADS = 8
HG_KEY_DIM = 128
HG_VAL_DIM = 128
HG_WIDTH = HG_HEADS * HG_KEY_DIM
HG_CHUNK = 64

FFN_HIDDEN = 4 * D_MODEL
N_MOD = 6
RMS_EPS = 1e-6

RW_SHIFT_COLS = 3 * RW_WIDTH + RW_DECAY_RANK + RW_ICL_RANK + RW_GATE_RANK
HG_COLS = 5 * HG_WIDTH
GATE_COLS = 2 * D_MODEL
IN_COLS = RW_SHIFT_COLS + HG_COLS + GATE_COLS
RW_SPLITS = (RW_WIDTH, 2 * RW_WIDTH, 3 * RW_WIDTH, 3 * RW_WIDTH + RW_DECAY_RANK,
             3 * RW_WIDTH + RW_DECAY_RANK + RW_ICL_RANK)

kernel_name = "hybrid_rwkv7_hgrn2_dit_block"


def _rmsnorm(x, w):
    xf = x.astype(jnp.float32)
    y = xf * lax.rsqrt(jnp.mean(xf * xf, axis=-1, keepdims=True) + RMS_EPS)
    return (y * w.astype(jnp.float32)).astype(x.dtype)


def _modulate(x, w, shift, scale):
    return _rmsnorm(x, w) * (1 + scale) + shift


def _heads(t, h):
    return t.reshape(*t.shape[:-1], h, t.shape[-1] // h)


def _shift_seq(z, mu):
    prev = jnp.pad(z[:, :-1], ((0, 0), (1, 0), (0, 0)))
    nxt = jnp.pad(z[:, 1:], ((0, 0), (0, 1), (0, 0)))
    return z + mu[0] * (prev - z) + mu[1] * (nxt - z)


def _shift_grid(z, mu, rows):
    bsz, t_len, ch = z.shape
    g = z.reshape(bsz, rows, GRID_W, ch)
    left = jnp.pad(g[:, :, :-1], ((0, 0), (0, 0), (1, 0), (0, 0)))
    right = jnp.pad(g[:, :, 1:], ((0, 0), (0, 0), (0, 1), (0, 0)))
    up = jnp.pad(g[:, :-1], ((0, 0), (1, 0), (0, 0), (0, 0)))
    down = jnp.pad(g[:, 1:], ((0, 0), (0, 1), (0, 0), (0, 0)))
    out = g + mu[0] * (left - g) + mu[1] * (right - g) + mu[2] * (up - g) + mu[3] * (down - g)
    return out.reshape(bsz, t_len, ch)


def _l2norm_heads(t):
    tf = t.astype(jnp.float32)
    n = jnp.sqrt(jnp.sum(tf * tf, axis=-1, keepdims=True))
    return (tf / jnp.maximum(n, 1e-12)).astype(t.dtype)


def _rwkv7_scan(r, w, k, v, kk, a, s0):
    def step(S, inp):
        r_t, w_t, k_t, v_t, kk_t, a_t = inp
        sa = jnp.einsum('bhvk,bhk->bhv', S, -kk_t)
        S = (S * w_t[:, :, None, :] + sa[..., None] * (kk_t * a_t)[:, :, None, :]
             + v_t[..., None] * k_t[:, :, None, :])
        return S, jnp.einsum('bhvk,bhk->bhv', S, r_t)
    xs = tuple(jnp.moveaxis(t, 1, 0) for t in (r, w, k, v, kk, a))
    s_fin, ys = lax.scan(step, s0, xs)
    return jnp.moveaxis(ys, 0, 1), s_fin


def _rwkv7_mix(r, k_raw, v, xw, xa, w0, w_up, a0, a_up, k_k, k_a, s0_f, s0_b):
    r_h, v_h = _heads(r, RW_HEADS), _heads(v, RW_HEADS)
    kk = _l2norm_heads(_heads(k_raw * k_k, RW_HEADS))
    xw_t = jnp.tanh(xw)
    ys, ks, states = [], [], []
    for d, s0 in enumerate((s0_f, s0_b)):
        wlog = -jax.nn.softplus(-(w0[d] + xw_t @ w_up[d])) - 0.5
        decay = _heads(jnp.exp(-jnp.exp(wlog)), RW_HEADS)
        a = jax.nn.sigmoid(a0[d] + xa @ a_up[d])
        k = _heads(k_raw * (1 + (a - 1) * k_a), RW_HEADS)
        seq = (r_h, decay, k, v_h, kk, _heads(a, RW_HEADS))
        if d == 1:
            seq = tuple(t[:, ::-1] for t in seq)
        y, s_fin = _rwkv7_scan(*seq, s0)
        ys.append(y[:, ::-1] if d == 1 else y)
        ks.append(k)
        states.append(s_fin)
    return ys[0] + ys[1], r_h, ks[0] + ks[1], v_h, states[0], states[1]


def _rwkv7_readout(y, r_h, k_sum, v_h, xg, g_up, r_k, ln_w, ln_b):
    yf = y.astype(jnp.float32)
    mean = jnp.mean(yf, axis=-1, keepdims=True)
    var = jnp.mean(jnp.square(yf - mean), axis=-1, keepdims=True)
    yn = (yf - mean) * lax.rsqrt(var + RW_GN_EPS) * _heads(ln_w, RW_HEADS) + _heads(ln_b, RW_HEADS)
    bonus = jnp.sum(r_h * k_sum * r_k, axis=-1, keepdims=True) * v_h
    out = (yn.astype(y.dtype) + bonus).reshape(*y.shape[:-2], RW_WIDTH)
    return out * (jax.nn.sigmoid(xg) @ g_up)


def _chunk_gla(q, k, v, logf, s0):
    bsz, t_len, h, dk = q.shape
    dv = v.shape[-1]
    n = t_len // HG_CHUNK
    dt = q.dtype
    q = q.reshape(bsz, n, HG_CHUNK, h, dk)
    k = k.reshape(bsz, n, HG_CHUNK, h, dk)
    v = v.reshape(bsz, n, HG_CHUNK, h, dv)
    b = jnp.cumsum(logf.reshape(bsz, n, HG_CHUNK, h, dk).astype(jnp.float32), axis=2)
    b_ref = b[:, :, HG_CHUNK // 2 - 1:HG_CHUNK // 2]
    b_last = b[:, :, HG_CHUNK - 1:]
    q_in = q * jnp.exp(b - b_ref).astype(dt)
    k_in = k * jnp.exp(b_ref - b).astype(dt)
    scores = jnp.einsum('bnthk,bnshk->bnhts', q_in, k_in)
    mask = jnp.tril(jnp.ones((HG_CHUNK, HG_CHUNK), dtype=bool))
    scores = jnp.where(mask, scores, 0)
    o_intra = jnp.einsum('bnhts,bnshv->bnthv', scores, v)
    kv = jnp.einsum('bnshk,bnshv->bnhkv', k * jnp.exp(b_last - b).astype(dt), v)
    chunk_decay = jnp.exp(b_last[:, :, 0]).astype(dt)

    def step(S, inp):
        dec, kv_c = inp
        return dec[..., None] * S + kv_c, S
    s_fin, s_prev = lax.scan(step, s0, (jnp.moveaxis(chunk_decay, 1, 0), jnp.moveaxis(kv, 1, 0)))
    s_prev = jnp.moveaxis(s_prev, 0, 1)
    o_inter = jnp.einsum('bnthk,bnhkv->bnthv', q * jnp.exp(b).astype(dt), s_prev)
    return (o_intra + o_inter).reshape(bsz, t_len, h, dv), s_fin


def _hgrn2_mix(q, f_f, f_b, i, lb, s0_f, s0_b):
    q_h, i_h = _heads(q, HG_HEADS), _heads(i, HG_HEADS)
    outs, states = [], []
    for d, (fz, s0) in enumerate(((f_f, s0_f), (f_b, s0_b))):
        f = lb + (1 - lb) * jax.nn.sigmoid(fz)
        seq = (q_h, _heads(1 - f, HG_HEADS), i_h, _heads(jnp.log(f), HG_HEADS))
        if d == 1:
            seq = tuple(t[:, ::-1] for t in seq)
        o, s_fin = _chunk_gla(*seq, s0)
        outs.append(o[:, ::-1] if d == 1 else o)
        states.append(s_fin)
    return outs[0] + outs[1], states[0], states[1]


def _hgrn2_readout(o, gz, norm_w):
    on = _rmsnorm(o, _heads(norm_w, HG_HEADS))
    return on.reshape(*o.shape[:-2], HG_WIDTH) * jax.nn.silu(gz)


def _merge(y_rw, y_hg, zg, p_a, p_b, w_out):
    ga, gb = jnp.split(zg, 2, axis=-1)
    return (jax.nn.sigmoid(ga) * (y_rw @ p_a) + jax.nn.sigmoid(gb) * (y_hg @ p_b)) @ w_out


def _sqrelu_mlp(h, w1, w2):
    return jnp.square(jax.nn.relu(h @ w1)) @ w2


def setup_inputs(seed: int = 0) -> dict:
    key = jax.random.key(seed)
    ks = jax.random.split(key, 28)
    nrm = lambda k, shape, s: jax.random.normal(k, shape, jnp.float32) * s
    L = DEPTH
    return {
        "x": nrm(ks[0], (BATCH, SEQ, D_MODEL), 1.0),
        "c": nrm(ks[1], (BATCH, D_MODEL), 1.0),
        "ctx": nrm(ks[2], (BATCH, CTX_LEN, D_MODEL), 1.0),
        "c_ctx": nrm(ks[3], (D_MODEL,), 1.0),
        "norm1_w": 1.0 + nrm(ks[4], (L, D_MODEL), 0.02),
        "norm2_w": 1.0 + nrm(ks[5], (L, D_MODEL), 0.02),
        "w_mod": nrm(ks[6], (L, D_MODEL, N_MOD * D_MODEL), 0.5 * D_MODEL ** -0.5),
        "b_mod": nrm(ks[7], (L, N_MOD * D_MODEL), 0.02),
        "w_in": nrm(ks[8], (L, D_MODEL, IN_COLS), D_MODEL ** -0.5),
        "rw_mu": jax.random.uniform(ks[9], (L, 4, RW_SHIFT_COLS), jnp.float32, 0.05, 0.45),
        "rw_w0": jax.random.uniform(ks[10], (L, 2, RW_WIDTH), jnp.float32, -6.0, -1.0),
        "rw_w_up": nrm(ks[11], (L, 2, RW_DECAY_RANK, RW_WIDTH), 0.1),
        "rw_a0": nrm(ks[12], (L, 2, RW_WIDTH), 0.1),
        "rw_a_up": nrm(ks[13], (L, 2, RW_ICL_RANK, RW_WIDTH), 0.5 * RW_ICL_RANK ** -0.5),
        "rw_g_up": nrm(ks[14], (L, RW_GATE_RANK, RW_WIDTH), RW_GATE_RANK ** -0.5),
        "rw_k_k": 0.85 + nrm(ks[15], (L, RW_WIDTH), 0.02),
        "rw_k_a": 1.0 + nrm(ks[16], (L, RW_WIDTH), 0.02),
        "rw_r_k": nrm(ks[17], (L, RW_HEADS, RW_HEAD_DIM), 0.1),
        "rw_ln_w": 1.0 + nrm(ks[18], (L, RW_WIDTH), 0.02),
        "rw_ln_b": nrm(ks[19], (L, RW_WIDTH), 0.02),
        "hg_lb": nrm(ks[20], (L + 1, HG_WIDTH), 0.1),
        "hg_norm_w": 1.0 + nrm(ks[21], (L, HG_WIDTH), 0.02),
        "p_a": nrm(ks[22], (L, RW_WIDTH, D_MODEL), RW_WIDTH ** -0.5),
        "p_b": nrm(ks[23], (L, HG_WIDTH, D_MODEL), HG_WIDTH ** -0.5),
        "w_out": nrm(ks[24], (L, D_MODEL, D_MODEL), D_MODEL ** -0.5),
        "w_fc1": nrm(ks[25], (L, D_MODEL, FFN_HIDDEN), D_MODEL ** -0.5),
        "w_fc2": nrm(ks[26], (L, FFN_HIDDEN, D_MODEL), FFN_HIDDEN ** -0.5),
        "final_norm_w": 1.0 + nrm(ks[27], (D_MODEL,), 0.02),
    }


def reference(x, c, ctx, c_ctx, norm1_w, norm2_w, w_mod, b_mod, w_in, rw_mu, rw_w0, rw_w_up, rw_a0,
              rw_a_up, rw_g_up, rw_k_k, rw_k_a, rw_r_k, rw_ln_w, rw_ln_b, hg_lb, hg_norm_w, p_a, p_b,
              w_out, w_fc1, w_fc2, final_norm_w):
    bsz = x.shape[0]
    rows = x.shape[1] // GRID_W
    dt = x.dtype
    c_act = jax.nn.silu(c)
    cc_act = jax.nn.silu(c_ctx)
    lb_all = jnp.cumsum(jax.nn.softmax(hg_lb.astype(jnp.float32), axis=0), axis=0)
    zeros_rw = jnp.zeros((bsz, RW_HEADS, RW_HEAD_DIM, RW_HEAD_DIM), dt)
    zeros_hg = jnp.zeros((bsz, HG_HEADS, HG_KEY_DIM, HG_VAL_DIM), dt)
    for l in range(DEPTH):
        mod = (c_act @ w_mod[l] + b_mod[l])[:, None, :]
        mod_c = cc_act @ w_mod[l] + b_mod[l]
        sh1, sc1, g1, sh2, sc2, g2 = jnp.split(mod, N_MOD, axis=-1)
        sh1c, sc1c, g1c, sh2c, sc2c, g2c = jnp.split(mod_c, N_MOD, axis=-1)
        lb = lb_all[l].astype(dt)
        rw_args = (rw_w0[l], rw_w_up[l], rw_a0[l], rw_a_up[l], rw_k_k[l], rw_k_a[l])

        h = _modulate(x, norm1_w[l], sh1, sc1)
        hc = _modulate(ctx, norm1_w[l], sh1c, sc1c)
        zr, zh, zg = jnp.split(h @ w_in[l], [RW_SHIFT_COLS, RW_SHIFT_COLS + HG_COLS], axis=-1)
        zrc, zhc, zgc = jnp.split(hc @ w_in[l], [RW_SHIFT_COLS, RW_SHIFT_COLS + HG_COLS], axis=-1)
        zr = _shift_grid(zr, rw_mu[l], rows)
        zrc = _shift_seq(zrc, rw_mu[l])
        r, k_raw, v, xw, xa, xg = jnp.split(zr, RW_SPLITS, axis=-1)
        rc, k_rawc, vc, xwc, xac, xgc = jnp.split(zrc, RW_SPLITS, axis=-1)
        q, f_f, f_b, i_in, gz = jnp.split(zh, 5, axis=-1)
        qc, f_fc, f_bc, i_inc, gzc = jnp.split(zhc, 5, axis=-1)

        yc_rw, rc_h, kc_sum, vc_h, s_rw_f, s_rw_b = _rwkv7_mix(rc, k_rawc, vc, xwc, xac, *rw_args,
                                                               zeros_rw, zeros_rw)
        oc_hg, s_hg_f, s_hg_b = _hgrn2_mix(qc, f_fc, f_bc, i_inc, lb, zeros_hg, zeros_hg)
        y_rw, r_h, k_sum, v_h, _, _ = _rwkv7_mix(r, k_raw, v, xw, xa, *rw_args, s_rw_f, s_rw_b)
        o_hg, _, _ = _hgrn2_mix(q, f_f, f_b, i_in, lb, s_hg_f, s_hg_b)

        y_rw = _rwkv7_readout(y_rw, r_h, k_sum, v_h, xg, rw_g_up[l], rw_r_k[l], rw_ln_w[l], rw_ln_b[l])
        y_hg = _hgrn2_readout(o_hg, gz, hg_norm_w[l])
        x = x + g1 * _merge(y_rw, y_hg, zg, p_a[l], p_b[l], w_out[l])
        x = x + g2 * _sqrelu_mlp(_modulate(x, norm2_w[l], sh2, sc2), w_fc1[l], w_fc2[l])

        if l < DEPTH - 1:
            yc_rw = _rwkv7_readout(yc_rw, rc_h, kc_sum, vc_h, xgc, rw_g_up[l], rw_r_k[l], rw_ln_w[l],
                                   rw_ln_b[l])
            yc_hg = _hgrn2_readout(oc_hg, gzc, hg_norm_w[l])
            ctx = ctx + g1c * _merge(yc_rw, yc_hg, zgc, p_a[l], p_b[l], w_out[l])
            ctx = ctx + g2c * _sqrelu_mlp(_modulate(ctx, norm2_w[l], sh2c, sc2c), w_fc1[l], w_fc2[l])
    return _rmsnorm(x, final_norm_w)
```

```python
import functools

import jax
import jax.numpy as jnp
from jax import lax
from jax.experimental import pallas as pl
from jax.experimental.pallas import tpu as pltpu

F32 = jnp.float32
BF16 = jnp.bfloat16

GRID_W = 64
RW_HEADS = 16
RW_N = 64
HG_HEADS = 8
HG_N = 128
HG_CHUNK = 64
N_MOD = 6
RMS_EPS = 1e-6
RW_GN_EPS = 64e-5
L2_EPS = 1e-12

SUBLANES = 8
LANES = 128
ROW_TILE = 256
SCAN_TILE = 64
VMEM_LIMIT = 56 * 1024 * 1024


def _params(*sem):
    return pltpu.CompilerParams(dimension_semantics=sem, vmem_limit_bytes=VMEM_LIMIT)


def _bdot(a, b):
    return jnp.dot(a, b, preferred_element_type=F32)


def _sigmoid(x):
    return 1.0 / (1.0 + jnp.exp(-x))


def _mod_kernel(c_ref, w_ref, b_ref, o_ref):
    c = c_ref[...]
    act = c * _sigmoid(c)
    o_ref[...] = _bdot(act.astype(BF16), w_ref[...].astype(BF16)) + b_ref[...]


def _mod(c_rows, w_mod, b_mod):
    rows, d = c_rows.shape
    n = w_mod.shape[1]
    return pl.pallas_call(
        _mod_kernel,
        grid=(n // d,),
        in_specs=[pl.BlockSpec((rows, d), lambda j: (0, 0)),
                  pl.BlockSpec((d, d), lambda j: (0, j)),
                  pl.BlockSpec((1, d), lambda j: (0, j))],
        out_specs=pl.BlockSpec((rows, d), lambda j: (0, j)),
        out_shape=jax.ShapeDtypeStruct((rows, n), F32),
        compiler_params=_params("arbitrary"),
        name="mod",
    )(c_rows, w_mod, b_mod)


def _modulated_norm(x, nw, sh, sc):
    ms = jnp.mean(x * x, axis=-1, keepdims=True)
    return (x * lax.rsqrt(ms + RMS_EPS) * nw) * (1.0 + sc) + sh


def _in_proj_kernel(x_ref, nw_ref, sh_ref, sc_ref, w_ref, zr_ref, zh_ref, zg_ref):
    hb = _modulated_norm(x_ref[0], nw_ref[...], sh_ref[0], sc_ref[0]).astype(BF16)
    n_r, n_h = zr_ref.shape[2], zh_ref.shape[2]
    zr_ref[0] = _bdot(hb, w_ref[:, :n_r])
    zh_ref[0] = _bdot(hb, w_ref[:, n_r:n_r + n_h])
    zg_ref[0] = _bdot(hb, w_ref[:, n_r + n_h:])


def _in_proj(xc, nw, modcat, w_bf, n_r, n_h, n_g):
    b, tt, d = xc.shape
    nblk = tt // ROW_TILE
    mod_idx = lambda col: (lambda bi, i: (2 * bi + jnp.minimum(i, 1), 0, col))
    return pl.pallas_call(
        _in_proj_kernel,
        grid=(b, nblk),
        in_specs=[pl.BlockSpec((1, ROW_TILE, d), lambda bi, i: (bi, i, 0)),
                  pl.BlockSpec((1, d), lambda bi, i: (0, 0)),
                  pl.BlockSpec((1, 1, d), mod_idx(0)),
                  pl.BlockSpec((1, 1, d), mod_idx(1)),
                  pl.BlockSpec(w_bf.shape, lambda bi, i: (0, 0), pipeline_mode=pl.Buffered(1))],
        out_specs=[pl.BlockSpec((1, ROW_TILE, n), lambda bi, i: (bi, i, 0)) for n in (n_r, n_h, n_g)],
        out_shape=[jax.ShapeDtypeStruct((b, tt, n), F32) for n in (n_r, n_h, n_g)],
        compiler_params=_params("arbitrary", "arbitrary"),
        name="in_proj",
    )(xc, nw, modcat, modcat, w_bf)


def _rw_prep_kernel(z_ref, zp_ref, zn_ref, mu_ref, w0_ref, wup_ref, a0_ref, aup_ref, gup_ref,
                    r_ref, k_ref, v_ref, wd_ref, ad_ref, g_ref):
    i = pl.program_id(1)
    nblk = pl.num_programs(1)
    tt = z_ref.shape[1]
    d = r_ref.shape[2]
    is_lat = i > 0
    row = lax.broadcasted_iota(jnp.int32, (tt, 1), 0)
    col = row % GRID_W
    lmask = jnp.where(is_lat, col, row) == 0
    rmask = jnp.where(is_lat, col, row - (tt - GRID_W)) == GRID_W - 1
    latf = is_lat.astype(F32)
    up_ok = (i > 1).astype(F32)
    dn_ok = (i < nblk - 1).astype(F32)

    def shifted(c0, c1):
        z = z_ref[0, :, c0:c1]
        mu = mu_ref[:, c0:c1]
        left = jnp.where(lmask, 0.0, pltpu.roll(z, 1, 0))
        right = jnp.where(rmask, 0.0, pltpu.roll(z, tt - 1, 0))
        up = jnp.concatenate([zp_ref[0, :, c0:c1] * up_ok, z[:tt - GRID_W]], axis=0)
        down = jnp.concatenate([z[GRID_W:], zn_ref[0, :, c0:c1] * dn_ok], axis=0)
        out = z + mu[0:1] * (left - z) + mu[1:2] * (right - z)
        return out + latf * (mu[2:3] * (up - z) + mu[3:4] * (down - z))

    r_ref[0] = shifted(0, d)
    k_ref[0] = shifted(d, 2 * d)
    v_ref[0] = shifted(2 * d, 3 * d)
    rest = shifted(3 * d, z_ref.shape[2])
    xwa = rest[:, :LANES]
    xw_t = jnp.tanh(xwa).astype(BF16)
    xa_b = xwa.astype(BF16)
    for dr in range(2):
        u = -(w0_ref[dr:dr + 1, :] + _bdot(xw_t, wup_ref[dr]))
        softplus = jnp.maximum(u, 0.0) + jnp.log(1.0 + jnp.exp(-jnp.abs(u)))
        wd_ref[dr, 0] = jnp.exp(-jnp.exp(-softplus - 0.5))
        ad_ref[dr, 0] = _sigmoid(a0_ref[dr:dr + 1, :] + _bdot(xa_b, aup_ref[dr]))
    g_ref[0] = _bdot(_sigmoid(rest[:, LANES:]).astype(BF16), gup_ref[...])


def _rw_prep(zr, mu, w0, wup, a0, aup, gup, d):
    b, tt, nr = zr.shape
    nblk = tt // ROW_TILE
    per = ROW_TILE // GRID_W
    last = tt // GRID_W - 1
    const = lambda shape: pl.BlockSpec(shape, lambda bi, i: (0,) * len(shape))
    row_spec = pl.BlockSpec((1, ROW_TILE, d), lambda bi, i: (bi, i, 0))
    dir_spec = pl.BlockSpec((2, 1, ROW_TILE, d), lambda bi, i: (0, bi, i, 0))
    row_shape = jax.ShapeDtypeStruct((b, tt, d), F32)
    dir_shape = jax.ShapeDtypeStruct((2, b, tt, d), F32)
    return pl.pallas_call(
        _rw_prep_kernel,
        grid=(b, nblk),
        in_specs=[pl.BlockSpec((1, ROW_TILE, nr), lambda bi, i: (bi, i, 0)),
                  pl.BlockSpec((1, GRID_W, nr), lambda bi, i: (bi, jnp.maximum(i * per - 1, 0), 0)),
                  pl.BlockSpec((1, GRID_W, nr), lambda bi, i: (bi, jnp.minimum(i * per + per, last), 0)),
                  const(mu.shape), const(w0.shape), const(wup.shape), const(a0.shape),
                  const(aup.shape), const(gup.shape)],
        out_specs=[row_spec, row_spec, row_spec, dir_spec, dir_spec, row_spec],
        out_shape=[row_shape, row_shape, row_shape, dir_shape, dir_shape, row_shape],
        compiler_params=_params("arbitrary", "arbitrary"),
        name="rw_prep",
    )(zr, zr, zr, mu, w0, wup, a0, aup, gup)


def _seq_block(n_ctx_blocks, n_blocks):
    def blk(d, i):
        bwd = jnp.where(i < n_ctx_blocks, n_ctx_blocks - 1 - i, n_blocks - 1 + n_ctx_blocks - i)
        return jnp.where(d == 0, i, bwd)
    return blk


def _rw_scan_kernel(r_ref, k_ref, v_ref, w_ref, a_ref, kkp_ref, kap_ref, y_ref,
                    s_ref, kk_s, b_s, k_s):
    d = pl.program_id(0)
    i = pl.program_id(1)
    tb = r_ref.shape[0]
    nv = RW_N // SUBLANES

    @pl.when(i == 0)
    def _():
        s_ref[...] = jnp.zeros_like(s_ref)

    def bcast(ref, *idx):
        k = idx[-1]
        row = ref[(*idx[:-1], pl.ds(k, 1), slice(None))]
        return jnp.broadcast_to(row, (SUBLANES, LANES))

    def step(s, carry):
        t = jnp.where(d == 0, s, tb - 1 - s)
        kraw = k_ref[t]
        a = a_ref[0, t]
        kkr = kraw * kkp_ref[...]
        nrm = jnp.sqrt(jnp.sum(kkr * kkr, axis=0, keepdims=True))
        kk = kkr / jnp.maximum(nrm, L2_EPS)
        kk_s[...] = kk
        b_s[...] = kk * a
        k_s[...] = kraw * (1.0 + (a - 1.0) * kap_ref[...])
        vt = v_ref[t]
        vs = [vt[SUBLANES * j:SUBLANES * (j + 1)] for j in range(nv)]

        acc = [None] * nv
        for k in range(RW_N):
            kkb = bcast(kk_s, k)
            for j in range(nv):
                p = s_ref[k, SUBLANES * j:SUBLANES * (j + 1), :] * kkb
                acc[j] = p if acc[j] is None else acc[j] + p
        sa = [-x for x in acc]

        yacc = [None] * nv
        for k in range(RW_N):
            wb = bcast(w_ref, 0, t, k)
            bb = bcast(b_s, k)
            kb = bcast(k_s, k)
            rb = bcast(r_ref, t, k)
            for j in range(nv):
                rows = slice(SUBLANES * j, SUBLANES * (j + 1))
                sn = s_ref[k, rows, :] * wb + (sa[j] * bb + vs[j] * kb)
                s_ref[k, rows, :] = sn
                p = sn * rb
                yacc[j] = p if yacc[j] is None else yacc[j] + p
        y_ref[0, t] = jnp.concatenate(yacc, axis=0)
        return carry

    lax.fori_loop(0, tb, step, 0)


def _rw_scan(r_t, k_t, v_t, w_t, a_t, kkp, kap, n_ctx_blocks):
    tt = r_t.shape[0]
    nblk = tt // SCAN_TILE
    blk = _seq_block(n_ctx_blocks, nblk)
    tile = (SCAN_TILE, RW_N, LANES)
    shared = pl.BlockSpec(tile, lambda d, i: (blk(d, i), 0, 0))
    per_dir = pl.BlockSpec((1,) + tile, lambda d, i: (d, blk(d, i), 0, 0))
    const = pl.BlockSpec((RW_N, LANES), lambda d, i: (0, 0))
    return pl.pallas_call(
        _rw_scan_kernel,
        grid=(2, nblk),
        in_specs=[shared, shared, shared, per_dir, per_dir, const, const],
        out_specs=per_dir,
        out_shape=jax.ShapeDtypeStruct((2, tt, RW_N, LANES), F32),
        scratch_shapes=[pltpu.VMEM((RW_N, RW_N, LANES), F32),
                        pltpu.VMEM((RW_N, LANES), F32),
                        pltpu.VMEM((RW_N, LANES), F32),
                        pltpu.VMEM((RW_N, LANES), F32)],
        compiler_params=_params("arbitrary", "arbitrary"),
        name="rw_scan",
    )(r_t, k_t, v_t, w_t, a_t, kkp, kap)


def _rw_readout_kernel(y_ref, r_ref, k_ref, v_ref, a_ref, kap_ref, rkp_ref, lnw_ref, lnb_ref, o_ref):
    y = y_ref[0] + y_ref[1]
    mean = jnp.mean(y, axis=1, keepdims=True)
    yc = y - mean
    var = jnp.mean(yc * yc, axis=1, keepdims=True)
    yn = yc * lax.rsqrt(var + RW_GN_EPS) * lnw_ref[...] + lnb_ref[...]
    kraw = k_ref[...]
    kap = kap_ref[...]
    k_sum = kraw * (1.0 + (a_ref[0] - 1.0) * kap) + kraw * (1.0 + (a_ref[1] - 1.0) * kap)
    bonus = jnp.sum(r_ref[...] * k_sum * rkp_ref[...], axis=1, keepdims=True) * v_ref[...]
    o_ref[...] = yn + bonus


def _rw_readout(y_t, r_t, k_t, v_t, a_t, kap, rkp, lnw, lnb, n_ctx):
    tt = r_t.shape[0]
    tb = 32
    off = n_ctx // tb
    tile = (tb, RW_N, LANES)
    shared = pl.BlockSpec(tile, lambda i: (i + off, 0, 0))
    both = pl.BlockSpec((2,) + tile, lambda i: (0, i + off, 0, 0))
    const = pl.BlockSpec((RW_N, LANES), lambda i: (0, 0))
    return pl.pallas_call(
        _rw_readout_kernel,
        grid=((tt - n_ctx) // tb,),
        in_specs=[both, shared, shared, shared, both, const, const, const, const],
        out_specs=pl.BlockSpec(tile, lambda i: (i, 0, 0)),
        out_shape=jax.ShapeDtypeStruct((tt - n_ctx, RW_N, LANES), F32),
        compiler_params=_params("arbitrary"),
        name="rw_readout",
    )(y_t, r_t, k_t, v_t, a_t, kap, rkp, lnw, lnb)


def _hg_chunk(q, fz, vv, lbh, st, tri, mask, ref_row, last_row):
    f = lbh + (1.0 - lbh) * _sigmoid(fz)
    kk = 1.0 - f
    lf = jnp.log(f)
    hi = lf.astype(BF16)
    lo = (lf - hi.astype(F32)).astype(BF16)
    b = _bdot(tri, hi) + _bdot(tri, lo)
    b_mid = b[ref_row:ref_row + 1]
    b_last = b[last_row:last_row + 1]
    q_in = (q * jnp.exp(b - b_mid)).astype(BF16)
    k_in = (kk * jnp.exp(b_mid - b)).astype(BF16)
    scores = lax.dot_general(q_in, k_in, (((1,), (1,)), ((), ())), preferred_element_type=F32)
    scores = jnp.where(mask, scores, 0.0).astype(BF16)
    vb = vv.astype(BF16)
    o = _bdot(scores, vb)
    o = o + lax.dot_general((q * jnp.exp(b)).astype(BF16), st.astype(BF16),
                            (((1,), (1,)), ((), ())), preferred_element_type=F32)
    k_dec = (kk * jnp.exp(b_last - b)).astype(BF16)
    kv_t = lax.dot_general(vb, k_dec, (((0,), (0,)), ((), ())), preferred_element_type=F32)
    return o, jnp.exp(b_last) * st + kv_t


def _hg_kernel(q_ref, f_ref, i_ref, lbp_ref, o_ref, st_ref):
    d = pl.program_id(1)
    i = pl.program_id(2)
    n_chunks = q_ref.shape[1] // HG_CHUNK

    @pl.when(i == 0)
    def _():
        st_ref[...] = jnp.zeros_like(st_ref)

    lbp = lbp_ref[...]
    e = jnp.exp(lbp - jnp.max(lbp, axis=0, keepdims=True))
    lb = e[0:1] / jnp.sum(e, axis=0, keepdims=True)
    t_idx = lax.broadcasted_iota(jnp.int32, (HG_CHUNK, HG_CHUNK), 0)
    s_idx = lax.broadcasted_iota(jnp.int32, (HG_CHUNK, HG_CHUNK), 1)

    def run(fwd):
        mask = (s_idx <= t_idx) if fwd else (s_idx >= t_idx)
        tri = mask.astype(F32).astype(BF16)
        ref_row = HG_CHUNK // 2 - 1 if fwd else HG_CHUNK // 2
        last_row = HG_CHUNK - 1 if fwd else 0
        for c in (range(n_chunks) if fwd else reversed(range(n_chunks))):
            rows = slice(c * HG_CHUNK, (c + 1) * HG_CHUNK)
            for h in range(HG_HEADS):
                cols = slice(h * HG_N, (h + 1) * HG_N)
                o, st = _hg_chunk(q_ref[0, rows, cols], f_ref[0, rows, cols], i_ref[0, rows, cols],
                                  lb[:, cols], st_ref[h], tri, mask, ref_row, last_row)
                o_ref[0, 0, rows, cols] = o
                st_ref[h] = st

    @pl.when(d == 0)
    def _():
        run(True)

    @pl.when(d == 1)
    def _():
        run(False)


def _hg(zh, hg_lb, n_ctx):
    b, tt, _ = zh.shape
    w = HG_HEADS * HG_N
    nblk = tt // ROW_TILE
    blk = _seq_block(n_ctx // ROW_TILE, nblk)
    return pl.pallas_call(
        _hg_kernel,
        grid=(b, 2, nblk),
        in_specs=[pl.BlockSpec((1, ROW_TILE, w), lambda bi, d, i: (bi, blk(d, i), 0)),
                  pl.BlockSpec((1, ROW_TILE, w), lambda bi, d, i: (bi, blk(d, i), 1 + d)),
                  pl.BlockSpec((1, ROW_TILE, w), lambda bi, d, i: (bi, blk(d, i), 3)),
                  pl.BlockSpec(hg_lb.shape, lambda bi, d, i: (0, 0))],
        out_specs=pl.BlockSpec((1, 1, ROW_TILE, w), lambda bi, d, i: (d, bi, blk(d, i), 0)),
        out_shape=jax.ShapeDtypeStruct((2, b, tt, w), F32),
        scratch_shapes=[pltpu.VMEM((HG_HEADS, HG_N, HG_N), F32)],
        compiler_params=_params("arbitrary", "arbitrary", "arbitrary"),
        name="hg",
    )(zh, zh, zh, hg_lb)


def _merge_kernel(yrw_ref, g_ref, o_ref, gz_ref, zg_ref, x_ref, g1_ref, hnw_ref,
                  pa_ref, pb_ref, wo_ref, out_ref):
    d = x_ref.shape[2]
    y_rw = (yrw_ref[0] * g_ref[0]).astype(BF16)
    o = o_ref[0, 0] + o_ref[1, 0]
    gz = gz_ref[0]
    hnw = hnw_ref[...]
    parts = []
    for h in range(HG_HEADS):
        cols = slice(h * HG_N, (h + 1) * HG_N)
        oh = o[:, cols]
        ms = jnp.mean(oh * oh, axis=-1, keepdims=True)
        parts.append(oh * lax.rsqrt(ms + RMS_EPS) * hnw[:, cols])
    y_hg = (jnp.concatenate(parts, axis=1) * (gz * _sigmoid(gz))).astype(BF16)
    zg = zg_ref[0]
    m = _sigmoid(zg[:, :d]) * _bdot(y_rw, pa_ref[...]) + _sigmoid(zg[:, d:]) * _bdot(y_hg, pb_ref[...])
    out_ref[0] = x_ref[0] + g1_ref[0] * _bdot(m.astype(BF16), wo_ref[...])


def _merge(y_rw, g, o_hg, zh, zg, x, modcat, hnw, pa, pb, wo, n_ctx):
    b, t, d = x.shape
    off = n_ctx // ROW_TILE
    lat = lambda bi, i: (bi, i, 0)
    cat = lambda bi, i: (bi, i + off, 0)
    const = lambda shape: pl.BlockSpec(shape, lambda bi, i: (0,) * len(shape))
    return pl.pallas_call(
        _merge_kernel,
        grid=(b, t // ROW_TILE),
        in_specs=[pl.BlockSpec((1, ROW_TILE, d), lat),
                  pl.BlockSpec((1, ROW_TILE, d), cat),
                  pl.BlockSpec((2, 1, ROW_TILE, d), lambda bi, i: (0, bi, i + off, 0)),
                  pl.BlockSpec((1, ROW_TILE, d), lambda bi, i: (bi, i + off, 4)),
                  pl.BlockSpec((1, ROW_TILE, 2 * d), cat),
                  pl.BlockSpec((1, ROW_TILE, d), lat),
                  pl.BlockSpec((1, 1, d), lambda bi, i: (2 * bi + 1, 0, 2)),
                  const(hnw.shape), const(pa.shape), const(pb.shape), const(wo.shape)],
        out_specs=pl.BlockSpec((1, ROW_TILE, d), lat),
        out_shape=jax.ShapeDtypeStruct((b, t, d), F32),
        compiler_params=_params("arbitrary", "arbitrary"),
        name="merge",
    )(y_rw, g, o_hg, zh, zg, x, modcat, hnw, pa, pb, wo)


def _mlp_kernel(x_ref, nw_ref, sh_ref, sc_ref, g2_ref, w1_ref, w2_ref, fw_ref, out_ref):
    x = x_ref[0]
    hb = _modulated_norm(x, nw_ref[...], sh_ref[0], sc_ref[0]).astype(BF16)
    u = jnp.maximum(_bdot(hb, w1_ref[...]), 0.0)
    y = x + g2_ref[0] * _bdot((u * u).astype(BF16), w2_ref[...])
    ms = jnp.mean(y * y, axis=-1, keepdims=True)
    out_ref[0] = y * lax.rsqrt(ms + RMS_EPS) * fw_ref[...]


def _mlp(x1, nw, modcat, w1, w2, fw):
    b, t, d = x1.shape
    mod_idx = lambda col: (lambda bi, i: (2 * bi + 1, 0, col))
    const = lambda shape, **kw: pl.BlockSpec(shape, lambda bi, i: (0,) * len(shape), **kw)
    return pl.pallas_call(
        _mlp_kernel,
        grid=(b, t // ROW_TILE),
        in_specs=[pl.BlockSpec((1, ROW_TILE, d), lambda bi, i: (bi, i, 0)),
                  const(nw.shape),
                  pl.BlockSpec((1, 1, d), mod_idx(3)),
                  pl.BlockSpec((1, 1, d), mod_idx(4)),
                  pl.BlockSpec((1, 1, d), mod_idx(5)),
                  const(w1.shape, pipeline_mode=pl.Buffered(1)),
                  const(w2.shape, pipeline_mode=pl.Buffered(1)),
                  const(fw.shape)],
        out_specs=pl.BlockSpec((1, ROW_TILE, d), lambda bi, i: (bi, i, 0)),
        out_shape=jax.ShapeDtypeStruct((b, t, d), F32),
        compiler_params=_params("arbitrary", "arbitrary"),
        name="mlp",
    )(x1, nw, modcat, modcat, modcat, w1, w2, fw)


def _to_scan(a):
    *lead, b, t, _ = a.shape
    n = len(lead)
    a = a.reshape(*lead, b, t, RW_HEADS, RW_N)
    a = jnp.transpose(a, (*range(n), n + 1, n + 3, n, n + 2))
    return a.reshape(*lead, t, RW_N, b * RW_HEADS)


def _from_scan(a, b):
    t = a.shape[0]
    a = a.reshape(t, RW_N, b, RW_HEADS)
    return jnp.transpose(a, (2, 0, 3, 1)).reshape(b, t, RW_HEADS * RW_N)


def _head_tile(p, b):
    return jnp.tile(p.reshape(RW_HEADS, RW_N).T, (1, b))


def kernel(x, c, ctx, c_ctx, norm1_w, norm2_w, w_mod, b_mod, w_in, rw_mu, rw_w0, rw_w_up, rw_a0, rw_a_up, rw_g_up, rw_k_k, rw_k_a, rw_r_k, rw_ln_w, rw_ln_b, hg_lb, hg_norm_w, p_a, p_b, w_out, w_fc1, w_fc2, final_norm_w):
    b, t, d = x.shape
    n_ctx = ctx.shape[1]
    assert w_mod.shape[0] == 1, "single-layer block"
    assert b * RW_HEADS == LANES and n_ctx == ROW_TILE and t % ROW_TILE == 0
    assert d == RW_HEADS * RW_N == HG_HEADS * HG_N
    n_r = rw_mu.shape[2]
    n_h = 5 * d
    n_g = 2 * d
    rank_w, rank_a = rw_w_up.shape[2], rw_a_up.shape[2]
    assert rank_w + rank_a == LANES and n_r == 3 * d + 2 * LANES

    c_rows = jnp.zeros((2 * SUBLANES, d), F32).at[:b].set(c).at[b].set(c_ctx)
    mod = _mod(c_rows, w_mod[0], b_mod)
    modcat = jnp.stack([jnp.broadcast_to(mod[b], (b, N_MOD * d)), mod[:b]], axis=1).reshape(2 * b, 1, N_MOD * d)

    xc = jnp.concatenate([ctx, x], axis=1)
    zr, zh, zg = _in_proj(xc, norm1_w, modcat, w_in[0].astype(BF16), n_r, n_h, n_g)

    wup = jnp.pad(rw_w_up[0], ((0, 0), (0, rank_a), (0, 0))).astype(BF16)
    aup = jnp.pad(rw_a_up[0], ((0, 0), (rank_w, 0), (0, 0))).astype(BF16)
    r, k_raw, v, wd, ad, g = _rw_prep(zr, rw_mu[0], rw_w0[0], wup, rw_a0[0], aup,
                                      rw_g_up[0].astype(BF16), d)

    r_t, k_t, v_t, w_t, a_t = (_to_scan(a) for a in (r, k_raw, v, wd, ad))
    kap = _head_tile(rw_k_a[0], b)
    y_t = _rw_scan(r_t, k_t, v_t, w_t, a_t, _head_tile(rw_k_k[0], b), kap, n_ctx // SCAN_TILE)
    y_rw_t = _rw_readout(y_t, r_t, k_t, v_t, a_t, kap, _head_tile(rw_r_k[0].reshape(-1), b),
                         _head_tile(rw_ln_w[0], b), _head_tile(rw_ln_b[0], b), n_ctx)
    y_rw = _from_scan(y_rw_t, b)

    o_hg = _hg(zh, hg_lb, n_ctx)

    x1 = _merge(y_rw, g, o_hg, zh, zg, x, modcat, hg_norm_w, p_a[0].astype(BF16), p_b[0].astype(BF16),
                w_out[0].astype(BF16), n_ctx)
    return _mlp(x1, norm2_w, modcat, w_fc1[0].astype(BF16), w_fc2[0].astype(BF16),
                final_norm_w.reshape(1, d))
```

```python
import functools

import jax
import jax.numpy as jnp
from jax import lax
from jax.experimental import pallas as pl
from jax.experimental.pallas import tpu as pltpu

F32 = jnp.float32
BF16 = jnp.bfloat16

GRID_W = 64
RW_HEADS = 16
RW_N = 64
HG_HEADS = 8
HG_N = 128
HG_CHUNK = 64
N_MOD = 6
RMS_EPS = 1e-6
RW_GN_EPS = 64e-5
L2_EPS = 1e-12

SUBLANES = 8
LANES = 128
ROW_TILE = 256
SCAN_TILE = 64
VMEM_LIMIT = 56 * 1024 * 1024


def _params(*sem):
    return pltpu.CompilerParams(dimension_semantics=sem, vmem_limit_bytes=VMEM_LIMIT)


def _bdot(a, b):
    return jnp.dot(a, b, preferred_element_type=F32)


def _sigmoid(x):
    return 1.0 / (1.0 + jnp.exp(-x))


def _mod_kernel(c_ref, w_ref, b_ref, o_ref):
    c = c_ref[...]
    act = c * _sigmoid(c)
    o_ref[...] = _bdot(act.astype(BF16), w_ref[...].astype(BF16)) + b_ref[...]


def _mod(c_rows, w_mod, b_mod):
    rows, d = c_rows.shape
    n = w_mod.shape[1]
    return pl.pallas_call(
        _mod_kernel,
        grid=(n // d,),
        in_specs=[pl.BlockSpec((rows, d), lambda j: (0, 0)),
                  pl.BlockSpec((d, d), lambda j: (0, j)),
                  pl.BlockSpec((1, d), lambda j: (0, j))],
        out_specs=pl.BlockSpec((rows, d), lambda j: (0, j)),
        out_shape=jax.ShapeDtypeStruct((rows, n), F32),
        compiler_params=_params("arbitrary"),
        name="mod",
    )(c_rows, w_mod, b_mod)


def _modulated_norm(x, nw, sh, sc):
    ms = jnp.mean(x * x, axis=-1, keepdims=True)
    return (x * lax.rsqrt(ms + RMS_EPS) * nw) * (1.0 + sc) + sh


def _in_proj_kernel(x_ref, nw_ref, sh_ref, sc_ref, w_ref, zr_ref, zh_ref, zg_ref):
    hb = _modulated_norm(x_ref[0], nw_ref[...], sh_ref[0], sc_ref[0]).astype(BF16)
    n_r, n_h = zr_ref.shape[2], zh_ref.shape[2]
    zr_ref[0] = _bdot(hb, w_ref[:, :n_r])
    zh_ref[0] = _bdot(hb, w_ref[:, n_r:n_r + n_h])
    zg_ref[0] = _bdot(hb, w_ref[:, n_r + n_h:])


def _in_proj(xc, nw, modcat, w_bf, n_r, n_h, n_g):
    b, tt, d = xc.shape
    nblk = tt // ROW_TILE
    mod_idx = lambda col: (lambda bi, i: (2 * bi + jnp.minimum(i, 1), 0, col))
    return pl.pallas_call(
        _in_proj_kernel,
        grid=(b, nblk),
        in_specs=[pl.BlockSpec((1, ROW_TILE, d), lambda bi, i: (bi, i, 0)),
                  pl.BlockSpec((1, d), lambda bi, i: (0, 0)),
                  pl.BlockSpec((1, 1, d), mod_idx(0)),
                  pl.BlockSpec((1, 1, d), mod_idx(1)),
                  pl.BlockSpec(w_bf.shape, lambda bi, i: (0, 0), pipeline_mode=pl.Buffered(1))],
        out_specs=[pl.BlockSpec((1, ROW_TILE, n), lambda bi, i: (bi, i, 0)) for n in (n_r, n_h, n_g)],
        out_shape=[jax.ShapeDtypeStruct((b, tt, n), F32) for n in (n_r, n_h, n_g)],
        compiler_params=_params("arbitrary", "arbitrary"),
        name="in_proj",
    )(xc, nw, modcat, modcat, w_bf)


def _rw_prep_kernel(z_ref, zp_ref, zn_ref, mu_ref, w0_ref, wup_ref, a0_ref, aup_ref, gup_ref,
                    r_ref, k_ref, v_ref, wd_ref, ad_ref, g_ref):
    i = pl.program_id(1)
    nblk = pl.num_programs(1)
    tt = z_ref.shape[1]
    d = r_ref.shape[2]
    is_lat = i > 0
    row = lax.broadcasted_iota(jnp.int32, (tt, 1), 0)
    col = row % GRID_W
    lmask = jnp.where(is_lat, col, row) == 0
    rmask = jnp.where(is_lat, col, row - (tt - GRID_W)) == GRID_W - 1
    latf = is_lat.astype(F32)
    up_ok = (i > 1).astype(F32)
    dn_ok = (i < nblk - 1).astype(F32)

    def shifted(c0, c1):
        z = z_ref[0, :, c0:c1]
        mu = mu_ref[:, c0:c1]
        left = jnp.where(lmask, 0.0, pltpu.roll(z, 1, 0))
        right = jnp.where(rmask, 0.0, pltpu.roll(z, tt - 1, 0))
        up = jnp.concatenate([zp_ref[0, :, c0:c1] * up_ok, z[:tt - GRID_W]], axis=0)
        down = jnp.concatenate([z[GRID_W:], zn_ref[0, :, c0:c1] * dn_ok], axis=0)
        out = z + mu[0:1] * (left - z) + mu[1:2] * (right - z)
        return out + latf * (mu[2:3] * (up - z) + mu[3:4] * (down - z))

    r_ref[0] = shifted(0, d)
    k_ref[0] = shifted(d, 2 * d)
    v_ref[0] = shifted(2 * d, 3 * d)
    rest = shifted(3 * d, z_ref.shape[2])
    xwa = rest[:, :LANES]
    xw_t = jnp.tanh(xwa).astype(BF16)
    xa_b = xwa.astype(BF16)
    for dr in range(2):
        u = -(w0_ref[dr:dr + 1, :] + _bdot(xw_t, wup_ref[dr]))
        softplus = jnp.maximum(u, 0.0) + jnp.log(1.0 + jnp.exp(-jnp.abs(u)))
        wd_ref[dr, 0] = jnp.exp(-jnp.exp(-softplus - 0.5))
        ad_ref[dr, 0] = _sigmoid(a0_ref[dr:dr + 1, :] + _bdot(xa_b, aup_ref[dr]))
    g_ref[0] = _bdot(_sigmoid(rest[:, LANES:]).astype(BF16), gup_ref[...])


def _rw_prep(zr, mu, w0, wup, a0, aup, gup, d):
    b, tt, nr = zr.shape
    nblk = tt // ROW_TILE
    per = ROW_TILE // GRID_W
    last = tt // GRID_W - 1
    const = lambda shape: pl.BlockSpec(shape, lambda bi, i: (0,) * len(shape))
    row_spec = pl.BlockSpec((1, ROW_TILE, d), lambda bi, i: (bi, i, 0))
    dir_spec = pl.BlockSpec((2, 1, ROW_TILE, d), lambda bi, i: (0, bi, i, 0))
    row_shape = jax.ShapeDtypeStruct((b, tt, d), F32)
    dir_shape = jax.ShapeDtypeStruct((2, b, tt, d), F32)
    return pl.pallas_call(
        _rw_prep_kernel,
        grid=(b, nblk),
        in_specs=[pl.BlockSpec((1, ROW_TILE, nr), lambda bi, i: (bi, i, 0)),
                  pl.BlockSpec((1, GRID_W, nr), lambda bi, i: (bi, jnp.maximum(i * per - 1, 0), 0)),
                  pl.BlockSpec((1, GRID_W, nr), lambda bi, i: (bi, jnp.minimum(i * per + per, last), 0)),
                  const(mu.shape), const(w0.shape), const(wup.shape), const(a0.shape),
                  const(aup.shape), const(gup.shape)],
        out_specs=[row_spec, row_spec, row_spec, dir_spec, dir_spec, row_spec],
        out_shape=[row_shape, row_shape, row_shape, dir_shape, dir_shape, row_shape],
        compiler_params=_params("arbitrary", "arbitrary"),
        name="rw_prep",
    )(zr, zr, zr, mu, w0, wup, a0, aup, gup)


K_UNROLL = 16
ACC_WAYS = 1


def _tree_sum(xs):
    while len(xs) > 1:
        xs = [xs[i] + xs[i + 1] for i in range(0, len(xs) - 1, 2)] + ([xs[-1]] if len(xs) % 2 else [])
    return xs[0]


def _seq_block(n_ctx_blocks, n_blocks):
    def blk(d, i):
        bwd = jnp.where(i < n_ctx_blocks, n_ctx_blocks - 1 - i, n_blocks - 1 + n_ctx_blocks - i)
        return jnp.where(d == 0, i, bwd)
    return blk


def _rw_scan_kernel(r_ref, k_ref, v_ref, w_ref, a_ref, kkp_ref, kap_ref, y_ref,
                    s_ref, kk_s, b_s, k_s, sa_s):
    d = pl.program_id(0)
    i = pl.program_id(1)
    tb = r_ref.shape[0]
    nv = RW_N // SUBLANES

    @pl.when(i == 0)
    def _():
        s_ref[...] = jnp.zeros_like(s_ref)

    def bcast(ref, *idx):
        k = idx[-1]
        row = ref[(*idx[:-1], pl.ds(k, 1), slice(None))]
        return jnp.broadcast_to(row, (SUBLANES, LANES))

    def time_index(s):
        s = jnp.minimum(s, tb - 1)
        return jnp.where(d == 0, s, tb - 1 - s)

    def prepare_kk(s, slot):
        kkr = k_ref[time_index(s)] * kkp_ref[...]
        nrm = jnp.sqrt(jnp.sum(kkr * kkr, axis=0, keepdims=True))
        kk_s[slot] = kkr / jnp.maximum(nrm, L2_EPS)

    def prepare_bk(s, slot):
        t = time_index(s)
        a = a_ref[0, t]
        b_s[slot] = kk_s[slot] * a
        k_s[slot] = k_ref[t] * (1.0 + (a - 1.0) * kap_ref[...])

    def state_dot(slot):
        acc = [[None] * nv for _ in range(ACC_WAYS)]
        for k in range(RW_N):
            kkb = bcast(kk_s, slot, k)
            part = acc[k % ACC_WAYS]
            for j in range(nv):
                p = s_ref[k, SUBLANES * j:SUBLANES * (j + 1), :] * kkb
                part[j] = p if part[j] is None else part[j] + p
        sa_s[...] = -jnp.concatenate([_tree_sum([acc[w][j] for w in range(ACC_WAYS)]) for j in range(nv)], axis=0)

    def sweep(s, slot):
        t = time_index(s)
        zero = jnp.zeros((SUBLANES, LANES), F32)

        def key_block(kblk, carry):
            yacc = [list(x) for x in carry[0]]
            acc = [list(x) for x in carry[1]]
            for kk in range(K_UNROLL):
                k = kblk * K_UNROLL + kk
                ypart, apart = yacc[kk % ACC_WAYS], acc[kk % ACC_WAYS]
                wb = bcast(w_ref, 0, t, k)
                bb = bcast(b_s, slot, k)
                kb = bcast(k_s, slot, k)
                rb = bcast(r_ref, t, k)
                kkn = bcast(kk_s, 1 - slot, k)
                for j in range(nv):
                    rows = slice(SUBLANES * j, SUBLANES * (j + 1))
                    sn = s_ref[k, rows, :] * wb + (sa_s[rows, :] * bb + v_ref[t, rows, :] * kb)
                    s_ref[k, rows, :] = sn
                    ypart[j] = ypart[j] + sn * rb
                    apart[j] = apart[j] + sn * kkn
            return yacc, acc

        init = [[zero] * nv for _ in range(ACC_WAYS)]
        n_kblk = RW_N // K_UNROLL
        carry = lax.fori_loop(0, n_kblk - 1, key_block, (init, init))
        yacc, acc = key_block(n_kblk - 1, carry)
        y_ref[0, t] = jnp.concatenate([_tree_sum([yacc[w][j] for w in range(ACC_WAYS)]) for j in range(nv)], axis=0)
        sa_s[...] = -jnp.concatenate([_tree_sum([acc[w][j] for w in range(ACC_WAYS)]) for j in range(nv)], axis=0)

    for s0 in (0, 1):
        prepare_kk(s0, s0)
        prepare_bk(s0, s0)
    state_dot(0)

    def pair(p, carry):
        s = 2 * p
        sweep(s, 0)
        prepare_kk(s + 2, 0)
        prepare_bk(s + 2, 0)
        sweep(s + 1, 1)
        prepare_kk(s + 3, 1)
        prepare_bk(s + 3, 1)
        return carry

    lax.fori_loop(0, tb // 2, pair, 0)


def _rw_scan(r_t, k_t, v_t, w_t, a_t, kkp, kap, n_ctx_blocks):
    tt = r_t.shape[0]
    nblk = tt // SCAN_TILE
    blk = _seq_block(n_ctx_blocks, nblk)
    tile = (SCAN_TILE, RW_N, LANES)
    shared = pl.BlockSpec(tile, lambda d, i: (blk(d, i), 0, 0))
    per_dir = pl.BlockSpec((1,) + tile, lambda d, i: (d, blk(d, i), 0, 0))
    const = pl.BlockSpec((RW_N, LANES), lambda d, i: (0, 0))
    return pl.pallas_call(
        _rw_scan_kernel,
        grid=(2, nblk),
        in_specs=[shared, shared, shared, per_dir, per_dir, const, const],
        out_specs=per_dir,
        out_shape=jax.ShapeDtypeStruct((2, tt, RW_N, LANES), F32),
        scratch_shapes=[pltpu.VMEM((RW_N, RW_N, LANES), F32),
                        pltpu.VMEM((2, RW_N, LANES), F32),
                        pltpu.VMEM((2, RW_N, LANES), F32),
                        pltpu.VMEM((2, RW_N, LANES), F32),
                        pltpu.VMEM((RW_N, LANES), F32)],
        compiler_params=_params("arbitrary", "arbitrary"),
        name="rw_scan",
    )(r_t, k_t, v_t, w_t, a_t, kkp, kap)


def _rw_readout_kernel(y_ref, r_ref, k_ref, v_ref, a_ref, kap_ref, rkp_ref, lnw_ref, lnb_ref, o_ref):
    y = y_ref[0] + y_ref[1]
    mean = jnp.mean(y, axis=1, keepdims=True)
    yc = y - mean
    var = jnp.mean(yc * yc, axis=1, keepdims=True)
    yn = yc * lax.rsqrt(var + RW_GN_EPS) * lnw_ref[...] + lnb_ref[...]
    kraw = k_ref[...]
    kap = kap_ref[...]
    k_sum = kraw * (1.0 + (a_ref[0] - 1.0) * kap) + kraw * (1.0 + (a_ref[1] - 1.0) * kap)
    bonus = jnp.sum(r_ref[...] * k_sum * rkp_ref[...], axis=1, keepdims=True) * v_ref[...]
    o_ref[...] = yn + bonus


def _rw_readout(y_t, r_t, k_t, v_t, a_t, kap, rkp, lnw, lnb, n_ctx):
    tt = r_t.shape[0]
    tb = 32
    off = n_ctx // tb
    tile = (tb, RW_N, LANES)
    shared = pl.BlockSpec(tile, lambda i: (i + off, 0, 0))
    both = pl.BlockSpec((2,) + tile, lambda i: (0, i + off, 0, 0))
    const = pl.BlockSpec((RW_N, LANES), lambda i: (0, 0))
    return pl.pallas_call(
        _rw_readout_kernel,
        grid=((tt - n_ctx) // tb,),
        in_specs=[both, shared, shared, shared, both, const, const, const, const],
        out_specs=pl.BlockSpec(tile, lambda i: (i, 0, 0)),
        out_shape=jax.ShapeDtypeStruct((tt - n_ctx, RW_N, LANES), F32),
        compiler_params=_params("arbitrary"),
        name="rw_readout",
    )(y_t, r_t, k_t, v_t, a_t, kap, rkp, lnw, lnb)


def _hg_kernel(q_ref, f_ref, i_ref, lbp_ref, tri_ref, o_ref, st_ref, qd_scr, kv_scr, dec_scr):
    d = pl.program_id(1)
    i = pl.program_id(2)
    n_chunks = q_ref.shape[1] // HG_CHUNK

    @pl.when(i == 0)
    def _():
        st_ref[...] = jnp.zeros_like(st_ref)

    lbp = lbp_ref[...]
    e = jnp.exp(lbp - jnp.max(lbp, axis=0, keepdims=True))
    lb = e[0:1] / jnp.sum(e, axis=0, keepdims=True)
    t_idx = lax.broadcasted_iota(jnp.int32, (HG_CHUNK, HG_CHUNK), 0)
    s_idx = lax.broadcasted_iota(jnp.int32, (HG_CHUNK, HG_CHUNK), 1)
    nt = (((1,), (1,)), ((), ()))
    tn = (((0,), (0,)), ((), ()))

    def run(fwd):
        mask = (s_idx <= t_idx) if fwd else (s_idx >= t_idx)
        mid_row = HG_CHUNK // 2 - 1 if fwd else HG_CHUNK // 2
        last_row = HG_CHUNK - 1 if fwd else 0
        order = list(range(n_chunks)) if fwd else list(reversed(range(n_chunks)))

        f = lb + (1.0 - lb) * _sigmoid(f_ref[0])
        lf = jnp.log(f)
        hi = lf.astype(BF16)
        lo = (lf - hi.astype(F32)).astype(BF16)
        tri = tri_ref[0]
        b = _bdot(tri, hi) + _bdot(tri, lo)
        for c in order:
            rows = slice(c * HG_CHUNK, (c + 1) * HG_CHUNK)
            bc = b[rows]
            b_mid = bc[mid_row:mid_row + 1]
            b_last = bc[last_row:last_row + 1]
            q_in = q_ref[0, rows, :] * jnp.exp(bc - b_mid)
            k_in = (1.0 - f[rows]) * jnp.exp(b_mid - bc)
            qd_scr[rows, :] = (q_in * jnp.exp(b_mid)).astype(BF16)
            k_dec = (k_in * jnp.exp(b_last - b_mid)).astype(BF16)
            dec_scr[c:c + 1, :] = jnp.exp(b_last)
            q_in = q_in.astype(BF16)
            k_in = k_in.astype(BF16)
            vb = i_ref[0, rows, :].astype(BF16)
            heads = [slice(h * HG_N, (h + 1) * HG_N) for h in range(HG_HEADS)]
            scores = [lax.dot_general(q_in[:, cols], k_in[:, cols], nt, preferred_element_type=F32)
                      for cols in heads]
            for h, cols in enumerate(heads):
                kv_scr[c, h] = lax.dot_general(vb[:, cols], k_dec[:, cols], tn, preferred_element_type=F32)
            for h, cols in enumerate(heads):
                o_ref[0, 0, rows, cols] = _bdot(jnp.where(mask, scores[h], 0.0).astype(BF16), vb[:, cols])
        for c in order:
            rows = slice(c * HG_CHUNK, (c + 1) * HG_CHUNK)
            for h in range(HG_HEADS):
                cols = slice(h * HG_N, (h + 1) * HG_N)
                st = st_ref[h]
                o_ref[0, 0, rows, cols] += lax.dot_general(qd_scr[rows, cols], st.astype(BF16), nt,
                                                           preferred_element_type=F32)
                st_ref[h] = dec_scr[c:c + 1, cols] * st + kv_scr[c, h]

    @pl.when(d == 0)
    def _():
        run(True)

    @pl.when(d == 1)
    def _():
        run(False)


def _hg_tri(n_rows):
    t = jnp.arange(n_rows)[:, None]
    s = jnp.arange(n_rows)[None, :]
    same = (t // HG_CHUNK) == (s // HG_CHUNK)
    return jnp.stack([same & (s <= t), same & (s >= t)]).astype(BF16)


def _hg(zh, hg_lb, n_ctx):
    b, tt, _ = zh.shape
    w = HG_HEADS * HG_N
    nblk = tt // ROW_TILE
    n_chunks = ROW_TILE // HG_CHUNK
    blk = _seq_block(n_ctx // ROW_TILE, nblk)
    return pl.pallas_call(
        _hg_kernel,
        grid=(b, 2, nblk),
        in_specs=[pl.BlockSpec((1, ROW_TILE, w), lambda bi, d, i: (bi, blk(d, i), 0)),
                  pl.BlockSpec((1, ROW_TILE, w), lambda bi, d, i: (bi, blk(d, i), 1 + d)),
                  pl.BlockSpec((1, ROW_TILE, w), lambda bi, d, i: (bi, blk(d, i), 3)),
                  pl.BlockSpec(hg_lb.shape, lambda bi, d, i: (0, 0)),
                  pl.BlockSpec((1, ROW_TILE, ROW_TILE), lambda bi, d, i: (d, 0, 0))],
        out_specs=pl.BlockSpec((1, 1, ROW_TILE, w), lambda bi, d, i: (d, bi, blk(d, i), 0)),
        out_shape=jax.ShapeDtypeStruct((2, b, tt, w), F32),
        scratch_shapes=[pltpu.VMEM((HG_HEADS, HG_N, HG_N), F32),
                        pltpu.VMEM((ROW_TILE, w), BF16),
                        pltpu.VMEM((n_chunks, HG_HEADS, HG_N, HG_N), F32),
                        pltpu.VMEM((SUBLANES, w), F32)],
        compiler_params=_params("arbitrary", "arbitrary", "arbitrary"),
        name="hg",
    )(zh, zh, zh, hg_lb, _hg_tri(ROW_TILE))


def _merge_kernel(yrw_ref, g_ref, o_ref, gz_ref, zg_ref, x_ref, g1_ref, hnw_ref,
                  pa_ref, pb_ref, wo_ref, out_ref):
    d = x_ref.shape[2]
    y_rw = (yrw_ref[0] * g_ref[0]).astype(BF16)
    o = o_ref[0, 0] + o_ref[1, 0]
    gz = gz_ref[0]
    hnw = hnw_ref[...]
    parts = []
    for h in range(HG_HEADS):
        cols = slice(h * HG_N, (h + 1) * HG_N)
        oh = o[:, cols]
        ms = jnp.mean(oh * oh, axis=-1, keepdims=True)
        parts.append(oh * lax.rsqrt(ms + RMS_EPS) * hnw[:, cols])
    y_hg = (jnp.concatenate(parts, axis=1) * (gz * _sigmoid(gz))).astype(BF16)
    zg = zg_ref[0]
    m = _sigmoid(zg[:, :d]) * _bdot(y_rw, pa_ref[...]) + _sigmoid(zg[:, d:]) * _bdot(y_hg, pb_ref[...])
    out_ref[0] = x_ref[0] + g1_ref[0] * _bdot(m.astype(BF16), wo_ref[...])


def _merge(y_rw, g, o_hg, zh, zg, x, modcat, hnw, pa, pb, wo, n_ctx):
    b, t, d = x.shape
    off = n_ctx // ROW_TILE
    lat = lambda bi, i: (bi, i, 0)
    cat = lambda bi, i: (bi, i + off, 0)
    const = lambda shape: pl.BlockSpec(shape, lambda bi, i: (0,) * len(shape))
    return pl.pallas_call(
        _merge_kernel,
        grid=(b, t // ROW_TILE),
        in_specs=[pl.BlockSpec((1, ROW_TILE, d), lat),
                  pl.BlockSpec((1, ROW_TILE, d), cat),
                  pl.BlockSpec((2, 1, ROW_TILE, d), lambda bi, i: (0, bi, i + off, 0)),
                  pl.BlockSpec((1, ROW_TILE, d), lambda bi, i: (bi, i + off, 4)),
                  pl.BlockSpec((1, ROW_TILE, 2 * d), cat),
                  pl.BlockSpec((1, ROW_TILE, d), lat),
                  pl.BlockSpec((1, 1, d), lambda bi, i: (2 * bi + 1, 0, 2)),
                  const(hnw.shape), const(pa.shape), const(pb.shape), const(wo.shape)],
        out_specs=pl.BlockSpec((1, ROW_TILE, d), lat),
        out_shape=jax.ShapeDtypeStruct((b, t, d), F32),
        compiler_params=_params("arbitrary", "arbitrary"),
        name="merge",
    )(y_rw, g, o_hg, zh, zg, x, modcat, hnw, pa, pb, wo)


def _mlp_kernel(x_ref, nw_ref, sh_ref, sc_ref, g2_ref, w1_ref, w2_ref, fw_ref, out_ref):
    x = x_ref[0]
    hb = _modulated_norm(x, nw_ref[...], sh_ref[0], sc_ref[0]).astype(BF16)
    u = jnp.maximum(_bdot(hb, w1_ref[...]), 0.0)
    y = x + g2_ref[0] * _bdot((u * u).astype(BF16), w2_ref[...])
    ms = jnp.mean(y * y, axis=-1, keepdims=True)
    out_ref[0] = y * lax.rsqrt(ms + RMS_EPS) * fw_ref[...]


def _mlp(x1, nw, modcat, w1, w2, fw):
    b, t, d = x1.shape
    mod_idx = lambda col: (lambda bi, i: (2 * bi + 1, 0, col))
    const = lambda shape, **kw: pl.BlockSpec(shape, lambda bi, i: (0,) * len(shape), **kw)
    return pl.pallas_call(
        _mlp_kernel,
        grid=(b, t // ROW_TILE),
        in_specs=[pl.BlockSpec((1, ROW_TILE, d), lambda bi, i: (bi, i, 0)),
                  const(nw.shape),
                  pl.BlockSpec((1, 1, d), mod_idx(3)),
                  pl.BlockSpec((1, 1, d), mod_idx(4)),
                  pl.BlockSpec((1, 1, d), mod_idx(5)),
                  const(w1.shape, pipeline_mode=pl.Buffered(1)),
                  const(w2.shape, pipeline_mode=pl.Buffered(1)),
                  const(fw.shape)],
        out_specs=pl.BlockSpec((1, ROW_TILE, d), lambda bi, i: (bi, i, 0)),
        out_shape=jax.ShapeDtypeStruct((b, t, d), F32),
        compiler_params=_params("arbitrary", "arbitrary"),
        name="mlp",
    )(x1, nw, modcat, modcat, modcat, w1, w2, fw)


def _to_scan(a):
    *lead, b, t, _ = a.shape
    n = len(lead)
    a = a.reshape(*lead, b, t, RW_HEADS, RW_N)
    a = jnp.transpose(a, (*range(n), n + 1, n + 3, n, n + 2))
    return a.reshape(*lead, t, RW_N, b * RW_HEADS)


def _from_scan(a, b):
    t = a.shape[0]
    a = a.reshape(t, RW_N, b, RW_HEADS)
    return jnp.transpose(a, (2, 0, 3, 1)).reshape(b, t, RW_HEADS * RW_N)


def _head_tile(p, b):
    return jnp.tile(p.reshape(RW_HEADS, RW_N).T, (1, b))


def kernel(x, c, ctx, c_ctx, norm1_w, norm2_w, w_mod, b_mod, w_in, rw_mu, rw_w0, rw_w_up, rw_a0, rw_a_up, rw_g_up, rw_k_k, rw_k_a, rw_r_k, rw_ln_w, rw_ln_b, hg_lb, hg_norm_w, p_a, p_b, w_out, w_fc1, w_fc2, final_norm_w):
    b, t, d = x.shape
    n_ctx = ctx.shape[1]
    assert w_mod.shape[0] == 1, "single-layer block"
    assert b * RW_HEADS == LANES and n_ctx == ROW_TILE and t % ROW_TILE == 0
    assert d == RW_HEADS * RW_N == HG_HEADS * HG_N
    n_r = rw_mu.shape[2]
    n_h = 5 * d
    n_g = 2 * d
    rank_w, rank_a = rw_w_up.shape[2], rw_a_up.shape[2]
    assert rank_w + rank_a == LANES and n_r == 3 * d + 2 * LANES

    c_rows = jnp.zeros((2 * SUBLANES, d), F32).at[:b].set(c).at[b].set(c_ctx)
    mod = _mod(c_rows, w_mod[0], b_mod)
    modcat = jnp.stack([jnp.broadcast_to(mod[b], (b, N_MOD * d)), mod[:b]], axis=1).reshape(2 * b, 1, N_MOD * d)

    xc = jnp.concatenate([ctx, x], axis=1)
    zr, zh, zg = _in_proj(xc, norm1_w, modcat, w_in[0].astype(BF16), n_r, n_h, n_g)

    wup = jnp.pad(rw_w_up[0], ((0, 0), (0, rank_a), (0, 0))).astype(BF16)
    aup = jnp.pad(rw_a_up[0], ((0, 0), (rank_w, 0), (0, 0))).astype(BF16)
    r, k_raw, v, wd, ad, g = _rw_prep(zr, rw_mu[0], rw_w0[0], wup, rw_a0[0], aup,
                                      rw_g_up[0].astype(BF16), d)

    r_t, k_t, v_t, w_t, a_t = (_to_scan(a) for a in (r, k_raw, v, wd, ad))
    kap = _head_tile(rw_k_a[0], b)
    y_t = _rw_scan(r_t, k_t, v_t, w_t, a_t, _head_tile(rw_k_k[0], b), kap, n_ctx // SCAN_TILE)
    y_rw_t = _rw_readout(y_t, r_t, k_t, v_t, a_t, kap, _head_tile(rw_r_k[0].reshape(-1), b),
                         _head_tile(rw_ln_w[0], b), _head_tile(rw_ln_b[0], b), n_ctx)
    y_rw = _from_scan(y_rw_t, b)

    o_hg = _hg(zh, hg_lb, n_ctx)

    x1 = _merge(y_rw, g, o_hg, zh, zg, x, modcat, hg_norm_w, p_a[0].astype(BF16), p_b[0].astype(BF16),
                w_out[0].astype(BF16), n_ctx)
    return _mlp(x1, norm2_w, modcat, w_fc1[0].astype(BF16), w_fc2[0].astype(BF16),
                final_norm_w.reshape(1, d))
```

```python
import functools

import jax
import jax.numpy as jnp
from jax import lax
from jax.experimental import pallas as pl
from jax.experimental.pallas import tpu as pltpu

F32 = jnp.float32
BF16 = jnp.bfloat16

GRID_W = 64
RW_HEADS = 16
RW_N = 64
HG_HEADS = 8
HG_N = 128
HG_CHUNK = 64
N_MOD = 6
RMS_EPS = 1e-6
RW_GN_EPS = 64e-5
L2_EPS = 1e-12

SUBLANES = 8
LANES = 128
ROW_TILE = 256
SCAN_TILE = 64
VMEM_LIMIT = 56 * 1024 * 1024


def _params(*sem):
    return pltpu.CompilerParams(dimension_semantics=sem, vmem_limit_bytes=VMEM_LIMIT)


def _bdot(a, b):
    return jnp.dot(a, b, preferred_element_type=F32)


def _sigmoid(x):
    return 1.0 / (1.0 + jnp.exp(-x))


def _mod_kernel(c_ref, w_ref, b_ref, o_ref):
    c = c_ref[...]
    act = c * _sigmoid(c)
    o_ref[...] = _bdot(act.astype(BF16), w_ref[...].astype(BF16)) + b_ref[...]


def _mod(c_rows, w_mod, b_mod):
    rows, d = c_rows.shape
    n = w_mod.shape[1]
    return pl.pallas_call(
        _mod_kernel,
        grid=(n // d,),
        in_specs=[pl.BlockSpec((rows, d), lambda j: (0, 0)),
                  pl.BlockSpec((d, d), lambda j: (0, j)),
                  pl.BlockSpec((1, d), lambda j: (0, j))],
        out_specs=pl.BlockSpec((rows, d), lambda j: (0, j)),
        out_shape=jax.ShapeDtypeStruct((rows, n), F32),
        compiler_params=_params("arbitrary"),
        name="mod",
    )(c_rows, w_mod, b_mod)


def _modulated_norm(x, nw, sh, sc):
    ms = jnp.mean(x * x, axis=-1, keepdims=True)
    return (x * lax.rsqrt(ms + RMS_EPS) * nw) * (1.0 + sc) + sh


def _in_proj_kernel(ctx_ref, x_ref, nw_ref, sh_ref, sc_ref, w_ref, zr_ref, zh_ref, zg_ref):
    tokens = jnp.where(pl.program_id(1) == 0, ctx_ref[0], x_ref[0])
    hb = _modulated_norm(tokens, nw_ref[...], sh_ref[0], sc_ref[0]).astype(BF16)
    n_r, n_h = zr_ref.shape[2], zh_ref.shape[2]
    zr_ref[0] = _bdot(hb, w_ref[:, :n_r])
    zh_ref[0] = _bdot(hb, w_ref[:, n_r:n_r + n_h])
    zg_ref[0] = _bdot(hb, w_ref[:, n_r + n_h:])


def _in_proj(ctx, x, nw, modcat, w_bf, n_r, n_h, n_g):
    b, t, d = x.shape
    tt = t + ctx.shape[1]
    nblk = tt // ROW_TILE
    mod_idx = lambda col: (lambda bi, i: (2 * bi + jnp.minimum(i, 1), 0, col))
    return pl.pallas_call(
        _in_proj_kernel,
        grid=(b, nblk),
        in_specs=[pl.BlockSpec((1, ROW_TILE, d), lambda bi, i: (bi, 0, 0)),
                  pl.BlockSpec((1, ROW_TILE, d), lambda bi, i: (bi, jnp.maximum(i - 1, 0), 0)),
                  pl.BlockSpec((1, d), lambda bi, i: (0, 0)),
                  pl.BlockSpec((1, 1, d), mod_idx(0)),
                  pl.BlockSpec((1, 1, d), mod_idx(1)),
                  pl.BlockSpec(w_bf.shape, lambda bi, i: (0, 0), pipeline_mode=pl.Buffered(1))],
        out_specs=[pl.BlockSpec((1, ROW_TILE, n), lambda bi, i: (bi, i, 0)) for n in (n_r, n_h, n_g)],
        out_shape=[jax.ShapeDtypeStruct((b, tt, n), F32) for n in (n_r, n_h, n_g)],
        compiler_params=_params("arbitrary", "arbitrary"),
        name="in_proj",
    )(ctx, x, nw, modcat, modcat, w_bf)


def _rw_prep_kernel(z_ref, zp_ref, zn_ref, mu_ref, w0_ref, wup_ref, a0_ref, aup_ref, gup_ref,
                    r_ref, k_ref, v_ref, wd_ref, ad_ref, g_ref):
    i = pl.program_id(1)
    nblk = pl.num_programs(1)
    tt = z_ref.shape[1]
    d = r_ref.shape[1]
    is_lat = i > 0
    row = lax.broadcasted_iota(jnp.int32, (tt, 1), 0)
    col = row % GRID_W
    lmask = jnp.where(is_lat, col, row) == 0
    rmask = jnp.where(is_lat, col, row - (tt - GRID_W)) == GRID_W - 1
    latf = is_lat.astype(F32)
    up_ok = (i > 1).astype(F32)
    dn_ok = (i < nblk - 1).astype(F32)

    def shifted(c0, c1):
        z = z_ref[0, :, c0:c1]
        mu = mu_ref[:, c0:c1]
        left = jnp.where(lmask, 0.0, pltpu.roll(z, 1, 0))
        right = jnp.where(rmask, 0.0, pltpu.roll(z, tt - 1, 0))
        up = jnp.concatenate([zp_ref[0, :, c0:c1] * up_ok, z[:tt - GRID_W]], axis=0)
        down = jnp.concatenate([z[GRID_W:], zn_ref[0, :, c0:c1] * dn_ok], axis=0)
        out = z + mu[0:1] * (left - z) + mu[1:2] * (right - z)
        return out + latf * (mu[2:3] * (up - z) + mu[3:4] * (down - z))

    r_ref[...] = shifted(0, d)
    k_ref[...] = shifted(d, 2 * d)
    v_ref[...] = shifted(2 * d, 3 * d)
    rest = shifted(3 * d, z_ref.shape[2])
    xwa = rest[:, :LANES]
    xw_t = jnp.tanh(xwa).astype(BF16)
    xa_b = xwa.astype(BF16)
    for dr in range(2):
        u = -(w0_ref[dr:dr + 1, :] + _bdot(xw_t, wup_ref[dr]))
        softplus = jnp.maximum(u, 0.0) + jnp.log(1.0 + jnp.exp(-jnp.abs(u)))
        wd_ref[dr] = jnp.exp(-jnp.exp(-softplus - 0.5))
        ad_ref[dr] = _sigmoid(a0_ref[dr:dr + 1, :] + _bdot(xa_b, aup_ref[dr]))
    g_ref[...] = _bdot(_sigmoid(rest[:, LANES:]).astype(BF16), gup_ref[...])


def _rw_prep(zr, mu, w0, wup, a0, aup, gup, d):
    b, tt, nr = zr.shape
    nblk = tt // ROW_TILE
    per = ROW_TILE // GRID_W
    last = tt // GRID_W - 1
    const = lambda shape: pl.BlockSpec(shape, lambda bi, i: (0,) * len(shape))
    row_spec = pl.BlockSpec((ROW_TILE, d), lambda bi, i: (i, bi))
    dir_spec = pl.BlockSpec((2, ROW_TILE, d), lambda bi, i: (0, i, bi))
    row_shape = jax.ShapeDtypeStruct((tt, b * d), F32)
    dir_shape = jax.ShapeDtypeStruct((2, tt, b * d), F32)
    return pl.pallas_call(
        _rw_prep_kernel,
        grid=(b, nblk),
        in_specs=[pl.BlockSpec((1, ROW_TILE, nr), lambda bi, i: (bi, i, 0)),
                  pl.BlockSpec((1, GRID_W, nr), lambda bi, i: (bi, jnp.maximum(i * per - 1, 0), 0)),
                  pl.BlockSpec((1, GRID_W, nr), lambda bi, i: (bi, jnp.minimum(i * per + per, last), 0)),
                  const(mu.shape), const(w0.shape), const(wup.shape), const(a0.shape),
                  const(aup.shape), const(gup.shape)],
        out_specs=[row_spec, row_spec, row_spec, dir_spec, dir_spec, row_spec],
        out_shape=[row_shape, row_shape, row_shape, dir_shape, dir_shape, row_shape],
        compiler_params=_params("arbitrary", "arbitrary"),
        name="rw_prep",
    )(zr, zr, zr, mu, w0, wup, a0, aup, gup)


K_UNROLL = 16
ACC_WAYS = 1


def _tree_sum(xs):
    while len(xs) > 1:
        xs = [xs[i] + xs[i + 1] for i in range(0, len(xs) - 1, 2)] + ([xs[-1]] if len(xs) % 2 else [])
    return xs[0]


def _seq_block(n_ctx_blocks, n_blocks):
    def blk(d, i):
        bwd = jnp.where(i < n_ctx_blocks, n_ctx_blocks - 1 - i, n_blocks - 1 + n_ctx_blocks - i)
        return jnp.where(d == 0, i, bwd)
    return blk


def _rw_scan_kernel(r_ref, k_ref, v_ref, w_ref, a_ref, kkp_ref, kap_ref, y_ref,
                    s_ref, kk_s, b_s, k_s, sa_s):
    d = pl.program_id(0)
    i = pl.program_id(1)
    tb = r_ref.shape[0]
    nv = RW_N // SUBLANES

    @pl.when(i == 0)
    def _():
        s_ref[...] = jnp.zeros_like(s_ref)

    def bcast(ref, *idx):
        k = idx[-1]
        row = ref[(*idx[:-1], pl.ds(k, 1), slice(None))]
        return jnp.broadcast_to(row, (SUBLANES, LANES))

    def time_index(s):
        s = jnp.minimum(s, tb - 1)
        return jnp.where(d == 0, s, tb - 1 - s)

    def prepare_kk(s, slot):
        kkr = k_ref[time_index(s)] * kkp_ref[...]
        nrm = jnp.sqrt(jnp.sum(kkr * kkr, axis=0, keepdims=True))
        kk_s[slot] = kkr / jnp.maximum(nrm, L2_EPS)

    def prepare_bk(s, slot):
        t = time_index(s)
        a = a_ref[0, t]
        b_s[slot] = kk_s[slot] * a
        k_s[slot] = k_ref[t] * (1.0 + (a - 1.0) * kap_ref[...])

    def state_dot(slot):
        acc = [[None] * nv for _ in range(ACC_WAYS)]
        for k in range(RW_N):
            kkb = bcast(kk_s, slot, k)
            part = acc[k % ACC_WAYS]
            for j in range(nv):
                p = s_ref[k, SUBLANES * j:SUBLANES * (j + 1), :] * kkb
                part[j] = p if part[j] is None else part[j] + p
        sa_s[...] = -jnp.concatenate([_tree_sum([acc[w][j] for w in range(ACC_WAYS)]) for j in range(nv)], axis=0)

    def sweep(s, slot):
        t = time_index(s)
        zero = jnp.zeros((SUBLANES, LANES), F32)

        def key_block(kblk, carry):
            yacc = [list(x) for x in carry[0]]
            acc = [list(x) for x in carry[1]]
            for kk in range(K_UNROLL):
                k = kblk * K_UNROLL + kk
                ypart, apart = yacc[kk % ACC_WAYS], acc[kk % ACC_WAYS]
                wb = bcast(w_ref, 0, t, k)
                bb = bcast(b_s, slot, k)
                kb = bcast(k_s, slot, k)
                rb = bcast(r_ref, t, k)
                kkn = bcast(kk_s, 1 - slot, k)
                for j in range(nv):
                    rows = slice(SUBLANES * j, SUBLANES * (j + 1))
                    sn = s_ref[k, rows, :] * wb + (sa_s[rows, :] * bb + v_ref[t, rows, :] * kb)
                    s_ref[k, rows, :] = sn
                    ypart[j] = ypart[j] + sn * rb
                    apart[j] = apart[j] + sn * kkn
            return yacc, acc

        init = [[zero] * nv for _ in range(ACC_WAYS)]
        n_kblk = RW_N // K_UNROLL
        carry = lax.fori_loop(0, n_kblk - 1, key_block, (init, init))
        yacc, acc = key_block(n_kblk - 1, carry)
        y_ref[0, t] = jnp.concatenate([_tree_sum([yacc[w][j] for w in range(ACC_WAYS)]) for j in range(nv)], axis=0)
        sa_s[...] = -jnp.concatenate([_tree_sum([acc[w][j] for w in range(ACC_WAYS)]) for j in range(nv)], axis=0)

    for s0 in (0, 1):
        prepare_kk(s0, s0)
        prepare_bk(s0, s0)
    state_dot(0)

    def pair(p, carry):
        s = 2 * p
        sweep(s, 0)
        prepare_kk(s + 2, 0)
        prepare_bk(s + 2, 0)
        sweep(s + 1, 1)
        prepare_kk(s + 3, 1)
        prepare_bk(s + 3, 1)
        return carry

    lax.fori_loop(0, tb // 2, pair, 0)


def _rw_scan(r_t, k_t, v_t, w_t, a_t, kkp, kap, n_ctx_blocks):
    tt = r_t.shape[0]
    nblk = tt // SCAN_TILE
    blk = _seq_block(n_ctx_blocks, nblk)
    tile = (SCAN_TILE, RW_N, LANES)
    shared = pl.BlockSpec(tile, lambda d, i: (blk(d, i), 0, 0))
    per_dir = pl.BlockSpec((1,) + tile, lambda d, i: (d, blk(d, i), 0, 0))
    const = pl.BlockSpec((RW_N, LANES), lambda d, i: (0, 0))
    return pl.pallas_call(
        _rw_scan_kernel,
        grid=(2, nblk),
        in_specs=[shared, shared, shared, per_dir, per_dir, const, const],
        out_specs=per_dir,
        out_shape=jax.ShapeDtypeStruct((2, tt, RW_N, LANES), F32),
        scratch_shapes=[pltpu.VMEM((RW_N, RW_N, LANES), F32),
                        pltpu.VMEM((2, RW_N, LANES), F32),
                        pltpu.VMEM((2, RW_N, LANES), F32),
                        pltpu.VMEM((2, RW_N, LANES), F32),
                        pltpu.VMEM((RW_N, LANES), F32)],
        compiler_params=_params("arbitrary", "arbitrary"),
        name="rw_scan",
    )(r_t, k_t, v_t, w_t, a_t, kkp, kap)


def _rw_readout_kernel(y_ref, r_ref, k_ref, v_ref, a_ref, kap_ref, rkp_ref, lnw_ref, lnb_ref, o_ref):
    y = y_ref[0] + y_ref[1]
    mean = jnp.mean(y, axis=1, keepdims=True)
    yc = y - mean
    var = jnp.mean(yc * yc, axis=1, keepdims=True)
    yn = yc * lax.rsqrt(var + RW_GN_EPS) * lnw_ref[...] + lnb_ref[...]
    kraw = k_ref[...]
    kap = kap_ref[...]
    k_sum = kraw * (1.0 + (a_ref[0] - 1.0) * kap) + kraw * (1.0 + (a_ref[1] - 1.0) * kap)
    bonus = jnp.sum(r_ref[...] * k_sum * rkp_ref[...], axis=1, keepdims=True) * v_ref[...]
    o_ref[...] = yn + bonus


def _rw_readout(y_t, r_t, k_t, v_t, a_t, kap, rkp, lnw, lnb, n_ctx):
    tt = r_t.shape[0]
    tb = 32
    off = n_ctx // tb
    tile = (tb, RW_N, LANES)
    shared = pl.BlockSpec(tile, lambda i: (i + off, 0, 0))
    both = pl.BlockSpec((2,) + tile, lambda i: (0, i + off, 0, 0))
    const = pl.BlockSpec((RW_N, LANES), lambda i: (0, 0))
    return pl.pallas_call(
        _rw_readout_kernel,
        grid=((tt - n_ctx) // tb,),
        in_specs=[both, shared, shared, shared, both, const, const, const, const],
        out_specs=pl.BlockSpec(tile, lambda i: (i, 0, 0)),
        out_shape=jax.ShapeDtypeStruct((tt - n_ctx, RW_N, LANES), F32),
        compiler_params=_params("arbitrary"),
        name="rw_readout",
    )(y_t, r_t, k_t, v_t, a_t, kap, rkp, lnw, lnb)


def _hg_kernel(q_ref, f_ref, i_ref, lbp_ref, tri_ref, o_ref, st_ref, qd_scr, kv_scr, dec_scr):
    d = pl.program_id(1)
    i = pl.program_id(2)
    n_chunks = q_ref.shape[1] // HG_CHUNK

    @pl.when(i == 0)
    def _():
        st_ref[...] = jnp.zeros_like(st_ref)

    lbp = lbp_ref[...]
    e = jnp.exp(lbp - jnp.max(lbp, axis=0, keepdims=True))
    lb = e[0:1] / jnp.sum(e, axis=0, keepdims=True)
    t_idx = lax.broadcasted_iota(jnp.int32, (HG_CHUNK, HG_CHUNK), 0)
    s_idx = lax.broadcasted_iota(jnp.int32, (HG_CHUNK, HG_CHUNK), 1)
    nt = (((1,), (1,)), ((), ()))
    tn = (((0,), (0,)), ((), ()))

    def run(fwd):
        mask = (s_idx <= t_idx) if fwd else (s_idx >= t_idx)
        mid_row = HG_CHUNK // 2 - 1 if fwd else HG_CHUNK // 2
        last_row = HG_CHUNK - 1 if fwd else 0
        order = list(range(n_chunks)) if fwd else list(reversed(range(n_chunks)))

        f = lb + (1.0 - lb) * _sigmoid(f_ref[0])
        lf = jnp.log(f)
        hi = lf.astype(BF16)
        lo = (lf - hi.astype(F32)).astype(BF16)
        tri = tri_ref[0]
        b = _bdot(tri, hi) + _bdot(tri, lo)
        for c in order:
            rows = slice(c * HG_CHUNK, (c + 1) * HG_CHUNK)
            bc = b[rows]
            b_mid = bc[mid_row:mid_row + 1]
            b_last = bc[last_row:last_row + 1]
            q_in = q_ref[0, rows, :] * jnp.exp(bc - b_mid)
            k_in = (1.0 - f[rows]) * jnp.exp(b_mid - bc)
            qd_scr[rows, :] = (q_in * jnp.exp(b_mid)).astype(BF16)
            k_dec = (k_in * jnp.exp(b_last - b_mid)).astype(BF16)
            dec_scr[c:c + 1, :] = jnp.exp(b_last)
            q_in = q_in.astype(BF16)
            k_in = k_in.astype(BF16)
            vb = i_ref[0, rows, :].astype(BF16)
            heads = [slice(h * HG_N, (h + 1) * HG_N) for h in range(HG_HEADS)]
            scores = [lax.dot_general(q_in[:, cols], k_in[:, cols], nt, preferred_element_type=F32)
                      for cols in heads]
            for h, cols in enumerate(heads):
                kv_scr[c, h] = lax.dot_general(vb[:, cols], k_dec[:, cols], tn, preferred_element_type=F32)
            for h, cols in enumerate(heads):
                o_ref[0, 0, rows, cols] = _bdot(jnp.where(mask, scores[h], 0.0).astype(BF16), vb[:, cols])
        for c in order:
            rows = slice(c * HG_CHUNK, (c + 1) * HG_CHUNK)
            for h in range(HG_HEADS):
                cols = slice(h * HG_N, (h + 1) * HG_N)
                st = st_ref[h]
                o_ref[0, 0, rows, cols] += lax.dot_general(qd_scr[rows, cols], st.astype(BF16), nt,
                                                           preferred_element_type=F32)
                st_ref[h] = dec_scr[c:c + 1, cols] * st + kv_scr[c, h]

    @pl.when(d == 0)
    def _():
        run(True)

    @pl.when(d == 1)
    def _():
        run(False)


def _hg_tri(n_rows):
    t = jnp.arange(n_rows)[:, None]
    s = jnp.arange(n_rows)[None, :]
    same = (t // HG_CHUNK) == (s // HG_CHUNK)
    return jnp.stack([same & (s <= t), same & (s >= t)]).astype(BF16)


def _hg(zh, hg_lb, n_ctx):
    b, tt, _ = zh.shape
    w = HG_HEADS * HG_N
    nblk = tt // ROW_TILE
    n_chunks = ROW_TILE // HG_CHUNK
    blk = _seq_block(n_ctx // ROW_TILE, nblk)
    return pl.pallas_call(
        _hg_kernel,
        grid=(b, 2, nblk),
        in_specs=[pl.BlockSpec((1, ROW_TILE, w), lambda bi, d, i: (bi, blk(d, i), 0)),
                  pl.BlockSpec((1, ROW_TILE, w), lambda bi, d, i: (bi, blk(d, i), 1 + d)),
                  pl.BlockSpec((1, ROW_TILE, w), lambda bi, d, i: (bi, blk(d, i), 3)),
                  pl.BlockSpec(hg_lb.shape, lambda bi, d, i: (0, 0)),
                  pl.BlockSpec((1, ROW_TILE, ROW_TILE), lambda bi, d, i: (d, 0, 0))],
        out_specs=pl.BlockSpec((1, 1, ROW_TILE, w), lambda bi, d, i: (d, bi, blk(d, i), 0)),
        out_shape=jax.ShapeDtypeStruct((2, b, tt, w), F32),
        scratch_shapes=[pltpu.VMEM((HG_HEADS, HG_N, HG_N), F32),
                        pltpu.VMEM((ROW_TILE, w), BF16),
                        pltpu.VMEM((n_chunks, HG_HEADS, HG_N, HG_N), F32),
                        pltpu.VMEM((SUBLANES, w), F32)],
        compiler_params=_params("arbitrary", "arbitrary", "arbitrary"),
        name="hg",
    )(zh, zh, zh, hg_lb, _hg_tri(ROW_TILE))


def _merge_kernel(yrw_ref, g_ref, o_ref, gz_ref, zg_ref, x_ref, g1_ref, hnw_ref,
                  pa_ref, pb_ref, wo_ref, out_ref):
    d = x_ref.shape[2]
    y_rw = (yrw_ref[...] * g_ref[...]).astype(BF16)
    o = o_ref[0, 0] + o_ref[1, 0]
    gz = gz_ref[0]
    hnw = hnw_ref[...]
    parts = []
    for h in range(HG_HEADS):
        cols = slice(h * HG_N, (h + 1) * HG_N)
        oh = o[:, cols]
        ms = jnp.mean(oh * oh, axis=-1, keepdims=True)
        parts.append(oh * lax.rsqrt(ms + RMS_EPS) * hnw[:, cols])
    y_hg = (jnp.concatenate(parts, axis=1) * (gz * _sigmoid(gz))).astype(BF16)
    zg = zg_ref[0]
    m = _sigmoid(zg[:, :d]) * _bdot(y_rw, pa_ref[...]) + _sigmoid(zg[:, d:]) * _bdot(y_hg, pb_ref[...])
    out_ref[0] = x_ref[0] + g1_ref[0] * _bdot(m.astype(BF16), wo_ref[...])


def _merge(y_rw, g, o_hg, zh, zg, x, modcat, hnw, pa, pb, wo, n_ctx):
    b, t, d = x.shape
    off = n_ctx // ROW_TILE
    lat = lambda bi, i: (bi, i, 0)
    cat = lambda bi, i: (bi, i + off, 0)
    const = lambda shape: pl.BlockSpec(shape, lambda bi, i: (0,) * len(shape))
    return pl.pallas_call(
        _merge_kernel,
        grid=(b, t // ROW_TILE),
        in_specs=[pl.BlockSpec((ROW_TILE, d), lambda bi, i: (i, bi)),
                  pl.BlockSpec((ROW_TILE, d), lambda bi, i: (i + off, bi)),
                  pl.BlockSpec((2, 1, ROW_TILE, d), lambda bi, i: (0, bi, i + off, 0)),
                  pl.BlockSpec((1, ROW_TILE, d), lambda bi, i: (bi, i + off, 4)),
                  pl.BlockSpec((1, ROW_TILE, 2 * d), cat),
                  pl.BlockSpec((1, ROW_TILE, d), lat),
                  pl.BlockSpec((1, 1, d), lambda bi, i: (2 * bi + 1, 0, 2)),
                  const(hnw.shape), const(pa.shape), const(pb.shape), const(wo.shape)],
        out_specs=pl.BlockSpec((1, ROW_TILE, d), lat),
        out_shape=jax.ShapeDtypeStruct((b, t, d), F32),
        compiler_params=_params("arbitrary", "arbitrary"),
        name="merge",
    )(y_rw, g, o_hg, zh, zg, x, modcat, hnw, pa, pb, wo)


def _mlp_kernel(x_ref, nw_ref, sh_ref, sc_ref, g2_ref, w1_ref, w2_ref, fw_ref, out_ref):
    x = x_ref[0]
    hb = _modulated_norm(x, nw_ref[...], sh_ref[0], sc_ref[0]).astype(BF16)
    u = jnp.maximum(_bdot(hb, w1_ref[...]), 0.0)
    y = x + g2_ref[0] * _bdot((u * u).astype(BF16), w2_ref[...])
    ms = jnp.mean(y * y, axis=-1, keepdims=True)
    out_ref[0] = y * lax.rsqrt(ms + RMS_EPS) * fw_ref[...]


def _mlp(x1, nw, modcat, w1, w2, fw):
    b, t, d = x1.shape
    mod_idx = lambda col: (lambda bi, i: (2 * bi + 1, 0, col))
    const = lambda shape, **kw: pl.BlockSpec(shape, lambda bi, i: (0,) * len(shape), **kw)
    return pl.pallas_call(
        _mlp_kernel,
        grid=(b, t // ROW_TILE),
        in_specs=[pl.BlockSpec((1, ROW_TILE, d), lambda bi, i: (bi, i, 0)),
                  const(nw.shape),
                  pl.BlockSpec((1, 1, d), mod_idx(3)),
                  pl.BlockSpec((1, 1, d), mod_idx(4)),
                  pl.BlockSpec((1, 1, d), mod_idx(5)),
                  const(w1.shape, pipeline_mode=pl.Buffered(1)),
                  const(w2.shape, pipeline_mode=pl.Buffered(1)),
                  const(fw.shape)],
        out_specs=pl.BlockSpec((1, ROW_TILE, d), lambda bi, i: (bi, i, 0)),
        out_shape=jax.ShapeDtypeStruct((b, t, d), F32),
        compiler_params=_params("arbitrary", "arbitrary"),
        name="mlp",
    )(x1, nw, modcat, modcat, modcat, w1, w2, fw)


def _to_scan(a):
    *lead, t, w = a.shape
    return jnp.swapaxes(a.reshape(*lead, t, w // RW_N, RW_N), -1, -2)


def _from_scan(a):
    t, n, bh = a.shape
    return jnp.swapaxes(a, -1, -2).reshape(t, bh * n)


def _head_tile(p, b):
    return jnp.tile(p.reshape(RW_HEADS, RW_N).T, (1, b))


def kernel(x, c, ctx, c_ctx, norm1_w, norm2_w, w_mod, b_mod, w_in, rw_mu, rw_w0, rw_w_up, rw_a0, rw_a_up, rw_g_up, rw_k_k, rw_k_a, rw_r_k, rw_ln_w, rw_ln_b, hg_lb, hg_norm_w, p_a, p_b, w_out, w_fc1, w_fc2, final_norm_w):
    b, t, d = x.shape
    n_ctx = ctx.shape[1]
    assert w_mod.shape[0] == 1, "single-layer block"
    assert b * RW_HEADS == LANES and n_ctx == ROW_TILE and t % ROW_TILE == 0
    assert d == RW_HEADS * RW_N == HG_HEADS * HG_N
    n_r = rw_mu.shape[2]
    n_h = 5 * d
    n_g = 2 * d
    rank_w, rank_a = rw_w_up.shape[2], rw_a_up.shape[2]
    assert rank_w + rank_a == LANES and n_r == 3 * d + 2 * LANES

    c_rows = jnp.zeros((2 * SUBLANES, d), F32).at[:b].set(c).at[b].set(c_ctx)
    mod = _mod(c_rows, w_mod[0], b_mod)
    modcat = jnp.stack([jnp.broadcast_to(mod[b], (b, N_MOD * d)), mod[:b]], axis=1).reshape(2 * b, 1, N_MOD * d)

    zr, zh, zg = _in_proj(ctx, x, norm1_w, modcat, w_in[0].astype(BF16), n_r, n_h, n_g)

    wup = jnp.pad(rw_w_up[0], ((0, 0), (0, rank_a), (0, 0))).astype(BF16)
    aup = jnp.pad(rw_a_up[0], ((0, 0), (rank_w, 0), (0, 0))).astype(BF16)
    r, k_raw, v, wd, ad, g = _rw_prep(zr, rw_mu[0], rw_w0[0], wup, rw_a0[0], aup,
                                      rw_g_up[0].astype(BF16), d)

    r_t, k_t, v_t, w_t, a_t = (_to_scan(a) for a in (r, k_raw, v, wd, ad))
    kap = _head_tile(rw_k_a[0], b)
    y_t = _rw_scan(r_t, k_t, v_t, w_t, a_t, _head_tile(rw_k_k[0], b), kap, n_ctx // SCAN_TILE)
    y_rw_t = _rw_readout(y_t, r_t, k_t, v_t, a_t, kap, _head_tile(rw_r_k[0].reshape(-1), b),
                         _head_tile(rw_ln_w[0], b), _head_tile(rw_ln_b[0], b), n_ctx)
    y_rw = _from_scan(y_rw_t)

    o_hg = _hg(zh, hg_lb, n_ctx)

    x1 = _merge(y_rw, g, o_hg, zh, zg, x, modcat, hg_norm_w, p_a[0].astype(BF16), p_b[0].astype(BF16),
                w_out[0].astype(BF16), n_ctx)
    return _mlp(x1, norm2_w, modcat, w_fc1[0].astype(BF16), w_fc2[0].astype(BF16),
                final_norm_w.reshape(1, d))
```

```python
import functools

import jax
import jax.numpy as jnp
from jax import lax
from jax.experimental import pallas as pl
from jax.experimental.pallas import tpu as pltpu

F32 = jnp.float32
BF16 = jnp.bfloat16

GRID_W = 64
RW_HEADS = 16
RW_N = 64
HG_HEADS = 8
HG_N = 128
HG_CHUNK = 64
N_MOD = 6
RMS_EPS = 1e-6
RW_GN_EPS = 64e-5
L2_EPS = 1e-12

SUBLANES = 8
LANES = 128
ROW_TILE = 256
SCAN_TILE = 64
VMEM_LIMIT = 56 * 1024 * 1024


def _params(*sem):
    return pltpu.CompilerParams(dimension_semantics=sem, vmem_limit_bytes=VMEM_LIMIT)


def _bdot(a, b):
    return jnp.dot(a, b, preferred_element_type=F32)


def _sigmoid(x):
    return 1.0 / (1.0 + jnp.exp(-x))


def _mod_kernel(c_ref, w_ref, b_ref, o_ref):
    c = c_ref[...]
    act = c * _sigmoid(c)
    o_ref[...] = _bdot(act.astype(BF16), w_ref[...].astype(BF16)) + b_ref[...]


def _mod(c_rows, w_mod, b_mod):
    rows, d = c_rows.shape
    n = w_mod.shape[1]
    return pl.pallas_call(
        _mod_kernel,
        grid=(n // d,),
        in_specs=[pl.BlockSpec((rows, d), lambda j: (0, 0)),
                  pl.BlockSpec((d, d), lambda j: (0, j)),
                  pl.BlockSpec((1, d), lambda j: (0, j))],
        out_specs=pl.BlockSpec((rows, d), lambda j: (0, j)),
        out_shape=jax.ShapeDtypeStruct((rows, n), F32),
        compiler_params=_params("arbitrary"),
        name="mod",
    )(c_rows, w_mod, b_mod)


def _modulated_norm(x, nw, sh, sc):
    ms = jnp.mean(x * x, axis=-1, keepdims=True)
    return (x * lax.rsqrt(ms + RMS_EPS) * nw) * (1.0 + sc) + sh


def _in_proj_kernel(ctx_ref, x_ref, nw_ref, sh_ref, sc_ref, w_ref, zr_ref, zh_ref, zg_ref):
    tokens = jnp.where(pl.program_id(1) == 0, ctx_ref[0], x_ref[0])
    hb = _modulated_norm(tokens, nw_ref[...], sh_ref[0], sc_ref[0]).astype(BF16)
    n_r, n_h = zr_ref.shape[2], zh_ref.shape[2]
    zr_ref[0] = _bdot(hb, w_ref[:, :n_r])
    zh_ref[0] = _bdot(hb, w_ref[:, n_r:n_r + n_h])
    zg_ref[0] = _bdot(hb, w_ref[:, n_r + n_h:])


def _in_proj(ctx, x, nw, modcat, w_bf, n_r, n_h, n_g):
    b, t, d = x.shape
    tt = t + ctx.shape[1]
    nblk = tt // ROW_TILE
    mod_idx = lambda col: (lambda bi, i: (2 * bi + jnp.minimum(i, 1), 0, col))
    return pl.pallas_call(
        _in_proj_kernel,
        grid=(b, nblk),
        in_specs=[pl.BlockSpec((1, ROW_TILE, d), lambda bi, i: (bi, 0, 0)),
                  pl.BlockSpec((1, ROW_TILE, d), lambda bi, i: (bi, jnp.maximum(i - 1, 0), 0)),
                  pl.BlockSpec((1, d), lambda bi, i: (0, 0)),
                  pl.BlockSpec((1, 1, d), mod_idx(0)),
                  pl.BlockSpec((1, 1, d), mod_idx(1)),
                  pl.BlockSpec(w_bf.shape, lambda bi, i: (0, 0), pipeline_mode=pl.Buffered(1))],
        out_specs=[pl.BlockSpec((1, ROW_TILE, n), lambda bi, i: (bi, i, 0)) for n in (n_r, n_h, n_g)],
        out_shape=[jax.ShapeDtypeStruct((b, tt, n), F32) for n in (n_r, n_h, n_g)],
        compiler_params=_params("arbitrary", "arbitrary"),
        name="in_proj",
    )(ctx, x, nw, modcat, modcat, w_bf)


def _rw_prep_kernel(z_ref, zp_ref, zn_ref, mu_ref, w0_ref, wup_ref, a0_ref, aup_ref, gup_ref,
                    r_ref, k_ref, v_ref, wd_ref, ad_ref, g_ref):
    i = pl.program_id(1)
    nblk = pl.num_programs(1)
    tt = z_ref.shape[1]
    d = r_ref.shape[0]
    is_lat = i > 0
    row = lax.broadcasted_iota(jnp.int32, (tt, 1), 0)
    col = row % GRID_W
    lmask = jnp.where(is_lat, col, row) == 0
    rmask = jnp.where(is_lat, col, row - (tt - GRID_W)) == GRID_W - 1
    latf = is_lat.astype(F32)
    up_ok = (i > 1).astype(F32)
    dn_ok = (i < nblk - 1).astype(F32)

    def shifted(c0, c1):
        z = z_ref[0, :, c0:c1]
        mu = mu_ref[:, c0:c1]
        left = jnp.where(lmask, 0.0, pltpu.roll(z, 1, 0))
        right = jnp.where(rmask, 0.0, pltpu.roll(z, tt - 1, 0))
        up = jnp.concatenate([zp_ref[0, :, c0:c1] * up_ok, z[:tt - GRID_W]], axis=0)
        down = jnp.concatenate([z[GRID_W:], zn_ref[0, :, c0:c1] * dn_ok], axis=0)
        out = z + mu[0:1] * (left - z) + mu[1:2] * (right - z)
        return out + latf * (mu[2:3] * (up - z) + mu[3:4] * (down - z))

    r_ref[...] = shifted(0, d).T
    k_ref[...] = shifted(d, 2 * d).T
    v_ref[...] = shifted(2 * d, 3 * d).T
    rest = shifted(3 * d, z_ref.shape[2])
    xwa = rest[:, :LANES]
    xw_t = jnp.tanh(xwa).astype(BF16)
    xa_b = xwa.astype(BF16)
    for dr in range(2):
        u = -(w0_ref[dr:dr + 1, :] + _bdot(xw_t, wup_ref[dr]))
        softplus = jnp.maximum(u, 0.0) + jnp.log(1.0 + jnp.exp(-jnp.abs(u)))
        wd_ref[dr] = jnp.exp(-jnp.exp(-softplus - 0.5)).T
        ad_ref[dr] = _sigmoid(a0_ref[dr:dr + 1, :] + _bdot(xa_b, aup_ref[dr])).T
    g_ref[...] = _bdot(_sigmoid(rest[:, LANES:]).astype(BF16), gup_ref[...])


def _rw_prep(zr, mu, w0, wup, a0, aup, gup, d):
    b, tt, nr = zr.shape
    nblk = tt // ROW_TILE
    per = ROW_TILE // GRID_W
    last = tt // GRID_W - 1
    const = lambda shape: pl.BlockSpec(shape, lambda bi, i: (0,) * len(shape))
    row_spec = pl.BlockSpec((d, ROW_TILE), lambda bi, i: (bi, i))
    dir_spec = pl.BlockSpec((2, d, ROW_TILE), lambda bi, i: (0, bi, i))
    gate_spec = pl.BlockSpec((ROW_TILE, d), lambda bi, i: (i, bi))
    row_shape = jax.ShapeDtypeStruct((b * d, tt), F32)
    dir_shape = jax.ShapeDtypeStruct((2, b * d, tt), F32)
    gate_shape = jax.ShapeDtypeStruct((tt, b * d), F32)
    return pl.pallas_call(
        _rw_prep_kernel,
        grid=(b, nblk),
        in_specs=[pl.BlockSpec((1, ROW_TILE, nr), lambda bi, i: (bi, i, 0)),
                  pl.BlockSpec((1, GRID_W, nr), lambda bi, i: (bi, jnp.maximum(i * per - 1, 0), 0)),
                  pl.BlockSpec((1, GRID_W, nr), lambda bi, i: (bi, jnp.minimum(i * per + per, last), 0)),
                  const(mu.shape), const(w0.shape), const(wup.shape), const(a0.shape),
                  const(aup.shape), const(gup.shape)],
        out_specs=[row_spec, row_spec, row_spec, dir_spec, dir_spec, gate_spec],
        out_shape=[row_shape, row_shape, row_shape, dir_shape, dir_shape, gate_shape],
        compiler_params=_params("arbitrary", "arbitrary"),
        name="rw_prep",
    )(zr, zr, zr, mu, w0, wup, a0, aup, gup)


K_UNROLL = 16
ACC_WAYS = 1


def _tree_sum(xs):
    while len(xs) > 1:
        xs = [xs[i] + xs[i + 1] for i in range(0, len(xs) - 1, 2)] + ([xs[-1]] if len(xs) % 2 else [])
    return xs[0]


def _seq_block(n_ctx_blocks, n_blocks):
    def blk(d, i):
        bwd = jnp.where(i < n_ctx_blocks, n_ctx_blocks - 1 - i, n_blocks - 1 + n_ctx_blocks - i)
        return jnp.where(d == 0, i, bwd)
    return blk


def _rw_scan_kernel(r_ref, k_ref, v_ref, w_ref, a_ref, kkp_ref, kap_ref, _order_ref, y_ref,
                    s_ref, kk_s, b_s, k_s, sa_s):
    d = pl.program_id(0)
    i = pl.program_id(1)
    tb = r_ref.shape[0]
    nv = RW_N // SUBLANES

    @pl.when(i == 0)
    def _():
        s_ref[...] = jnp.zeros_like(s_ref)

    def bcast(ref, *idx):
        k = idx[-1]
        row = ref[(*idx[:-1], pl.ds(k, 1), slice(None))]
        return jnp.broadcast_to(row, (SUBLANES, LANES))

    def time_index(s):
        s = jnp.minimum(s, tb - 1)
        return jnp.where(d == 0, s, tb - 1 - s)

    def prepare_kk(s, slot):
        kkr = k_ref[time_index(s)] * kkp_ref[...]
        nrm = jnp.sqrt(jnp.sum(kkr * kkr, axis=0, keepdims=True))
        kk_s[slot] = kkr / jnp.maximum(nrm, L2_EPS)

    def prepare_bk(s, slot):
        t = time_index(s)
        a = a_ref[0, t]
        b_s[slot] = kk_s[slot] * a
        k_s[slot] = k_ref[t] * (1.0 + (a - 1.0) * kap_ref[...])

    def state_dot(slot):
        acc = [[None] * nv for _ in range(ACC_WAYS)]
        for k in range(RW_N):
            kkb = bcast(kk_s, slot, k)
            part = acc[k % ACC_WAYS]
            for j in range(nv):
                p = s_ref[k, SUBLANES * j:SUBLANES * (j + 1), :] * kkb
                part[j] = p if part[j] is None else part[j] + p
        sa_s[...] = -jnp.concatenate([_tree_sum([acc[w][j] for w in range(ACC_WAYS)]) for j in range(nv)], axis=0)

    def sweep(s, slot):
        t = time_index(s)
        zero = jnp.zeros((SUBLANES, LANES), F32)

        def key_block(kblk, carry):
            yacc = [list(x) for x in carry[0]]
            acc = [list(x) for x in carry[1]]
            for kk in range(K_UNROLL):
                k = kblk * K_UNROLL + kk
                ypart, apart = yacc[kk % ACC_WAYS], acc[kk % ACC_WAYS]
                wb = bcast(w_ref, 0, t, k)
                bb = bcast(b_s, slot, k)
                kb = bcast(k_s, slot, k)
                rb = bcast(r_ref, t, k)
                kkn = bcast(kk_s, 1 - slot, k)
                for j in range(nv):
                    rows = slice(SUBLANES * j, SUBLANES * (j + 1))
                    sn = s_ref[k, rows, :] * wb + (sa_s[rows, :] * bb + v_ref[t, rows, :] * kb)
                    s_ref[k, rows, :] = sn
                    ypart[j] = ypart[j] + sn * rb
                    apart[j] = apart[j] + sn * kkn
            return yacc, acc

        init = [[zero] * nv for _ in range(ACC_WAYS)]
        n_kblk = RW_N // K_UNROLL
        carry = lax.fori_loop(0, n_kblk - 1, key_block, (init, init))
        yacc, acc = key_block(n_kblk - 1, carry)
        y_ref[0, t] = jnp.concatenate([_tree_sum([yacc[w][j] for w in range(ACC_WAYS)]) for j in range(nv)], axis=0)
        sa_s[...] = -jnp.concatenate([_tree_sum([acc[w][j] for w in range(ACC_WAYS)]) for j in range(nv)], axis=0)

    for s0 in (0, 1):
        prepare_kk(s0, s0)
        prepare_bk(s0, s0)
    state_dot(0)

    def pair(p, carry):
        s = 2 * p
        sweep(s, 0)
        prepare_kk(s + 2, 0)
        prepare_bk(s + 2, 0)
        sweep(s + 1, 1)
        prepare_kk(s + 3, 1)
        prepare_bk(s + 3, 1)
        return carry

    lax.fori_loop(0, tb // 2, pair, 0)


def _rw_scan(r_t, k_t, v_t, w_t, a_t, kkp, kap, n_ctx_blocks, run_after):
    tt = r_t.shape[0]
    nblk = tt // SCAN_TILE
    blk = _seq_block(n_ctx_blocks, nblk)
    tile = (SCAN_TILE, RW_N, LANES)
    shared = pl.BlockSpec(tile, lambda d, i: (blk(d, i), 0, 0))
    per_dir = pl.BlockSpec((1,) + tile, lambda d, i: (d, blk(d, i), 0, 0))
    const = pl.BlockSpec((RW_N, LANES), lambda d, i: (0, 0))
    return pl.pallas_call(
        _rw_scan_kernel,
        grid=(2, nblk),
        in_specs=[shared, shared, shared, per_dir, per_dir, const, const, pl.BlockSpec(memory_space=pl.ANY)],
        out_specs=per_dir,
        out_shape=jax.ShapeDtypeStruct((2, tt, RW_N, LANES), F32),
        scratch_shapes=[pltpu.VMEM((RW_N, RW_N, LANES), F32),
                        pltpu.VMEM((2, RW_N, LANES), F32),
                        pltpu.VMEM((2, RW_N, LANES), F32),
                        pltpu.VMEM((2, RW_N, LANES), F32),
                        pltpu.VMEM((RW_N, LANES), F32)],
        compiler_params=_params("arbitrary", "arbitrary"),
        name="rw_scan",
    )(r_t, k_t, v_t, w_t, a_t, kkp, kap, run_after)


def _rw_readout_kernel(y_ref, r_ref, k_ref, v_ref, a_ref, kap_ref, rkp_ref, lnw_ref, lnb_ref, o_ref):
    y = y_ref[0] + y_ref[1]
    mean = jnp.mean(y, axis=1, keepdims=True)
    yc = y - mean
    var = jnp.mean(yc * yc, axis=1, keepdims=True)
    yn = yc * lax.rsqrt(var + RW_GN_EPS) * lnw_ref[...] + lnb_ref[...]
    kraw = k_ref[...]
    kap = kap_ref[...]
    k_sum = kraw * (1.0 + (a_ref[0] - 1.0) * kap) + kraw * (1.0 + (a_ref[1] - 1.0) * kap)
    bonus = jnp.sum(r_ref[...] * k_sum * rkp_ref[...], axis=1, keepdims=True) * v_ref[...]
    o_ref[...] = yn + bonus


def _rw_readout(y_t, r_t, k_t, v_t, a_t, kap, rkp, lnw, lnb, n_ctx):
    tt = r_t.shape[0]
    tb = 32
    off = n_ctx // tb
    tile = (tb, RW_N, LANES)
    shared = pl.BlockSpec(tile, lambda i: (i + off, 0, 0))
    both = pl.BlockSpec((2,) + tile, lambda i: (0, i + off, 0, 0))
    const = pl.BlockSpec((RW_N, LANES), lambda i: (0, 0))
    return pl.pallas_call(
        _rw_readout_kernel,
        grid=((tt - n_ctx) // tb,),
        in_specs=[both, shared, shared, shared, both, const, const, const, const],
        out_specs=pl.BlockSpec(tile, lambda i: (i, 0, 0)),
        out_shape=jax.ShapeDtypeStruct((tt - n_ctx, RW_N, LANES), F32),
        compiler_params=_params("arbitrary"),
        name="rw_readout",
    )(y_t, r_t, k_t, v_t, a_t, kap, rkp, lnw, lnb)


def _hg_kernel(q_ref, f_ref, i_ref, lbp_ref, tri_ref, o_ref, st_ref, qd_scr, kv_scr, dec_scr):
    d = pl.program_id(1)
    i = pl.program_id(2)
    n_chunks = q_ref.shape[1] // HG_CHUNK

    @pl.when(i == 0)
    def _():
        st_ref[...] = jnp.zeros_like(st_ref)

    lbp = lbp_ref[...]
    e = jnp.exp(lbp - jnp.max(lbp, axis=0, keepdims=True))
    lb = e[0:1] / jnp.sum(e, axis=0, keepdims=True)
    t_idx = lax.broadcasted_iota(jnp.int32, (HG_CHUNK, HG_CHUNK), 0)
    s_idx = lax.broadcasted_iota(jnp.int32, (HG_CHUNK, HG_CHUNK), 1)
    nt = (((1,), (1,)), ((), ()))
    tn = (((0,), (0,)), ((), ()))

    def run(fwd):
        mask = (s_idx <= t_idx) if fwd else (s_idx >= t_idx)
        mid_row = HG_CHUNK // 2 - 1 if fwd else HG_CHUNK // 2
        last_row = HG_CHUNK - 1 if fwd else 0
        order = list(range(n_chunks)) if fwd else list(reversed(range(n_chunks)))

        f = lb + (1.0 - lb) * _sigmoid(f_ref[0])
        lf = jnp.log(f)
        hi = lf.astype(BF16)
        lo = (lf - hi.astype(F32)).astype(BF16)
        tri = tri_ref[0]
        b = _bdot(tri, hi) + _bdot(tri, lo)
        for c in order:
            rows = slice(c * HG_CHUNK, (c + 1) * HG_CHUNK)
            bc = b[rows]
            b_mid = bc[mid_row:mid_row + 1]
            b_last = bc[last_row:last_row + 1]
            q_in = q_ref[0, rows, :] * jnp.exp(bc - b_mid)
            k_in = (1.0 - f[rows]) * jnp.exp(b_mid - bc)
            qd_scr[rows, :] = (q_in * jnp.exp(b_mid)).astype(BF16)
            k_dec = (k_in * jnp.exp(b_last - b_mid)).astype(BF16)
            dec_scr[c:c + 1, :] = jnp.exp(b_last)
            q_in = q_in.astype(BF16)
            k_in = k_in.astype(BF16)
            vb = i_ref[0, rows, :].astype(BF16)
            heads = [slice(h * HG_N, (h + 1) * HG_N) for h in range(HG_HEADS)]
            scores = [lax.dot_general(q_in[:, cols], k_in[:, cols], nt, preferred_element_type=F32)
                      for cols in heads]
            for h, cols in enumerate(heads):
                kv_scr[c, h] = lax.dot_general(vb[:, cols], k_dec[:, cols], tn, preferred_element_type=F32)
            for h, cols in enumerate(heads):
                o_ref[0, 0, rows, cols] = _bdot(jnp.where(mask, scores[h], 0.0).astype(BF16), vb[:, cols])
        for c in order:
            rows = slice(c * HG_CHUNK, (c + 1) * HG_CHUNK)
            for h in range(HG_HEADS):
                cols = slice(h * HG_N, (h + 1) * HG_N)
                st = st_ref[h]
                o_ref[0, 0, rows, cols] += lax.dot_general(qd_scr[rows, cols], st.astype(BF16), nt,
                                                           preferred_element_type=F32)
                st_ref[h] = dec_scr[c:c + 1, cols] * st + kv_scr[c, h]

    @pl.when(d == 0)
    def _():
        run(True)

    @pl.when(d == 1)
    def _():
        run(False)


def _hg_tri(n_rows):
    t = jnp.arange(n_rows)[:, None]
    s = jnp.arange(n_rows)[None, :]
    same = (t // HG_CHUNK) == (s // HG_CHUNK)
    return jnp.stack([same & (s <= t), same & (s >= t)]).astype(BF16)


def _hg(zh, hg_lb, n_ctx):
    b, tt, _ = zh.shape
    w = HG_HEADS * HG_N
    nblk = tt // ROW_TILE
    n_chunks = ROW_TILE // HG_CHUNK
    blk = _seq_block(n_ctx // ROW_TILE, nblk)
    return pl.pallas_call(
        _hg_kernel,
        grid=(b, 2, nblk),
        in_specs=[pl.BlockSpec((1, ROW_TILE, w), lambda bi, d, i: (bi, blk(d, i), 0)),
                  pl.BlockSpec((1, ROW_TILE, w), lambda bi, d, i: (bi, blk(d, i), 1 + d)),
                  pl.BlockSpec((1, ROW_TILE, w), lambda bi, d, i: (bi, blk(d, i), 3)),
                  pl.BlockSpec(hg_lb.shape, lambda bi, d, i: (0, 0)),
                  pl.BlockSpec((1, ROW_TILE, ROW_TILE), lambda bi, d, i: (d, 0, 0))],
        out_specs=pl.BlockSpec((1, 1, ROW_TILE, w), lambda bi, d, i: (d, bi, blk(d, i), 0)),
        out_shape=jax.ShapeDtypeStruct((2, b, tt, w), F32),
        scratch_shapes=[pltpu.VMEM((HG_HEADS, HG_N, HG_N), F32),
                        pltpu.VMEM((ROW_TILE, w), BF16),
                        pltpu.VMEM((n_chunks, HG_HEADS, HG_N, HG_N), F32),
                        pltpu.VMEM((SUBLANES, w), F32)],
        compiler_params=_params("arbitrary", "arbitrary", "arbitrary"),
        name="hg",
    )(zh, zh, zh, hg_lb, _hg_tri(ROW_TILE))


def _merge_kernel(yrw_ref, g_ref, o_ref, gz_ref, zg_ref, x_ref, g1_ref, hnw_ref,
                  pa_ref, pb_ref, wo_ref, out_ref):
    d = x_ref.shape[2]
    y_rw = (yrw_ref[...].T * g_ref[...]).astype(BF16)
    o = o_ref[0, 0] + o_ref[1, 0]
    gz = gz_ref[0]
    hnw = hnw_ref[...]
    parts = []
    for h in range(HG_HEADS):
        cols = slice(h * HG_N, (h + 1) * HG_N)
        oh = o[:, cols]
        ms = jnp.mean(oh * oh, axis=-1, keepdims=True)
        parts.append(oh * lax.rsqrt(ms + RMS_EPS) * hnw[:, cols])
    y_hg = (jnp.concatenate(parts, axis=1) * (gz * _sigmoid(gz))).astype(BF16)
    zg = zg_ref[0]
    m = _sigmoid(zg[:, :d]) * _bdot(y_rw, pa_ref[...]) + _sigmoid(zg[:, d:]) * _bdot(y_hg, pb_ref[...])
    out_ref[0] = x_ref[0] + g1_ref[0] * _bdot(m.astype(BF16), wo_ref[...])


def _merge(y_rw, g, o_hg, zh, zg, x, modcat, hnw, pa, pb, wo, n_ctx):
    b, t, d = x.shape
    off = n_ctx // ROW_TILE
    lat = lambda bi, i: (bi, i, 0)
    cat = lambda bi, i: (bi, i + off, 0)
    const = lambda shape: pl.BlockSpec(shape, lambda bi, i: (0,) * len(shape))
    return pl.pallas_call(
        _merge_kernel,
        grid=(b, t // ROW_TILE),
        in_specs=[pl.BlockSpec((d, ROW_TILE), lambda bi, i: (bi, i)),
                  pl.BlockSpec((ROW_TILE, d), lambda bi, i: (i + off, bi)),
                  pl.BlockSpec((2, 1, ROW_TILE, d), lambda bi, i: (0, bi, i + off, 0)),
                  pl.BlockSpec((1, ROW_TILE, d), lambda bi, i: (bi, i + off, 4)),
                  pl.BlockSpec((1, ROW_TILE, 2 * d), cat),
                  pl.BlockSpec((1, ROW_TILE, d), lat),
                  pl.BlockSpec((1, 1, d), lambda bi, i: (2 * bi + 1, 0, 2)),
                  const(hnw.shape), const(pa.shape), const(pb.shape), const(wo.shape)],
        out_specs=pl.BlockSpec((1, ROW_TILE, d), lat),
        out_shape=jax.ShapeDtypeStruct((b, t, d), F32),
        compiler_params=_params("arbitrary", "arbitrary"),
        name="merge",
    )(y_rw, g, o_hg, zh, zg, x, modcat, hnw, pa, pb, wo)


def _mlp_kernel(x_ref, nw_ref, sh_ref, sc_ref, g2_ref, w1_ref, w2_ref, fw_ref, out_ref):
    x = x_ref[0]
    hb = _modulated_norm(x, nw_ref[...], sh_ref[0], sc_ref[0]).astype(BF16)
    u = jnp.maximum(_bdot(hb, w1_ref[...]), 0.0)
    y = x + g2_ref[0] * _bdot((u * u).astype(BF16), w2_ref[...])
    ms = jnp.mean(y * y, axis=-1, keepdims=True)
    out_ref[0] = y * lax.rsqrt(ms + RMS_EPS) * fw_ref[...]


def _mlp(x1, nw, modcat, w1, w2, fw):
    b, t, d = x1.shape
    mod_idx = lambda col: (lambda bi, i: (2 * bi + 1, 0, col))
    const = lambda shape, **kw: pl.BlockSpec(shape, lambda bi, i: (0,) * len(shape), **kw)
    return pl.pallas_call(
        _mlp_kernel,
        grid=(b, t // ROW_TILE),
        in_specs=[pl.BlockSpec((1, ROW_TILE, d), lambda bi, i: (bi, i, 0)),
                  const(nw.shape),
                  pl.BlockSpec((1, 1, d), mod_idx(3)),
                  pl.BlockSpec((1, 1, d), mod_idx(4)),
                  pl.BlockSpec((1, 1, d), mod_idx(5)),
                  const(w1.shape, pipeline_mode=pl.Buffered(1)),
                  const(w2.shape, pipeline_mode=pl.Buffered(1)),
                  const(fw.shape)],
        out_specs=pl.BlockSpec((1, ROW_TILE, d), lambda bi, i: (bi, i, 0)),
        out_shape=jax.ShapeDtypeStruct((b, t, d), F32),
        compiler_params=_params("arbitrary", "arbitrary"),
        name="mlp",
    )(x1, nw, modcat, modcat, modcat, w1, w2, fw)


def _to_scan(a):
    *lead, w, t = a.shape
    return jnp.swapaxes(a.reshape(*lead, w // RW_N, RW_N, t), -1, -3)


def _from_scan(a):
    t, n, bh = a.shape
    return jnp.swapaxes(a, 0, 2).reshape(bh * n, t)


def _head_tile(p, b):
    return jnp.tile(p.reshape(RW_HEADS, RW_N).T, (1, b))


def kernel(x, c, ctx, c_ctx, norm1_w, norm2_w, w_mod, b_mod, w_in, rw_mu, rw_w0, rw_w_up, rw_a0, rw_a_up, rw_g_up, rw_k_k, rw_k_a, rw_r_k, rw_ln_w, rw_ln_b, hg_lb, hg_norm_w, p_a, p_b, w_out, w_fc1, w_fc2, final_norm_w):
    b, t, d = x.shape
    n_ctx = ctx.shape[1]
    assert w_mod.shape[0] == 1, "single-layer block"
    assert b * RW_HEADS == LANES and n_ctx == ROW_TILE and t % ROW_TILE == 0
    assert d == RW_HEADS * RW_N == HG_HEADS * HG_N
    n_r = rw_mu.shape[2]
    n_h = 5 * d
    n_g = 2 * d
    rank_w, rank_a = rw_w_up.shape[2], rw_a_up.shape[2]
    assert rank_w + rank_a == LANES and n_r == 3 * d + 2 * LANES

    c_rows = jnp.zeros((2 * SUBLANES, d), F32).at[:b].set(c).at[b].set(c_ctx)
    mod = _mod(c_rows, w_mod[0], b_mod)
    modcat = jnp.stack([jnp.broadcast_to(mod[b], (b, N_MOD * d)), mod[:b]], axis=1).reshape(2 * b, 1, N_MOD * d)

    zr, zh, zg = _in_proj(ctx, x, norm1_w, modcat, w_in[0].astype(BF16), n_r, n_h, n_g)

    wup = jnp.pad(rw_w_up[0], ((0, 0), (0, rank_a), (0, 0))).astype(BF16)
    aup = jnp.pad(rw_a_up[0], ((0, 0), (rank_w, 0), (0, 0))).astype(BF16)
    r, k_raw, v, wd, ad, g = _rw_prep(zr, rw_mu[0], rw_w0[0], wup, rw_a0[0], aup,
                                      rw_g_up[0].astype(BF16), d)

    r_t, k_t, v_t, w_t, a_t = (_to_scan(a) for a in (r, k_raw, v, wd, ad))
    o_hg = _hg(zh, hg_lb, n_ctx)
    kap = _head_tile(rw_k_a[0], b)
    y_t = _rw_scan(r_t, k_t, v_t, w_t, a_t, _head_tile(rw_k_k[0], b), kap, n_ctx // SCAN_TILE, o_hg)
    y_rw_t = _rw_readout(y_t, r_t, k_t, v_t, a_t, kap, _head_tile(rw_r_k[0].reshape(-1), b),
                         _head_tile(rw_ln_w[0], b), _head_tile(rw_ln_b[0], b), n_ctx)
    y_rw = _from_scan(y_rw_t)

    x1 = _merge(y_rw, g, o_hg, zh, zg, x, modcat, hg_norm_w, p_a[0].astype(BF16), p_b[0].astype(BF16),
                w_out[0].astype(BF16), n_ctx)
    return _mlp(x1, norm2_w, modcat, w_fc1[0].astype(BF16), w_fc2[0].astype(BF16),
                final_norm_w.reshape(1, d))
```

```python
import functools

import jax
import jax.numpy as jnp
from jax import lax
from jax.experimental import pallas as pl
from jax.experimental.pallas import tpu as pltpu

F32 = jnp.float32
BF16 = jnp.bfloat16

GRID_W = 64
RW_HEADS = 16
RW_N = 64
HG_HEADS = 8
HG_N = 128
HG_CHUNK = 64
N_MOD = 6
RMS_EPS = 1e-6
RW_GN_EPS = 64e-5
L2_EPS = 1e-12

SUBLANES = 8
LANES = 128
ROW_TILE = 256
SCAN_TILE = 64
VMEM_LIMIT = 56 * 1024 * 1024


def _params(*sem):
    return pltpu.CompilerParams(dimension_semantics=sem, vmem_limit_bytes=VMEM_LIMIT)


def _bdot(a, b):
    return jnp.dot(a, b, preferred_element_type=F32)


def _sigmoid(x):
    return 1.0 / (1.0 + jnp.exp(-x))


def _mod_kernel(c_ref, w_ref, b_ref, o_ref):
    c = c_ref[...]
    act = c * _sigmoid(c)
    o_ref[...] = _bdot(act.astype(BF16), w_ref[...].astype(BF16)) + b_ref[...]


def _mod(c_rows, w_mod, b_mod):
    rows, d = c_rows.shape
    n = w_mod.shape[1]
    return pl.pallas_call(
        _mod_kernel,
        grid=(n // d,),
        in_specs=[pl.BlockSpec((rows, d), lambda j: (0, 0)),
                  pl.BlockSpec((d, d), lambda j: (0, j)),
                  pl.BlockSpec((1, d), lambda j: (0, j))],
        out_specs=pl.BlockSpec((rows, d), lambda j: (0, j)),
        out_shape=jax.ShapeDtypeStruct((rows, n), F32),
        compiler_params=_params("arbitrary"),
        name="mod",
    )(c_rows, w_mod, b_mod)


def _modulated_norm(x, nw, sh, sc):
    ms = jnp.mean(x * x, axis=-1, keepdims=True)
    return (x * lax.rsqrt(ms + RMS_EPS) * nw) * (1.0 + sc) + sh


def _in_proj_kernel(ctx_ref, x_ref, nw_ref, sh_ref, sc_ref, w_ref, zr_ref, zh_ref, zg_ref):
    tokens = jnp.where(pl.program_id(1) == 0, ctx_ref[0], x_ref[0])
    hb = _modulated_norm(tokens, nw_ref[...], sh_ref[0], sc_ref[0]).astype(BF16)
    n_r, n_h = zr_ref.shape[2], zh_ref.shape[2]
    zr_ref[0] = _bdot(hb, w_ref[:, :n_r])
    zh_ref[0] = _bdot(hb, w_ref[:, n_r:n_r + n_h])
    zg_ref[0] = _bdot(hb, w_ref[:, n_r + n_h:])


def _in_proj(ctx, x, nw, modcat, w_bf, n_r, n_h, n_g):
    b, t, d = x.shape
    tt = t + ctx.shape[1]
    nblk = tt // ROW_TILE
    mod_idx = lambda col: (lambda bi, i: (2 * bi + jnp.minimum(i, 1), 0, col))
    return pl.pallas_call(
        _in_proj_kernel,
        grid=(b, nblk),
        in_specs=[pl.BlockSpec((1, ROW_TILE, d), lambda bi, i: (bi, 0, 0)),
                  pl.BlockSpec((1, ROW_TILE, d), lambda bi, i: (bi, jnp.maximum(i - 1, 0), 0)),
                  pl.BlockSpec((1, d), lambda bi, i: (0, 0)),
                  pl.BlockSpec((1, 1, d), mod_idx(0)),
                  pl.BlockSpec((1, 1, d), mod_idx(1)),
                  pl.BlockSpec(w_bf.shape, lambda bi, i: (0, 0), pipeline_mode=pl.Buffered(1))],
        out_specs=[pl.BlockSpec((1, ROW_TILE, n), lambda bi, i: (bi, i, 0)) for n in (n_r, n_h, n_g)],
        out_shape=[jax.ShapeDtypeStruct((b, tt, n), F32) for n in (n_r, n_h, n_g)],
        compiler_params=_params("arbitrary", "arbitrary"),
        name="in_proj",
    )(ctx, x, nw, modcat, modcat, w_bf)


def _rw_prep_kernel(z_ref, zp_ref, zn_ref, mu_ref, w0_ref, wup_ref, a0_ref, aup_ref, gup_ref,
                    r_ref, k_ref, v_ref, wd_ref, ad_ref, g_ref):
    i = pl.program_id(1)
    nblk = pl.num_programs(1)
    tt = z_ref.shape[1]
    d = r_ref.shape[0]
    is_lat = i > 0
    row = lax.broadcasted_iota(jnp.int32, (tt, 1), 0)
    col = row % GRID_W
    lmask = jnp.where(is_lat, col, row) == 0
    rmask = jnp.where(is_lat, col, row - (tt - GRID_W)) == GRID_W - 1
    latf = is_lat.astype(F32)
    up_ok = (i > 1).astype(F32)
    dn_ok = (i < nblk - 1).astype(F32)

    def shifted(c0, c1):
        z = z_ref[0, :, c0:c1]
        mu = mu_ref[:, c0:c1]
        left = jnp.where(lmask, 0.0, pltpu.roll(z, 1, 0))
        right = jnp.where(rmask, 0.0, pltpu.roll(z, tt - 1, 0))
        up = jnp.concatenate([zp_ref[0, :, c0:c1] * up_ok, z[:tt - GRID_W]], axis=0)
        down = jnp.concatenate([z[GRID_W:], zn_ref[0, :, c0:c1] * dn_ok], axis=0)
        out = z + mu[0:1] * (left - z) + mu[1:2] * (right - z)
        return out + latf * (mu[2:3] * (up - z) + mu[3:4] * (down - z))

    r_ref[...] = shifted(0, d).T
    k_ref[...] = shifted(d, 2 * d).T
    v_ref[...] = shifted(2 * d, 3 * d).T
    rest = shifted(3 * d, z_ref.shape[2])
    xwa = rest[:, :LANES]
    xw_t = jnp.tanh(xwa).astype(BF16)
    xa_b = xwa.astype(BF16)
    for dr in range(2):
        u = -(w0_ref[dr:dr + 1, :] + _bdot(xw_t, wup_ref[dr]))
        softplus = jnp.maximum(u, 0.0) + jnp.log(1.0 + jnp.exp(-jnp.abs(u)))
        wd_ref[dr] = jnp.exp(-jnp.exp(-softplus - 0.5)).T
        ad_ref[dr] = _sigmoid(a0_ref[dr:dr + 1, :] + _bdot(xa_b, aup_ref[dr])).T
    g_ref[...] = _bdot(_sigmoid(rest[:, LANES:]).astype(BF16), gup_ref[...])


def _rw_prep(zr, mu, w0, wup, a0, aup, gup, d):
    b, tt, nr = zr.shape
    nblk = tt // ROW_TILE
    per = ROW_TILE // GRID_W
    last = tt // GRID_W - 1
    const = lambda shape: pl.BlockSpec(shape, lambda bi, i: (0,) * len(shape))
    row_spec = pl.BlockSpec((d, ROW_TILE), lambda bi, i: (bi, i))
    dir_spec = pl.BlockSpec((2, d, ROW_TILE), lambda bi, i: (0, bi, i))
    gate_spec = pl.BlockSpec((ROW_TILE, d), lambda bi, i: (i, bi))
    row_shape = jax.ShapeDtypeStruct((b * d, tt), F32)
    dir_shape = jax.ShapeDtypeStruct((2, b * d, tt), F32)
    gate_shape = jax.ShapeDtypeStruct((tt, b * d), F32)
    return pl.pallas_call(
        _rw_prep_kernel,
        grid=(b, nblk),
        in_specs=[pl.BlockSpec((1, ROW_TILE, nr), lambda bi, i: (bi, i, 0)),
                  pl.BlockSpec((1, GRID_W, nr), lambda bi, i: (bi, jnp.maximum(i * per - 1, 0), 0)),
                  pl.BlockSpec((1, GRID_W, nr), lambda bi, i: (bi, jnp.minimum(i * per + per, last), 0)),
                  const(mu.shape), const(w0.shape), const(wup.shape), const(a0.shape),
                  const(aup.shape), const(gup.shape)],
        out_specs=[row_spec, row_spec, row_spec, dir_spec, dir_spec, gate_spec],
        out_shape=[row_shape, row_shape, row_shape, dir_shape, dir_shape, gate_shape],
        compiler_params=_params("arbitrary", "arbitrary"),
        name="rw_prep",
    )(zr, zr, zr, mu, w0, wup, a0, aup, gup)


K_UNROLL = 16
ACC_WAYS = 1


def _tree_sum(xs):
    while len(xs) > 1:
        xs = [xs[i] + xs[i + 1] for i in range(0, len(xs) - 1, 2)] + ([xs[-1]] if len(xs) % 2 else [])
    return xs[0]


def _seq_block(n_ctx_blocks, n_blocks):
    def blk(d, i):
        bwd = jnp.where(i < n_ctx_blocks, n_ctx_blocks - 1 - i, n_blocks - 1 + n_ctx_blocks - i)
        return jnp.where(d == 0, i, bwd)
    return blk


def _rw_scan_kernel(r_ref, k_ref, v_ref, w_ref, a_ref, kkp_ref, kap_ref, _order_ref, y_ref,
                    s_ref, kk_s, b_s, k_s, sa_s, kkd_s, rd_s):
    d = pl.program_id(0)
    i = pl.program_id(1)
    tb = r_ref.shape[0]
    nv = RW_N // SUBLANES

    @pl.when(i == 0)
    def _():
        s_ref[...] = jnp.zeros_like(s_ref)

    def bcast(ref, *idx):
        k = idx[-1]
        row = ref[(*idx[:-1], pl.ds(k, 1), slice(None))]
        return jnp.broadcast_to(row, (SUBLANES, LANES))

    def time_index(s):
        s = jnp.minimum(s, tb - 1)
        return jnp.where(d == 0, s, tb - 1 - s)

    def bf16_pair(x):
        hi = lax.bitcast_convert_type(x.astype(BF16).astype(F32), jnp.uint32)
        return lax.bitcast_convert_type(hi | (hi >> 16), F32)

    def prepare_kk(s, slot):
        kkr = k_ref[time_index(s)] * kkp_ref[...]
        nrm = jnp.sqrt(jnp.sum(kkr * kkr, axis=0, keepdims=True))
        kk = kkr / jnp.maximum(nrm, L2_EPS)
        kk_s[slot] = kk
        kkd_s[slot] = bf16_pair(kk)

    def prepare_bk(s, slot):
        t = time_index(s)
        a = a_ref[0, t]
        b_s[slot] = kk_s[slot] * a
        rd_s[slot] = bf16_pair(r_ref[t])
        k_s[slot] = k_ref[t] * (1.0 + (a - 1.0) * kap_ref[...])

    def state_dot(slot):
        acc = [[None] * nv for _ in range(ACC_WAYS)]
        for k in range(RW_N):
            kkb = bcast(kk_s, slot, k)
            part = acc[k % ACC_WAYS]
            for j in range(nv):
                p = s_ref[k, SUBLANES * j:SUBLANES * (j + 1), :] * kkb
                part[j] = p if part[j] is None else part[j] + p
        sa_s[...] = -jnp.concatenate([_tree_sum([acc[w][j] for w in range(ACC_WAYS)]) for j in range(nv)], axis=0)

    def sweep(s, slot):
        t = time_index(s)
        zero = jnp.zeros((SUBLANES, LANES), F32)

        def key_block(kblk, carry):
            yacc, acc = list(carry[0]), list(carry[1])
            yb = [jnp.zeros((2 * SUBLANES, LANES), BF16)] * (nv // 2)
            ab = [jnp.zeros((2 * SUBLANES, LANES), BF16)] * (nv // 2)
            for kk in range(K_UNROLL):
                k = kblk * K_UNROLL + kk
                wb = bcast(w_ref, 0, t, k)
                bb = bcast(b_s, slot, k)
                kb = bcast(k_s, slot, k)
                rb = pltpu.bitcast(bcast(rd_s, slot, k), BF16)
                kkn = pltpu.bitcast(bcast(kkd_s, 1 - slot, k), BF16)
                for m in range(nv // 2):
                    pair_rows = []
                    for j in (2 * m, 2 * m + 1):
                        rows = slice(SUBLANES * j, SUBLANES * (j + 1))
                        sn = s_ref[k, rows, :] * wb + (sa_s[rows, :] * bb + v_ref[t, rows, :] * kb)
                        s_ref[k, rows, :] = sn
                        pair_rows.append(sn)
                    snp = jnp.concatenate(pair_rows, axis=0).astype(BF16)
                    yb[m] = yb[m] + snp * rb
                    ab[m] = ab[m] + snp * kkn
            for m in range(nv // 2):
                y32 = yb[m].astype(F32)
                a32 = ab[m].astype(F32)
                for h, j in enumerate((2 * m, 2 * m + 1)):
                    yacc[j] = yacc[j] + y32[SUBLANES * h:SUBLANES * (h + 1)]
                    acc[j] = acc[j] + a32[SUBLANES * h:SUBLANES * (h + 1)]
            return yacc, acc

        init = [zero] * nv
        n_kblk = RW_N // K_UNROLL
        carry = lax.fori_loop(0, n_kblk - 1, key_block, (init, init))
        yacc, acc = key_block(n_kblk - 1, carry)
        y_ref[0, t] = jnp.concatenate(yacc, axis=0)
        sa_s[...] = -jnp.concatenate(acc, axis=0)

    for s0 in (0, 1):
        prepare_kk(s0, s0)
        prepare_bk(s0, s0)
    state_dot(0)

    def pair(p, carry):
        s = 2 * p
        sweep(s, 0)
        prepare_kk(s + 2, 0)
        prepare_bk(s + 2, 0)
        sweep(s + 1, 1)
        prepare_kk(s + 3, 1)
        prepare_bk(s + 3, 1)
        return carry

    lax.fori_loop(0, tb // 2, pair, 0)


def _rw_scan(r_t, k_t, v_t, w_t, a_t, kkp, kap, n_ctx_blocks, run_after):
    tt = r_t.shape[0]
    nblk = tt // SCAN_TILE
    blk = _seq_block(n_ctx_blocks, nblk)
    tile = (SCAN_TILE, RW_N, LANES)
    shared = pl.BlockSpec(tile, lambda d, i: (blk(d, i), 0, 0))
    per_dir = pl.BlockSpec((1,) + tile, lambda d, i: (d, blk(d, i), 0, 0))
    const = pl.BlockSpec((RW_N, LANES), lambda d, i: (0, 0))
    return pl.pallas_call(
        _rw_scan_kernel,
        grid=(2, nblk),
        in_specs=[shared, shared, shared, per_dir, per_dir, const, const, pl.BlockSpec(memory_space=pl.ANY)],
        out_specs=per_dir,
        out_shape=jax.ShapeDtypeStruct((2, tt, RW_N, LANES), F32),
        scratch_shapes=[pltpu.VMEM((RW_N, RW_N, LANES), F32),
                        pltpu.VMEM((2, RW_N, LANES), F32),
                        pltpu.VMEM((2, RW_N, LANES), F32),
                        pltpu.VMEM((2, RW_N, LANES), F32),
                        pltpu.VMEM((RW_N, LANES), F32),
                        pltpu.VMEM((2, RW_N, LANES), F32),
                        pltpu.VMEM((2, RW_N, LANES), F32)],
        compiler_params=_params("arbitrary", "arbitrary"),
        name="rw_scan",
    )(r_t, k_t, v_t, w_t, a_t, kkp, kap, run_after)


def _rw_readout_kernel(y_ref, r_ref, k_ref, v_ref, a_ref, kap_ref, rkp_ref, lnw_ref, lnb_ref, o_ref):
    y = y_ref[0] + y_ref[1]
    mean = jnp.mean(y, axis=1, keepdims=True)
    yc = y - mean
    var = jnp.mean(yc * yc, axis=1, keepdims=True)
    yn = yc * lax.rsqrt(var + RW_GN_EPS) * lnw_ref[...] + lnb_ref[...]
    kraw = k_ref[...]
    kap = kap_ref[...]
    k_sum = kraw * (1.0 + (a_ref[0] - 1.0) * kap) + kraw * (1.0 + (a_ref[1] - 1.0) * kap)
    bonus = jnp.sum(r_ref[...] * k_sum * rkp_ref[...], axis=1, keepdims=True) * v_ref[...]
    o_ref[...] = yn + bonus


def _rw_readout(y_t, r_t, k_t, v_t, a_t, kap, rkp, lnw, lnb, n_ctx):
    tt = r_t.shape[0]
    tb = 32
    off = n_ctx // tb
    tile = (tb, RW_N, LANES)
    shared = pl.BlockSpec(tile, lambda i: (i + off, 0, 0))
    both = pl.BlockSpec((2,) + tile, lambda i: (0, i + off, 0, 0))
    const = pl.BlockSpec((RW_N, LANES), lambda i: (0, 0))
    return pl.pallas_call(
        _rw_readout_kernel,
        grid=((tt - n_ctx) // tb,),
        in_specs=[both, shared, shared, shared, both, const, const, const, const],
        out_specs=pl.BlockSpec(tile, lambda i: (i, 0, 0)),
        out_shape=jax.ShapeDtypeStruct((tt - n_ctx, RW_N, LANES), F32),
        compiler_params=_params("arbitrary"),
        name="rw_readout",
    )(y_t, r_t, k_t, v_t, a_t, kap, rkp, lnw, lnb)


def _hg_kernel(q_ref, f_ref, i_ref, lbp_ref, tri_ref, o_ref, st_ref, qd_scr, kv_scr, dec_scr):
    d = pl.program_id(1)
    i = pl.program_id(2)
    n_chunks = q_ref.shape[1] // HG_CHUNK

    @pl.when(i == 0)
    def _():
        st_ref[...] = jnp.zeros_like(st_ref)

    lbp = lbp_ref[...]
    e = jnp.exp(lbp - jnp.max(lbp, axis=0, keepdims=True))
    lb = e[0:1] / jnp.sum(e, axis=0, keepdims=True)
    t_idx = lax.broadcasted_iota(jnp.int32, (HG_CHUNK, HG_CHUNK), 0)
    s_idx = lax.broadcasted_iota(jnp.int32, (HG_CHUNK, HG_CHUNK), 1)
    nt = (((1,), (1,)), ((), ()))
    tn = (((0,), (0,)), ((), ()))

    def run(fwd):
        mask = (s_idx <= t_idx) if fwd else (s_idx >= t_idx)
        mid_row = HG_CHUNK // 2 - 1 if fwd else HG_CHUNK // 2
        last_row = HG_CHUNK - 1 if fwd else 0
        order = list(range(n_chunks)) if fwd else list(reversed(range(n_chunks)))

        f = lb + (1.0 - lb) * _sigmoid(f_ref[0])
        lf = jnp.log(f)
        hi = lf.astype(BF16)
        lo = (lf - hi.astype(F32)).astype(BF16)
        tri = tri_ref[0]
        b = _bdot(tri, hi) + _bdot(tri, lo)
        for c in order:
            rows = slice(c * HG_CHUNK, (c + 1) * HG_CHUNK)
            bc = b[rows]
            b_mid = bc[mid_row:mid_row + 1]
            b_last = bc[last_row:last_row + 1]
            q_in = q_ref[0, rows, :] * jnp.exp(bc - b_mid)
            k_in = (1.0 - f[rows]) * jnp.exp(b_mid - bc)
            qd_scr[rows, :] = (q_in * jnp.exp(b_mid)).astype(BF16)
            k_dec = (k_in * jnp.exp(b_last - b_mid)).astype(BF16)
            dec_scr[c:c + 1, :] = jnp.exp(b_last)
            q_in = q_in.astype(BF16)
            k_in = k_in.astype(BF16)
            vb = i_ref[0, rows, :].astype(BF16)
            heads = [slice(h * HG_N, (h + 1) * HG_N) for h in range(HG_HEADS)]
            scores = [lax.dot_general(q_in[:, cols], k_in[:, cols], nt, preferred_element_type=F32)
                      for cols in heads]
            for h, cols in enumerate(heads):
                kv_scr[c, h] = lax.dot_general(vb[:, cols], k_dec[:, cols], tn, preferred_element_type=F32)
            for h, cols in enumerate(heads):
                o_ref[0, 0, rows, cols] = _bdot(jnp.where(mask, scores[h], 0.0).astype(BF16), vb[:, cols])
        for c in order:
            rows = slice(c * HG_CHUNK, (c + 1) * HG_CHUNK)
            for h in range(HG_HEADS):
                cols = slice(h * HG_N, (h + 1) * HG_N)
                st = st_ref[h]
                o_ref[0, 0, rows, cols] += lax.dot_general(qd_scr[rows, cols], st.astype(BF16), nt,
                                                           preferred_element_type=F32)
                st_ref[h] = dec_scr[c:c + 1, cols] * st + kv_scr[c, h]

    @pl.when(d == 0)
    def _():
        run(True)

    @pl.when(d == 1)
    def _():
        run(False)


def _hg_tri(n_rows):
    t = jnp.arange(n_rows)[:, None]
    s = jnp.arange(n_rows)[None, :]
    same = (t // HG_CHUNK) == (s // HG_CHUNK)
    return jnp.stack([same & (s <= t), same & (s >= t)]).astype(BF16)


def _hg(zh, hg_lb, n_ctx):
    b, tt, _ = zh.shape
    w = HG_HEADS * HG_N
    nblk = tt // ROW_TILE
    n_chunks = ROW_TILE // HG_CHUNK
    blk = _seq_block(n_ctx // ROW_TILE, nblk)
    return pl.pallas_call(
        _hg_kernel,
        grid=(b, 2, nblk),
        in_specs=[pl.BlockSpec((1, ROW_TILE, w), lambda bi, d, i: (bi, blk(d, i), 0)),
                  pl.BlockSpec((1, ROW_TILE, w), lambda bi, d, i: (bi, blk(d, i), 1 + d)),
                  pl.BlockSpec((1, ROW_TILE, w), lambda bi, d, i: (bi, blk(d, i), 3)),
                  pl.BlockSpec(hg_lb.shape, lambda bi, d, i: (0, 0)),
                  pl.BlockSpec((1, ROW_TILE, ROW_TILE), lambda bi, d, i: (d, 0, 0))],
        out_specs=pl.BlockSpec((1, 1, ROW_TILE, w), lambda bi, d, i: (d, bi, blk(d, i), 0)),
        out_shape=jax.ShapeDtypeStruct((2, b, tt, w), F32),
        scratch_shapes=[pltpu.VMEM((HG_HEADS, HG_N, HG_N), F32),
                        pltpu.VMEM((ROW_TILE, w), BF16),
                        pltpu.VMEM((n_chunks, HG_HEADS, HG_N, HG_N), F32),
                        pltpu.VMEM((SUBLANES, w), F32)],
        compiler_params=_params("arbitrary", "arbitrary", "arbitrary"),
        name="hg",
    )(zh, zh, zh, hg_lb, _hg_tri(ROW_TILE))


def _merge_kernel(yrw_ref, g_ref, o_ref, gz_ref, zg_ref, x_ref, g1_ref, hnw_ref,
                  pa_ref, pb_ref, wo_ref, out_ref):
    d = x_ref.shape[2]
    y_rw = (yrw_ref[...].T * g_ref[...]).astype(BF16)
    o = o_ref[0, 0] + o_ref[1, 0]
    gz = gz_ref[0]
    hnw = hnw_ref[...]
    parts = []
    for h in range(HG_HEADS):
        cols = slice(h * HG_N, (h + 1) * HG_N)
        oh = o[:, cols]
        ms = jnp.mean(oh * oh, axis=-1, keepdims=True)
        parts.append(oh * lax.rsqrt(ms + RMS_EPS) * hnw[:, cols])
    y_hg = (jnp.concatenate(parts, axis=1) * (gz * _sigmoid(gz))).astype(BF16)
    zg = zg_ref[0]
    m = _sigmoid(zg[:, :d]) * _bdot(y_rw, pa_ref[...]) + _sigmoid(zg[:, d:]) * _bdot(y_hg, pb_ref[...])
    out_ref[0] = x_ref[0] + g1_ref[0] * _bdot(m.astype(BF16), wo_ref[...])


def _merge(y_rw, g, o_hg, zh, zg, x, modcat, hnw, pa, pb, wo, n_ctx):
    b, t, d = x.shape
    off = n_ctx // ROW_TILE
    lat = lambda bi, i: (bi, i, 0)
    cat = lambda bi, i: (bi, i + off, 0)
    const = lambda shape: pl.BlockSpec(shape, lambda bi, i: (0,) * len(shape))
    return pl.pallas_call(
        _merge_kernel,
        grid=(b, t // ROW_TILE),
        in_specs=[pl.BlockSpec((d, ROW_TILE), lambda bi, i: (bi, i)),
                  pl.BlockSpec((ROW_TILE, d), lambda bi, i: (i + off, bi)),
                  pl.BlockSpec((2, 1, ROW_TILE, d), lambda bi, i: (0, bi, i + off, 0)),
                  pl.BlockSpec((1, ROW_TILE, d), lambda bi, i: (bi, i + off, 4)),
                  pl.BlockSpec((1, ROW_TILE, 2 * d), cat),
                  pl.BlockSpec((1, ROW_TILE, d), lat),
                  pl.BlockSpec((1, 1, d), lambda bi, i: (2 * bi + 1, 0, 2)),
                  const(hnw.shape), const(pa.shape), const(pb.shape), const(wo.shape)],
        out_specs=pl.BlockSpec((1, ROW_TILE, d), lat),
        out_shape=jax.ShapeDtypeStruct((b, t, d), F32),
        compiler_params=_params("arbitrary", "arbitrary"),
        name="merge",
    )(y_rw, g, o_hg, zh, zg, x, modcat, hnw, pa, pb, wo)


def _mlp_kernel(x_ref, nw_ref, sh_ref, sc_ref, g2_ref, w1_ref, w2_ref, fw_ref, out_ref):
    x = x_ref[0]
    hb = _modulated_norm(x, nw_ref[...], sh_ref[0], sc_ref[0]).astype(BF16)
    u = jnp.maximum(_bdot(hb, w1_ref[...]), 0.0)
    y = x + g2_ref[0] * _bdot((u * u).astype(BF16), w2_ref[...])
    ms = jnp.mean(y * y, axis=-1, keepdims=True)
    out_ref[0] = y * lax.rsqrt(ms + RMS_EPS) * fw_ref[...]


def _mlp(x1, nw, modcat, w1, w2, fw):
    b, t, d = x1.shape
    mod_idx = lambda col: (lambda bi, i: (2 * bi + 1, 0, col))
    const = lambda shape, **kw: pl.BlockSpec(shape, lambda bi, i: (0,) * len(shape), **kw)
    return pl.pallas_call(
        _mlp_kernel,
        grid=(b, t // ROW_TILE),
        in_specs=[pl.BlockSpec((1, ROW_TILE, d), lambda bi, i: (bi, i, 0)),
                  const(nw.shape),
                  pl.BlockSpec((1, 1, d), mod_idx(3)),
                  pl.BlockSpec((1, 1, d), mod_idx(4)),
                  pl.BlockSpec((1, 1, d), mod_idx(5)),
                  const(w1.shape, pipeline_mode=pl.Buffered(1)),
                  const(w2.shape, pipeline_mode=pl.Buffered(1)),
                  const(fw.shape)],
        out_specs=pl.BlockSpec((1, ROW_TILE, d), lambda bi, i: (bi, i, 0)),
        out_shape=jax.ShapeDtypeStruct((b, t, d), F32),
        compiler_params=_params("arbitrary", "arbitrary"),
        name="mlp",
    )(x1, nw, modcat, modcat, modcat, w1, w2, fw)


def _to_scan(a):
    *lead, w, t = a.shape
    return jnp.swapaxes(a.reshape(*lead, w // RW_N, RW_N, t), -1, -3)


def _from_scan(a):
    t, n, bh = a.shape
    return jnp.swapaxes(a, 0, 2).reshape(bh * n, t)


def _head_tile(p, b):
    return jnp.tile(p.reshape(RW_HEADS, RW_N).T, (1, b))


def kernel(x, c, ctx, c_ctx, norm1_w, norm2_w, w_mod, b_mod, w_in, rw_mu, rw_w0, rw_w_up, rw_a0, rw_a_up, rw_g_up, rw_k_k, rw_k_a, rw_r_k, rw_ln_w, rw_ln_b, hg_lb, hg_norm_w, p_a, p_b, w_out, w_fc1, w_fc2, final_norm_w):
    b, t, d = x.shape
    n_ctx = ctx.shape[1]
    assert w_mod.shape[0] == 1, "single-layer block"
    assert b * RW_HEADS == LANES and n_ctx == ROW_TILE and t % ROW_TILE == 0
    assert d == RW_HEADS * RW_N == HG_HEADS * HG_N
    n_r = rw_mu.shape[2]
    n_h = 5 * d
    n_g = 2 * d
    rank_w, rank_a = rw_w_up.shape[2], rw_a_up.shape[2]
    assert rank_w + rank_a == LANES and n_r == 3 * d + 2 * LANES

    c_rows = jnp.zeros((2 * SUBLANES, d), F32).at[:b].set(c).at[b].set(c_ctx)
    mod = _mod(c_rows, w_mod[0], b_mod)
    modcat = jnp.stack([jnp.broadcast_to(mod[b], (b, N_MOD * d)), mod[:b]], axis=1).reshape(2 * b, 1, N_MOD * d)

    zr, zh, zg = _in_proj(ctx, x, norm1_w, modcat, w_in[0].astype(BF16), n_r, n_h, n_g)

    wup = jnp.pad(rw_w_up[0], ((0, 0), (0, rank_a), (0, 0))).astype(BF16)
    aup = jnp.pad(rw_a_up[0], ((0, 0), (rank_w, 0), (0, 0))).astype(BF16)
    r, k_raw, v, wd, ad, g = _rw_prep(zr, rw_mu[0], rw_w0[0], wup, rw_a0[0], aup,
                                      rw_g_up[0].astype(BF16), d)

    r_t, k_t, v_t, w_t, a_t = (_to_scan(a) for a in (r, k_raw, v, wd, ad))
    o_hg = _hg(zh, hg_lb, n_ctx)
    kap = _head_tile(rw_k_a[0], b)
    y_t = _rw_scan(r_t, k_t, v_t, w_t, a_t, _head_tile(rw_k_k[0], b), kap, n_ctx // SCAN_TILE, o_hg)
    y_rw_t = _rw_readout(y_t, r_t, k_t, v_t, a_t, kap, _head_tile(rw_r_k[0].reshape(-1), b),
                         _head_tile(rw_ln_w[0], b), _head_tile(rw_ln_b[0], b), n_ctx)
    y_rw = _from_scan(y_rw_t)

    x1 = _merge(y_rw, g, o_hg, zh, zg, x, modcat, hg_norm_w, p_a[0].astype(BF16), p_b[0].astype(BF16),
                w_out[0].astype(BF16), n_ctx)
    return _mlp(x1, norm2_w, modcat, w_fc1[0].astype(BF16), w_fc2[0].astype(BF16),
                final_norm_w.reshape(1, d))
```

```python
import functools

import jax
import jax.numpy as jnp
from jax import lax
from jax.experimental import pallas as pl
from jax.experimental.pallas import tpu as pltpu

F32 = jnp.float32
BF16 = jnp.bfloat16

GRID_W = 64
RW_HEADS = 16
RW_N = 64
HG_HEADS = 8
HG_N = 128
HG_CHUNK = 64
N_MOD = 6
RMS_EPS = 1e-6
RW_GN_EPS = 64e-5
L2_EPS = 1e-12

SUBLANES = 8
LANES = 128
ROW_TILE = 256
SCAN_TILE = 64
VMEM_LIMIT = 56 * 1024 * 1024


def _params(*sem):
    return pltpu.CompilerParams(dimension_semantics=sem, vmem_limit_bytes=VMEM_LIMIT)


def _bdot(a, b):
    return jnp.dot(a, b, preferred_element_type=F32)


def _sigmoid(x):
    return 1.0 / (1.0 + jnp.exp(-x))


def _mod_kernel(c_ref, w_ref, b_ref, o_ref):
    c = c_ref[...]
    act = c * _sigmoid(c)
    o_ref[...] = _bdot(act.astype(BF16), w_ref[...].astype(BF16)) + b_ref[...]


def _mod(c_rows, w_mod, b_mod):
    rows, d = c_rows.shape
    n = w_mod.shape[1]
    return pl.pallas_call(
        _mod_kernel,
        grid=(n // d,),
        in_specs=[pl.BlockSpec((rows, d), lambda j: (0, 0)),
                  pl.BlockSpec((d, d), lambda j: (0, j)),
                  pl.BlockSpec((1, d), lambda j: (0, j))],
        out_specs=pl.BlockSpec((rows, d), lambda j: (0, j)),
        out_shape=jax.ShapeDtypeStruct((rows, n), F32),
        compiler_params=_params("arbitrary"),
        name="mod",
    )(c_rows, w_mod, b_mod)


def _modulated_norm(x, nw, sh, sc):
    ms = jnp.mean(x * x, axis=-1, keepdims=True)
    return (x * lax.rsqrt(ms + RMS_EPS) * nw) * (1.0 + sc) + sh


def _in_proj_kernel(ctx_ref, x_ref, nw_ref, sh_ref, sc_ref, w_ref, zr_ref, zh_ref, zg_ref):
    tokens = jnp.where(pl.program_id(1) == 0, ctx_ref[0], x_ref[0])
    hb = _modulated_norm(tokens, nw_ref[...], sh_ref[0], sc_ref[0]).astype(BF16)
    n_r, n_h = zr_ref.shape[2], zh_ref.shape[2]
    zr_ref[0] = _bdot(hb, w_ref[:, :n_r])
    zh_ref[0] = _bdot(hb, w_ref[:, n_r:n_r + n_h])
    zg_ref[0] = _bdot(hb, w_ref[:, n_r + n_h:])


def _in_proj(ctx, x, nw, modcat, w_bf, n_r, n_h, n_g):
    b, t, d = x.shape
    tt = t + ctx.shape[1]
    nblk = tt // ROW_TILE
    mod_idx = lambda col: (lambda bi, i: (2 * bi + jnp.minimum(i, 1), 0, col))
    return pl.pallas_call(
        _in_proj_kernel,
        grid=(b, nblk),
        in_specs=[pl.BlockSpec((1, ROW_TILE, d), lambda bi, i: (bi, 0, 0)),
                  pl.BlockSpec((1, ROW_TILE, d), lambda bi, i: (bi, jnp.maximum(i - 1, 0), 0)),
                  pl.BlockSpec((1, d), lambda bi, i: (0, 0)),
                  pl.BlockSpec((1, 1, d), mod_idx(0)),
                  pl.BlockSpec((1, 1, d), mod_idx(1)),
                  pl.BlockSpec(w_bf.shape, lambda bi, i: (0, 0), pipeline_mode=pl.Buffered(1))],
        out_specs=[pl.BlockSpec((1, ROW_TILE, n), lambda bi, i: (bi, i, 0)) for n in (n_r, n_h, n_g)],
        out_shape=[jax.ShapeDtypeStruct((b, tt, n), F32) for n in (n_r, n_h, n_g)],
        compiler_params=_params("arbitrary", "arbitrary"),
        name="in_proj",
    )(ctx, x, nw, modcat, modcat, w_bf)


def _rw_prep_kernel(z_ref, zp_ref, zn_ref, mu_ref, w0_ref, wup_ref, a0_ref, aup_ref, gup_ref,
                    r_ref, k_ref, v_ref, wd_ref, ad_ref, g_ref):
    i = pl.program_id(1)
    nblk = pl.num_programs(1)
    tt = z_ref.shape[1]
    d = r_ref.shape[0]
    is_lat = i > 0
    row = lax.broadcasted_iota(jnp.int32, (tt, 1), 0)
    col = row % GRID_W
    lmask = jnp.where(is_lat, col, row) == 0
    rmask = jnp.where(is_lat, col, row - (tt - GRID_W)) == GRID_W - 1
    latf = is_lat.astype(F32)
    up_ok = (i > 1).astype(F32)
    dn_ok = (i < nblk - 1).astype(F32)

    def shifted(c0, c1):
        z = z_ref[0, :, c0:c1]
        mu = mu_ref[:, c0:c1]
        left = jnp.where(lmask, 0.0, pltpu.roll(z, 1, 0))
        right = jnp.where(rmask, 0.0, pltpu.roll(z, tt - 1, 0))
        up = jnp.concatenate([zp_ref[0, :, c0:c1] * up_ok, z[:tt - GRID_W]], axis=0)
        down = jnp.concatenate([z[GRID_W:], zn_ref[0, :, c0:c1] * dn_ok], axis=0)
        out = z + mu[0:1] * (left - z) + mu[1:2] * (right - z)
        return out + latf * (mu[2:3] * (up - z) + mu[3:4] * (down - z))

    r_ref[...] = shifted(0, d).T
    k_ref[...] = shifted(d, 2 * d).T
    v_ref[...] = shifted(2 * d, 3 * d).T
    rest = shifted(3 * d, z_ref.shape[2])
    xwa = rest[:, :LANES]
    xw_t = jnp.tanh(xwa).astype(BF16)
    xa_b = xwa.astype(BF16)
    for dr in range(2):
        u = -(w0_ref[dr:dr + 1, :] + _bdot(xw_t, wup_ref[dr]))
        softplus = jnp.maximum(u, 0.0) + jnp.log(1.0 + jnp.exp(-jnp.abs(u)))
        wd_ref[dr] = jnp.exp(-jnp.exp(-softplus - 0.5)).T
        ad_ref[dr] = _sigmoid(a0_ref[dr:dr + 1, :] + _bdot(xa_b, aup_ref[dr])).T
    g_ref[...] = _bdot(_sigmoid(rest[:, LANES:]).astype(BF16), gup_ref[...])


def _rw_prep(zr, mu, w0, wup, a0, aup, gup, d):
    b, tt, nr = zr.shape
    nblk = tt // ROW_TILE
    per = ROW_TILE // GRID_W
    last = tt // GRID_W - 1
    const = lambda shape: pl.BlockSpec(shape, lambda bi, i: (0,) * len(shape))
    row_spec = pl.BlockSpec((d, ROW_TILE), lambda bi, i: (bi, i))
    dir_spec = pl.BlockSpec((2, d, ROW_TILE), lambda bi, i: (0, bi, i))
    gate_spec = pl.BlockSpec((ROW_TILE, d), lambda bi, i: (i, bi))
    row_shape = jax.ShapeDtypeStruct((b * d, tt), F32)
    dir_shape = jax.ShapeDtypeStruct((2, b * d, tt), F32)
    gate_shape = jax.ShapeDtypeStruct((tt, b * d), F32)
    return pl.pallas_call(
        _rw_prep_kernel,
        grid=(b, nblk),
        in_specs=[pl.BlockSpec((1, ROW_TILE, nr), lambda bi, i: (bi, i, 0)),
                  pl.BlockSpec((1, GRID_W, nr), lambda bi, i: (bi, jnp.maximum(i * per - 1, 0), 0)),
                  pl.BlockSpec((1, GRID_W, nr), lambda bi, i: (bi, jnp.minimum(i * per + per, last), 0)),
                  const(mu.shape), const(w0.shape), const(wup.shape), const(a0.shape),
                  const(aup.shape), const(gup.shape)],
        out_specs=[row_spec, row_spec, row_spec, dir_spec, dir_spec, gate_spec],
        out_shape=[row_shape, row_shape, row_shape, dir_shape, dir_shape, gate_shape],
        compiler_params=_params("arbitrary", "arbitrary"),
        name="rw_prep",
    )(zr, zr, zr, mu, w0, wup, a0, aup, gup)


K_UNROLL = 16
ACC_WAYS = 1


def _tree_sum(xs):
    while len(xs) > 1:
        xs = [xs[i] + xs[i + 1] for i in range(0, len(xs) - 1, 2)] + ([xs[-1]] if len(xs) % 2 else [])
    return xs[0]


def _seq_block(n_ctx_blocks, n_blocks):
    def blk(d, i):
        bwd = jnp.where(i < n_ctx_blocks, n_ctx_blocks - 1 - i, n_blocks - 1 + n_ctx_blocks - i)
        return jnp.where(d == 0, i, bwd)
    return blk


def _rw_scan_kernel(*refs, backward):
    if backward:
        (r_ref, k_ref, v_ref, w_ref, a_ref, kkp_ref, kap_ref, yf_ref, af_ref, rkp_ref, lnw_ref, lnb_ref,
         y_ref, s_ref, kk_s, b_s, k_s, sa_s, kkd_s, rd_s) = refs
    else:
        (r_ref, k_ref, v_ref, w_ref, a_ref, kkp_ref, kap_ref, _order_ref,
         y_ref, s_ref, kk_s, b_s, k_s, sa_s, kkd_s, rd_s) = refs
    i = pl.program_id(0)
    tb = r_ref.shape[0]
    nv = RW_N // SUBLANES

    @pl.when(i == 0)
    def _():
        s_ref[...] = jnp.zeros_like(s_ref)

    def bcast(ref, *idx):
        k = idx[-1]
        row = ref[(*idx[:-1], pl.ds(k, 1), slice(None))]
        return jnp.broadcast_to(row, (SUBLANES, LANES))

    def time_index(s):
        s = jnp.minimum(s, tb - 1)
        return tb - 1 - s if backward else s

    def bf16_pair(x):
        hi = lax.bitcast_convert_type(x.astype(BF16).astype(F32), jnp.uint32)
        return lax.bitcast_convert_type(hi | (hi >> 16), F32)

    def prepare_kk(s, slot):
        kkr = k_ref[time_index(s)] * kkp_ref[...]
        nrm = jnp.sqrt(jnp.sum(kkr * kkr, axis=0, keepdims=True))
        kk = kkr / jnp.maximum(nrm, L2_EPS)
        kk_s[slot] = kk
        kkd_s[slot] = bf16_pair(kk)

    def prepare_bk(s, slot):
        t = time_index(s)
        a = a_ref[0, t]
        b_s[slot] = kk_s[slot] * a
        rd_s[slot] = bf16_pair(r_ref[t])
        k_s[slot] = k_ref[t] * (1.0 + (a - 1.0) * kap_ref[...])

    def state_dot(slot):
        acc = [[None] * nv for _ in range(ACC_WAYS)]
        for k in range(RW_N):
            kkb = bcast(kk_s, slot, k)
            part = acc[k % ACC_WAYS]
            for j in range(nv):
                p = s_ref[k, SUBLANES * j:SUBLANES * (j + 1), :] * kkb
                part[j] = p if part[j] is None else part[j] + p
        sa_s[...] = -jnp.concatenate([_tree_sum([acc[w][j] for w in range(ACC_WAYS)]) for j in range(nv)], axis=0)

    def sweep(s, slot):
        t = time_index(s)
        zero = jnp.zeros((SUBLANES, LANES), F32)

        def key_block(kblk, carry):
            yacc, acc = list(carry[0]), list(carry[1])
            yb = [jnp.zeros((2 * SUBLANES, LANES), BF16)] * (nv // 2)
            ab = [jnp.zeros((2 * SUBLANES, LANES), BF16)] * (nv // 2)
            for kk in range(K_UNROLL):
                k = kblk * K_UNROLL + kk
                wb = bcast(w_ref, 0, t, k)
                bb = bcast(b_s, slot, k)
                kb = bcast(k_s, slot, k)
                rb = pltpu.bitcast(bcast(rd_s, slot, k), BF16)
                kkn = pltpu.bitcast(bcast(kkd_s, 1 - slot, k), BF16)
                for m in range(nv // 2):
                    pair_rows = []
                    for j in (2 * m, 2 * m + 1):
                        rows = slice(SUBLANES * j, SUBLANES * (j + 1))
                        sn = s_ref[k, rows, :] * wb + (sa_s[rows, :] * bb + v_ref[t, rows, :] * kb)
                        s_ref[k, rows, :] = sn
                        pair_rows.append(sn)
                    snp = jnp.concatenate(pair_rows, axis=0).astype(BF16)
                    yb[m] = yb[m] + snp * rb
                    ab[m] = ab[m] + snp * kkn
            for m in range(nv // 2):
                y32 = yb[m].astype(F32)
                a32 = ab[m].astype(F32)
                for h, j in enumerate((2 * m, 2 * m + 1)):
                    yacc[j] = yacc[j] + y32[SUBLANES * h:SUBLANES * (h + 1)]
                    acc[j] = acc[j] + a32[SUBLANES * h:SUBLANES * (h + 1)]
            return yacc, acc

        init = [zero] * nv
        n_kblk = RW_N // K_UNROLL
        carry = lax.fori_loop(0, n_kblk - 1, key_block, (init, init))
        yacc, acc = key_block(n_kblk - 1, carry)
        sa_s[...] = -jnp.concatenate(acc, axis=0)
        y = jnp.concatenate(yacc, axis=0)
        if backward:
            y = y + yf_ref[t]
            yc = y - jnp.mean(y, axis=0, keepdims=True)
            var = jnp.mean(yc * yc, axis=0, keepdims=True)
            y = yc * lax.rsqrt(var + RW_GN_EPS) * lnw_ref[...] + lnb_ref[...]
            k_sum = k_ref[t] * (1.0 + (af_ref[0, t] - 1.0) * kap_ref[...]) + k_s[slot]
            y = y + jnp.sum(r_ref[t] * k_sum * rkp_ref[...], axis=0, keepdims=True) * v_ref[t]
        y_ref[t] = y

    for s0 in (0, 1):
        prepare_kk(s0, s0)
        prepare_bk(s0, s0)
    state_dot(0)

    def pair(p, carry):
        s = 2 * p
        sweep(s, 0)
        prepare_kk(s + 2, 0)
        prepare_bk(s + 2, 0)
        sweep(s + 1, 1)
        prepare_kk(s + 3, 1)
        prepare_bk(s + 3, 1)
        return carry

    lax.fori_loop(0, tb // 2, pair, 0)


def _rw_scan(r_t, k_t, v_t, w_t, a_t, kkp, kap, n_ctx_blocks, *, run_after=None, readout=None):
    backward = readout is not None
    direction = int(backward)
    tt = r_t.shape[0]
    nblk = tt // SCAN_TILE
    seq = _seq_block(n_ctx_blocks, nblk)
    blk = lambda i: seq(direction, i)
    tile = (SCAN_TILE, RW_N, LANES)
    shared = pl.BlockSpec(tile, lambda i: (blk(i), 0, 0))
    per_dir = pl.BlockSpec((1,) + tile, lambda i: (direction, blk(i), 0, 0))
    const = pl.BlockSpec((RW_N, LANES), lambda i: (0, 0))
    in_specs = [shared, shared, shared, per_dir, per_dir, const, const]
    args = [r_t, k_t, v_t, w_t, a_t, kkp, kap]
    if backward:
        y_fwd, rkp, lnw, lnb = readout
        fwd_rate = pl.BlockSpec((1,) + tile, lambda i: (0, blk(i), 0, 0))
        in_specs += [shared, fwd_rate, const, const, const]
        args += [y_fwd, a_t, rkp, lnw, lnb]
    else:
        in_specs += [pl.BlockSpec(memory_space=pl.ANY)]
        args += [run_after]
    return pl.pallas_call(
        functools.partial(_rw_scan_kernel, backward=backward),
        grid=(nblk,),
        in_specs=in_specs,
        out_specs=shared,
        out_shape=jax.ShapeDtypeStruct((tt, RW_N, LANES), F32),
        scratch_shapes=[pltpu.VMEM((RW_N, RW_N, LANES), F32),
                        pltpu.VMEM((2, RW_N, LANES), F32),
                        pltpu.VMEM((2, RW_N, LANES), F32),
                        pltpu.VMEM((2, RW_N, LANES), F32),
                        pltpu.VMEM((RW_N, LANES), F32),
                        pltpu.VMEM((2, RW_N, LANES), F32),
                        pltpu.VMEM((2, RW_N, LANES), F32)],
        compiler_params=_params("arbitrary"),
        name="rw_scan_bwd" if backward else "rw_scan_fwd",
    )(*args)


def _hg_kernel(q_ref, f_ref, i_ref, lbp_ref, tri_ref, o_ref, st_ref, qd_scr, kv_scr, dec_scr):
    d = pl.program_id(1)
    i = pl.program_id(2)
    n_chunks = q_ref.shape[1] // HG_CHUNK

    @pl.when(i == 0)
    def _():
        st_ref[...] = jnp.zeros_like(st_ref)

    lbp = lbp_ref[...]
    e = jnp.exp(lbp - jnp.max(lbp, axis=0, keepdims=True))
    lb = e[0:1] / jnp.sum(e, axis=0, keepdims=True)
    t_idx = lax.broadcasted_iota(jnp.int32, (HG_CHUNK, HG_CHUNK), 0)
    s_idx = lax.broadcasted_iota(jnp.int32, (HG_CHUNK, HG_CHUNK), 1)
    nt = (((1,), (1,)), ((), ()))
    tn = (((0,), (0,)), ((), ()))

    def run(fwd):
        mask = (s_idx <= t_idx) if fwd else (s_idx >= t_idx)
        mid_row = HG_CHUNK // 2 - 1 if fwd else HG_CHUNK // 2
        last_row = HG_CHUNK - 1 if fwd else 0
        order = list(range(n_chunks)) if fwd else list(reversed(range(n_chunks)))

        f = lb + (1.0 - lb) * _sigmoid(f_ref[0])
        lf = jnp.log(f)
        hi = lf.astype(BF16)
        lo = (lf - hi.astype(F32)).astype(BF16)
        tri = tri_ref[0]
        b = _bdot(tri, hi) + _bdot(tri, lo)
        for c in order:
            rows = slice(c * HG_CHUNK, (c + 1) * HG_CHUNK)
            bc = b[rows]
            b_mid = bc[mid_row:mid_row + 1]
            b_last = bc[last_row:last_row + 1]
            q_in = q_ref[0, rows, :] * jnp.exp(bc - b_mid)
            k_in = (1.0 - f[rows]) * jnp.exp(b_mid - bc)
            qd_scr[rows, :] = (q_in * jnp.exp(b_mid)).astype(BF16)
            k_dec = (k_in * jnp.exp(b_last - b_mid)).astype(BF16)
            dec_scr[c:c + 1, :] = jnp.exp(b_last)
            q_in = q_in.astype(BF16)
            k_in = k_in.astype(BF16)
            vb = i_ref[0, rows, :].astype(BF16)
            heads = [slice(h * HG_N, (h + 1) * HG_N) for h in range(HG_HEADS)]
            scores = [lax.dot_general(q_in[:, cols], k_in[:, cols], nt, preferred_element_type=F32)
                      for cols in heads]
            for h, cols in enumerate(heads):
                kv_scr[c, h] = lax.dot_general(vb[:, cols], k_dec[:, cols], tn, preferred_element_type=F32)
            for h, cols in enumerate(heads):
                o_ref[0, 0, rows, cols] = _bdot(jnp.where(mask, scores[h], 0.0).astype(BF16), vb[:, cols])
        for c in order:
            rows = slice(c * HG_CHUNK, (c + 1) * HG_CHUNK)
            for h in range(HG_HEADS):
                cols = slice(h * HG_N, (h + 1) * HG_N)
                st = st_ref[h]
                o_ref[0, 0, rows, cols] += lax.dot_general(qd_scr[rows, cols], st.astype(BF16), nt,
                                                           preferred_element_type=F32)
                st_ref[h] = dec_scr[c:c + 1, cols] * st + kv_scr[c, h]

    @pl.when(d == 0)
    def _():
        run(True)

    @pl.when(d == 1)
    def _():
        run(False)


def _hg_tri(n_rows):
    t = jnp.arange(n_rows)[:, None]
    s = jnp.arange(n_rows)[None, :]
    same = (t // HG_CHUNK) == (s // HG_CHUNK)
    return jnp.stack([same & (s <= t), same & (s >= t)]).astype(BF16)


def _hg(zh, hg_lb, n_ctx):
    b, tt, _ = zh.shape
    w = HG_HEADS * HG_N
    nblk = tt // ROW_TILE
    n_chunks = ROW_TILE // HG_CHUNK
    blk = _seq_block(n_ctx // ROW_TILE, nblk)
    return pl.pallas_call(
        _hg_kernel,
        grid=(b, 2, nblk),
        in_specs=[pl.BlockSpec((1, ROW_TILE, w), lambda bi, d, i: (bi, blk(d, i), 0)),
                  pl.BlockSpec((1, ROW_TILE, w), lambda bi, d, i: (bi, blk(d, i), 1 + d)),
                  pl.BlockSpec((1, ROW_TILE, w), lambda bi, d, i: (bi, blk(d, i), 3)),
                  pl.BlockSpec(hg_lb.shape, lambda bi, d, i: (0, 0)),
                  pl.BlockSpec((1, ROW_TILE, ROW_TILE), lambda bi, d, i: (d, 0, 0))],
        out_specs=pl.BlockSpec((1, 1, ROW_TILE, w), lambda bi, d, i: (d, bi, blk(d, i), 0)),
        out_shape=jax.ShapeDtypeStruct((2, b, tt, w), F32),
        scratch_shapes=[pltpu.VMEM((HG_HEADS, HG_N, HG_N), F32),
                        pltpu.VMEM((ROW_TILE, w), BF16),
                        pltpu.VMEM((n_chunks, HG_HEADS, HG_N, HG_N), F32),
                        pltpu.VMEM((SUBLANES, w), F32)],
        compiler_params=_params("arbitrary", "arbitrary", "arbitrary"),
        name="hg",
    )(zh, zh, zh, hg_lb, _hg_tri(ROW_TILE))


def _merge_kernel(yrw_ref, g_ref, o_ref, gz_ref, zg_ref, x_ref, g1_ref, hnw_ref,
                  pa_ref, pb_ref, wo_ref, out_ref):
    d = x_ref.shape[2]
    y_rw = (yrw_ref[...].T * g_ref[...]).astype(BF16)
    o = o_ref[0, 0] + o_ref[1, 0]
    gz = gz_ref[0]
    hnw = hnw_ref[...]
    parts = []
    for h in range(HG_HEADS):
        cols = slice(h * HG_N, (h + 1) * HG_N)
        oh = o[:, cols]
        ms = jnp.mean(oh * oh, axis=-1, keepdims=True)
        parts.append(oh * lax.rsqrt(ms + RMS_EPS) * hnw[:, cols])
    y_hg = (jnp.concatenate(parts, axis=1) * (gz * _sigmoid(gz))).astype(BF16)
    zg = zg_ref[0]
    m = _sigmoid(zg[:, :d]) * _bdot(y_rw, pa_ref[...]) + _sigmoid(zg[:, d:]) * _bdot(y_hg, pb_ref[...])
    out_ref[0] = x_ref[0] + g1_ref[0] * _bdot(m.astype(BF16), wo_ref[...])


def _merge(y_rw, g, o_hg, zh, zg, x, modcat, hnw, pa, pb, wo, n_ctx):
    b, t, d = x.shape
    off = n_ctx // ROW_TILE
    lat = lambda bi, i: (bi, i, 0)
    cat = lambda bi, i: (bi, i + off, 0)
    const = lambda shape: pl.BlockSpec(shape, lambda bi, i: (0,) * len(shape))
    return pl.pallas_call(
        _merge_kernel,
        grid=(b, t // ROW_TILE),
        in_specs=[pl.BlockSpec((d, ROW_TILE), lambda bi, i: (bi, i)),
                  pl.BlockSpec((ROW_TILE, d), lambda bi, i: (i + off, bi)),
                  pl.BlockSpec((2, 1, ROW_TILE, d), lambda bi, i: (0, bi, i + off, 0)),
                  pl.BlockSpec((1, ROW_TILE, d), lambda bi, i: (bi, i + off, 4)),
                  pl.BlockSpec((1, ROW_TILE, 2 * d), cat),
                  pl.BlockSpec((1, ROW_TILE, d), lat),
                  pl.BlockSpec((1, 1, d), lambda bi, i: (2 * bi + 1, 0, 2)),
                  const(hnw.shape), const(pa.shape), const(pb.shape), const(wo.shape)],
        out_specs=pl.BlockSpec((1, ROW_TILE, d), lat),
        out_shape=jax.ShapeDtypeStruct((b, t, d), F32),
        compiler_params=_params("arbitrary", "arbitrary"),
        name="merge",
    )(y_rw, g, o_hg, zh, zg, x, modcat, hnw, pa, pb, wo)


def _mlp_kernel(x_ref, nw_ref, sh_ref, sc_ref, g2_ref, w1_ref, w2_ref, fw_ref, out_ref):
    x = x_ref[0]
    hb = _modulated_norm(x, nw_ref[...], sh_ref[0], sc_ref[0]).astype(BF16)
    u = jnp.maximum(_bdot(hb, w1_ref[...]), 0.0)
    y = x + g2_ref[0] * _bdot((u * u).astype(BF16), w2_ref[...])
    ms = jnp.mean(y * y, axis=-1, keepdims=True)
    out_ref[0] = y * lax.rsqrt(ms + RMS_EPS) * fw_ref[...]


def _mlp(x1, nw, modcat, w1, w2, fw):
    b, t, d = x1.shape
    mod_idx = lambda col: (lambda bi, i: (2 * bi + 1, 0, col))
    const = lambda shape, **kw: pl.BlockSpec(shape, lambda bi, i: (0,) * len(shape), **kw)
    return pl.pallas_call(
        _mlp_kernel,
        grid=(b, t // ROW_TILE),
        in_specs=[pl.BlockSpec((1, ROW_TILE, d), lambda bi, i: (bi, i, 0)),
                  const(nw.shape),
                  pl.BlockSpec((1, 1, d), mod_idx(3)),
                  pl.BlockSpec((1, 1, d), mod_idx(4)),
                  pl.BlockSpec((1, 1, d), mod_idx(5)),
                  const(w1.shape, pipeline_mode=pl.Buffered(1)),
                  const(w2.shape, pipeline_mode=pl.Buffered(1)),
                  const(fw.shape)],
        out_specs=pl.BlockSpec((1, ROW_TILE, d), lambda bi, i: (bi, i, 0)),
        out_shape=jax.ShapeDtypeStruct((b, t, d), F32),
        compiler_params=_params("arbitrary", "arbitrary"),
        name="mlp",
    )(x1, nw, modcat, modcat, modcat, w1, w2, fw)


def _to_scan(a):
    *lead, w, t = a.shape
    return jnp.swapaxes(a.reshape(*lead, w // RW_N, RW_N, t), -1, -3)


def _from_scan(a):
    t, n, bh = a.shape
    return jnp.swapaxes(a, 0, 2).reshape(bh * n, t)


def _head_tile(p, b):
    return jnp.tile(p.reshape(RW_HEADS, RW_N).T, (1, b))


def kernel(x, c, ctx, c_ctx, norm1_w, norm2_w, w_mod, b_mod, w_in, rw_mu, rw_w0, rw_w_up, rw_a0, rw_a_up, rw_g_up, rw_k_k, rw_k_a, rw_r_k, rw_ln_w, rw_ln_b, hg_lb, hg_norm_w, p_a, p_b, w_out, w_fc1, w_fc2, final_norm_w):
    b, t, d = x.shape
    n_ctx = ctx.shape[1]
    assert w_mod.shape[0] == 1, "single-layer block"
    assert b * RW_HEADS == LANES and n_ctx == ROW_TILE and t % ROW_TILE == 0
    assert d == RW_HEADS * RW_N == HG_HEADS * HG_N
    n_r = rw_mu.shape[2]
    n_h = 5 * d
    n_g = 2 * d
    rank_w, rank_a = rw_w_up.shape[2], rw_a_up.shape[2]
    assert rank_w + rank_a == LANES and n_r == 3 * d + 2 * LANES

    c_rows = jnp.zeros((2 * SUBLANES, d), F32).at[:b].set(c).at[b].set(c_ctx)
    mod = _mod(c_rows, w_mod[0], b_mod)
    modcat = jnp.stack([jnp.broadcast_to(mod[b], (b, N_MOD * d)), mod[:b]], axis=1).reshape(2 * b, 1, N_MOD * d)

    zr, zh, zg = _in_proj(ctx, x, norm1_w, modcat, w_in[0].astype(BF16), n_r, n_h, n_g)

    wup = jnp.pad(rw_w_up[0], ((0, 0), (0, rank_a), (0, 0))).astype(BF16)
    aup = jnp.pad(rw_a_up[0], ((0, 0), (rank_w, 0), (0, 0))).astype(BF16)
    r, k_raw, v, wd, ad, g = _rw_prep(zr, rw_mu[0], rw_w0[0], wup, rw_a0[0], aup,
                                      rw_g_up[0].astype(BF16), d)

    r_t, k_t, v_t, w_t, a_t = (_to_scan(a) for a in (r, k_raw, v, wd, ad))
    o_hg = _hg(zh, hg_lb, n_ctx)
    kap = _head_tile(rw_k_a[0], b)
    scan_args = (r_t, k_t, v_t, w_t, a_t, _head_tile(rw_k_k[0], b), kap, n_ctx // SCAN_TILE)
    y_fwd = _rw_scan(*scan_args, run_after=o_hg)
    y_rw_t = _rw_scan(*scan_args, readout=(y_fwd, _head_tile(rw_r_k[0].reshape(-1), b),
                                           _head_tile(rw_ln_w[0], b), _head_tile(rw_ln_b[0], b)))
    y_rw = _from_scan(y_rw_t[n_ctx:])

    x1 = _merge(y_rw, g, o_hg, zh, zg, x, modcat, hg_norm_w, p_a[0].astype(BF16), p_b[0].astype(BF16),
                w_out[0].astype(BF16), n_ctx)
    return _mlp(x1, norm2_w, modcat, w_fc1[0].astype(BF16), w_fc2[0].astype(BF16),
                final_norm_w.reshape(1, d))
```

```python
import functools

import jax
import jax.numpy as jnp
from jax import lax
from jax.experimental import pallas as pl
from jax.experimental.pallas import tpu as pltpu

F32 = jnp.float32
BF16 = jnp.bfloat16

GRID_W = 64
RW_HEADS = 16
RW_N = 64
HG_HEADS = 8
HG_N = 128
HG_CHUNK = 64
N_MOD = 6
RMS_EPS = 1e-6
RW_GN_EPS = 64e-5
L2_EPS = 1e-12

SUBLANES = 8
LANES = 128
ROW_TILE = 256
SCAN_TILE = 64
VMEM_LIMIT = 56 * 1024 * 1024


def _params(*sem):
    return pltpu.CompilerParams(dimension_semantics=sem, vmem_limit_bytes=VMEM_LIMIT)


def _bdot(a, b):
    return jnp.dot(a, b, preferred_element_type=F32)


def _sigmoid(x):
    return 1.0 / (1.0 + jnp.exp(-x))


def _mod_kernel(c_ref, w_ref, b_ref, o_ref):
    c = c_ref[...]
    act = c * _sigmoid(c)
    o_ref[...] = _bdot(act.astype(BF16), w_ref[...].astype(BF16)) + b_ref[...]


def _mod(c_rows, w_mod, b_mod):
    rows, d = c_rows.shape
    n = w_mod.shape[1]
    return pl.pallas_call(
        _mod_kernel,
        grid=(n // d,),
        in_specs=[pl.BlockSpec((rows, d), lambda j: (0, 0)),
                  pl.BlockSpec((d, d), lambda j: (0, j)),
                  pl.BlockSpec((1, d), lambda j: (0, j))],
        out_specs=pl.BlockSpec((rows, d), lambda j: (0, j)),
        out_shape=jax.ShapeDtypeStruct((rows, n), F32),
        compiler_params=_params("arbitrary"),
        name="mod",
    )(c_rows, w_mod, b_mod)


def _modulated_norm(x, nw, sh, sc):
    ms = jnp.mean(x * x, axis=-1, keepdims=True)
    return (x * lax.rsqrt(ms + RMS_EPS) * nw) * (1.0 + sc) + sh


def _in_proj_kernel(ctx_ref, x_ref, nw_ref, sh_ref, sc_ref, w_ref, zr_ref, zh_ref, zg_ref):
    tokens = jnp.where(pl.program_id(1) == 0, ctx_ref[0], x_ref[0])
    hb = _modulated_norm(tokens, nw_ref[...], sh_ref[0], sc_ref[0]).astype(BF16)
    n_r, n_h = zr_ref.shape[2], zh_ref.shape[2]
    zr_ref[0] = _bdot(hb, w_ref[:, :n_r])
    zh_ref[0] = _bdot(hb, w_ref[:, n_r:n_r + n_h])
    zg_ref[0] = _bdot(hb, w_ref[:, n_r + n_h:])


def _in_proj(ctx, x, nw, modcat, w_bf, n_r, n_h, n_g):
    b, t, d = x.shape
    tt = t + ctx.shape[1]
    nblk = tt // ROW_TILE
    mod_idx = lambda col: (lambda bi, i: (2 * bi + jnp.minimum(i, 1), 0, col))
    return pl.pallas_call(
        _in_proj_kernel,
        grid=(b, nblk),
        in_specs=[pl.BlockSpec((1, ROW_TILE, d), lambda bi, i: (bi, 0, 0)),
                  pl.BlockSpec((1, ROW_TILE, d), lambda bi, i: (bi, jnp.maximum(i - 1, 0), 0)),
                  pl.BlockSpec((1, d), lambda bi, i: (0, 0)),
                  pl.BlockSpec((1, 1, d), mod_idx(0)),
                  pl.BlockSpec((1, 1, d), mod_idx(1)),
                  pl.BlockSpec(w_bf.shape, lambda bi, i: (0, 0), pipeline_mode=pl.Buffered(1))],
        out_specs=[pl.BlockSpec((1, ROW_TILE, n), lambda bi, i: (bi, i, 0)) for n in (n_r, n_h, n_g)],
        out_shape=[jax.ShapeDtypeStruct((b, tt, n), F32) for n in (n_r, n_h, n_g)],
        compiler_params=_params("arbitrary", "arbitrary"),
        name="in_proj",
    )(ctx, x, nw, modcat, modcat, w_bf)


def _rw_prep_kernel(z_ref, zp_ref, zn_ref, mu_ref, w0_ref, wup_ref, a0_ref, aup_ref, gup_ref,
                    r_ref, k_ref, v_ref, wd_ref, ad_ref, g_ref):
    i = pl.program_id(1)
    nblk = pl.num_programs(1)
    tt = z_ref.shape[1]
    d = r_ref.shape[0]
    is_lat = i > 0
    row = lax.broadcasted_iota(jnp.int32, (tt, 1), 0)
    col = row % GRID_W
    lmask = jnp.where(is_lat, col, row) == 0
    rmask = jnp.where(is_lat, col, row - (tt - GRID_W)) == GRID_W - 1
    latf = is_lat.astype(F32)
    up_ok = (i > 1).astype(F32)
    dn_ok = (i < nblk - 1).astype(F32)

    def shifted(c0, c1):
        z = z_ref[0, :, c0:c1]
        mu = mu_ref[:, c0:c1]
        left = jnp.where(lmask, 0.0, pltpu.roll(z, 1, 0))
        right = jnp.where(rmask, 0.0, pltpu.roll(z, tt - 1, 0))
        up = jnp.concatenate([zp_ref[0, :, c0:c1] * up_ok, z[:tt - GRID_W]], axis=0)
        down = jnp.concatenate([z[GRID_W:], zn_ref[0, :, c0:c1] * dn_ok], axis=0)
        out = z + mu[0:1] * (left - z) + mu[1:2] * (right - z)
        return out + latf * (mu[2:3] * (up - z) + mu[3:4] * (down - z))

    r_ref[...] = shifted(0, d).T
    k_ref[...] = shifted(d, 2 * d).T
    v_ref[...] = shifted(2 * d, 3 * d).T
    rest = shifted(3 * d, z_ref.shape[2])
    xwa = rest[:, :LANES]
    xw_t = jnp.tanh(xwa).astype(BF16)
    xa_b = xwa.astype(BF16)
    for dr in range(2):
        u = -(w0_ref[dr:dr + 1, :] + _bdot(xw_t, wup_ref[dr]))
        softplus = jnp.maximum(u, 0.0) + jnp.log(1.0 + jnp.exp(-jnp.abs(u)))
        wd_ref[dr] = jnp.exp(-jnp.exp(-softplus - 0.5)).T
        ad_ref[dr] = _sigmoid(a0_ref[dr:dr + 1, :] + _bdot(xa_b, aup_ref[dr])).T
    g_ref[...] = _bdot(_sigmoid(rest[:, LANES:]).astype(BF16), gup_ref[...])


def _rw_prep(zr, mu, w0, wup, a0, aup, gup, d):
    b, tt, nr = zr.shape
    nblk = tt // ROW_TILE
    per = ROW_TILE // GRID_W
    last = tt // GRID_W - 1
    const = lambda shape: pl.BlockSpec(shape, lambda bi, i: (0,) * len(shape))
    row_spec = pl.BlockSpec((d, ROW_TILE), lambda bi, i: (bi, i))
    dir_spec = pl.BlockSpec((2, d, ROW_TILE), lambda bi, i: (0, bi, i))
    gate_spec = pl.BlockSpec((ROW_TILE, d), lambda bi, i: (i, bi))
    row_shape = jax.ShapeDtypeStruct((b * d, tt), F32)
    dir_shape = jax.ShapeDtypeStruct((2, b * d, tt), F32)
    gate_shape = jax.ShapeDtypeStruct((tt, b * d), F32)
    return pl.pallas_call(
        _rw_prep_kernel,
        grid=(b, nblk),
        in_specs=[pl.BlockSpec((1, ROW_TILE, nr), lambda bi, i: (bi, i, 0)),
                  pl.BlockSpec((1, GRID_W, nr), lambda bi, i: (bi, jnp.maximum(i * per - 1, 0), 0)),
                  pl.BlockSpec((1, GRID_W, nr), lambda bi, i: (bi, jnp.minimum(i * per + per, last), 0)),
                  const(mu.shape), const(w0.shape), const(wup.shape), const(a0.shape),
                  const(aup.shape), const(gup.shape)],
        out_specs=[row_spec, row_spec, row_spec, dir_spec, dir_spec, gate_spec],
        out_shape=[row_shape, row_shape, row_shape, dir_shape, dir_shape, gate_shape],
        compiler_params=_params("arbitrary", "arbitrary"),
        name="rw_prep",
    )(zr, zr, zr, mu, w0, wup, a0, aup, gup)


K_UNROLL = 16
ACC_WAYS = 1


def _tree_sum(xs):
    while len(xs) > 1:
        xs = [xs[i] + xs[i + 1] for i in range(0, len(xs) - 1, 2)] + ([xs[-1]] if len(xs) % 2 else [])
    return xs[0]


def _seq_block(n_ctx_blocks, n_blocks):
    def blk(d, i):
        bwd = jnp.where(i < n_ctx_blocks, n_ctx_blocks - 1 - i, n_blocks - 1 + n_ctx_blocks - i)
        return jnp.where(d == 0, i, bwd)
    return blk


def _rw_scan_kernel(*refs, backward):
    if backward:
        (r_ref, k_ref, v_ref, w_ref, a_ref, kkp_ref, kap_ref, yf_ref, af_ref, rkp_ref, lnw_ref, lnb_ref,
         y_ref, s_ref, p_s, sa_s, kk0_s, bt_s, kt_s, rd_s, kkd_s) = refs
    else:
        (r_ref, k_ref, v_ref, w_ref, a_ref, kkp_ref, kap_ref, _order_ref,
         y_ref, s_ref, p_s, sa_s, kk0_s, bt_s, kt_s, rd_s, kkd_s) = refs
    i = pl.program_id(0)
    tb = r_ref.shape[0]
    nv = RW_N // SUBLANES

    @pl.when(i == 0)
    def _():
        s_ref[...] = jnp.zeros_like(s_ref)
        p_s[...] = jnp.ones_like(p_s)

    def bcast(ref, *idx):
        k = idx[-1]
        row = ref[(*idx[:-1], pl.ds(k, 1), slice(None))]
        return jnp.broadcast_to(row, (SUBLANES, LANES))

    def time_index(s):
        s = jnp.minimum(s, tb - 1)
        return tb - 1 - s if backward else s

    def bf16_pair(x):
        hi = lax.bitcast_convert_type(x.astype(BF16).astype(F32), jnp.uint32)
        return lax.bitcast_convert_type(hi | (hi >> 16), F32)

    def norm_key(t):
        kkr = k_ref[t] * kkp_ref[...]
        nrm = jnp.sqrt(jnp.sum(kkr * kkr, axis=0, keepdims=True))
        return kkr / jnp.maximum(nrm, L2_EPS)

    def scaled_key(t, a):
        return k_ref[t] * (1.0 + (a - 1.0) * kap_ref[...])

    def prepare(s, slot):
        t = time_index(s)
        a = a_ref[0, t]
        kk = norm_key(t)
        p_prev = p_s[...]
        kkd_s[1 - slot] = bf16_pair(p_prev * kk)
        p = p_prev * jnp.where(s < tb, w_ref[0, t], 1.0)
        p_s[...] = p
        inv_p = 1.0 / p
        bt_s[slot] = kk * a * inv_p
        kt_s[slot] = scaled_key(t, a) * inv_p
        rd_s[slot] = bf16_pair(p * r_ref[t])

    def restart():
        kk0_s[...] = norm_key(time_index(0))
        acc = [None] * nv
        for k in range(RW_N):
            pb = bcast(p_s, k)
            kkb = bcast(kk0_s, k)
            for j in range(nv):
                rows = slice(SUBLANES * j, SUBLANES * (j + 1))
                sn = s_ref[k, rows, :] * pb
                s_ref[k, rows, :] = sn
                acc[j] = sn * kkb if acc[j] is None else acc[j] + sn * kkb
        sa_s[...] = -jnp.concatenate(acc, axis=0)
        p_s[...] = jnp.ones_like(p_s)

    def sweep(s, slot):
        t = time_index(s)
        zero = jnp.zeros((SUBLANES, LANES), F32)

        def key_block(kblk, carry):
            yacc, acc = list(carry[0]), list(carry[1])
            yb = [jnp.zeros((2 * SUBLANES, LANES), BF16)] * (nv // 2)
            ab = [jnp.zeros((2 * SUBLANES, LANES), BF16)] * (nv // 2)
            for kk in range(K_UNROLL):
                k = kblk * K_UNROLL + kk
                bb = bcast(bt_s, slot, k)
                kb = bcast(kt_s, slot, k)
                rb = pltpu.bitcast(bcast(rd_s, slot, k), BF16)
                kkn = pltpu.bitcast(bcast(kkd_s, slot, k), BF16)
                for m in range(nv // 2):
                    pair_rows = []
                    for j in (2 * m, 2 * m + 1):
                        rows = slice(SUBLANES * j, SUBLANES * (j + 1))
                        sn = s_ref[k, rows, :] + (sa_s[rows, :] * bb + v_ref[t, rows, :] * kb)
                        s_ref[k, rows, :] = sn
                        pair_rows.append(sn)
                    snp = jnp.concatenate(pair_rows, axis=0).astype(BF16)
                    yb[m] = yb[m] + snp * rb
                    ab[m] = ab[m] + snp * kkn
            for m in range(nv // 2):
                y32 = yb[m].astype(F32)
                a32 = ab[m].astype(F32)
                for h, j in enumerate((2 * m, 2 * m + 1)):
                    yacc[j] = yacc[j] + y32[SUBLANES * h:SUBLANES * (h + 1)]
                    acc[j] = acc[j] + a32[SUBLANES * h:SUBLANES * (h + 1)]
            return yacc, acc

        init = [zero] * nv
        n_kblk = RW_N // K_UNROLL
        carry = lax.fori_loop(0, n_kblk - 1, key_block, (init, init))
        yacc, acc = key_block(n_kblk - 1, carry)
        sa_s[...] = -jnp.concatenate(acc, axis=0)
        y = jnp.concatenate(yacc, axis=0)
        if backward:
            y = y + yf_ref[t]
            yc = y - jnp.mean(y, axis=0, keepdims=True)
            var = jnp.mean(yc * yc, axis=0, keepdims=True)
            y = yc * lax.rsqrt(var + RW_GN_EPS) * lnw_ref[...] + lnb_ref[...]
            k_sum = scaled_key(t, af_ref[0, t]) + scaled_key(t, a_ref[0, t])
            y = y + jnp.sum(r_ref[t] * k_sum * rkp_ref[...], axis=0, keepdims=True) * v_ref[t]
        y_ref[t] = y

    restart()
    prepare(0, 0)
    prepare(1, 1)

    def pair(p, carry):
        s = 2 * p
        sweep(s, 0)
        prepare(s + 2, 0)
        sweep(s + 1, 1)
        prepare(s + 3, 1)
        return carry

    lax.fori_loop(0, tb // 2, pair, 0)


def _rw_scan(r_t, k_t, v_t, w_t, a_t, kkp, kap, n_ctx_blocks, *, run_after=None, readout=None):
    backward = readout is not None
    direction = int(backward)
    tt = r_t.shape[0]
    nblk = tt // SCAN_TILE
    seq = _seq_block(n_ctx_blocks, nblk)
    blk = lambda i: seq(direction, i)
    tile = (SCAN_TILE, RW_N, LANES)
    shared = pl.BlockSpec(tile, lambda i: (blk(i), 0, 0))
    per_dir = pl.BlockSpec((1,) + tile, lambda i: (direction, blk(i), 0, 0))
    const = pl.BlockSpec((RW_N, LANES), lambda i: (0, 0))
    y_blk = lambda i: jnp.where(i < n_ctx_blocks, blk(n_ctx_blocks), blk(i)) - n_ctx_blocks
    y_spec = pl.BlockSpec(tile, lambda i: (y_blk(i), 0, 0))
    in_specs = [shared, shared, shared, per_dir, per_dir, const, const]
    args = [r_t, k_t, v_t, w_t, a_t, kkp, kap]
    if backward:
        y_fwd, rkp, lnw, lnb = readout
        fwd_rate = pl.BlockSpec((1,) + tile, lambda i: (0, blk(i), 0, 0))
        in_specs += [y_spec, fwd_rate, const, const, const]
        args += [y_fwd, a_t, rkp, lnw, lnb]
    else:
        in_specs += [pl.BlockSpec(memory_space=pl.ANY)]
        args += [run_after]
    return pl.pallas_call(
        functools.partial(_rw_scan_kernel, backward=backward),
        grid=(nblk,),
        in_specs=in_specs,
        out_specs=y_spec,
        out_shape=jax.ShapeDtypeStruct((tt - n_ctx_blocks * SCAN_TILE, RW_N, LANES), F32),
        scratch_shapes=[pltpu.VMEM((RW_N, RW_N, LANES), F32)] + [pltpu.VMEM((RW_N, LANES), F32)] * 3
                       + [pltpu.VMEM((2, RW_N, LANES), F32)] * 4,
        compiler_params=_params("arbitrary"),
        name="rw_scan_bwd" if backward else "rw_scan_fwd",
    )(*args)


def _hg_kernel(q_ref, f_ref, i_ref, lbp_ref, tri_ref, o_ref, st_ref, qd_scr, kv_scr, dec_scr):
    d = pl.program_id(1)
    i = pl.program_id(2)
    n_chunks = q_ref.shape[1] // HG_CHUNK

    @pl.when(i == 0)
    def _():
        st_ref[...] = jnp.zeros_like(st_ref)

    lbp = lbp_ref[...]
    e = jnp.exp(lbp - jnp.max(lbp, axis=0, keepdims=True))
    lb = e[0:1] / jnp.sum(e, axis=0, keepdims=True)
    t_idx = lax.broadcasted_iota(jnp.int32, (HG_CHUNK, HG_CHUNK), 0)
    s_idx = lax.broadcasted_iota(jnp.int32, (HG_CHUNK, HG_CHUNK), 1)
    nt = (((1,), (1,)), ((), ()))
    tn = (((0,), (0,)), ((), ()))

    def run(fwd):
        mask = (s_idx <= t_idx) if fwd else (s_idx >= t_idx)
        mid_row = HG_CHUNK // 2 - 1 if fwd else HG_CHUNK // 2
        last_row = HG_CHUNK - 1 if fwd else 0
        order = list(range(n_chunks)) if fwd else list(reversed(range(n_chunks)))

        f = lb + (1.0 - lb) * _sigmoid(f_ref[0])
        lf = jnp.log(f)
        hi = lf.astype(BF16)
        lo = (lf - hi.astype(F32)).astype(BF16)
        tri = tri_ref[0]
        b = _bdot(tri, hi) + _bdot(tri, lo)
        for c in order:
            rows = slice(c * HG_CHUNK, (c + 1) * HG_CHUNK)
            bc = b[rows]
            b_mid = bc[mid_row:mid_row + 1]
            b_last = bc[last_row:last_row + 1]
            q_in = q_ref[0, rows, :] * jnp.exp(bc - b_mid)
            k_in = (1.0 - f[rows]) * jnp.exp(b_mid - bc)
            qd_scr[rows, :] = (q_in * jnp.exp(b_mid)).astype(BF16)
            k_dec = (k_in * jnp.exp(b_last - b_mid)).astype(BF16)
            dec_scr[c:c + 1, :] = jnp.exp(b_last)
            q_in = q_in.astype(BF16)
            k_in = k_in.astype(BF16)
            vb = i_ref[0, rows, :].astype(BF16)
            heads = [slice(h * HG_N, (h + 1) * HG_N) for h in range(HG_HEADS)]
            scores = [lax.dot_general(q_in[:, cols], k_in[:, cols], nt, preferred_element_type=F32)
                      for cols in heads]
            for h, cols in enumerate(heads):
                kv_scr[c, h] = lax.dot_general(vb[:, cols], k_dec[:, cols], tn, preferred_element_type=F32)
            for h, cols in enumerate(heads):
                o_ref[0, 0, rows, cols] = _bdot(jnp.where(mask, scores[h], 0.0).astype(BF16), vb[:, cols])
        for c in order:
            rows = slice(c * HG_CHUNK, (c + 1) * HG_CHUNK)
            for h in range(HG_HEADS):
                cols = slice(h * HG_N, (h + 1) * HG_N)
                st = st_ref[h]
                o_ref[0, 0, rows, cols] += lax.dot_general(qd_scr[rows, cols], st.astype(BF16), nt,
                                                           preferred_element_type=F32)
                st_ref[h] = dec_scr[c:c + 1, cols] * st + kv_scr[c, h]

    @pl.when(d == 0)
    def _():
        run(True)

    @pl.when(d == 1)
    def _():
        run(False)


def _hg_tri(n_rows):
    t = jnp.arange(n_rows)[:, None]
    s = jnp.arange(n_rows)[None, :]
    same = (t // HG_CHUNK) == (s // HG_CHUNK)
    return jnp.stack([same & (s <= t), same & (s >= t)]).astype(BF16)


def _hg(zh, hg_lb, n_ctx):
    b, tt, _ = zh.shape
    w = HG_HEADS * HG_N
    nblk = tt // ROW_TILE
    n_chunks = ROW_TILE // HG_CHUNK
    blk = _seq_block(n_ctx // ROW_TILE, nblk)
    return pl.pallas_call(
        _hg_kernel,
        grid=(b, 2, nblk),
        in_specs=[pl.BlockSpec((1, ROW_TILE, w), lambda bi, d, i: (bi, blk(d, i), 0)),
                  pl.BlockSpec((1, ROW_TILE, w), lambda bi, d, i: (bi, blk(d, i), 1 + d)),
                  pl.BlockSpec((1, ROW_TILE, w), lambda bi, d, i: (bi, blk(d, i), 3)),
                  pl.BlockSpec(hg_lb.shape, lambda bi, d, i: (0, 0)),
                  pl.BlockSpec((1, ROW_TILE, ROW_TILE), lambda bi, d, i: (d, 0, 0))],
        out_specs=pl.BlockSpec((1, 1, ROW_TILE, w), lambda bi, d, i: (d, bi, blk(d, i), 0)),
        out_shape=jax.ShapeDtypeStruct((2, b, tt, w), F32),
        scratch_shapes=[pltpu.VMEM((HG_HEADS, HG_N, HG_N), F32),
                        pltpu.VMEM((ROW_TILE, w), BF16),
                        pltpu.VMEM((n_chunks, HG_HEADS, HG_N, HG_N), F32),
                        pltpu.VMEM((SUBLANES, w), F32)],
        compiler_params=_params("arbitrary", "arbitrary", "arbitrary"),
        name="hg",
    )(zh, zh, zh, hg_lb, _hg_tri(ROW_TILE))


def _merge_kernel(yrw_ref, g_ref, o_ref, gz_ref, zg_ref, x_ref, g1_ref, hnw_ref,
                  pa_ref, pb_ref, wo_ref, out_ref):
    d = x_ref.shape[2]
    y_rw = (yrw_ref[...].T * g_ref[...]).astype(BF16)
    o = o_ref[0, 0] + o_ref[1, 0]
    gz = gz_ref[0]
    hnw = hnw_ref[...]
    parts = []
    for h in range(HG_HEADS):
        cols = slice(h * HG_N, (h + 1) * HG_N)
        oh = o[:, cols]
        ms = jnp.mean(oh * oh, axis=-1, keepdims=True)
        parts.append(oh * lax.rsqrt(ms + RMS_EPS) * hnw[:, cols])
    y_hg = (jnp.concatenate(parts, axis=1) * (gz * _sigmoid(gz))).astype(BF16)
    zg = zg_ref[0]
    m = _sigmoid(zg[:, :d]) * _bdot(y_rw, pa_ref[...]) + _sigmoid(zg[:, d:]) * _bdot(y_hg, pb_ref[...])
    out_ref[0] = x_ref[0] + g1_ref[0] * _bdot(m.astype(BF16), wo_ref[...])


def _merge(y_rw, g, o_hg, zh, zg, x, modcat, hnw, pa, pb, wo, n_ctx):
    b, t, d = x.shape
    off = n_ctx // ROW_TILE
    lat = lambda bi, i: (bi, i, 0)
    cat = lambda bi, i: (bi, i + off, 0)
    const = lambda shape: pl.BlockSpec(shape, lambda bi, i: (0,) * len(shape))
    return pl.pallas_call(
        _merge_kernel,
        grid=(b, t // ROW_TILE),
        in_specs=[pl.BlockSpec((d, ROW_TILE), lambda bi, i: (bi, i)),
                  pl.BlockSpec((ROW_TILE, d), lambda bi, i: (i + off, bi)),
                  pl.BlockSpec((2, 1, ROW_TILE, d), lambda bi, i: (0, bi, i + off, 0)),
                  pl.BlockSpec((1, ROW_TILE, d), lambda bi, i: (bi, i + off, 4)),
                  pl.BlockSpec((1, ROW_TILE, 2 * d), cat),
                  pl.BlockSpec((1, ROW_TILE, d), lat),
                  pl.BlockSpec((1, 1, d), lambda bi, i: (2 * bi + 1, 0, 2)),
                  const(hnw.shape), const(pa.shape), const(pb.shape), const(wo.shape)],
        out_specs=pl.BlockSpec((1, ROW_TILE, d), lat),
        out_shape=jax.ShapeDtypeStruct((b, t, d), F32),
        compiler_params=_params("arbitrary", "arbitrary"),
        name="merge",
    )(y_rw, g, o_hg, zh, zg, x, modcat, hnw, pa, pb, wo)


def _mlp_kernel(x_ref, nw_ref, sh_ref, sc_ref, g2_ref, w1_ref, w2_ref, fw_ref, out_ref):
    x = x_ref[0]
    hb = _modulated_norm(x, nw_ref[...], sh_ref[0], sc_ref[0]).astype(BF16)
    u = jnp.maximum(_bdot(hb, w1_ref[...]), 0.0)
    y = x + g2_ref[0] * _bdot((u * u).astype(BF16), w2_ref[...])
    ms = jnp.mean(y * y, axis=-1, keepdims=True)
    out_ref[0] = y * lax.rsqrt(ms + RMS_EPS) * fw_ref[...]


def _mlp(x1, nw, modcat, w1, w2, fw):
    b, t, d = x1.shape
    mod_idx = lambda col: (lambda bi, i: (2 * bi + 1, 0, col))
    const = lambda shape, **kw: pl.BlockSpec(shape, lambda bi, i: (0,) * len(shape), **kw)
    return pl.pallas_call(
        _mlp_kernel,
        grid=(b, t // ROW_TILE),
        in_specs=[pl.BlockSpec((1, ROW_TILE, d), lambda bi, i: (bi, i, 0)),
                  const(nw.shape),
                  pl.BlockSpec((1, 1, d), mod_idx(3)),
                  pl.BlockSpec((1, 1, d), mod_idx(4)),
                  pl.BlockSpec((1, 1, d), mod_idx(5)),
                  const(w1.shape, pipeline_mode=pl.Buffered(1)),
                  const(w2.shape, pipeline_mode=pl.Buffered(1)),
                  const(fw.shape)],
        out_specs=pl.BlockSpec((1, ROW_TILE, d), lambda bi, i: (bi, i, 0)),
        out_shape=jax.ShapeDtypeStruct((b, t, d), F32),
        compiler_params=_params("arbitrary", "arbitrary"),
        name="mlp",
    )(x1, nw, modcat, modcat, modcat, w1, w2, fw)


def _to_scan(a):
    *lead, w, t = a.shape
    return jnp.swapaxes(a.reshape(*lead, w // RW_N, RW_N, t), -1, -3)


def _from_scan(a):
    t, n, bh = a.shape
    return jnp.swapaxes(a, 0, 2).reshape(bh * n, t)


def _head_tile(p, b):
    return jnp.tile(p.reshape(RW_HEADS, RW_N).T, (1, b))


def kernel(x, c, ctx, c_ctx, norm1_w, norm2_w, w_mod, b_mod, w_in, rw_mu, rw_w0, rw_w_up, rw_a0, rw_a_up, rw_g_up, rw_k_k, rw_k_a, rw_r_k, rw_ln_w, rw_ln_b, hg_lb, hg_norm_w, p_a, p_b, w_out, w_fc1, w_fc2, final_norm_w):
    b, t, d = x.shape
    n_ctx = ctx.shape[1]
    assert w_mod.shape[0] == 1, "single-layer block"
    assert b * RW_HEADS == LANES and n_ctx == ROW_TILE and t % ROW_TILE == 0
    assert d == RW_HEADS * RW_N == HG_HEADS * HG_N
    n_r = rw_mu.shape[2]
    n_h = 5 * d
    n_g = 2 * d
    rank_w, rank_a = rw_w_up.shape[2], rw_a_up.shape[2]
    assert rank_w + rank_a == LANES and n_r == 3 * d + 2 * LANES

    c_rows = jnp.zeros((2 * SUBLANES, d), F32).at[:b].set(c).at[b].set(c_ctx)
    mod = _mod(c_rows, w_mod[0], b_mod)
    modcat = jnp.stack([jnp.broadcast_to(mod[b], (b, N_MOD * d)), mod[:b]], axis=1).reshape(2 * b, 1, N_MOD * d)

    zr, zh, zg = _in_proj(ctx, x, norm1_w, modcat, w_in[0].astype(BF16), n_r, n_h, n_g)

    wup = jnp.pad(rw_w_up[0], ((0, 0), (0, rank_a), (0, 0))).astype(BF16)
    aup = jnp.pad(rw_a_up[0], ((0, 0), (rank_w, 0), (0, 0))).astype(BF16)
    r, k_raw, v, wd, ad, g = _rw_prep(zr, rw_mu[0], rw_w0[0], wup, rw_a0[0], aup,
                                      rw_g_up[0].astype(BF16), d)

    r_t, k_t, v_t, w_t, a_t = (_to_scan(a) for a in (r, k_raw, v, wd, ad))
    o_hg = _hg(zh, hg_lb, n_ctx)
    kap = _head_tile(rw_k_a[0], b)
    scan_args = (r_t, k_t, v_t, w_t, a_t, _head_tile(rw_k_k[0], b), kap, n_ctx // SCAN_TILE)
    y_fwd = _rw_scan(*scan_args, run_after=o_hg)
    y_rw_t = _rw_scan(*scan_args, readout=(y_fwd, _head_tile(rw_r_k[0].reshape(-1), b),
                                           _head_tile(rw_ln_w[0], b), _head_tile(rw_ln_b[0], b)))
    y_rw = _from_scan(y_rw_t)

    x1 = _merge(y_rw, g, o_hg, zh, zg, x, modcat, hg_norm_w, p_a[0].astype(BF16), p_b[0].astype(BF16),
                w_out[0].astype(BF16), n_ctx)
    return _mlp(x1, norm2_w, modcat, w_fc1[0].astype(BF16), w_fc2[0].astype(BF16),
                final_norm_w.reshape(1, d))
```

```python
import functools

import jax
import jax.numpy as jnp
from jax import lax
from jax.experimental import pallas as pl
from jax.experimental.pallas import tpu as pltpu

F32 = jnp.float32
BF16 = jnp.bfloat16

GRID_W = 64
RW_HEADS = 16
RW_N = 64
HG_HEADS = 8
HG_N = 128
HG_CHUNK = 64
N_MOD = 6
RMS_EPS = 1e-6
RW_GN_EPS = 64e-5
L2_EPS = 1e-12

SUBLANES = 8
LANES = 128
ROW_TILE = 256
SCAN_TILE = 64
VMEM_LIMIT = 56 * 1024 * 1024


def _params(*sem):
    return pltpu.CompilerParams(dimension_semantics=sem, vmem_limit_bytes=VMEM_LIMIT)


def _bdot(a, b):
    return jnp.dot(a, b, preferred_element_type=F32)


def _sigmoid(x):
    return 1.0 / (1.0 + jnp.exp(-x))


def _mod_kernel(c_ref, w_ref, b_ref, o_ref):
    c = c_ref[...]
    act = c * _sigmoid(c)
    o_ref[...] = _bdot(act.astype(BF16), w_ref[...].astype(BF16)) + b_ref[...]


def _mod(c_rows, w_mod, b_mod):
    rows, d = c_rows.shape
    n = w_mod.shape[1]
    return pl.pallas_call(
        _mod_kernel,
        grid=(n // d,),
        in_specs=[pl.BlockSpec((rows, d), lambda j: (0, 0)),
                  pl.BlockSpec((d, d), lambda j: (0, j)),
                  pl.BlockSpec((1, d), lambda j: (0, j))],
        out_specs=pl.BlockSpec((rows, d), lambda j: (0, j)),
        out_shape=jax.ShapeDtypeStruct((rows, n), F32),
        compiler_params=_params("arbitrary"),
        name="mod",
    )(c_rows, w_mod, b_mod)


def _modulated_norm(x, nw, sh, sc):
    ms = jnp.mean(x * x, axis=-1, keepdims=True)
    return (x * lax.rsqrt(ms + RMS_EPS) * nw) * (1.0 + sc) + sh


def _in_proj_kernel(ctx_ref, x_ref, nw_ref, sh_ref, sc_ref, w_ref, zh_ref, zg_ref):
    tokens = jnp.where(pl.program_id(1) == 0, ctx_ref[0], x_ref[0])
    hb = _modulated_norm(tokens, nw_ref[...], sh_ref[0], sc_ref[0]).astype(BF16)
    n_h = zh_ref.shape[2]
    zh_ref[0] = _bdot(hb, w_ref[:, :n_h])
    zg_ref[0] = _bdot(hb, w_ref[:, n_h:])


def _in_proj(ctx, x, nw, modcat, w_bf, n_h, n_g):
    b, t, d = x.shape
    tt = t + ctx.shape[1]
    nblk = tt // ROW_TILE
    mod_idx = lambda col: (lambda bi, i: (2 * bi + jnp.minimum(i, 1), 0, col))
    return pl.pallas_call(
        _in_proj_kernel,
        grid=(b, nblk),
        in_specs=[pl.BlockSpec((1, ROW_TILE, d), lambda bi, i: (bi, 0, 0)),
                  pl.BlockSpec((1, ROW_TILE, d), lambda bi, i: (bi, jnp.maximum(i - 1, 0), 0)),
                  pl.BlockSpec((1, d), lambda bi, i: (0, 0)),
                  pl.BlockSpec((1, 1, d), mod_idx(0)),
                  pl.BlockSpec((1, 1, d), mod_idx(1)),
                  pl.BlockSpec(w_bf.shape, lambda bi, i: (0, 0), pipeline_mode=pl.Buffered(1))],
        out_specs=[pl.BlockSpec((1, ROW_TILE, n), lambda bi, i: (bi, i, 0)) for n in (n_h, n_g)],
        out_shape=[jax.ShapeDtypeStruct((b, tt, n), F32) for n in (n_h, n_g)],
        compiler_params=_params("arbitrary", "arbitrary"),
        name="in_proj",
    )(ctx, x, nw, modcat, modcat, w_bf)


def _rw_prep_kernel(ctx_ref, x_ref, xp_ref, xn_ref, nw_ref, sh_ref, sc_ref, w_ref, mu_ref, w0_ref, wup_ref,
                    a0_ref, aup_ref, gup_ref, r_ref, k_ref, v_ref, wd_ref, ad_ref, g_ref):
    i = pl.program_id(1)
    nblk = pl.num_programs(1)
    tt = x_ref.shape[1]
    d = r_ref.shape[0]
    is_lat = i > 0
    row = lax.broadcasted_iota(jnp.int32, (tt, 1), 0)
    col = row % GRID_W
    lmask = jnp.where(is_lat, col, row) == 0
    rmask = jnp.where(is_lat, col, row - (tt - GRID_W)) == GRID_W - 1
    latf = is_lat.astype(F32)
    up_ok = (i > 1).astype(F32)
    dn_ok = (i < nblk - 1).astype(F32)

    norm = lambda tok: _modulated_norm(tok, nw_ref[...], sh_ref[0], sc_ref[0]).astype(BF16)
    tokens = jnp.where(is_lat, x_ref[0], ctx_ref[0])
    hb = jnp.concatenate([norm(xp_ref[0]), norm(tokens), norm(xn_ref[0])], axis=0)

    def shifted(c0, c1):
        ze = _bdot(hb, w_ref[:, c0:c1])
        z = ze[GRID_W:GRID_W + tt]
        mu = mu_ref[:, c0:c1]
        left = jnp.where(lmask, 0.0, pltpu.roll(z, 1, 0))
        right = jnp.where(rmask, 0.0, pltpu.roll(z, tt - 1, 0))
        up = jnp.concatenate([ze[:GRID_W] * up_ok, ze[GRID_W:tt]], axis=0)
        down = jnp.concatenate([ze[2 * GRID_W:GRID_W + tt], ze[GRID_W + tt:] * dn_ok], axis=0)
        out = z + mu[0:1] * (left - z) + mu[1:2] * (right - z)
        return out + latf * (mu[2:3] * (up - z) + mu[3:4] * (down - z))

    r_ref[...] = shifted(0, d).T
    k_ref[...] = shifted(d, 2 * d).T
    v_ref[...] = shifted(2 * d, 3 * d).T
    rest = shifted(3 * d, w_ref.shape[1])
    xwa = rest[:, :LANES]
    xw_t = jnp.tanh(xwa).astype(BF16)
    xa_b = xwa.astype(BF16)
    for dr in range(2):
        u = -(w0_ref[dr:dr + 1, :] + _bdot(xw_t, wup_ref[dr]))
        softplus = jnp.maximum(u, 0.0) + jnp.log(1.0 + jnp.exp(-jnp.abs(u)))
        wd_ref[dr] = jnp.exp(-jnp.exp(-softplus - 0.5)).T
        ad_ref[dr] = _sigmoid(a0_ref[dr:dr + 1, :] + _bdot(xa_b, aup_ref[dr])).T
    g_ref[...] = _bdot(_sigmoid(rest[:, LANES:]).astype(BF16), gup_ref[...])


def _rw_prep(ctx, x, nw, modcat, w_r, mu, w0, wup, a0, aup, gup):
    b, t, d = x.shape
    tt = t + ctx.shape[1]
    nblk = tt // ROW_TILE
    per = ROW_TILE // GRID_W
    last = t // GRID_W - 1
    halo = lambda off: (lambda bi, i: (bi, jnp.clip((i - 1) * per + off, 0, last), 0))
    mod_idx = lambda col: (lambda bi, i: (2 * bi + jnp.minimum(i, 1), 0, col))
    const = lambda shape, **kw: pl.BlockSpec(shape, lambda bi, i: (0,) * len(shape), **kw)
    row_spec = pl.BlockSpec((d, ROW_TILE), lambda bi, i: (bi, i))
    dir_spec = pl.BlockSpec((2, d, ROW_TILE), lambda bi, i: (0, bi, i))
    gate_spec = pl.BlockSpec((ROW_TILE, d), lambda bi, i: (i, bi))
    row_shape = jax.ShapeDtypeStruct((b * d, tt), F32)
    dir_shape = jax.ShapeDtypeStruct((2, b * d, tt), F32)
    gate_shape = jax.ShapeDtypeStruct((tt, b * d), F32)
    return pl.pallas_call(
        _rw_prep_kernel,
        grid=(b, nblk),
        in_specs=[pl.BlockSpec((1, ROW_TILE, d), lambda bi, i: (bi, 0, 0)),
                  pl.BlockSpec((1, ROW_TILE, d), lambda bi, i: (bi, jnp.maximum(i - 1, 0), 0)),
                  pl.BlockSpec((1, GRID_W, d), halo(-1)),
                  pl.BlockSpec((1, GRID_W, d), halo(per)),
                  const(nw.shape),
                  pl.BlockSpec((1, 1, d), mod_idx(0)),
                  pl.BlockSpec((1, 1, d), mod_idx(1)),
                  const(w_r.shape, pipeline_mode=pl.Buffered(1)),
                  const(mu.shape), const(w0.shape), const(wup.shape), const(a0.shape),
                  const(aup.shape), const(gup.shape)],
        out_specs=[row_spec, row_spec, row_spec, dir_spec, dir_spec, gate_spec],
        out_shape=[row_shape, row_shape, row_shape, dir_shape, dir_shape, gate_shape],
        compiler_params=_params("arbitrary", "arbitrary"),
        name="rw_prep",
    )(ctx, x, x, x, nw, modcat, modcat, w_r, mu, w0, wup, a0, aup, gup)


K_UNROLL = 16
ACC_WAYS = 1


def _tree_sum(xs):
    while len(xs) > 1:
        xs = [xs[i] + xs[i + 1] for i in range(0, len(xs) - 1, 2)] + ([xs[-1]] if len(xs) % 2 else [])
    return xs[0]


def _seq_block(n_ctx_blocks, n_blocks):
    def blk(d, i):
        bwd = jnp.where(i < n_ctx_blocks, n_ctx_blocks - 1 - i, n_blocks - 1 + n_ctx_blocks - i)
        return jnp.where(d == 0, i, bwd)
    return blk


def _rw_scan_kernel(*refs, backward):
    if backward:
        (r_ref, k_ref, v_ref, w_ref, a_ref, kkp_ref, kap_ref, yf_ref, af_ref, rkp_ref, lnw_ref, lnb_ref,
         y_ref, s_ref, p_s, sa_s, kk0_s, bt_s, kt_s, rd_s, kkd_s) = refs
    else:
        (r_ref, k_ref, v_ref, w_ref, a_ref, kkp_ref, kap_ref, _order_ref,
         y_ref, s_ref, p_s, sa_s, kk0_s, bt_s, kt_s, rd_s, kkd_s) = refs
    i = pl.program_id(0)
    tb = r_ref.shape[0]
    nv = RW_N // SUBLANES

    @pl.when(i == 0)
    def _():
        s_ref[...] = jnp.zeros_like(s_ref)
        p_s[...] = jnp.ones_like(p_s)

    def bcast(ref, *idx):
        k = idx[-1]
        row = ref[(*idx[:-1], pl.ds(k, 1), slice(None))]
        return jnp.broadcast_to(row, (SUBLANES, LANES))

    def time_index(s):
        s = jnp.minimum(s, tb - 1)
        return tb - 1 - s if backward else s

    def bf16_pair(x):
        hi = lax.bitcast_convert_type(x.astype(BF16).astype(F32), jnp.uint32)
        return lax.bitcast_convert_type(hi | (hi >> 16), F32)

    def norm_key(t):
        kkr = k_ref[t] * kkp_ref[...]
        nrm = jnp.sqrt(jnp.sum(kkr * kkr, axis=0, keepdims=True))
        return kkr / jnp.maximum(nrm, L2_EPS)

    def scaled_key(t, a):
        return k_ref[t] * (1.0 + (a - 1.0) * kap_ref[...])

    def prepare(s, slot):
        t = time_index(s)
        a = a_ref[0, t]
        kk = norm_key(t)
        p_prev = p_s[...]
        kkd_s[1 - slot] = bf16_pair(p_prev * kk)
        p = p_prev * jnp.where(s < tb, w_ref[0, t], 1.0)
        p_s[...] = p
        inv_p = 1.0 / p
        bt_s[slot] = kk * a * inv_p
        kt_s[slot] = scaled_key(t, a) * inv_p
        rd_s[slot] = bf16_pair(p * r_ref[t])

    def restart():
        kk0_s[...] = norm_key(time_index(0))
        acc = [None] * nv
        for k in range(RW_N):
            pb = bcast(p_s, k)
            kkb = bcast(kk0_s, k)
            for j in range(nv):
                rows = slice(SUBLANES * j, SUBLANES * (j + 1))
                sn = s_ref[k, rows, :] * pb
                s_ref[k, rows, :] = sn
                acc[j] = sn * kkb if acc[j] is None else acc[j] + sn * kkb
        sa_s[...] = -jnp.concatenate(acc, axis=0)
        p_s[...] = jnp.ones_like(p_s)

    def sweep(s, slot):
        t = time_index(s)
        zero = jnp.zeros((SUBLANES, LANES), F32)

        def key_block(kblk, carry):
            yacc, acc = list(carry[0]), list(carry[1])
            yb = [jnp.zeros((2 * SUBLANES, LANES), BF16)] * (nv // 2)
            ab = [jnp.zeros((2 * SUBLANES, LANES), BF16)] * (nv // 2)
            for kk in range(K_UNROLL):
                k = kblk * K_UNROLL + kk
                bb = bcast(bt_s, slot, k)
                kb = bcast(kt_s, slot, k)
                rb = pltpu.bitcast(bcast(rd_s, slot, k), BF16)
                kkn = pltpu.bitcast(bcast(kkd_s, slot, k), BF16)
                for m in range(nv // 2):
                    pair_rows = []
                    for j in (2 * m, 2 * m + 1):
                        rows = slice(SUBLANES * j, SUBLANES * (j + 1))
                        sn = s_ref[k, rows, :] + (sa_s[rows, :] * bb + v_ref[t, rows, :] * kb)
                        s_ref[k, rows, :] = sn
                        pair_rows.append(sn)
                    snp = jnp.concatenate(pair_rows, axis=0).astype(BF16)
                    yb[m] = yb[m] + snp * rb
                    ab[m] = ab[m] + snp * kkn
            for m in range(nv // 2):
                y32 = yb[m].astype(F32)
                a32 = ab[m].astype(F32)
                for h, j in enumerate((2 * m, 2 * m + 1)):
                    yacc[j] = yacc[j] + y32[SUBLANES * h:SUBLANES * (h + 1)]
                    acc[j] = acc[j] + a32[SUBLANES * h:SUBLANES * (h + 1)]
            return yacc, acc

        init = [zero] * nv
        n_kblk = RW_N // K_UNROLL
        carry = lax.fori_loop(0, n_kblk - 1, key_block, (init, init))
        yacc, acc = key_block(n_kblk - 1, carry)
        sa_s[...] = -jnp.concatenate(acc, axis=0)
        y = jnp.concatenate(yacc, axis=0)
        if backward:
            y = y + yf_ref[t]
            yc = y - jnp.mean(y, axis=0, keepdims=True)
            var = jnp.mean(yc * yc, axis=0, keepdims=True)
            y = yc * lax.rsqrt(var + RW_GN_EPS) * lnw_ref[...] + lnb_ref[...]
            k_sum = scaled_key(t, af_ref[0, t]) + scaled_key(t, a_ref[0, t])
            y = y + jnp.sum(r_ref[t] * k_sum * rkp_ref[...], axis=0, keepdims=True) * v_ref[t]
        y_ref[t] = y

    restart()
    prepare(0, 0)
    prepare(1, 1)

    def pair(p, carry):
        s = 2 * p
        sweep(s, 0)
        prepare(s + 2, 0)
        sweep(s + 1, 1)
        prepare(s + 3, 1)
        return carry

    lax.fori_loop(0, tb // 2, pair, 0)


def _rw_scan(r_t, k_t, v_t, w_t, a_t, kkp, kap, n_ctx_blocks, *, run_after=None, readout=None):
    backward = readout is not None
    direction = int(backward)
    tt = r_t.shape[0]
    nblk = tt // SCAN_TILE
    seq = _seq_block(n_ctx_blocks, nblk)
    blk = lambda i: seq(direction, i)
    tile = (SCAN_TILE, RW_N, LANES)
    shared = pl.BlockSpec(tile, lambda i: (blk(i), 0, 0))
    per_dir = pl.BlockSpec((1,) + tile, lambda i: (direction, blk(i), 0, 0))
    const = pl.BlockSpec((RW_N, LANES), lambda i: (0, 0))
    y_blk = lambda i: jnp.where(i < n_ctx_blocks, blk(n_ctx_blocks), blk(i)) - n_ctx_blocks
    y_spec = pl.BlockSpec(tile, lambda i: (y_blk(i), 0, 0))
    in_specs = [shared, shared, shared, per_dir, per_dir, const, const]
    args = [r_t, k_t, v_t, w_t, a_t, kkp, kap]
    if backward:
        y_fwd, rkp, lnw, lnb = readout
        fwd_rate = pl.BlockSpec((1,) + tile, lambda i: (0, blk(i), 0, 0))
        in_specs += [y_spec, fwd_rate, const, const, const]
        args += [y_fwd, a_t, rkp, lnw, lnb]
    else:
        in_specs += [pl.BlockSpec(memory_space=pl.ANY)]
        args += [run_after]
    return pl.pallas_call(
        functools.partial(_rw_scan_kernel, backward=backward),
        grid=(nblk,),
        in_specs=in_specs,
        out_specs=y_spec,
        out_shape=jax.ShapeDtypeStruct((tt - n_ctx_blocks * SCAN_TILE, RW_N, LANES), F32),
        scratch_shapes=[pltpu.VMEM((RW_N, RW_N, LANES), F32)] + [pltpu.VMEM((RW_N, LANES), F32)] * 3
                       + [pltpu.VMEM((2, RW_N, LANES), F32)] * 4,
        compiler_params=_params("arbitrary"),
        name="rw_scan_bwd" if backward else "rw_scan_fwd",
    )(*args)


def _hg_kernel(q_ref, f_ref, i_ref, lbp_ref, tri_ref, o_ref, st_ref, qd_scr, kv_scr, dec_scr):
    d = pl.program_id(1)
    i = pl.program_id(2)
    n_chunks = q_ref.shape[1] // HG_CHUNK

    @pl.when(i == 0)
    def _():
        st_ref[...] = jnp.zeros_like(st_ref)

    lbp = lbp_ref[...]
    e = jnp.exp(lbp - jnp.max(lbp, axis=0, keepdims=True))
    lb = e[0:1] / jnp.sum(e, axis=0, keepdims=True)
    t_idx = lax.broadcasted_iota(jnp.int32, (HG_CHUNK, HG_CHUNK), 0)
    s_idx = lax.broadcasted_iota(jnp.int32, (HG_CHUNK, HG_CHUNK), 1)
    nt = (((1,), (1,)), ((), ()))
    tn = (((0,), (0,)), ((), ()))

    def run(fwd):
        mask = (s_idx <= t_idx) if fwd else (s_idx >= t_idx)
        mid_row = HG_CHUNK // 2 - 1 if fwd else HG_CHUNK // 2
        last_row = HG_CHUNK - 1 if fwd else 0
        order = list(range(n_chunks)) if fwd else list(reversed(range(n_chunks)))

        f = lb + (1.0 - lb) * _sigmoid(f_ref[0])
        lf = jnp.log(f)
        hi = lf.astype(BF16)
        lo = (lf - hi.astype(F32)).astype(BF16)
        tri = tri_ref[0]
        b = _bdot(tri, hi) + _bdot(tri, lo)
        for c in order:
            rows = slice(c * HG_CHUNK, (c + 1) * HG_CHUNK)
            bc = b[rows]
            b_mid = bc[mid_row:mid_row + 1]
            b_last = bc[last_row:last_row + 1]
            q_in = q_ref[0, rows, :] * jnp.exp(bc - b_mid)
            k_in = (1.0 - f[rows]) * jnp.exp(b_mid - bc)
            qd_scr[rows, :] = (q_in * jnp.exp(b_mid)).astype(BF16)
            k_dec = (k_in * jnp.exp(b_last - b_mid)).astype(BF16)
            dec_scr[c:c + 1, :] = jnp.exp(b_last)
            q_in = q_in.astype(BF16)
            k_in = k_in.astype(BF16)
            vb = i_ref[0, rows, :].astype(BF16)
            heads = [slice(h * HG_N, (h + 1) * HG_N) for h in range(HG_HEADS)]
            scores = [lax.dot_general(q_in[:, cols], k_in[:, cols], nt, preferred_element_type=F32)
                      for cols in heads]
            for h, cols in enumerate(heads):
                kv_scr[c, h] = lax.dot_general(vb[:, cols], k_dec[:, cols], tn, preferred_element_type=F32)
            for h, cols in enumerate(heads):
                o_ref[0, 0, rows, cols] = _bdot(jnp.where(mask, scores[h], 0.0).astype(BF16), vb[:, cols])
        for c in order:
            rows = slice(c * HG_CHUNK, (c + 1) * HG_CHUNK)
            for h in range(HG_HEADS):
                cols = slice(h * HG_N, (h + 1) * HG_N)
                st = st_ref[h]
                o_ref[0, 0, rows, cols] += lax.dot_general(qd_scr[rows, cols], st.astype(BF16), nt,
                                                           preferred_element_type=F32)
                st_ref[h] = dec_scr[c:c + 1, cols] * st + kv_scr[c, h]

    @pl.when(d == 0)
    def _():
        run(True)

    @pl.when(d == 1)
    def _():
        run(False)


def _hg_tri(n_rows):
    t = jnp.arange(n_rows)[:, None]
    s = jnp.arange(n_rows)[None, :]
    same = (t // HG_CHUNK) == (s // HG_CHUNK)
    return jnp.stack([same & (s <= t), same & (s >= t)]).astype(BF16)


def _hg(zh, hg_lb, n_ctx):
    b, tt, _ = zh.shape
    w = HG_HEADS * HG_N
    nblk = tt // ROW_TILE
    n_chunks = ROW_TILE // HG_CHUNK
    blk = _seq_block(n_ctx // ROW_TILE, nblk)
    return pl.pallas_call(
        _hg_kernel,
        grid=(b, 2, nblk),
        in_specs=[pl.BlockSpec((1, ROW_TILE, w), lambda bi, d, i: (bi, blk(d, i), 0)),
                  pl.BlockSpec((1, ROW_TILE, w), lambda bi, d, i: (bi, blk(d, i), 1 + d)),
                  pl.BlockSpec((1, ROW_TILE, w), lambda bi, d, i: (bi, blk(d, i), 3)),
                  pl.BlockSpec(hg_lb.shape, lambda bi, d, i: (0, 0)),
                  pl.BlockSpec((1, ROW_TILE, ROW_TILE), lambda bi, d, i: (d, 0, 0))],
        out_specs=pl.BlockSpec((1, 1, ROW_TILE, w), lambda bi, d, i: (d, bi, blk(d, i), 0)),
        out_shape=jax.ShapeDtypeStruct((2, b, tt, w), F32),
        scratch_shapes=[pltpu.VMEM((HG_HEADS, HG_N, HG_N), F32),
                        pltpu.VMEM((ROW_TILE, w), BF16),
                        pltpu.VMEM((n_chunks, HG_HEADS, HG_N, HG_N), F32),
                        pltpu.VMEM((SUBLANES, w), F32)],
        compiler_params=_params("arbitrary", "arbitrary", "arbitrary"),
        name="hg",
    )(zh, zh, zh, hg_lb, _hg_tri(ROW_TILE))


def _merge_kernel(yrw_ref, g_ref, o_ref, gz_ref, zg_ref, x_ref, g1_ref, hnw_ref,
                  pa_ref, pb_ref, wo_ref, out_ref):
    d = x_ref.shape[2]
    y_rw = (yrw_ref[...].T * g_ref[...]).astype(BF16)
    o = o_ref[0, 0] + o_ref[1, 0]
    gz = gz_ref[0]
    hnw = hnw_ref[...]
    parts = []
    for h in range(HG_HEADS):
        cols = slice(h * HG_N, (h + 1) * HG_N)
        oh = o[:, cols]
        ms = jnp.mean(oh * oh, axis=-1, keepdims=True)
        parts.append(oh * lax.rsqrt(ms + RMS_EPS) * hnw[:, cols])
    y_hg = (jnp.concatenate(parts, axis=1) * (gz * _sigmoid(gz))).astype(BF16)
    zg = zg_ref[0]
    m = _sigmoid(zg[:, :d]) * _bdot(y_rw, pa_ref[...]) + _sigmoid(zg[:, d:]) * _bdot(y_hg, pb_ref[...])
    out_ref[0] = x_ref[0] + g1_ref[0] * _bdot(m.astype(BF16), wo_ref[...])


def _merge(y_rw, g, o_hg, zh, zg, x, modcat, hnw, pa, pb, wo, n_ctx):
    b, t, d = x.shape
    off = n_ctx // ROW_TILE
    lat = lambda bi, i: (bi, i, 0)
    cat = lambda bi, i: (bi, i + off, 0)
    const = lambda shape: pl.BlockSpec(shape, lambda bi, i: (0,) * len(shape))
    return pl.pallas_call(
        _merge_kernel,
        grid=(b, t // ROW_TILE),
        in_specs=[pl.BlockSpec((d, ROW_TILE), lambda bi, i: (bi, i)),
                  pl.BlockSpec((ROW_TILE, d), lambda bi, i: (i + off, bi)),
                  pl.BlockSpec((2, 1, ROW_TILE, d), lambda bi, i: (0, bi, i + off, 0)),
                  pl.BlockSpec((1, ROW_TILE, d), lambda bi, i: (bi, i + off, 4)),
                  pl.BlockSpec((1, ROW_TILE, 2 * d), cat),
                  pl.BlockSpec((1, ROW_TILE, d), lat),
                  pl.BlockSpec((1, 1, d), lambda bi, i: (2 * bi + 1, 0, 2)),
                  const(hnw.shape), const(pa.shape), const(pb.shape), const(wo.shape)],
        out_specs=pl.BlockSpec((1, ROW_TILE, d), lat),
        out_shape=jax.ShapeDtypeStruct((b, t, d), F32),
        compiler_params=_params("arbitrary", "arbitrary"),
        name="merge",
    )(y_rw, g, o_hg, zh, zg, x, modcat, hnw, pa, pb, wo)


def _mlp_kernel(x_ref, nw_ref, sh_ref, sc_ref, g2_ref, w1_ref, w2_ref, fw_ref, out_ref):
    x = x_ref[0]
    hb = _modulated_norm(x, nw_ref[...], sh_ref[0], sc_ref[0]).astype(BF16)
    u = jnp.maximum(_bdot(hb, w1_ref[...]), 0.0)
    y = x + g2_ref[0] * _bdot((u * u).astype(BF16), w2_ref[...])
    ms = jnp.mean(y * y, axis=-1, keepdims=True)
    out_ref[0] = y * lax.rsqrt(ms + RMS_EPS) * fw_ref[...]


def _mlp(x1, nw, modcat, w1, w2, fw):
    b, t, d = x1.shape
    mod_idx = lambda col: (lambda bi, i: (2 * bi + 1, 0, col))
    const = lambda shape, **kw: pl.BlockSpec(shape, lambda bi, i: (0,) * len(shape), **kw)
    return pl.pallas_call(
        _mlp_kernel,
        grid=(b, t // ROW_TILE),
        in_specs=[pl.BlockSpec((1, ROW_TILE, d), lambda bi, i: (bi, i, 0)),
                  const(nw.shape),
                  pl.BlockSpec((1, 1, d), mod_idx(3)),
                  pl.BlockSpec((1, 1, d), mod_idx(4)),
                  pl.BlockSpec((1, 1, d), mod_idx(5)),
                  const(w1.shape, pipeline_mode=pl.Buffered(1)),
                  const(w2.shape, pipeline_mode=pl.Buffered(1)),
                  const(fw.shape)],
        out_specs=pl.BlockSpec((1, ROW_TILE, d), lambda bi, i: (bi, i, 0)),
        out_shape=jax.ShapeDtypeStruct((b, t, d), F32),
        compiler_params=_params("arbitrary", "arbitrary"),
        name="mlp",
    )(x1, nw, modcat, modcat, modcat, w1, w2, fw)


def _to_scan(a):
    *lead, w, t = a.shape
    return jnp.swapaxes(a.reshape(*lead, w // RW_N, RW_N, t), -1, -3)


def _from_scan(a):
    t, n, bh = a.shape
    return jnp.swapaxes(a, 0, 2).reshape(bh * n, t)


def _head_tile(p, b):
    return jnp.tile(p.reshape(RW_HEADS, RW_N).T, (1, b))


def kernel(x, c, ctx, c_ctx, norm1_w, norm2_w, w_mod, b_mod, w_in, rw_mu, rw_w0, rw_w_up, rw_a0, rw_a_up, rw_g_up, rw_k_k, rw_k_a, rw_r_k, rw_ln_w, rw_ln_b, hg_lb, hg_norm_w, p_a, p_b, w_out, w_fc1, w_fc2, final_norm_w):
    b, t, d = x.shape
    n_ctx = ctx.shape[1]
    assert w_mod.shape[0] == 1, "single-layer block"
    assert b * RW_HEADS == LANES and n_ctx == ROW_TILE and t % ROW_TILE == 0
    assert d == RW_HEADS * RW_N == HG_HEADS * HG_N
    n_r = rw_mu.shape[2]
    n_h = 5 * d
    n_g = 2 * d
    rank_w, rank_a = rw_w_up.shape[2], rw_a_up.shape[2]
    assert rank_w + rank_a == LANES and n_r == 3 * d + 2 * LANES

    c_rows = jnp.zeros((2 * SUBLANES, d), F32).at[:b].set(c).at[b].set(c_ctx)
    mod = _mod(c_rows, w_mod[0], b_mod)
    modcat = jnp.stack([jnp.broadcast_to(mod[b], (b, N_MOD * d)), mod[:b]], axis=1).reshape(2 * b, 1, N_MOD * d)

    w_bf = w_in[0].astype(BF16)
    zh, zg = _in_proj(ctx, x, norm1_w, modcat, w_bf[:, n_r:], n_h, n_g)

    wup = jnp.pad(rw_w_up[0], ((0, 0), (0, rank_a), (0, 0))).astype(BF16)
    aup = jnp.pad(rw_a_up[0], ((0, 0), (rank_w, 0), (0, 0))).astype(BF16)
    r, k_raw, v, wd, ad, g = _rw_prep(ctx, x, norm1_w, modcat, w_bf[:, :n_r], rw_mu[0], rw_w0[0], wup,
                                      rw_a0[0], aup, rw_g_up[0].astype(BF16))

    r_t, k_t, v_t, w_t, a_t = (_to_scan(a) for a in (r, k_raw, v, wd, ad))
    o_hg = _hg(zh, hg_lb, n_ctx)
    kap = _head_tile(rw_k_a[0], b)
    scan_args = (r_t, k_t, v_t, w_t, a_t, _head_tile(rw_k_k[0], b), kap, n_ctx // SCAN_TILE)
    y_fwd = _rw_scan(*scan_args, run_after=o_hg)
    y_rw_t = _rw_scan(*scan_args, readout=(y_fwd, _head_tile(rw_r_k[0].reshape(-1), b),
                                           _head_tile(rw_ln_w[0], b), _head_tile(rw_ln_b[0], b)))
    y_rw = _from_scan(y_rw_t)

    x1 = _merge(y_rw, g, o_hg, zh, zg, x, modcat, hg_norm_w, p_a[0].astype(BF16), p_b[0].astype(BF16),
                w_out[0].astype(BF16), n_ctx)
    return _mlp(x1, norm2_w, modcat, w_fc1[0].astype(BF16), w_fc2[0].astype(BF16),
                final_norm_w.reshape(1, d))
```

```python
import functools

import jax
import jax.numpy as jnp
from jax import lax
from jax.experimental import pallas as pl
from jax.experimental.pallas import tpu as pltpu

F32 = jnp.float32
BF16 = jnp.bfloat16

GRID_W = 64
RW_HEADS = 16
RW_N = 64
HG_HEADS = 8
HG_N = 128
HG_CHUNK = 64
N_MOD = 6
RMS_EPS = 1e-6
RW_GN_EPS = 64e-5
L2_EPS = 1e-12

SUBLANES = 8
LANES = 128
ROW_TILE = 256
SCAN_TILE = 64
VMEM_LIMIT = 56 * 1024 * 1024


def _params(*sem):
    return pltpu.CompilerParams(dimension_semantics=sem, vmem_limit_bytes=VMEM_LIMIT)


def _bdot(a, b):
    return jnp.dot(a, b, preferred_element_type=F32)


def _sigmoid(x):
    return 1.0 / (1.0 + jnp.exp(-x))


def _mod_kernel(c_ref, w_ref, b_ref, o_ref):
    c = c_ref[...]
    act = c * _sigmoid(c)
    o_ref[...] = _bdot(act.astype(BF16), w_ref[...].astype(BF16)) + b_ref[...]


def _mod(c_rows, w_mod, b_mod):
    rows, d = c_rows.shape
    n = w_mod.shape[1]
    return pl.pallas_call(
        _mod_kernel,
        grid=(n // d,),
        in_specs=[pl.BlockSpec((rows, d), lambda j: (0, 0)),
                  pl.BlockSpec((d, d), lambda j: (0, j)),
                  pl.BlockSpec((1, d), lambda j: (0, j))],
        out_specs=pl.BlockSpec((rows, d), lambda j: (0, j)),
        out_shape=jax.ShapeDtypeStruct((rows, n), F32),
        compiler_params=_params("arbitrary"),
        name="mod",
    )(c_rows, w_mod, b_mod)


def _modulated_norm(x, nw, sh, sc):
    ms = jnp.mean(x * x, axis=-1, keepdims=True)
    return (x * lax.rsqrt(ms + RMS_EPS) * nw) * (1.0 + sc) + sh


def _in_proj_kernel(ctx_ref, x_ref, nw_ref, sh_ref, sc_ref, w_ref, zr_ref, zh_ref, zg_ref):
    tokens = jnp.where(pl.program_id(1) == 0, ctx_ref[0], x_ref[0])
    hb = _modulated_norm(tokens, nw_ref[...], sh_ref[0], sc_ref[0]).astype(BF16)
    n_r, n_h = zr_ref.shape[2], zh_ref.shape[2]
    zr_ref[0] = _bdot(hb, w_ref[:, :n_r])
    zh_ref[0] = _bdot(hb, w_ref[:, n_r:n_r + n_h])
    zg_ref[0] = _bdot(hb, w_ref[:, n_r + n_h:])


def _in_proj(ctx, x, nw, modcat, w_bf, n_r, n_h, n_g):
    b, t, d = x.shape
    tt = t + ctx.shape[1]
    nblk = tt // ROW_TILE
    mod_idx = lambda col: (lambda bi, i: (2 * bi + jnp.minimum(i, 1), 0, col))
    return pl.pallas_call(
        _in_proj_kernel,
        grid=(b, nblk),
        in_specs=[pl.BlockSpec((1, ROW_TILE, d), lambda bi, i: (bi, 0, 0)),
                  pl.BlockSpec((1, ROW_TILE, d), lambda bi, i: (bi, jnp.maximum(i - 1, 0), 0)),
                  pl.BlockSpec((1, d), lambda bi, i: (0, 0)),
                  pl.BlockSpec((1, 1, d), mod_idx(0)),
                  pl.BlockSpec((1, 1, d), mod_idx(1)),
                  pl.BlockSpec(w_bf.shape, lambda bi, i: (0, 0), pipeline_mode=pl.Buffered(1))],
        out_specs=[pl.BlockSpec((1, ROW_TILE, n), lambda bi, i: (bi, i, 0)) for n in (n_r, n_h, n_g)],
        out_shape=[jax.ShapeDtypeStruct((b, tt, n), F32) for n in (n_r, n_h, n_g)],
        compiler_params=_params("arbitrary", "arbitrary"),
        name="in_proj",
    )(ctx, x, nw, modcat, modcat, w_bf)


def _rw_prep_kernel(z_ref, zp_ref, zn_ref, mu_ref, w0_ref, wup_ref, a0_ref, aup_ref, gup_ref,
                    r_ref, k_ref, v_ref, wd_ref, ad_ref, g_ref):
    i = pl.program_id(1)
    nblk = pl.num_programs(1)
    tt = z_ref.shape[1]
    d = r_ref.shape[0]
    is_lat = i > 0
    row = lax.broadcasted_iota(jnp.int32, (tt, 1), 0)
    col = row % GRID_W
    lmask = jnp.where(is_lat, col, row) == 0
    rmask = jnp.where(is_lat, col, row - (tt - GRID_W)) == GRID_W - 1
    latf = is_lat.astype(F32)
    up_ok = (i > 1).astype(F32)
    dn_ok = (i < nblk - 1).astype(F32)

    def shifted(c0, c1):
        z = z_ref[0, :, c0:c1]
        mu = mu_ref[:, c0:c1]
        left = jnp.where(lmask, 0.0, pltpu.roll(z, 1, 0))
        right = jnp.where(rmask, 0.0, pltpu.roll(z, tt - 1, 0))
        up = jnp.concatenate([zp_ref[0, :, c0:c1] * up_ok, z[:tt - GRID_W]], axis=0)
        down = jnp.concatenate([z[GRID_W:], zn_ref[0, :, c0:c1] * dn_ok], axis=0)
        out = z + mu[0:1] * (left - z) + mu[1:2] * (right - z)
        return out + latf * (mu[2:3] * (up - z) + mu[3:4] * (down - z))

    r_ref[...] = shifted(0, d).T
    k_ref[...] = shifted(d, 2 * d).T
    v_ref[...] = shifted(2 * d, 3 * d).T
    rest = shifted(3 * d, z_ref.shape[2])
    xwa = rest[:, :LANES]
    xw_t = jnp.tanh(xwa).astype(BF16)
    xa_b = xwa.astype(BF16)
    for dr in range(2):
        u = -(w0_ref[dr:dr + 1, :] + _bdot(xw_t, wup_ref[dr]))
        softplus = jnp.maximum(u, 0.0) + jnp.log(1.0 + jnp.exp(-jnp.abs(u)))
        wd_ref[dr] = jnp.exp(-jnp.exp(-softplus - 0.5)).T
        ad_ref[dr] = _sigmoid(a0_ref[dr:dr + 1, :] + _bdot(xa_b, aup_ref[dr])).T
    g_ref[...] = _bdot(_sigmoid(rest[:, LANES:]).astype(BF16), gup_ref[...])


def _rw_prep(zr, mu, w0, wup, a0, aup, gup, d):
    b, tt, nr = zr.shape
    nblk = tt // ROW_TILE
    per = ROW_TILE // GRID_W
    last = tt // GRID_W - 1
    const = lambda shape: pl.BlockSpec(shape, lambda bi, i: (0,) * len(shape))
    row_spec = pl.BlockSpec((d, ROW_TILE), lambda bi, i: (bi, i))
    dir_spec = pl.BlockSpec((2, d, ROW_TILE), lambda bi, i: (0, bi, i))
    gate_spec = pl.BlockSpec((ROW_TILE, d), lambda bi, i: (i, bi))
    row_shape = jax.ShapeDtypeStruct((b * d, tt), F32)
    dir_shape = jax.ShapeDtypeStruct((2, b * d, tt), F32)
    gate_shape = jax.ShapeDtypeStruct((tt, b * d), F32)
    return pl.pallas_call(
        _rw_prep_kernel,
        grid=(b, nblk),
        in_specs=[pl.BlockSpec((1, ROW_TILE, nr), lambda bi, i: (bi, i, 0)),
                  pl.BlockSpec((1, GRID_W, nr), lambda bi, i: (bi, jnp.maximum(i * per - 1, 0), 0)),
                  pl.BlockSpec((1, GRID_W, nr), lambda bi, i: (bi, jnp.minimum(i * per + per, last), 0)),
                  const(mu.shape), const(w0.shape), const(wup.shape), const(a0.shape),
                  const(aup.shape), const(gup.shape)],
        out_specs=[row_spec, row_spec, row_spec, dir_spec, dir_spec, gate_spec],
        out_shape=[row_shape, row_shape, row_shape, dir_shape, dir_shape, gate_shape],
        compiler_params=_params("arbitrary", "arbitrary"),
        name="rw_prep",
    )(zr, zr, zr, mu, w0, wup, a0, aup, gup)


K_UNROLL = 16
ACC_WAYS = 1


def _tree_sum(xs):
    while len(xs) > 1:
        xs = [xs[i] + xs[i + 1] for i in range(0, len(xs) - 1, 2)] + ([xs[-1]] if len(xs) % 2 else [])
    return xs[0]


def _seq_block(n_ctx_blocks, n_blocks):
    def blk(d, i):
        bwd = jnp.where(i < n_ctx_blocks, n_ctx_blocks - 1 - i, n_blocks - 1 + n_ctx_blocks - i)
        return jnp.where(d == 0, i, bwd)
    return blk


def _rw_scan_kernel(*refs, backward):
    if backward:
        (r_ref, k_ref, v_ref, w_ref, a_ref, kkp_ref, kap_ref, yf_ref, af_ref, rkp_ref, lnw_ref, lnb_ref,
         y_ref, s_ref, p_s, sa_s, kk0_s, bt_s, kt_s, rd_s, kkd_s) = refs
    else:
        (r_ref, k_ref, v_ref, w_ref, a_ref, kkp_ref, kap_ref, _order_ref,
         y_ref, s_ref, p_s, sa_s, kk0_s, bt_s, kt_s, rd_s, kkd_s) = refs
    i = pl.program_id(0)
    tb = r_ref.shape[0]
    nv = RW_N // SUBLANES

    @pl.when(i == 0)
    def _():
        s_ref[...] = jnp.zeros_like(s_ref)
        p_s[...] = jnp.ones_like(p_s)

    def bcast(ref, *idx):
        k = idx[-1]
        row = ref[(*idx[:-1], pl.ds(k, 1), slice(None))]
        return jnp.broadcast_to(row, (SUBLANES, LANES))

    def time_index(s):
        s = jnp.minimum(s, tb - 1)
        return tb - 1 - s if backward else s

    def bf16_pair(x):
        hi = lax.bitcast_convert_type(x.astype(BF16).astype(F32), jnp.uint32)
        return lax.bitcast_convert_type(hi | (hi >> 16), F32)

    def norm_key(t):
        kkr = k_ref[t] * kkp_ref[...]
        nrm = jnp.sqrt(jnp.sum(kkr * kkr, axis=0, keepdims=True))
        return kkr / jnp.maximum(nrm, L2_EPS)

    def scaled_key(t, a):
        return k_ref[t] * (1.0 + (a - 1.0) * kap_ref[...])

    def prepare(s, slot):
        t = time_index(s)
        a = a_ref[0, t]
        kk = norm_key(t)
        p_prev = p_s[...]
        kkd_s[1 - slot] = bf16_pair(p_prev * kk)
        p = p_prev * jnp.where(s < tb, w_ref[0, t], 1.0)
        p_s[...] = p
        inv_p = 1.0 / p
        bt_s[slot] = kk * a * inv_p
        kt_s[slot] = scaled_key(t, a) * inv_p
        rd_s[slot] = bf16_pair(p * r_ref[t])

    def restart():
        kk0_s[...] = norm_key(time_index(0))
        acc = [None] * nv
        for k in range(RW_N):
            pb = bcast(p_s, k)
            kkb = bcast(kk0_s, k)
            for j in range(nv):
                rows = slice(SUBLANES * j, SUBLANES * (j + 1))
                sn = s_ref[k, rows, :] * pb
                s_ref[k, rows, :] = sn
                acc[j] = sn * kkb if acc[j] is None else acc[j] + sn * kkb
        sa_s[...] = -jnp.concatenate(acc, axis=0)
        p_s[...] = jnp.ones_like(p_s)

    def sweep(s, slot):
        t = time_index(s)
        zero = jnp.zeros((SUBLANES, LANES), F32)

        def key_block(kblk, carry):
            yacc, acc = list(carry[0]), list(carry[1])
            yb = [jnp.zeros((2 * SUBLANES, LANES), BF16)] * (nv // 2)
            ab = [jnp.zeros((2 * SUBLANES, LANES), BF16)] * (nv // 2)
            for kk in range(K_UNROLL):
                k = kblk * K_UNROLL + kk
                bb = bcast(bt_s, slot, k)
                kb = bcast(kt_s, slot, k)
                rb = pltpu.bitcast(bcast(rd_s, slot, k), BF16)
                kkn = pltpu.bitcast(bcast(kkd_s, slot, k), BF16)
                for m in range(nv // 2):
                    pair_rows = []
                    for j in (2 * m, 2 * m + 1):
                        rows = slice(SUBLANES * j, SUBLANES * (j + 1))
                        sn = s_ref[k, rows, :] + (sa_s[rows, :] * bb + v_ref[t, rows, :] * kb)
                        s_ref[k, rows, :] = sn
                        pair_rows.append(sn)
                    snp = jnp.concatenate(pair_rows, axis=0).astype(BF16)
                    yb[m] = yb[m] + snp * rb
                    ab[m] = ab[m] + snp * kkn
            for m in range(nv // 2):
                y32 = yb[m].astype(F32)
                a32 = ab[m].astype(F32)
                for h, j in enumerate((2 * m, 2 * m + 1)):
                    yacc[j] = yacc[j] + y32[SUBLANES * h:SUBLANES * (h + 1)]
                    acc[j] = acc[j] + a32[SUBLANES * h:SUBLANES * (h + 1)]
            return yacc, acc

        init = [zero] * nv
        n_kblk = RW_N // K_UNROLL
        carry = lax.fori_loop(0, n_kblk - 1, key_block, (init, init))
        yacc, acc = key_block(n_kblk - 1, carry)
        sa_s[...] = -jnp.concatenate(acc, axis=0)
        y = jnp.concatenate(yacc, axis=0)
        if backward:
            y = y + yf_ref[t]
            yc = y - jnp.mean(y, axis=0, keepdims=True)
            var = jnp.mean(yc * yc, axis=0, keepdims=True)
            y = yc * lax.rsqrt(var + RW_GN_EPS) * lnw_ref[...] + lnb_ref[...]
            k_sum = scaled_key(t, af_ref[0, t]) + scaled_key(t, a_ref[0, t])
            y = y + jnp.sum(r_ref[t] * k_sum * rkp_ref[...], axis=0, keepdims=True) * v_ref[t]
        y_ref[t] = y

    restart()
    prepare(0, 0)
    prepare(1, 1)

    def pair(p, carry):
        s = 2 * p
        sweep(s, 0)
        prepare(s + 2, 0)
        sweep(s + 1, 1)
        prepare(s + 3, 1)
        return carry

    lax.fori_loop(0, tb // 2, pair, 0)


def _rw_scan(r_t, k_t, v_t, w_t, a_t, kkp, kap, n_ctx_blocks, *, run_after=None, readout=None):
    backward = readout is not None
    direction = int(backward)
    tt = r_t.shape[0]
    nblk = tt // SCAN_TILE
    seq = _seq_block(n_ctx_blocks, nblk)
    blk = lambda i: seq(direction, i)
    tile = (SCAN_TILE, RW_N, LANES)
    shared = pl.BlockSpec(tile, lambda i: (blk(i), 0, 0))
    per_dir = pl.BlockSpec((1,) + tile, lambda i: (direction, blk(i), 0, 0))
    const = pl.BlockSpec((RW_N, LANES), lambda i: (0, 0))
    y_blk = lambda i: jnp.where(i < n_ctx_blocks, blk(n_ctx_blocks), blk(i)) - n_ctx_blocks
    y_spec = pl.BlockSpec(tile, lambda i: (y_blk(i), 0, 0))
    in_specs = [shared, shared, shared, per_dir, per_dir, const, const]
    args = [r_t, k_t, v_t, w_t, a_t, kkp, kap]
    if backward:
        y_fwd, rkp, lnw, lnb = readout
        fwd_rate = pl.BlockSpec((1,) + tile, lambda i: (0, blk(i), 0, 0))
        in_specs += [y_spec, fwd_rate, const, const, const]
        args += [y_fwd, a_t, rkp, lnw, lnb]
    else:
        in_specs += [pl.BlockSpec(memory_space=pl.ANY)]
        args += [run_after]
    return pl.pallas_call(
        functools.partial(_rw_scan_kernel, backward=backward),
        grid=(nblk,),
        in_specs=in_specs,
        out_specs=y_spec,
        out_shape=jax.ShapeDtypeStruct((tt - n_ctx_blocks * SCAN_TILE, RW_N, LANES), F32),
        scratch_shapes=[pltpu.VMEM((RW_N, RW_N, LANES), F32)] + [pltpu.VMEM((RW_N, LANES), F32)] * 3
                       + [pltpu.VMEM((2, RW_N, LANES), F32)] * 4,
        compiler_params=_params("arbitrary"),
        name="rw_scan_bwd" if backward else "rw_scan_fwd",
    )(*args)


def _hg_kernel(q_ref, f_ref, i_ref, lbp_ref, tri_ref, o_ref, st_ref, qd_scr, kv_scr, dec_scr):
    d = pl.program_id(1)
    i = pl.program_id(2)
    n_chunks = q_ref.shape[1] // HG_CHUNK

    @pl.when(i == 0)
    def _():
        st_ref[...] = jnp.zeros_like(st_ref)

    lbp = lbp_ref[...]
    e = jnp.exp(lbp - jnp.max(lbp, axis=0, keepdims=True))
    lb = e[0:1] / jnp.sum(e, axis=0, keepdims=True)
    t_idx = lax.broadcasted_iota(jnp.int32, (HG_CHUNK, HG_CHUNK), 0)
    s_idx = lax.broadcasted_iota(jnp.int32, (HG_CHUNK, HG_CHUNK), 1)
    nt = (((1,), (1,)), ((), ()))
    tn = (((0,), (0,)), ((), ()))

    def run(fwd):
        mask = (s_idx <= t_idx) if fwd else (s_idx >= t_idx)
        mid_row = HG_CHUNK // 2 - 1 if fwd else HG_CHUNK // 2
        last_row = HG_CHUNK - 1 if fwd else 0
        order = list(range(n_chunks)) if fwd else list(reversed(range(n_chunks)))

        f = lb + (1.0 - lb) * _sigmoid(f_ref[0])
        lf = jnp.log(f)
        hi = lf.astype(BF16)
        lo = (lf - hi.astype(F32)).astype(BF16)
        tri = tri_ref[0]
        b = _bdot(tri, hi) + _bdot(tri, lo)
        for c in order:
            rows = slice(c * HG_CHUNK, (c + 1) * HG_CHUNK)
            bc = b[rows]
            b_mid = bc[mid_row:mid_row + 1]
            b_last = bc[last_row:last_row + 1]
            q_in = q_ref[0, rows, :] * jnp.exp(bc - b_mid)
            k_in = (1.0 - f[rows]) * jnp.exp(b_mid - bc)
            qd_scr[rows, :] = (q_in * jnp.exp(b_mid)).astype(BF16)
            k_dec = (k_in * jnp.exp(b_last - b_mid)).astype(BF16)
            dec_scr[c:c + 1, :] = jnp.exp(b_last)
            q_in = q_in.astype(BF16)
            k_in = k_in.astype(BF16)
            vb = i_ref[0, rows, :].astype(BF16)
            heads = [slice(h * HG_N, (h + 1) * HG_N) for h in range(HG_HEADS)]
            scores = [lax.dot_general(q_in[:, cols], k_in[:, cols], nt, preferred_element_type=F32)
                      for cols in heads]
            for h, cols in enumerate(heads):
                kv_scr[c, h] = lax.dot_general(vb[:, cols], k_dec[:, cols], tn, preferred_element_type=F32)
            for h, cols in enumerate(heads):
                o_ref[0, 0, rows, cols] = _bdot(jnp.where(mask, scores[h], 0.0).astype(BF16), vb[:, cols])
        for c in order:
            rows = slice(c * HG_CHUNK, (c + 1) * HG_CHUNK)
            for h in range(HG_HEADS):
                cols = slice(h * HG_N, (h + 1) * HG_N)
                st = st_ref[h]
                o_ref[0, 0, rows, cols] += lax.dot_general(qd_scr[rows, cols], st.astype(BF16), nt,
                                                           preferred_element_type=F32)
                st_ref[h] = dec_scr[c:c + 1, cols] * st + kv_scr[c, h]

    @pl.when(d == 0)
    def _():
        run(True)

    @pl.when(d == 1)
    def _():
        run(False)


def _hg_tri(n_rows):
    t = jnp.arange(n_rows)[:, None]
    s = jnp.arange(n_rows)[None, :]
    same = (t // HG_CHUNK) == (s // HG_CHUNK)
    return jnp.stack([same & (s <= t), same & (s >= t)]).astype(BF16)


def _hg(zh, hg_lb, n_ctx):
    b, tt, _ = zh.shape
    w = HG_HEADS * HG_N
    nblk = tt // ROW_TILE
    n_chunks = ROW_TILE // HG_CHUNK
    blk = _seq_block(n_ctx // ROW_TILE, nblk)
    return pl.pallas_call(
        _hg_kernel,
        grid=(b, 2, nblk),
        in_specs=[pl.BlockSpec((1, ROW_TILE, w), lambda bi, d, i: (bi, blk(d, i), 0)),
                  pl.BlockSpec((1, ROW_TILE, w), lambda bi, d, i: (bi, blk(d, i), 1 + d)),
                  pl.BlockSpec((1, ROW_TILE, w), lambda bi, d, i: (bi, blk(d, i), 3)),
                  pl.BlockSpec(hg_lb.shape, lambda bi, d, i: (0, 0)),
                  pl.BlockSpec((1, ROW_TILE, ROW_TILE), lambda bi, d, i: (d, 0, 0))],
        out_specs=pl.BlockSpec((1, 1, ROW_TILE, w), lambda bi, d, i: (d, bi, blk(d, i), 0)),
        out_shape=jax.ShapeDtypeStruct((2, b, tt, w), F32),
        scratch_shapes=[pltpu.VMEM((HG_HEADS, HG_N, HG_N), F32),
                        pltpu.VMEM((ROW_TILE, w), BF16),
                        pltpu.VMEM((n_chunks, HG_HEADS, HG_N, HG_N), F32),
                        pltpu.VMEM((SUBLANES, w), F32)],
        compiler_params=_params("arbitrary", "arbitrary", "arbitrary"),
        name="hg",
    )(zh, zh, zh, hg_lb, _hg_tri(ROW_TILE))


def _merge_kernel(yrw_ref, g_ref, o_ref, gz_ref, zg_ref, x_ref, g1_ref, sh2_ref, sc2_ref, g2_ref, hnw_ref,
                  pa_ref, pb_ref, wo_ref, nw2_ref, w1_ref, w2_ref, fw_ref, out_ref):
    d = x_ref.shape[2]
    y_rw = (yrw_ref[...].T * g_ref[...]).astype(BF16)
    o = o_ref[0, 0] + o_ref[1, 0]
    gz = gz_ref[0]
    hnw = hnw_ref[...]
    parts = []
    for h in range(HG_HEADS):
        cols = slice(h * HG_N, (h + 1) * HG_N)
        oh = o[:, cols]
        ms = jnp.mean(oh * oh, axis=-1, keepdims=True)
        parts.append(oh * lax.rsqrt(ms + RMS_EPS) * hnw[:, cols])
    y_hg = (jnp.concatenate(parts, axis=1) * (gz * _sigmoid(gz))).astype(BF16)
    zg = zg_ref[0]
    m = _sigmoid(zg[:, :d]) * _bdot(y_rw, pa_ref[...]) + _sigmoid(zg[:, d:]) * _bdot(y_hg, pb_ref[...])
    x1 = x_ref[0] + g1_ref[0] * _bdot(m.astype(BF16), wo_ref[...])
    hb = _modulated_norm(x1, nw2_ref[...], sh2_ref[0], sc2_ref[0]).astype(BF16)
    u = jnp.maximum(_bdot(hb, w1_ref[...]), 0.0)
    y = x1 + g2_ref[0] * _bdot((u * u).astype(BF16), w2_ref[...])
    ms = jnp.mean(y * y, axis=-1, keepdims=True)
    out_ref[0] = y * lax.rsqrt(ms + RMS_EPS) * fw_ref[...]


def _merge(y_rw, g, o_hg, zh, zg, x, modcat, hnw, pa, pb, wo, nw2, w1, w2, fw, n_ctx):
    b, t, d = x.shape
    off = n_ctx // ROW_TILE
    lat = lambda bi, i: (bi, i, 0)
    cat = lambda bi, i: (bi, i + off, 0)
    mod = lambda col: pl.BlockSpec((1, 1, d), lambda bi, i: (2 * bi + 1, 0, col))
    const = lambda a: pl.BlockSpec(a.shape, lambda bi, i: (0,) * a.ndim, pipeline_mode=pl.Buffered(1))
    return pl.pallas_call(
        _merge_kernel,
        grid=(b, t // ROW_TILE),
        in_specs=[pl.BlockSpec((d, ROW_TILE), lambda bi, i: (bi, i)),
                  pl.BlockSpec((ROW_TILE, d), lambda bi, i: (i + off, bi)),
                  pl.BlockSpec((2, 1, ROW_TILE, d), lambda bi, i: (0, bi, i + off, 0)),
                  pl.BlockSpec((1, ROW_TILE, d), lambda bi, i: (bi, i + off, 4)),
                  pl.BlockSpec((1, ROW_TILE, 2 * d), cat),
                  pl.BlockSpec((1, ROW_TILE, d), lat),
                  mod(2), mod(3), mod(4), mod(5),
                  const(hnw), const(pa), const(pb), const(wo), const(nw2), const(w1), const(w2), const(fw)],
        out_specs=pl.BlockSpec((1, ROW_TILE, d), lat),
        out_shape=jax.ShapeDtypeStruct((b, t, d), F32),
        compiler_params=_params("arbitrary", "arbitrary"),
        name="merge_mlp",
    )(y_rw, g, o_hg, zh, zg, x, modcat, modcat, modcat, modcat, hnw, pa, pb, wo, nw2, w1, w2, fw)


def _to_scan(a):
    *lead, w, t = a.shape
    return jnp.swapaxes(a.reshape(*lead, w // RW_N, RW_N, t), -1, -3)


def _from_scan(a):
    t, n, bh = a.shape
    return jnp.swapaxes(a, 0, 2).reshape(bh * n, t)


def _head_tile(p, b):
    return jnp.tile(p.reshape(RW_HEADS, RW_N).T, (1, b))


def kernel(x, c, ctx, c_ctx, norm1_w, norm2_w, w_mod, b_mod, w_in, rw_mu, rw_w0, rw_w_up, rw_a0, rw_a_up, rw_g_up, rw_k_k, rw_k_a, rw_r_k, rw_ln_w, rw_ln_b, hg_lb, hg_norm_w, p_a, p_b, w_out, w_fc1, w_fc2, final_norm_w):
    b, t, d = x.shape
    n_ctx = ctx.shape[1]
    assert w_mod.shape[0] == 1, "single-layer block"
    assert b * RW_HEADS == LANES and n_ctx == ROW_TILE and t % ROW_TILE == 0
    assert d == RW_HEADS * RW_N == HG_HEADS * HG_N
    n_r = rw_mu.shape[2]
    n_h = 5 * d
    n_g = 2 * d
    rank_w, rank_a = rw_w_up.shape[2], rw_a_up.shape[2]
    assert rank_w + rank_a == LANES and n_r == 3 * d + 2 * LANES

    c_rows = jnp.zeros((2 * SUBLANES, d), F32).at[:b].set(c).at[b].set(c_ctx)
    mod = _mod(c_rows, w_mod[0], b_mod)
    modcat = jnp.stack([jnp.broadcast_to(mod[b], (b, N_MOD * d)), mod[:b]], axis=1).reshape(2 * b, 1, N_MOD * d)

    zr, zh, zg = _in_proj(ctx, x, norm1_w, modcat, w_in[0].astype(BF16), n_r, n_h, n_g)

    wup = jnp.pad(rw_w_up[0], ((0, 0), (0, rank_a), (0, 0))).astype(BF16)
    aup = jnp.pad(rw_a_up[0], ((0, 0), (rank_w, 0), (0, 0))).astype(BF16)
    r, k_raw, v, wd, ad, g = _rw_prep(zr, rw_mu[0], rw_w0[0], wup, rw_a0[0], aup,
                                      rw_g_up[0].astype(BF16), d)

    r_t, k_t, v_t, w_t, a_t = (_to_scan(a) for a in (r, k_raw, v, wd, ad))
    o_hg = _hg(zh, hg_lb, n_ctx)
    kap = _head_tile(rw_k_a[0], b)
    scan_args = (r_t, k_t, v_t, w_t, a_t, _head_tile(rw_k_k[0], b), kap, n_ctx // SCAN_TILE)
    y_fwd = _rw_scan(*scan_args, run_after=o_hg)
    y_rw_t = _rw_scan(*scan_args, readout=(y_fwd, _head_tile(rw_r_k[0].reshape(-1), b),
                                           _head_tile(rw_ln_w[0], b), _head_tile(rw_ln_b[0], b)))
    y_rw = _from_scan(y_rw_t)

    return _merge(y_rw, g, o_hg, zh, zg, x, modcat, hg_norm_w, p_a[0].astype(BF16), p_b[0].astype(BF16),
                  w_out[0].astype(BF16), norm2_w, w_fc1[0].astype(BF16), w_fc2[0].astype(BF16),
                  final_norm_w.reshape(1, d), n_ctx)
```

```python
import functools

import jax
import jax.numpy as jnp
from jax import lax
from jax.experimental import pallas as pl
from jax.experimental.pallas import tpu as pltpu

F32 = jnp.float32
BF16 = jnp.bfloat16

GRID_W = 64
RW_HEADS = 16
RW_N = 64
HG_HEADS = 8
HG_N = 128
HG_CHUNK = 64
N_MOD = 6
RMS_EPS = 1e-6
RW_GN_EPS = 64e-5
L2_EPS = 1e-12

SUBLANES = 8
LANES = 128
ROW_TILE = 256
SCAN_TILE = 64
VMEM_LIMIT = 56 * 1024 * 1024


def _params(*sem):
    return pltpu.CompilerParams(dimension_semantics=sem, vmem_limit_bytes=VMEM_LIMIT)


def _bdot(a, b):
    return jnp.dot(a, b, preferred_element_type=F32)


def _sigmoid(x):
    return 1.0 / (1.0 + jnp.exp(-x))


def _mod_kernel(c_ref, w_ref, b_ref, o_ref):
    c = c_ref[...]
    act = c * _sigmoid(c)
    o_ref[...] = _bdot(act.astype(BF16), w_ref[...].astype(BF16)) + b_ref[...]


def _mod(c_rows, w_mod, b_mod):
    rows, d = c_rows.shape
    n = w_mod.shape[1]
    return pl.pallas_call(
        _mod_kernel,
        grid=(n // d,),
        in_specs=[pl.BlockSpec((rows, d), lambda j: (0, 0)),
                  pl.BlockSpec((d, d), lambda j: (0, j)),
                  pl.BlockSpec((1, d), lambda j: (0, j))],
        out_specs=pl.BlockSpec((rows, d), lambda j: (0, j)),
        out_shape=jax.ShapeDtypeStruct((rows, n), F32),
        compiler_params=_params("arbitrary"),
        name="mod",
    )(c_rows, w_mod, b_mod)


def _modulated_norm(x, nw, sh, sc):
    ms = jnp.mean(x * x, axis=-1, keepdims=True)
    return (x * lax.rsqrt(ms + RMS_EPS) * nw) * (1.0 + sc) + sh


def _in_proj_kernel(ctx_ref, x_ref, nw_ref, sh_ref, sc_ref, w_ref, zr_ref, zh_ref, zg_ref):
    tokens = jnp.where(pl.program_id(1) == 0, ctx_ref[0], x_ref[0])
    hb = _modulated_norm(tokens, nw_ref[...], sh_ref[0], sc_ref[0]).astype(BF16)
    n_r, n_h = zr_ref.shape[2], zh_ref.shape[2]
    zr_ref[0] = _bdot(hb, w_ref[:, :n_r])
    zh_ref[0] = _bdot(hb, w_ref[:, n_r:n_r + n_h])
    zg_ref[0] = _bdot(hb, w_ref[:, n_r + n_h:])


def _in_proj(ctx, x, nw, modcat, w_bf, n_r, n_h, n_g):
    b, t, d = x.shape
    tt = t + ctx.shape[1]
    nblk = tt // ROW_TILE
    mod_idx = lambda col: (lambda bi, i: (2 * bi + jnp.minimum(i, 1), 0, col))
    return pl.pallas_call(
        _in_proj_kernel,
        grid=(b, nblk),
        in_specs=[pl.BlockSpec((1, ROW_TILE, d), lambda bi, i: (bi, 0, 0)),
                  pl.BlockSpec((1, ROW_TILE, d), lambda bi, i: (bi, jnp.maximum(i - 1, 0), 0)),
                  pl.BlockSpec((1, d), lambda bi, i: (0, 0)),
                  pl.BlockSpec((1, 1, d), mod_idx(0)),
                  pl.BlockSpec((1, 1, d), mod_idx(1)),
                  pl.BlockSpec(w_bf.shape, lambda bi, i: (0, 0), pipeline_mode=pl.Buffered(1))],
        out_specs=[pl.BlockSpec((1, ROW_TILE, n), lambda bi, i: (bi, i, 0)) for n in (n_r, n_h, n_g)],
        out_shape=[jax.ShapeDtypeStruct((b, tt, n), F32) for n in (n_r, n_h, n_g)],
        compiler_params=_params("arbitrary", "arbitrary"),
        name="in_proj",
    )(ctx, x, nw, modcat, modcat, w_bf)


def _rw_prep_kernel(z_ref, zp_ref, zn_ref, mu_ref, w0_ref, wup_ref, a0_ref, aup_ref, gup_ref,
                    r_ref, k_ref, v_ref, wf_ref, wb_ref, af_ref, ab_ref, g_ref):
    i = pl.program_id(1)
    nblk = pl.num_programs(1)
    tt = z_ref.shape[1]
    d = r_ref.shape[0]
    is_lat = i > 0
    row = lax.broadcasted_iota(jnp.int32, (tt, 1), 0)
    col = row % GRID_W
    lmask = jnp.where(is_lat, col, row) == 0
    rmask = jnp.where(is_lat, col, row - (tt - GRID_W)) == GRID_W - 1
    latf = is_lat.astype(F32)
    up_ok = (i > 1).astype(F32)
    dn_ok = (i < nblk - 1).astype(F32)

    def shifted(c0, c1):
        z = z_ref[0, :, c0:c1]
        mu = mu_ref[:, c0:c1]
        left = jnp.where(lmask, 0.0, pltpu.roll(z, 1, 0))
        right = jnp.where(rmask, 0.0, pltpu.roll(z, tt - 1, 0))
        up = jnp.concatenate([zp_ref[0, :, c0:c1] * up_ok, z[:tt - GRID_W]], axis=0)
        down = jnp.concatenate([z[GRID_W:], zn_ref[0, :, c0:c1] * dn_ok], axis=0)
        out = z + mu[0:1] * (left - z) + mu[1:2] * (right - z)
        return out + latf * (mu[2:3] * (up - z) + mu[3:4] * (down - z))

    r_ref[...] = shifted(0, d).T
    k_ref[...] = shifted(d, 2 * d).T
    v_ref[...] = shifted(2 * d, 3 * d).T
    rest = shifted(3 * d, z_ref.shape[2])
    xwa = rest[:, :LANES]
    xw_t = jnp.tanh(xwa).astype(BF16)
    xa_b = xwa.astype(BF16)
    for dr, (wd_ref, ad_ref) in enumerate(((wf_ref, af_ref), (wb_ref, ab_ref))):
        u = -(w0_ref[dr:dr + 1, :] + _bdot(xw_t, wup_ref[dr]))
        softplus = jnp.maximum(u, 0.0) + jnp.log(1.0 + jnp.exp(-jnp.abs(u)))
        wd_ref[...] = jnp.exp(-jnp.exp(-softplus - 0.5)).T
        ad_ref[...] = _sigmoid(a0_ref[dr:dr + 1, :] + _bdot(xa_b, aup_ref[dr])).T
    g_ref[...] = _bdot(_sigmoid(rest[:, LANES:]).astype(BF16), gup_ref[...])


def _rw_prep(zr, mu, w0, wup, a0, aup, gup, d):
    b, tt, nr = zr.shape
    nblk = tt // ROW_TILE
    per = ROW_TILE // GRID_W
    last = tt // GRID_W - 1
    const = lambda shape: pl.BlockSpec(shape, lambda bi, i: (0,) * len(shape))
    row_spec = pl.BlockSpec((d, ROW_TILE), lambda bi, i: (bi, i))
    gate_spec = pl.BlockSpec((ROW_TILE, d), lambda bi, i: (i, bi))
    row_shape = jax.ShapeDtypeStruct((b * d, tt), F32)
    gate_shape = jax.ShapeDtypeStruct((tt, b * d), F32)
    return pl.pallas_call(
        _rw_prep_kernel,
        grid=(b, nblk),
        in_specs=[pl.BlockSpec((1, ROW_TILE, nr), lambda bi, i: (bi, i, 0)),
                  pl.BlockSpec((1, GRID_W, nr), lambda bi, i: (bi, jnp.maximum(i * per - 1, 0), 0)),
                  pl.BlockSpec((1, GRID_W, nr), lambda bi, i: (bi, jnp.minimum(i * per + per, last), 0)),
                  const(mu.shape), const(w0.shape), const(wup.shape), const(a0.shape),
                  const(aup.shape), const(gup.shape)],
        out_specs=[row_spec] * 7 + [gate_spec],
        out_shape=[row_shape] * 7 + [gate_shape],
        compiler_params=_params("arbitrary", "arbitrary"),
        name="rw_prep",
    )(zr, zr, zr, mu, w0, wup, a0, aup, gup)


K_UNROLL = 16
ACC_WAYS = 1


def _tree_sum(xs):
    while len(xs) > 1:
        xs = [xs[i] + xs[i + 1] for i in range(0, len(xs) - 1, 2)] + ([xs[-1]] if len(xs) % 2 else [])
    return xs[0]


def _seq_block(n_ctx_blocks, n_blocks):
    def blk(d, i):
        bwd = jnp.where(i < n_ctx_blocks, n_ctx_blocks - 1 - i, n_blocks - 1 + n_ctx_blocks - i)
        return jnp.where(d == 0, i, bwd)
    return blk


def _rw_scan_kernel(*refs, backward):
    if backward:
        (r_ref, k_ref, v_ref, w_ref, a_ref, kkp_ref, kap_ref, yf_ref, af_ref, rkp_ref, lnw_ref, lnb_ref,
         y_ref, s_ref, p_s, sa_s, kk0_s, bt_s, kt_s, rd_s, kkd_s) = refs
    else:
        (r_ref, k_ref, v_ref, w_ref, a_ref, kkp_ref, kap_ref, _order_ref,
         y_ref, s_ref, p_s, sa_s, kk0_s, bt_s, kt_s, rd_s, kkd_s) = refs
    i = pl.program_id(0)
    tb = r_ref.shape[0]
    nv = RW_N // SUBLANES

    @pl.when(i == 0)
    def _():
        s_ref[...] = jnp.zeros_like(s_ref)
        p_s[...] = jnp.ones_like(p_s)

    def bcast(ref, *idx):
        k = idx[-1]
        row = ref[(*idx[:-1], pl.ds(k, 1), slice(None))]
        return jnp.broadcast_to(row, (SUBLANES, LANES))

    def time_index(s):
        s = jnp.minimum(s, tb - 1)
        return tb - 1 - s if backward else s

    def bf16_pair(x):
        hi = lax.bitcast_convert_type(x.astype(BF16).astype(F32), jnp.uint32)
        return lax.bitcast_convert_type(hi | (hi >> 16), F32)

    def norm_key(t):
        kkr = k_ref[t] * kkp_ref[...]
        nrm = jnp.sqrt(jnp.sum(kkr * kkr, axis=0, keepdims=True))
        return kkr / jnp.maximum(nrm, L2_EPS)

    def scaled_key(t, a):
        return k_ref[t] * (1.0 + (a - 1.0) * kap_ref[...])

    def prepare(s, slot):
        t = time_index(s)
        a = a_ref[t]
        kk = norm_key(t)
        p_prev = p_s[...]
        kkd_s[1 - slot] = bf16_pair(p_prev * kk)
        p = p_prev * jnp.where(s < tb, w_ref[t], 1.0)
        p_s[...] = p
        inv_p = 1.0 / p
        bt_s[slot] = kk * a * inv_p
        kt_s[slot] = scaled_key(t, a) * inv_p
        rd_s[slot] = bf16_pair(p * r_ref[t])

    def restart():
        kk0_s[...] = norm_key(time_index(0))
        acc = [None] * nv
        for k in range(RW_N):
            pb = bcast(p_s, k)
            kkb = bcast(kk0_s, k)
            for j in range(nv):
                rows = slice(SUBLANES * j, SUBLANES * (j + 1))
                sn = s_ref[k, rows, :] * pb
                s_ref[k, rows, :] = sn
                acc[j] = sn * kkb if acc[j] is None else acc[j] + sn * kkb
        sa_s[...] = -jnp.concatenate(acc, axis=0)
        p_s[...] = jnp.ones_like(p_s)

    def sweep(s, slot):
        t = time_index(s)
        zero = jnp.zeros((SUBLANES, LANES), F32)

        def key_block(kblk, carry):
            yacc, acc = list(carry[0]), list(carry[1])
            yb = [jnp.zeros((2 * SUBLANES, LANES), BF16)] * (nv // 2)
            ab = [jnp.zeros((2 * SUBLANES, LANES), BF16)] * (nv // 2)
            for kk in range(K_UNROLL):
                k = kblk * K_UNROLL + kk
                bb = bcast(bt_s, slot, k)
                kb = bcast(kt_s, slot, k)
                rb = pltpu.bitcast(bcast(rd_s, slot, k), BF16)
                kkn = pltpu.bitcast(bcast(kkd_s, slot, k), BF16)
                for m in range(nv // 2):
                    pair_rows = []
                    for j in (2 * m, 2 * m + 1):
                        rows = slice(SUBLANES * j, SUBLANES * (j + 1))
                        sn = s_ref[k, rows, :] + (sa_s[rows, :] * bb + v_ref[t, rows, :] * kb)
                        s_ref[k, rows, :] = sn
                        pair_rows.append(sn)
                    snp = jnp.concatenate(pair_rows, axis=0).astype(BF16)
                    yb[m] = yb[m] + snp * rb
                    ab[m] = ab[m] + snp * kkn
            for m in range(nv // 2):
                y32 = yb[m].astype(F32)
                a32 = ab[m].astype(F32)
                for h, j in enumerate((2 * m, 2 * m + 1)):
                    yacc[j] = yacc[j] + y32[SUBLANES * h:SUBLANES * (h + 1)]
                    acc[j] = acc[j] + a32[SUBLANES * h:SUBLANES * (h + 1)]
            return yacc, acc

        init = [zero] * nv
        n_kblk = RW_N // K_UNROLL
        carry = lax.fori_loop(0, n_kblk - 1, key_block, (init, init))
        yacc, acc = key_block(n_kblk - 1, carry)
        sa_s[...] = -jnp.concatenate(acc, axis=0)
        y = jnp.concatenate(yacc, axis=0)
        if backward:
            y = y + yf_ref[t]
            yc = y - jnp.mean(y, axis=0, keepdims=True)
            var = jnp.mean(yc * yc, axis=0, keepdims=True)
            y = yc * lax.rsqrt(var + RW_GN_EPS) * lnw_ref[...] + lnb_ref[...]
            k_sum = scaled_key(t, af_ref[t]) + scaled_key(t, a_ref[t])
            y = y + jnp.sum(r_ref[t] * k_sum * rkp_ref[...], axis=0, keepdims=True) * v_ref[t]
        y_ref[t] = y

    restart()
    prepare(0, 0)
    prepare(1, 1)

    def pair(p, carry):
        s = 2 * p
        sweep(s, 0)
        prepare(s + 2, 0)
        sweep(s + 1, 1)
        prepare(s + 3, 1)
        return carry

    lax.fori_loop(0, tb // 2, pair, 0)


def _rw_scan(r_t, k_t, v_t, w_t, a_t, kkp, kap, n_ctx_blocks, *, run_after=None, readout=None):
    backward = readout is not None
    direction = int(backward)
    tt = r_t.shape[0]
    nblk = tt // SCAN_TILE
    seq = _seq_block(n_ctx_blocks, nblk)
    blk = lambda i: seq(direction, i)
    tile = (SCAN_TILE, RW_N, LANES)
    shared = pl.BlockSpec(tile, lambda i: (blk(i), 0, 0))
    const = pl.BlockSpec((RW_N, LANES), lambda i: (0, 0))
    y_blk = lambda i: jnp.where(i < n_ctx_blocks, blk(n_ctx_blocks), blk(i)) - n_ctx_blocks
    y_spec = pl.BlockSpec(tile, lambda i: (y_blk(i), 0, 0))
    in_specs = [shared, shared, shared, shared, shared, const, const]
    args = [r_t, k_t, v_t, w_t, a_t, kkp, kap]
    if backward:
        y_fwd, a_fwd, rkp, lnw, lnb = readout
        in_specs += [y_spec, shared, const, const, const]
        args += [y_fwd, a_fwd, rkp, lnw, lnb]
    else:
        in_specs += [pl.BlockSpec(memory_space=pl.ANY)]
        args += [run_after]
    return pl.pallas_call(
        functools.partial(_rw_scan_kernel, backward=backward),
        grid=(nblk,),
        in_specs=in_specs,
        out_specs=y_spec,
        out_shape=jax.ShapeDtypeStruct((tt - n_ctx_blocks * SCAN_TILE, RW_N, LANES), F32),
        scratch_shapes=[pltpu.VMEM((RW_N, RW_N, LANES), F32)] + [pltpu.VMEM((RW_N, LANES), F32)] * 3
                       + [pltpu.VMEM((2, RW_N, LANES), F32)] * 4,
        compiler_params=_params("arbitrary"),
        name="rw_scan_bwd" if backward else "rw_scan_fwd",
    )(*args)


def _hg_kernel(q_ref, f_ref, i_ref, lbp_ref, tri_ref, o_ref, st_ref, qd_scr, kv_scr, dec_scr):
    d = pl.program_id(1)
    i = pl.program_id(2)
    n_chunks = q_ref.shape[1] // HG_CHUNK

    @pl.when(i == 0)
    def _():
        st_ref[...] = jnp.zeros_like(st_ref)

    lbp = lbp_ref[...]
    e = jnp.exp(lbp - jnp.max(lbp, axis=0, keepdims=True))
    lb = e[0:1] / jnp.sum(e, axis=0, keepdims=True)
    t_idx = lax.broadcasted_iota(jnp.int32, (HG_CHUNK, HG_CHUNK), 0)
    s_idx = lax.broadcasted_iota(jnp.int32, (HG_CHUNK, HG_CHUNK), 1)
    nt = (((1,), (1,)), ((), ()))
    tn = (((0,), (0,)), ((), ()))

    def run(fwd):
        mask = (s_idx <= t_idx) if fwd else (s_idx >= t_idx)
        mid_row = HG_CHUNK // 2 - 1 if fwd else HG_CHUNK // 2
        last_row = HG_CHUNK - 1 if fwd else 0
        order = list(range(n_chunks)) if fwd else list(reversed(range(n_chunks)))

        f = lb + (1.0 - lb) * _sigmoid(f_ref[0])
        lf = jnp.log(f)
        hi = lf.astype(BF16)
        lo = (lf - hi.astype(F32)).astype(BF16)
        tri = tri_ref[0]
        b = _bdot(tri, hi) + _bdot(tri, lo)
        for c in order:
            rows = slice(c * HG_CHUNK, (c + 1) * HG_CHUNK)
            bc = b[rows]
            b_mid = bc[mid_row:mid_row + 1]
            b_last = bc[last_row:last_row + 1]
            q_in = q_ref[0, rows, :] * jnp.exp(bc - b_mid)
            k_in = (1.0 - f[rows]) * jnp.exp(b_mid - bc)
            qd_scr[rows, :] = (q_in * jnp.exp(b_mid)).astype(BF16)
            k_dec = (k_in * jnp.exp(b_last - b_mid)).astype(BF16)
            dec_scr[c:c + 1, :] = jnp.exp(b_last)
            q_in = q_in.astype(BF16)
            k_in = k_in.astype(BF16)
            vb = i_ref[0, rows, :].astype(BF16)
            heads = [slice(h * HG_N, (h + 1) * HG_N) for h in range(HG_HEADS)]
            scores = [lax.dot_general(q_in[:, cols], k_in[:, cols], nt, preferred_element_type=F32)
                      for cols in heads]
            for h, cols in enumerate(heads):
                kv_scr[c, h] = lax.dot_general(vb[:, cols], k_dec[:, cols], tn, preferred_element_type=F32)
            for h, cols in enumerate(heads):
                o_ref[0, 0, rows, cols] = _bdot(jnp.where(mask, scores[h], 0.0).astype(BF16), vb[:, cols])
        for c in order:
            rows = slice(c * HG_CHUNK, (c + 1) * HG_CHUNK)
            for h in range(HG_HEADS):
                cols = slice(h * HG_N, (h + 1) * HG_N)
                st = st_ref[h]
                o_ref[0, 0, rows, cols] += lax.dot_general(qd_scr[rows, cols], st.astype(BF16), nt,
                                                           preferred_element_type=F32)
                st_ref[h] = dec_scr[c:c + 1, cols] * st + kv_scr[c, h]

    @pl.when(d == 0)
    def _():
        run(True)

    @pl.when(d == 1)
    def _():
        run(False)


def _hg_tri(n_rows):
    t = jnp.arange(n_rows)[:, None]
    s = jnp.arange(n_rows)[None, :]
    same = (t // HG_CHUNK) == (s // HG_CHUNK)
    return jnp.stack([same & (s <= t), same & (s >= t)]).astype(BF16)


def _hg(zh, hg_lb, n_ctx):
    b, tt, _ = zh.shape
    w = HG_HEADS * HG_N
    nblk = tt // ROW_TILE
    n_chunks = ROW_TILE // HG_CHUNK
    blk = _seq_block(n_ctx // ROW_TILE, nblk)
    return pl.pallas_call(
        _hg_kernel,
        grid=(b, 2, nblk),
        in_specs=[pl.BlockSpec((1, ROW_TILE, w), lambda bi, d, i: (bi, blk(d, i), 0)),
                  pl.BlockSpec((1, ROW_TILE, w), lambda bi, d, i: (bi, blk(d, i), 1 + d)),
                  pl.BlockSpec((1, ROW_TILE, w), lambda bi, d, i: (bi, blk(d, i), 3)),
                  pl.BlockSpec(hg_lb.shape, lambda bi, d, i: (0, 0)),
                  pl.BlockSpec((1, ROW_TILE, ROW_TILE), lambda bi, d, i: (d, 0, 0))],
        out_specs=pl.BlockSpec((1, 1, ROW_TILE, w), lambda bi, d, i: (d, bi, blk(d, i), 0)),
        out_shape=jax.ShapeDtypeStruct((2, b, tt, w), F32),
        scratch_shapes=[pltpu.VMEM((HG_HEADS, HG_N, HG_N), F32),
                        pltpu.VMEM((ROW_TILE, w), BF16),
                        pltpu.VMEM((n_chunks, HG_HEADS, HG_N, HG_N), F32),
                        pltpu.VMEM((SUBLANES, w), F32)],
        compiler_params=_params("arbitrary", "arbitrary", "arbitrary"),
        name="hg",
    )(zh, zh, zh, hg_lb, _hg_tri(ROW_TILE))


def _merge_kernel(yrw_ref, g_ref, o_ref, gz_ref, zg_ref, x_ref, g1_ref, sh2_ref, sc2_ref, g2_ref, hnw_ref,
                  pa_ref, pb_ref, wo_ref, nw2_ref, w1_ref, w2_ref, fw_ref, out_ref):
    d = x_ref.shape[2]
    y_rw = (yrw_ref[...].T * g_ref[...]).astype(BF16)
    o = o_ref[0, 0] + o_ref[1, 0]
    gz = gz_ref[0]
    hnw = hnw_ref[...]
    parts = []
    for h in range(HG_HEADS):
        cols = slice(h * HG_N, (h + 1) * HG_N)
        oh = o[:, cols]
        ms = jnp.mean(oh * oh, axis=-1, keepdims=True)
        parts.append(oh * lax.rsqrt(ms + RMS_EPS) * hnw[:, cols])
    y_hg = (jnp.concatenate(parts, axis=1) * (gz * _sigmoid(gz))).astype(BF16)
    zg = zg_ref[0]
    m = _sigmoid(zg[:, :d]) * _bdot(y_rw, pa_ref[...]) + _sigmoid(zg[:, d:]) * _bdot(y_hg, pb_ref[...])
    x1 = x_ref[0] + g1_ref[0] * _bdot(m.astype(BF16), wo_ref[...])
    hb = _modulated_norm(x1, nw2_ref[...], sh2_ref[0], sc2_ref[0]).astype(BF16)
    u = jnp.maximum(_bdot(hb, w1_ref[...]), 0.0)
    y = x1 + g2_ref[0] * _bdot((u * u).astype(BF16), w2_ref[...])
    ms = jnp.mean(y * y, axis=-1, keepdims=True)
    out_ref[0] = y * lax.rsqrt(ms + RMS_EPS) * fw_ref[...]


def _merge(y_rw, g, o_hg, zh, zg, x, modcat, hnw, pa, pb, wo, nw2, w1, w2, fw, n_ctx):
    b, t, d = x.shape
    off = n_ctx // ROW_TILE
    lat = lambda bi, i: (bi, i, 0)
    cat = lambda bi, i: (bi, i + off, 0)
    mod = lambda col: pl.BlockSpec((1, 1, d), lambda bi, i: (2 * bi + 1, 0, col))
    const = lambda a: pl.BlockSpec(a.shape, lambda bi, i: (0,) * a.ndim, pipeline_mode=pl.Buffered(1))
    return pl.pallas_call(
        _merge_kernel,
        grid=(b, t // ROW_TILE),
        in_specs=[pl.BlockSpec((d, ROW_TILE), lambda bi, i: (bi, i)),
                  pl.BlockSpec((ROW_TILE, d), lambda bi, i: (i + off, bi)),
                  pl.BlockSpec((2, 1, ROW_TILE, d), lambda bi, i: (0, bi, i + off, 0)),
                  pl.BlockSpec((1, ROW_TILE, d), lambda bi, i: (bi, i + off, 4)),
                  pl.BlockSpec((1, ROW_TILE, 2 * d), cat),
                  pl.BlockSpec((1, ROW_TILE, d), lat),
                  mod(2), mod(3), mod(4), mod(5),
                  const(hnw), const(pa), const(pb), const(wo), const(nw2), const(w1), const(w2), const(fw)],
        out_specs=pl.BlockSpec((1, ROW_TILE, d), lat),
        out_shape=jax.ShapeDtypeStruct((b, t, d), F32),
        compiler_params=_params("arbitrary", "arbitrary"),
        name="merge_mlp",
    )(y_rw, g, o_hg, zh, zg, x, modcat, modcat, modcat, modcat, hnw, pa, pb, wo, nw2, w1, w2, fw)


def _to_scan(a):
    *lead, w, t = a.shape
    return jnp.swapaxes(a.reshape(*lead, w // RW_N, RW_N, t), -1, -3)


def _from_scan(a):
    t, n, bh = a.shape
    return jnp.swapaxes(a, 0, 2).reshape(bh * n, t)


def _head_tile(p, b):
    return jnp.tile(p.reshape(RW_HEADS, RW_N).T, (1, b))


def kernel(x, c, ctx, c_ctx, norm1_w, norm2_w, w_mod, b_mod, w_in, rw_mu, rw_w0, rw_w_up, rw_a0, rw_a_up, rw_g_up, rw_k_k, rw_k_a, rw_r_k, rw_ln_w, rw_ln_b, hg_lb, hg_norm_w, p_a, p_b, w_out, w_fc1, w_fc2, final_norm_w):
    b, t, d = x.shape
    n_ctx = ctx.shape[1]
    assert w_mod.shape[0] == 1, "single-layer block"
    assert b * RW_HEADS == LANES and n_ctx == ROW_TILE and t % ROW_TILE == 0
    assert d == RW_HEADS * RW_N == HG_HEADS * HG_N
    n_r = rw_mu.shape[2]
    n_h = 5 * d
    n_g = 2 * d
    rank_w, rank_a = rw_w_up.shape[2], rw_a_up.shape[2]
    assert rank_w + rank_a == LANES and n_r == 3 * d + 2 * LANES

    c_rows = jnp.zeros((2 * SUBLANES, d), F32).at[:b].set(c).at[b].set(c_ctx)
    mod = _mod(c_rows, w_mod[0], b_mod)
    modcat = jnp.stack([jnp.broadcast_to(mod[b], (b, N_MOD * d)), mod[:b]], axis=1).reshape(2 * b, 1, N_MOD * d)

    zr, zh, zg = _in_proj(ctx, x, norm1_w, modcat, w_in[0].astype(BF16), n_r, n_h, n_g)

    wup = jnp.pad(rw_w_up[0], ((0, 0), (0, rank_a), (0, 0))).astype(BF16)
    aup = jnp.pad(rw_a_up[0], ((0, 0), (rank_w, 0), (0, 0))).astype(BF16)
    *scan_ops, g = _rw_prep(zr, rw_mu[0], rw_w0[0], wup, rw_a0[0], aup, rw_g_up[0].astype(BF16), d)
    r_t, k_t, v_t, wf_t, wb_t, af_t, ab_t = (_to_scan(a) for a in scan_ops)
    o_hg = _hg(zh, hg_lb, n_ctx)
    kap = _head_tile(rw_k_a[0], b)
    tiles = (_head_tile(rw_k_k[0], b), kap, n_ctx // SCAN_TILE)
    y_fwd = _rw_scan(r_t, k_t, v_t, wf_t, af_t, *tiles, run_after=o_hg)
    y_rw_t = _rw_scan(r_t, k_t, v_t, wb_t, ab_t, *tiles,
                      readout=(y_fwd, af_t, _head_tile(rw_r_k[0].reshape(-1), b),
                               _head_tile(rw_ln_w[0], b), _head_tile(rw_ln_b[0], b)))
    y_rw = _from_scan(y_rw_t)

    return _merge(y_rw, g, o_hg, zh, zg, x, modcat, hg_norm_w, p_a[0].astype(BF16), p_b[0].astype(BF16),
                  w_out[0].astype(BF16), norm2_w, w_fc1[0].astype(BF16), w_fc2[0].astype(BF16),
                  final_norm_w.reshape(1, d), n_ctx)
```

```python
import functools

import jax
import jax.numpy as jnp
from jax import lax
from jax.experimental import pallas as pl
from jax.experimental.pallas import tpu as pltpu

F32 = jnp.float32
BF16 = jnp.bfloat16

GRID_W = 64
RW_HEADS = 16
RW_N = 64
HG_HEADS = 8
HG_N = 128
HG_CHUNK = 64
N_MOD = 6
RMS_EPS = 1e-6
RW_GN_EPS = 64e-5
L2_EPS = 1e-12

SUBLANES = 8
LANES = 128
ROW_TILE = 256
SCAN_TILE = 64
VMEM_LIMIT = 56 * 1024 * 1024


def _params(*sem):
    return pltpu.CompilerParams(dimension_semantics=sem, vmem_limit_bytes=VMEM_LIMIT)


def _bdot(a, b):
    return jnp.dot(a, b, preferred_element_type=F32)


def _sigmoid(x):
    return 1.0 / (1.0 + jnp.exp(-x))


def _mod_kernel(c_ref, w_ref, b_ref, o_ref):
    c = c_ref[...]
    act = c * _sigmoid(c)
    o_ref[...] = _bdot(act.astype(BF16), w_ref[...].astype(BF16)) + b_ref[...]


def _mod(c_rows, w_mod, b_mod):
    rows, d = c_rows.shape
    n = w_mod.shape[1]
    return pl.pallas_call(
        _mod_kernel,
        grid=(n // d,),
        in_specs=[pl.BlockSpec((rows, d), lambda j: (0, 0)),
                  pl.BlockSpec((d, d), lambda j: (0, j)),
                  pl.BlockSpec((1, d), lambda j: (0, j))],
        out_specs=pl.BlockSpec((rows, d), lambda j: (0, j)),
        out_shape=jax.ShapeDtypeStruct((rows, n), F32),
        compiler_params=_params("arbitrary"),
        name="mod",
    )(c_rows, w_mod, b_mod)


def _modulated_norm(x, nw, sh, sc):
    ms = jnp.mean(x * x, axis=-1, keepdims=True)
    return (x * lax.rsqrt(ms + RMS_EPS) * nw) * (1.0 + sc) + sh


def _in_proj_kernel(ctx_ref, x_ref, nw_ref, sh_ref, sc_ref, w_ref, zr_ref, zqi_ref, zh_ref, zg_ref):
    tokens = jnp.where(pl.program_id(1) == 0, ctx_ref[0], x_ref[0])
    hb = _modulated_norm(tokens, nw_ref[...], sh_ref[0], sc_ref[0]).astype(BF16)
    d = x_ref.shape[2]
    n_r = zr_ref.shape[2]
    col = lambda j0, j1: _bdot(hb, w_ref[:, n_r + j0 * d:n_r + j1 * d])
    zr_ref[0] = _bdot(hb, w_ref[:, :n_r])
    zqi_ref[0, :, :d] = col(0, 1).astype(BF16)
    zqi_ref[0, :, d:] = col(3, 4).astype(BF16)
    zh_ref[0, :, :2 * d] = col(1, 3)
    zh_ref[0, :, 2 * d:] = col(4, 5)
    zg_ref[0] = col(5, 7)


def _in_proj(ctx, x, nw, modcat, w_bf, n_r):
    b, t, d = x.shape
    tt = t + ctx.shape[1]
    nblk = tt // ROW_TILE
    mod_idx = lambda col: (lambda bi, i: (2 * bi + jnp.minimum(i, 1), 0, col))
    outs = ((n_r, F32), (2 * d, BF16), (3 * d, F32), (2 * d, F32))
    return pl.pallas_call(
        _in_proj_kernel,
        grid=(b, nblk),
        in_specs=[pl.BlockSpec((1, ROW_TILE, d), lambda bi, i: (bi, 0, 0)),
                  pl.BlockSpec((1, ROW_TILE, d), lambda bi, i: (bi, jnp.maximum(i - 1, 0), 0)),
                  pl.BlockSpec((1, d), lambda bi, i: (0, 0)),
                  pl.BlockSpec((1, 1, d), mod_idx(0)),
                  pl.BlockSpec((1, 1, d), mod_idx(1)),
                  pl.BlockSpec(w_bf.shape, lambda bi, i: (0, 0), pipeline_mode=pl.Buffered(1))],
        out_specs=[pl.BlockSpec((1, ROW_TILE, n), lambda bi, i: (bi, i, 0)) for n, _ in outs],
        out_shape=[jax.ShapeDtypeStruct((b, tt, n), dt) for n, dt in outs],
        compiler_params=_params("arbitrary", "arbitrary"),
        name="in_proj",
    )(ctx, x, nw, modcat, modcat, w_bf)


def _rw_prep_kernel(z_ref, zp_ref, zn_ref, mu_ref, w0_ref, wup_ref, a0_ref, aup_ref, gup_ref,
                    r_ref, k_ref, v_ref, wf_ref, wb_ref, af_ref, ab_ref, g_ref):
    i = pl.program_id(1)
    nblk = pl.num_programs(1)
    tt = z_ref.shape[1]
    d = r_ref.shape[0]
    is_lat = i > 0
    row = lax.broadcasted_iota(jnp.int32, (tt, 1), 0)
    col = row % GRID_W
    lmask = jnp.where(is_lat, col, row) == 0
    rmask = jnp.where(is_lat, col, row - (tt - GRID_W)) == GRID_W - 1
    latf = is_lat.astype(F32)
    up_ok = (i > 1).astype(F32)
    dn_ok = (i < nblk - 1).astype(F32)

    def shifted(c0, c1):
        z = z_ref[0, :, c0:c1]
        mu = mu_ref[:, c0:c1]
        left = jnp.where(lmask, 0.0, pltpu.roll(z, 1, 0))
        right = jnp.where(rmask, 0.0, pltpu.roll(z, tt - 1, 0))
        up = jnp.concatenate([zp_ref[0, :, c0:c1] * up_ok, z[:tt - GRID_W]], axis=0)
        down = jnp.concatenate([z[GRID_W:], zn_ref[0, :, c0:c1] * dn_ok], axis=0)
        out = z + mu[0:1] * (left - z) + mu[1:2] * (right - z)
        return out + latf * (mu[2:3] * (up - z) + mu[3:4] * (down - z))

    r_ref[...] = shifted(0, d).T
    k_ref[...] = shifted(d, 2 * d).T
    v_ref[...] = shifted(2 * d, 3 * d).T
    rest = shifted(3 * d, z_ref.shape[2])
    xwa = rest[:, :LANES]
    xw_t = jnp.tanh(xwa).astype(BF16)
    xa_b = xwa.astype(BF16)
    for dr, (wd_ref, ad_ref) in enumerate(((wf_ref, af_ref), (wb_ref, ab_ref))):
        u = -(w0_ref[dr:dr + 1, :] + _bdot(xw_t, wup_ref[dr]))
        softplus = jnp.maximum(u, 0.0) + jnp.log(1.0 + jnp.exp(-jnp.abs(u)))
        wd_ref[...] = jnp.exp(-jnp.exp(-softplus - 0.5)).T
        ad_ref[...] = _sigmoid(a0_ref[dr:dr + 1, :] + _bdot(xa_b, aup_ref[dr])).T
    g_ref[...] = _bdot(_sigmoid(rest[:, LANES:]).astype(BF16), gup_ref[...])


def _rw_prep(zr, mu, w0, wup, a0, aup, gup, d):
    b, tt, nr = zr.shape
    nblk = tt // ROW_TILE
    per = ROW_TILE // GRID_W
    last = tt // GRID_W - 1
    const = lambda shape: pl.BlockSpec(shape, lambda bi, i: (0,) * len(shape))
    row_spec = pl.BlockSpec((d, ROW_TILE), lambda bi, i: (bi, i))
    gate_spec = pl.BlockSpec((ROW_TILE, d), lambda bi, i: (i, bi))
    row_shape = jax.ShapeDtypeStruct((b * d, tt), F32)
    gate_shape = jax.ShapeDtypeStruct((tt, b * d), F32)
    return pl.pallas_call(
        _rw_prep_kernel,
        grid=(b, nblk),
        in_specs=[pl.BlockSpec((1, ROW_TILE, nr), lambda bi, i: (bi, i, 0)),
                  pl.BlockSpec((1, GRID_W, nr), lambda bi, i: (bi, jnp.maximum(i * per - 1, 0), 0)),
                  pl.BlockSpec((1, GRID_W, nr), lambda bi, i: (bi, jnp.minimum(i * per + per, last), 0)),
                  const(mu.shape), const(w0.shape), const(wup.shape), const(a0.shape),
                  const(aup.shape), const(gup.shape)],
        out_specs=[row_spec] * 7 + [gate_spec],
        out_shape=[row_shape] * 7 + [gate_shape],
        compiler_params=_params("arbitrary", "arbitrary"),
        name="rw_prep",
    )(zr, zr, zr, mu, w0, wup, a0, aup, gup)


K_UNROLL = 16
ACC_WAYS = 1


def _tree_sum(xs):
    while len(xs) > 1:
        xs = [xs[i] + xs[i + 1] for i in range(0, len(xs) - 1, 2)] + ([xs[-1]] if len(xs) % 2 else [])
    return xs[0]


def _seq_block(n_ctx_blocks, n_blocks):
    def blk(d, i):
        bwd = jnp.where(i < n_ctx_blocks, n_ctx_blocks - 1 - i, n_blocks - 1 + n_ctx_blocks - i)
        return jnp.where(d == 0, i, bwd)
    return blk


def _rw_scan_kernel(*refs, backward):
    if backward:
        (r_ref, k_ref, v_ref, w_ref, a_ref, kkp_ref, kap_ref, yf_ref, af_ref, rkp_ref, lnw_ref, lnb_ref,
         y_ref, s_ref, p_s, sa_s, kk0_s, bt_s, kt_s, rd_s, kkd_s) = refs
    else:
        (r_ref, k_ref, v_ref, w_ref, a_ref, kkp_ref, kap_ref, _order_ref,
         y_ref, s_ref, p_s, sa_s, kk0_s, bt_s, kt_s, rd_s, kkd_s) = refs
    i = pl.program_id(0)
    tb = r_ref.shape[0]
    nv = RW_N // SUBLANES

    @pl.when(i == 0)
    def _():
        s_ref[...] = jnp.zeros_like(s_ref)
        p_s[...] = jnp.ones_like(p_s)

    def bcast(ref, *idx):
        k = idx[-1]
        row = ref[(*idx[:-1], pl.ds(k, 1), slice(None))]
        return jnp.broadcast_to(row, (SUBLANES, LANES))

    def time_index(s):
        s = jnp.minimum(s, tb - 1)
        return tb - 1 - s if backward else s

    def bf16_pair(x):
        hi = lax.bitcast_convert_type(x.astype(BF16).astype(F32), jnp.uint32)
        return lax.bitcast_convert_type(hi | (hi >> 16), F32)

    def norm_key(t):
        kkr = k_ref[t] * kkp_ref[...]
        nrm = jnp.sqrt(jnp.sum(kkr * kkr, axis=0, keepdims=True))
        return kkr / jnp.maximum(nrm, L2_EPS)

    def scaled_key(t, a):
        return k_ref[t] * (1.0 + (a - 1.0) * kap_ref[...])

    def prepare(s, slot):
        t = time_index(s)
        a = a_ref[t]
        kk = norm_key(t)
        p_prev = p_s[...]
        kkd_s[1 - slot] = bf16_pair(p_prev * kk)
        p = p_prev * jnp.where(s < tb, w_ref[t], 1.0)
        p_s[...] = p
        inv_p = 1.0 / p
        bt_s[slot] = kk * a * inv_p
        kt_s[slot] = scaled_key(t, a) * inv_p
        rd_s[slot] = bf16_pair(p * r_ref[t])

    def restart():
        kk0_s[...] = norm_key(time_index(0))
        acc = [None] * nv
        for k in range(RW_N):
            pb = bcast(p_s, k)
            kkb = bcast(kk0_s, k)
            for j in range(nv):
                rows = slice(SUBLANES * j, SUBLANES * (j + 1))
                sn = s_ref[k, rows, :] * pb
                s_ref[k, rows, :] = sn
                acc[j] = sn * kkb if acc[j] is None else acc[j] + sn * kkb
        sa_s[...] = -jnp.concatenate(acc, axis=0)
        p_s[...] = jnp.ones_like(p_s)

    def sweep(s, slot):
        t = time_index(s)
        zero = jnp.zeros((SUBLANES, LANES), F32)

        def key_block(kblk, carry):
            yacc, acc = list(carry[0]), list(carry[1])
            yb = [jnp.zeros((2 * SUBLANES, LANES), BF16)] * (nv // 2)
            ab = [jnp.zeros((2 * SUBLANES, LANES), BF16)] * (nv // 2)
            for kk in range(K_UNROLL):
                k = kblk * K_UNROLL + kk
                bb = bcast(bt_s, slot, k)
                kb = bcast(kt_s, slot, k)
                rb = pltpu.bitcast(bcast(rd_s, slot, k), BF16)
                kkn = pltpu.bitcast(bcast(kkd_s, slot, k), BF16)
                for m in range(nv // 2):
                    pair_rows = []
                    for j in (2 * m, 2 * m + 1):
                        rows = slice(SUBLANES * j, SUBLANES * (j + 1))
                        sn = s_ref[k, rows, :] + (sa_s[rows, :] * bb + v_ref[t, rows, :] * kb)
                        s_ref[k, rows, :] = sn
                        pair_rows.append(sn)
                    snp = jnp.concatenate(pair_rows, axis=0).astype(BF16)
                    yb[m] = yb[m] + snp * rb
                    ab[m] = ab[m] + snp * kkn
            for m in range(nv // 2):
                y32 = yb[m].astype(F32)
                a32 = ab[m].astype(F32)
                for h, j in enumerate((2 * m, 2 * m + 1)):
                    yacc[j] = yacc[j] + y32[SUBLANES * h:SUBLANES * (h + 1)]
                    acc[j] = acc[j] + a32[SUBLANES * h:SUBLANES * (h + 1)]
            return yacc, acc

        init = [zero] * nv
        n_kblk = RW_N // K_UNROLL
        carry = lax.fori_loop(0, n_kblk - 1, key_block, (init, init))
        yacc, acc = key_block(n_kblk - 1, carry)
        sa_s[...] = -jnp.concatenate(acc, axis=0)
        y = jnp.concatenate(yacc, axis=0)
        if backward:
            y = y + yf_ref[t]
            yc = y - jnp.mean(y, axis=0, keepdims=True)
            var = jnp.mean(yc * yc, axis=0, keepdims=True)
            y = yc * lax.rsqrt(var + RW_GN_EPS) * lnw_ref[...] + lnb_ref[...]
            k_sum = scaled_key(t, af_ref[t]) + scaled_key(t, a_ref[t])
            y = y + jnp.sum(r_ref[t] * k_sum * rkp_ref[...], axis=0, keepdims=True) * v_ref[t]
        y_ref[t] = y

    restart()
    prepare(0, 0)
    prepare(1, 1)

    def pair(p, carry):
        s = 2 * p
        sweep(s, 0)
        prepare(s + 2, 0)
        sweep(s + 1, 1)
        prepare(s + 3, 1)
        return carry

    lax.fori_loop(0, tb // 2, pair, 0)


def _rw_scan(r_t, k_t, v_t, w_t, a_t, kkp, kap, n_ctx_blocks, *, run_after=None, readout=None):
    backward = readout is not None
    direction = int(backward)
    tt = r_t.shape[0]
    nblk = tt // SCAN_TILE
    seq = _seq_block(n_ctx_blocks, nblk)
    blk = lambda i: seq(direction, i)
    tile = (SCAN_TILE, RW_N, LANES)
    shared = pl.BlockSpec(tile, lambda i: (blk(i), 0, 0))
    const = pl.BlockSpec((RW_N, LANES), lambda i: (0, 0))
    y_blk = lambda i: jnp.where(i < n_ctx_blocks, blk(n_ctx_blocks), blk(i)) - n_ctx_blocks
    y_spec = pl.BlockSpec(tile, lambda i: (y_blk(i), 0, 0))
    in_specs = [shared, shared, shared, shared, shared, const, const]
    args = [r_t, k_t, v_t, w_t, a_t, kkp, kap]
    if backward:
        y_fwd, a_fwd, rkp, lnw, lnb = readout
        in_specs += [y_spec, shared, const, const, const]
        args += [y_fwd, a_fwd, rkp, lnw, lnb]
    else:
        in_specs += [pl.BlockSpec(memory_space=pl.ANY)]
        args += [run_after]
    return pl.pallas_call(
        functools.partial(_rw_scan_kernel, backward=backward),
        grid=(nblk,),
        in_specs=in_specs,
        out_specs=y_spec,
        out_shape=jax.ShapeDtypeStruct((tt - n_ctx_blocks * SCAN_TILE, RW_N, LANES), F32),
        scratch_shapes=[pltpu.VMEM((RW_N, RW_N, LANES), F32)] + [pltpu.VMEM((RW_N, LANES), F32)] * 3
                       + [pltpu.VMEM((2, RW_N, LANES), F32)] * 4,
        compiler_params=_params("arbitrary"),
        name="rw_scan_bwd" if backward else "rw_scan_fwd",
    )(*args)


def _hg_kernel(q_ref, f_ref, i_ref, lbp_ref, tri_ref, o_ref, st_ref, qd_scr, kv_scr, dec_scr):
    d = pl.program_id(1)
    i = pl.program_id(2)
    n_chunks = q_ref.shape[1] // HG_CHUNK

    @pl.when(i == 0)
    def _():
        st_ref[...] = jnp.zeros_like(st_ref)

    lbp = lbp_ref[...]
    e = jnp.exp(lbp - jnp.max(lbp, axis=0, keepdims=True))
    lb = e[0:1] / jnp.sum(e, axis=0, keepdims=True)
    t_idx = lax.broadcasted_iota(jnp.int32, (HG_CHUNK, HG_CHUNK), 0)
    s_idx = lax.broadcasted_iota(jnp.int32, (HG_CHUNK, HG_CHUNK), 1)
    nt = (((1,), (1,)), ((), ()))
    tn = (((0,), (0,)), ((), ()))

    def run(fwd):
        mask = (s_idx <= t_idx) if fwd else (s_idx >= t_idx)
        mid_row = HG_CHUNK // 2 - 1 if fwd else HG_CHUNK // 2
        last_row = HG_CHUNK - 1 if fwd else 0
        order = list(range(n_chunks)) if fwd else list(reversed(range(n_chunks)))

        f = lb + (1.0 - lb) * _sigmoid(f_ref[0])
        lf = jnp.log(f)
        hi = lf.astype(BF16)
        lo = (lf - hi.astype(F32)).astype(BF16)
        tri = tri_ref[0]
        b = _bdot(tri, hi) + _bdot(tri, lo)
        for c in order:
            rows = slice(c * HG_CHUNK, (c + 1) * HG_CHUNK)
            bc = b[rows]
            b_mid = bc[mid_row:mid_row + 1]
            b_last = bc[last_row:last_row + 1]
            q_in = q_ref[0, rows, :] * jnp.exp(bc - b_mid)
            k_in = (1.0 - f[rows]) * jnp.exp(b_mid - bc)
            qd_scr[rows, :] = (q_in * jnp.exp(b_mid)).astype(BF16)
            k_dec = (k_in * jnp.exp(b_last - b_mid)).astype(BF16)
            dec_scr[c:c + 1, :] = jnp.exp(b_last)
            q_in = q_in.astype(BF16)
            k_in = k_in.astype(BF16)
            vb = i_ref[0, rows, :].astype(BF16)
            heads = [slice(h * HG_N, (h + 1) * HG_N) for h in range(HG_HEADS)]
            scores = [lax.dot_general(q_in[:, cols], k_in[:, cols], nt, preferred_element_type=F32)
                      for cols in heads]
            for h, cols in enumerate(heads):
                kv_scr[c, h] = lax.dot_general(vb[:, cols], k_dec[:, cols], tn, preferred_element_type=F32)
            for h, cols in enumerate(heads):
                o_ref[0, 0, rows, cols] = _bdot(jnp.where(mask, scores[h], 0.0).astype(BF16), vb[:, cols])
        for c in order:
            rows = slice(c * HG_CHUNK, (c + 1) * HG_CHUNK)
            for h in range(HG_HEADS):
                cols = slice(h * HG_N, (h + 1) * HG_N)
                st = st_ref[h]
                o_ref[0, 0, rows, cols] += lax.dot_general(qd_scr[rows, cols], st.astype(BF16), nt,
                                                           preferred_element_type=F32)
                st_ref[h] = dec_scr[c:c + 1, cols] * st + kv_scr[c, h]

    @pl.when(d == 0)
    def _():
        run(True)

    @pl.when(d == 1)
    def _():
        run(False)


def _hg_tri(n_rows):
    t = jnp.arange(n_rows)[:, None]
    s = jnp.arange(n_rows)[None, :]
    same = (t // HG_CHUNK) == (s // HG_CHUNK)
    return jnp.stack([same & (s <= t), same & (s >= t)]).astype(BF16)


def _hg(zqi, zh, hg_lb, n_ctx):
    b, tt, _ = zh.shape
    w = HG_HEADS * HG_N
    nblk = tt // ROW_TILE
    n_chunks = ROW_TILE // HG_CHUNK
    blk = _seq_block(n_ctx // ROW_TILE, nblk)
    return pl.pallas_call(
        _hg_kernel,
        grid=(b, 2, nblk),
        in_specs=[pl.BlockSpec((1, ROW_TILE, w), lambda bi, d, i: (bi, blk(d, i), 0)),
                  pl.BlockSpec((1, ROW_TILE, w), lambda bi, d, i: (bi, blk(d, i), d)),
                  pl.BlockSpec((1, ROW_TILE, w), lambda bi, d, i: (bi, blk(d, i), 1)),
                  pl.BlockSpec(hg_lb.shape, lambda bi, d, i: (0, 0)),
                  pl.BlockSpec((1, ROW_TILE, ROW_TILE), lambda bi, d, i: (d, 0, 0))],
        out_specs=pl.BlockSpec((1, 1, ROW_TILE, w), lambda bi, d, i: (d, bi, blk(d, i), 0)),
        out_shape=jax.ShapeDtypeStruct((2, b, tt, w), F32),
        scratch_shapes=[pltpu.VMEM((HG_HEADS, HG_N, HG_N), F32),
                        pltpu.VMEM((ROW_TILE, w), BF16),
                        pltpu.VMEM((n_chunks, HG_HEADS, HG_N, HG_N), F32),
                        pltpu.VMEM((SUBLANES, w), F32)],
        compiler_params=_params("arbitrary", "arbitrary", "arbitrary"),
        name="hg",
    )(zqi, zh, zqi, hg_lb, _hg_tri(ROW_TILE))


def _merge_kernel(yrw_ref, g_ref, o_ref, gz_ref, zg_ref, x_ref, g1_ref, sh2_ref, sc2_ref, g2_ref, hnw_ref,
                  pa_ref, pb_ref, wo_ref, nw2_ref, w1_ref, w2_ref, fw_ref, out_ref):
    d = x_ref.shape[2]
    y_rw = (yrw_ref[...].T * g_ref[...]).astype(BF16)
    o = o_ref[0, 0] + o_ref[1, 0]
    gz = gz_ref[0]
    hnw = hnw_ref[...]
    parts = []
    for h in range(HG_HEADS):
        cols = slice(h * HG_N, (h + 1) * HG_N)
        oh = o[:, cols]
        ms = jnp.mean(oh * oh, axis=-1, keepdims=True)
        parts.append(oh * lax.rsqrt(ms + RMS_EPS) * hnw[:, cols])
    y_hg = (jnp.concatenate(parts, axis=1) * (gz * _sigmoid(gz))).astype(BF16)
    zg = zg_ref[0]
    m = _sigmoid(zg[:, :d]) * _bdot(y_rw, pa_ref[...]) + _sigmoid(zg[:, d:]) * _bdot(y_hg, pb_ref[...])
    x1 = x_ref[0] + g1_ref[0] * _bdot(m.astype(BF16), wo_ref[...])
    hb = _modulated_norm(x1, nw2_ref[...], sh2_ref[0], sc2_ref[0]).astype(BF16)
    u = jnp.maximum(_bdot(hb, w1_ref[...]), 0.0)
    y = x1 + g2_ref[0] * _bdot((u * u).astype(BF16), w2_ref[...])
    ms = jnp.mean(y * y, axis=-1, keepdims=True)
    out_ref[0] = y * lax.rsqrt(ms + RMS_EPS) * fw_ref[...]


def _merge(y_rw, g, o_hg, zh, zg, x, modcat, hnw, pa, pb, wo, nw2, w1, w2, fw, n_ctx):
    b, t, d = x.shape
    off = n_ctx // ROW_TILE
    lat = lambda bi, i: (bi, i, 0)
    cat = lambda bi, i: (bi, i + off, 0)
    mod = lambda col: pl.BlockSpec((1, 1, d), lambda bi, i: (2 * bi + 1, 0, col))
    const = lambda a: pl.BlockSpec(a.shape, lambda bi, i: (0,) * a.ndim, pipeline_mode=pl.Buffered(1))
    return pl.pallas_call(
        _merge_kernel,
        grid=(b, t // ROW_TILE),
        in_specs=[pl.BlockSpec((d, ROW_TILE), lambda bi, i: (bi, i)),
                  pl.BlockSpec((ROW_TILE, d), lambda bi, i: (i + off, bi)),
                  pl.BlockSpec((2, 1, ROW_TILE, d), lambda bi, i: (0, bi, i + off, 0)),
                  pl.BlockSpec((1, ROW_TILE, d), lambda bi, i: (bi, i + off, 2)),
                  pl.BlockSpec((1, ROW_TILE, 2 * d), cat),
                  pl.BlockSpec((1, ROW_TILE, d), lat),
                  mod(2), mod(3), mod(4), mod(5),
                  const(hnw), const(pa), const(pb), const(wo), const(nw2), const(w1), const(w2), const(fw)],
        out_specs=pl.BlockSpec((1, ROW_TILE, d), lat),
        out_shape=jax.ShapeDtypeStruct((b, t, d), F32),
        compiler_params=_params("arbitrary", "arbitrary"),
        name="merge_mlp",
    )(y_rw, g, o_hg, zh, zg, x, modcat, modcat, modcat, modcat, hnw, pa, pb, wo, nw2, w1, w2, fw)


def _to_scan(a):
    *lead, w, t = a.shape
    return jnp.swapaxes(a.reshape(*lead, w // RW_N, RW_N, t), -1, -3)


def _from_scan(a):
    t, n, bh = a.shape
    return jnp.swapaxes(a, 0, 2).reshape(bh * n, t)


def _head_tile(p, b):
    return jnp.tile(p.reshape(RW_HEADS, RW_N).T, (1, b))


def kernel(x, c, ctx, c_ctx, norm1_w, norm2_w, w_mod, b_mod, w_in, rw_mu, rw_w0, rw_w_up, rw_a0, rw_a_up, rw_g_up, rw_k_k, rw_k_a, rw_r_k, rw_ln_w, rw_ln_b, hg_lb, hg_norm_w, p_a, p_b, w_out, w_fc1, w_fc2, final_norm_w):
    b, t, d = x.shape
    n_ctx = ctx.shape[1]
    assert w_mod.shape[0] == 1, "single-layer block"
    assert b * RW_HEADS == LANES and n_ctx == ROW_TILE and t % ROW_TILE == 0
    assert d == RW_HEADS * RW_N == HG_HEADS * HG_N
    n_r = rw_mu.shape[2]
    rank_w, rank_a = rw_w_up.shape[2], rw_a_up.shape[2]
    assert rank_w + rank_a == LANES and n_r == 3 * d + 2 * LANES and w_in.shape[2] == n_r + 7 * d

    c_rows = jnp.zeros((2 * SUBLANES, d), F32).at[:b].set(c).at[b].set(c_ctx)
    mod = _mod(c_rows, w_mod[0], b_mod)
    modcat = jnp.stack([jnp.broadcast_to(mod[b], (b, N_MOD * d)), mod[:b]], axis=1).reshape(2 * b, 1, N_MOD * d)

    zr, zqi, zh, zg = _in_proj(ctx, x, norm1_w, modcat, w_in[0].astype(BF16), n_r)

    wup = jnp.pad(rw_w_up[0], ((0, 0), (0, rank_a), (0, 0))).astype(BF16)
    aup = jnp.pad(rw_a_up[0], ((0, 0), (rank_w, 0), (0, 0))).astype(BF16)
    *scan_ops, g = _rw_prep(zr, rw_mu[0], rw_w0[0], wup, rw_a0[0], aup, rw_g_up[0].astype(BF16), d)
    r_t, k_t, v_t, wf_t, wb_t, af_t, ab_t = (_to_scan(a) for a in scan_ops)
    o_hg = _hg(zqi, zh, hg_lb, n_ctx)
    kap = _head_tile(rw_k_a[0], b)
    tiles = (_head_tile(rw_k_k[0], b), kap, n_ctx // SCAN_TILE)
    y_fwd = _rw_scan(r_t, k_t, v_t, wf_t, af_t, *tiles, run_after=o_hg)
    y_rw_t = _rw_scan(r_t, k_t, v_t, wb_t, ab_t, *tiles,
                      readout=(y_fwd, af_t, _head_tile(rw_r_k[0].reshape(-1), b),
                               _head_tile(rw_ln_w[0], b), _head_tile(rw_ln_b[0], b)))
    y_rw = _from_scan(y_rw_t)

    return _merge(y_rw, g, o_hg, zh, zg, x, modcat, hg_norm_w, p_a[0].astype(BF16), p_b[0].astype(BF16),
                  w_out[0].astype(BF16), norm2_w, w_fc1[0].astype(BF16), w_fc2[0].astype(BF16),
                  final_norm_w.reshape(1, d), n_ctx)
```

```python
import functools

import jax
import jax.numpy as jnp
from jax import lax
from jax.experimental import pallas as pl
from jax.experimental.pallas import tpu as pltpu

F32 = jnp.float32
BF16 = jnp.bfloat16

GRID_W = 64
RW_HEADS = 16
RW_N = 64
HG_HEADS = 8
HG_N = 128
HG_CHUNK = 64
N_MOD = 6
RMS_EPS = 1e-6
RW_GN_EPS = 64e-5
L2_EPS = 1e-12

SUBLANES = 8
LANES = 128
ROW_TILE = 256
SCAN_TILE = 64
VMEM_LIMIT = 56 * 1024 * 1024


def _params(*sem):
    return pltpu.CompilerParams(dimension_semantics=sem, vmem_limit_bytes=VMEM_LIMIT)


def _bdot(a, b):
    return jnp.dot(a, b, preferred_element_type=F32)


def _sigmoid(x):
    return 1.0 / (1.0 + jnp.exp(-x))


def _mod_kernel(c_ref, w_ref, b_ref, o_ref):
    c = c_ref[...]
    act = c * _sigmoid(c)
    o_ref[...] = _bdot(act.astype(BF16), w_ref[...].astype(BF16)) + b_ref[...]


def _mod(c_rows, w_mod, b_mod):
    rows, d = c_rows.shape
    n = w_mod.shape[1]
    return pl.pallas_call(
        _mod_kernel,
        grid=(n // d,),
        in_specs=[pl.BlockSpec((rows, d), lambda j: (0, 0)),
                  pl.BlockSpec((d, d), lambda j: (0, j)),
                  pl.BlockSpec((1, d), lambda j: (0, j))],
        out_specs=pl.BlockSpec((rows, d), lambda j: (0, j)),
        out_shape=jax.ShapeDtypeStruct((rows, n), F32),
        compiler_params=_params("arbitrary"),
        name="mod",
    )(c_rows, w_mod, b_mod)


def _modulated_norm(x, nw, sh, sc):
    ms = jnp.mean(x * x, axis=-1, keepdims=True)
    return (x * lax.rsqrt(ms + RMS_EPS) * nw) * (1.0 + sc) + sh


def _in_proj_kernel(ctx_ref, x_ref, nw_ref, sh_ref, sc_ref, w_ref, zr_ref, zqi_ref, zh_ref, zg_ref):
    tokens = jnp.where(pl.program_id(1) == 0, ctx_ref[0], x_ref[0])
    hb = _modulated_norm(tokens, nw_ref[...], sh_ref[0], sc_ref[0]).astype(BF16)
    d = x_ref.shape[2]
    n_r = zr_ref.shape[2]
    col = lambda j0, j1: _bdot(hb, w_ref[:, n_r + j0 * d:n_r + j1 * d])
    zr_ref[0] = _bdot(hb, w_ref[:, :n_r])
    zqi_ref[0, :, :d] = col(0, 1).astype(BF16)
    zqi_ref[0, :, d:] = col(3, 4).astype(BF16)
    zh_ref[0, :, :2 * d] = col(1, 3)
    zh_ref[0, :, 2 * d:] = col(4, 5)
    zg_ref[0] = col(5, 7)


def _in_proj(ctx, x, nw, modcat, w_bf, n_r):
    b, t, d = x.shape
    tt = t + ctx.shape[1]
    nblk = tt // ROW_TILE
    mod_idx = lambda col: (lambda bi, i: (2 * bi + jnp.minimum(i, 1), 0, col))
    outs = ((n_r, F32), (2 * d, BF16), (3 * d, F32), (2 * d, F32))
    return pl.pallas_call(
        _in_proj_kernel,
        grid=(b, nblk),
        in_specs=[pl.BlockSpec((1, ROW_TILE, d), lambda bi, i: (bi, 0, 0)),
                  pl.BlockSpec((1, ROW_TILE, d), lambda bi, i: (bi, jnp.maximum(i - 1, 0), 0)),
                  pl.BlockSpec((1, d), lambda bi, i: (0, 0)),
                  pl.BlockSpec((1, 1, d), mod_idx(0)),
                  pl.BlockSpec((1, 1, d), mod_idx(1)),
                  pl.BlockSpec(w_bf.shape, lambda bi, i: (0, 0), pipeline_mode=pl.Buffered(1))],
        out_specs=[pl.BlockSpec((1, ROW_TILE, n), lambda bi, i: (bi, i, 0)) for n, _ in outs],
        out_shape=[jax.ShapeDtypeStruct((b, tt, n), dt) for n, dt in outs],
        compiler_params=_params("arbitrary", "arbitrary"),
        name="in_proj",
    )(ctx, x, nw, modcat, modcat, w_bf)


def _rw_prep_kernel(z_ref, zp_ref, zn_ref, mu_ref, w0_ref, wup_ref, a0_ref, aup_ref, gup_ref,
                    r_ref, k_ref, v_ref, wf_ref, wb_ref, af_ref, ab_ref, g_ref):
    i = pl.program_id(1)
    nblk = pl.num_programs(1)
    tt = z_ref.shape[1]
    d = r_ref.shape[0]
    is_lat = i > 0
    row = lax.broadcasted_iota(jnp.int32, (tt, 1), 0)
    col = row % GRID_W
    lmask = jnp.where(is_lat, col, row) == 0
    rmask = jnp.where(is_lat, col, row - (tt - GRID_W)) == GRID_W - 1
    latf = is_lat.astype(F32)
    up_ok = (i > 1).astype(F32)
    dn_ok = (i < nblk - 1).astype(F32)

    def shifted(c0, c1):
        z = z_ref[0, :, c0:c1]
        mu = mu_ref[:, c0:c1]
        left = jnp.where(lmask, 0.0, pltpu.roll(z, 1, 0))
        right = jnp.where(rmask, 0.0, pltpu.roll(z, tt - 1, 0))
        up = jnp.concatenate([zp_ref[0, :, c0:c1] * up_ok, z[:tt - GRID_W]], axis=0)
        down = jnp.concatenate([z[GRID_W:], zn_ref[0, :, c0:c1] * dn_ok], axis=0)
        out = z + mu[0:1] * (left - z) + mu[1:2] * (right - z)
        return out + latf * (mu[2:3] * (up - z) + mu[3:4] * (down - z))

    r_ref[...] = shifted(0, d).T.astype(BF16)
    k_ref[...] = shifted(d, 2 * d).T
    v_ref[...] = shifted(2 * d, 3 * d).T
    rest = shifted(3 * d, z_ref.shape[2])
    xwa = rest[:, :LANES]
    xw_t = jnp.tanh(xwa).astype(BF16)
    xa_b = xwa.astype(BF16)
    for dr, (wd_ref, ad_ref) in enumerate(((wf_ref, af_ref), (wb_ref, ab_ref))):
        u = -(w0_ref[dr:dr + 1, :] + _bdot(xw_t, wup_ref[dr]))
        softplus = jnp.maximum(u, 0.0) + jnp.log(1.0 + jnp.exp(-jnp.abs(u)))
        wd_ref[...] = jnp.exp(-jnp.exp(-softplus - 0.5)).T
        ad_ref[...] = _sigmoid(a0_ref[dr:dr + 1, :] + _bdot(xa_b, aup_ref[dr])).T
    g_ref[...] = _bdot(_sigmoid(rest[:, LANES:]).astype(BF16), gup_ref[...])


def _rw_prep(zr, mu, w0, wup, a0, aup, gup, d):
    b, tt, nr = zr.shape
    nblk = tt // ROW_TILE
    per = ROW_TILE // GRID_W
    last = tt // GRID_W - 1
    const = lambda shape: pl.BlockSpec(shape, lambda bi, i: (0,) * len(shape))
    row_spec = pl.BlockSpec((d, ROW_TILE), lambda bi, i: (bi, i))
    gate_spec = pl.BlockSpec((ROW_TILE, d), lambda bi, i: (i, bi))
    row_shape = jax.ShapeDtypeStruct((b * d, tt), F32)
    gate_shape = jax.ShapeDtypeStruct((tt, b * d), F32)
    return pl.pallas_call(
        _rw_prep_kernel,
        grid=(b, nblk),
        in_specs=[pl.BlockSpec((1, ROW_TILE, nr), lambda bi, i: (bi, i, 0)),
                  pl.BlockSpec((1, GRID_W, nr), lambda bi, i: (bi, jnp.maximum(i * per - 1, 0), 0)),
                  pl.BlockSpec((1, GRID_W, nr), lambda bi, i: (bi, jnp.minimum(i * per + per, last), 0)),
                  const(mu.shape), const(w0.shape), const(wup.shape), const(a0.shape),
                  const(aup.shape), const(gup.shape)],
        out_specs=[row_spec] * 7 + [gate_spec],
        out_shape=[jax.ShapeDtypeStruct(row_shape.shape, BF16)] + [row_shape] * 6 + [gate_shape],
        compiler_params=_params("arbitrary", "arbitrary"),
        name="rw_prep",
    )(zr, zr, zr, mu, w0, wup, a0, aup, gup)


K_UNROLL = 16


def _seq_block(n_ctx_blocks, n_blocks):
    def blk(d, i):
        bwd = jnp.where(i < n_ctx_blocks, n_ctx_blocks - 1 - i, n_blocks - 1 + n_ctx_blocks - i)
        return jnp.where(d == 0, i, bwd)
    return blk


def _rw_scan_kernel(*refs, backward):
    if backward:
        (r_ref, k_ref, v_ref, w_ref, a_ref, kkp_ref, kap_ref, yf_ref, af_ref, rkp_ref, lnw_ref, lnb_ref,
         y_ref, s_ref, p_s, sa_s, kk0_s, bt_s, kt_s, rd_s, kkd_s) = refs
    else:
        (r_ref, k_ref, v_ref, w_ref, a_ref, kkp_ref, kap_ref, _order_ref,
         y_ref, s_ref, p_s, sa_s, kk0_s, bt_s, kt_s, rd_s, kkd_s) = refs
    i = pl.program_id(0)
    tb = r_ref.shape[0]
    nv = RW_N // SUBLANES

    @pl.when(i == 0)
    def _():
        s_ref[...] = jnp.zeros_like(s_ref)
        p_s[...] = jnp.ones_like(p_s)

    def bcast(ref, *idx):
        k = idx[-1]
        row = ref[(*idx[:-1], pl.ds(k, 1), slice(None))]
        return jnp.broadcast_to(row, (SUBLANES, LANES))

    def time_index(s):
        s = jnp.minimum(s, tb - 1)
        return tb - 1 - s if backward else s

    def bf16_pair(x):
        hi = lax.bitcast_convert_type(x.astype(BF16).astype(F32), jnp.uint32)
        return lax.bitcast_convert_type(hi | (hi >> 16), F32)

    def norm_key(t):
        kkr = k_ref[t] * kkp_ref[...]
        nrm = jnp.sqrt(jnp.sum(kkr * kkr, axis=0, keepdims=True))
        return kkr / jnp.maximum(nrm, L2_EPS)

    def scaled_key(t, a):
        return k_ref[t] * (1.0 + (a - 1.0) * kap_ref[...])

    def prepare(s, slot):
        t = time_index(s)
        a = a_ref[t]
        kk = norm_key(t)
        p_prev = p_s[...]
        kkd_s[1 - slot] = bf16_pair(p_prev * kk)
        p = p_prev * jnp.where(s < tb, w_ref[t], 1.0)
        p_s[...] = p
        inv_p = 1.0 / p
        bt_s[slot] = kk * a * inv_p
        kt_s[slot] = scaled_key(t, a) * inv_p
        rd_s[slot] = bf16_pair(p * r_ref[t].astype(F32))

    def restart():
        kk0_s[...] = norm_key(time_index(0))
        acc = [None] * nv
        for k in range(RW_N):
            pb = bcast(p_s, k)
            kkb = bcast(kk0_s, k)
            for j in range(nv):
                rows = slice(SUBLANES * j, SUBLANES * (j + 1))
                sn = s_ref[k, rows, :] * pb
                s_ref[k, rows, :] = sn
                acc[j] = sn * kkb if acc[j] is None else acc[j] + sn * kkb
        sa_s[...] = -jnp.concatenate(acc, axis=0)
        p_s[...] = jnp.ones_like(p_s)

    def sweep(s, slot):
        t = time_index(s)
        zero = jnp.zeros((SUBLANES, LANES), F32)

        def key_block(kblk, carry):
            yacc, acc = list(carry[0]), list(carry[1])
            yb = [jnp.zeros((2 * SUBLANES, LANES), BF16)] * (nv // 2)
            ab = [jnp.zeros((2 * SUBLANES, LANES), BF16)] * (nv // 2)
            for kk in range(K_UNROLL):
                k = kblk * K_UNROLL + kk
                bb = bcast(bt_s, slot, k)
                kb = bcast(kt_s, slot, k)
                rb = pltpu.bitcast(bcast(rd_s, slot, k), BF16)
                kkn = pltpu.bitcast(bcast(kkd_s, slot, k), BF16)
                for m in range(nv // 2):
                    pair_rows = []
                    for j in (2 * m, 2 * m + 1):
                        rows = slice(SUBLANES * j, SUBLANES * (j + 1))
                        sn = s_ref[k, rows, :] + (sa_s[rows, :] * bb + v_ref[t, rows, :] * kb)
                        s_ref[k, rows, :] = sn
                        pair_rows.append(sn)
                    snp = jnp.concatenate(pair_rows, axis=0).astype(BF16)
                    yb[m] = yb[m] + snp * rb
                    ab[m] = ab[m] + snp * kkn
            for m in range(nv // 2):
                y32 = yb[m].astype(F32)
                a32 = ab[m].astype(F32)
                for h, j in enumerate((2 * m, 2 * m + 1)):
                    yacc[j] = yacc[j] + y32[SUBLANES * h:SUBLANES * (h + 1)]
                    acc[j] = acc[j] + a32[SUBLANES * h:SUBLANES * (h + 1)]
            return yacc, acc

        init = [zero] * nv
        n_kblk = RW_N // K_UNROLL
        carry = lax.fori_loop(0, n_kblk - 1, key_block, (init, init))
        yacc, acc = key_block(n_kblk - 1, carry)
        sa_s[...] = -jnp.concatenate(acc, axis=0)
        y = jnp.concatenate(yacc, axis=0)
        if backward:
            y = y + yf_ref[t]
            yc = y - jnp.mean(y, axis=0, keepdims=True)
            var = jnp.mean(yc * yc, axis=0, keepdims=True)
            y = yc * lax.rsqrt(var + RW_GN_EPS) * lnw_ref[...] + lnb_ref[...]
            k_sum = scaled_key(t, af_ref[t]) + scaled_key(t, a_ref[t])
            y = y + jnp.sum(r_ref[t].astype(F32) * k_sum * rkp_ref[...], axis=0, keepdims=True) * v_ref[t]
        y_ref[t] = y

    restart()
    prepare(0, 0)
    prepare(1, 1)

    def pair(p, carry):
        s = 2 * p
        sweep(s, 0)
        prepare(s + 2, 0)
        sweep(s + 1, 1)
        prepare(s + 3, 1)
        return carry

    lax.fori_loop(0, tb // 2, pair, 0)


def _rw_scan(r_t, k_t, v_t, w_t, a_t, kkp, kap, n_ctx_blocks, *, run_after=None, readout=None):
    backward = readout is not None
    direction = int(backward)
    tt = r_t.shape[0]
    nblk = tt // SCAN_TILE
    seq = _seq_block(n_ctx_blocks, nblk)
    blk = lambda i: seq(direction, i)
    tile = (SCAN_TILE, RW_N, LANES)
    shared = pl.BlockSpec(tile, lambda i: (blk(i), 0, 0))
    const = pl.BlockSpec((RW_N, LANES), lambda i: (0, 0))
    y_blk = lambda i: jnp.where(i < n_ctx_blocks, blk(n_ctx_blocks), blk(i)) - n_ctx_blocks
    y_spec = pl.BlockSpec(tile, lambda i: (y_blk(i), 0, 0))
    in_specs = [shared, shared, shared, shared, shared, const, const]
    args = [r_t, k_t, v_t, w_t, a_t, kkp, kap]
    if backward:
        y_fwd, a_fwd, rkp, lnw, lnb = readout
        in_specs += [y_spec, shared, const, const, const]
        args += [y_fwd, a_fwd, rkp, lnw, lnb]
    else:
        in_specs += [pl.BlockSpec(memory_space=pl.ANY)]
        args += [run_after]
    return pl.pallas_call(
        functools.partial(_rw_scan_kernel, backward=backward),
        grid=(nblk,),
        in_specs=in_specs,
        out_specs=y_spec,
        out_shape=jax.ShapeDtypeStruct((tt - n_ctx_blocks * SCAN_TILE, RW_N, LANES), F32),
        scratch_shapes=[pltpu.VMEM((RW_N, RW_N, LANES), F32)] + [pltpu.VMEM((RW_N, LANES), F32)] * 3
                       + [pltpu.VMEM((2, RW_N, LANES), F32)] * 4,
        compiler_params=_params("arbitrary"),
        name="rw_scan_bwd" if backward else "rw_scan_fwd",
    )(*args)


def _hg_kernel(q_ref, f_ref, i_ref, lbp_ref, tri_ref, o_ref, st_ref, qd_scr, kv_scr, dec_scr):
    d = pl.program_id(1)
    i = pl.program_id(2)
    n_chunks = q_ref.shape[1] // HG_CHUNK

    @pl.when(i == 0)
    def _():
        st_ref[...] = jnp.zeros_like(st_ref)

    lbp = lbp_ref[...]
    e = jnp.exp(lbp - jnp.max(lbp, axis=0, keepdims=True))
    lb = e[0:1] / jnp.sum(e, axis=0, keepdims=True)
    t_idx = lax.broadcasted_iota(jnp.int32, (HG_CHUNK, HG_CHUNK), 0)
    s_idx = lax.broadcasted_iota(jnp.int32, (HG_CHUNK, HG_CHUNK), 1)
    nt = (((1,), (1,)), ((), ()))
    tn = (((0,), (0,)), ((), ()))

    def run(fwd):
        mask = (s_idx <= t_idx) if fwd else (s_idx >= t_idx)
        mid_row = HG_CHUNK // 2 - 1 if fwd else HG_CHUNK // 2
        last_row = HG_CHUNK - 1 if fwd else 0
        order = list(range(n_chunks)) if fwd else list(reversed(range(n_chunks)))

        f = lb + (1.0 - lb) * _sigmoid(f_ref[0])
        lf = jnp.log(f)
        hi = lf.astype(BF16)
        lo = (lf - hi.astype(F32)).astype(BF16)
        tri = tri_ref[0]
        b = _bdot(tri, hi) + _bdot(tri, lo)
        for c in order:
            rows = slice(c * HG_CHUNK, (c + 1) * HG_CHUNK)
            bc = b[rows]
            b_mid = bc[mid_row:mid_row + 1]
            b_last = bc[last_row:last_row + 1]
            q_in = q_ref[0, rows, :] * jnp.exp(bc - b_mid)
            k_in = (1.0 - f[rows]) * jnp.exp(b_mid - bc)
            qd_scr[rows, :] = (q_in * jnp.exp(b_mid)).astype(BF16)
            k_dec = (k_in * jnp.exp(b_last - b_mid)).astype(BF16)
            dec_scr[c:c + 1, :] = jnp.exp(b_last)
            q_in = q_in.astype(BF16)
            k_in = k_in.astype(BF16)
            vb = i_ref[0, rows, :].astype(BF16)
            heads = [slice(h * HG_N, (h + 1) * HG_N) for h in range(HG_HEADS)]
            scores = [lax.dot_general(q_in[:, cols], k_in[:, cols], nt, preferred_element_type=F32)
                      for cols in heads]
            for h, cols in enumerate(heads):
                kv_scr[c, h] = lax.dot_general(vb[:, cols], k_dec[:, cols], tn, preferred_element_type=F32)
            for h, cols in enumerate(heads):
                o_ref[0, 0, rows, cols] = _bdot(jnp.where(mask, scores[h], 0.0).astype(BF16), vb[:, cols])
        for c in order:
            rows = slice(c * HG_CHUNK, (c + 1) * HG_CHUNK)
            for h in range(HG_HEADS):
                cols = slice(h * HG_N, (h + 1) * HG_N)
                st = st_ref[h]
                o_ref[0, 0, rows, cols] += lax.dot_general(qd_scr[rows, cols], st.astype(BF16), nt,
                                                           preferred_element_type=F32)
                st_ref[h] = dec_scr[c:c + 1, cols] * st + kv_scr[c, h]

    @pl.when(d == 0)
    def _():
        run(True)

    @pl.when(d == 1)
    def _():
        run(False)


def _hg_tri(n_rows):
    t = jnp.arange(n_rows)[:, None]
    s = jnp.arange(n_rows)[None, :]
    same = (t // HG_CHUNK) == (s // HG_CHUNK)
    return jnp.stack([same & (s <= t), same & (s >= t)]).astype(BF16)


def _hg(zqi, zh, hg_lb, n_ctx):
    b, tt, _ = zh.shape
    w = HG_HEADS * HG_N
    nblk = tt // ROW_TILE
    n_chunks = ROW_TILE // HG_CHUNK
    blk = _seq_block(n_ctx // ROW_TILE, nblk)
    return pl.pallas_call(
        _hg_kernel,
        grid=(b, 2, nblk),
        in_specs=[pl.BlockSpec((1, ROW_TILE, w), lambda bi, d, i: (bi, blk(d, i), 0)),
                  pl.BlockSpec((1, ROW_TILE, w), lambda bi, d, i: (bi, blk(d, i), d)),
                  pl.BlockSpec((1, ROW_TILE, w), lambda bi, d, i: (bi, blk(d, i), 1)),
                  pl.BlockSpec(hg_lb.shape, lambda bi, d, i: (0, 0)),
                  pl.BlockSpec((1, ROW_TILE, ROW_TILE), lambda bi, d, i: (d, 0, 0))],
        out_specs=pl.BlockSpec((1, 1, ROW_TILE, w), lambda bi, d, i: (d, bi, blk(d, i), 0)),
        out_shape=jax.ShapeDtypeStruct((2, b, tt, w), F32),
        scratch_shapes=[pltpu.VMEM((HG_HEADS, HG_N, HG_N), F32),
                        pltpu.VMEM((ROW_TILE, w), BF16),
                        pltpu.VMEM((n_chunks, HG_HEADS, HG_N, HG_N), F32),
                        pltpu.VMEM((SUBLANES, w), F32)],
        compiler_params=_params("arbitrary", "arbitrary", "arbitrary"),
        name="hg",
    )(zqi, zh, zqi, hg_lb, _hg_tri(ROW_TILE))


def _merge_kernel(yrw_ref, g_ref, o_ref, gz_ref, zg_ref, x_ref, g1_ref, sh2_ref, sc2_ref, g2_ref, hnw_ref,
                  pa_ref, pb_ref, wo_ref, nw2_ref, w1_ref, w2_ref, fw_ref, out_ref):
    d = x_ref.shape[2]
    y_rw = (yrw_ref[...].T * g_ref[...]).astype(BF16)
    o = o_ref[0, 0] + o_ref[1, 0]
    gz = gz_ref[0]
    hnw = hnw_ref[...]
    parts = []
    for h in range(HG_HEADS):
        cols = slice(h * HG_N, (h + 1) * HG_N)
        oh = o[:, cols]
        ms = jnp.mean(oh * oh, axis=-1, keepdims=True)
        parts.append(oh * lax.rsqrt(ms + RMS_EPS) * hnw[:, cols])
    y_hg = (jnp.concatenate(parts, axis=1) * (gz * _sigmoid(gz))).astype(BF16)
    zg = zg_ref[0]
    m = _sigmoid(zg[:, :d]) * _bdot(y_rw, pa_ref[...]) + _sigmoid(zg[:, d:]) * _bdot(y_hg, pb_ref[...])
    x1 = x_ref[0] + g1_ref[0] * _bdot(m.astype(BF16), wo_ref[...])
    hb = _modulated_norm(x1, nw2_ref[...], sh2_ref[0], sc2_ref[0]).astype(BF16)
    u = jnp.maximum(_bdot(hb, w1_ref[...]), 0.0)
    y = x1 + g2_ref[0] * _bdot((u * u).astype(BF16), w2_ref[...])
    ms = jnp.mean(y * y, axis=-1, keepdims=True)
    out_ref[0] = y * lax.rsqrt(ms + RMS_EPS) * fw_ref[...]


def _merge(y_rw, g, o_hg, zh, zg, x, modcat, hnw, pa, pb, wo, nw2, w1, w2, fw, n_ctx):
    b, t, d = x.shape
    off = n_ctx // ROW_TILE
    lat = lambda bi, i: (bi, i, 0)
    cat = lambda bi, i: (bi, i + off, 0)
    mod = lambda col: pl.BlockSpec((1, 1, d), lambda bi, i: (2 * bi + 1, 0, col))
    const = lambda a: pl.BlockSpec(a.shape, lambda bi, i: (0,) * a.ndim, pipeline_mode=pl.Buffered(1))
    return pl.pallas_call(
        _merge_kernel,
        grid=(b, t // ROW_TILE),
        in_specs=[pl.BlockSpec((d, ROW_TILE), lambda bi, i: (bi, i)),
                  pl.BlockSpec((ROW_TILE, d), lambda bi, i: (i + off, bi)),
                  pl.BlockSpec((2, 1, ROW_TILE, d), lambda bi, i: (0, bi, i + off, 0)),
                  pl.BlockSpec((1, ROW_TILE, d), lambda bi, i: (bi, i + off, 2)),
                  pl.BlockSpec((1, ROW_TILE, 2 * d), cat),
                  pl.BlockSpec((1, ROW_TILE, d), lat),
                  mod(2), mod(3), mod(4), mod(5),
                  const(hnw), const(pa), const(pb), const(wo), const(nw2), const(w1), const(w2), const(fw)],
        out_specs=pl.BlockSpec((1, ROW_TILE, d), lat),
        out_shape=jax.ShapeDtypeStruct((b, t, d), F32),
        compiler_params=_params("arbitrary", "arbitrary"),
        name="merge_mlp",
    )(y_rw, g, o_hg, zh, zg, x, modcat, modcat, modcat, modcat, hnw, pa, pb, wo, nw2, w1, w2, fw)


def _to_scan(a):
    *lead, w, t = a.shape
    return jnp.swapaxes(a.reshape(*lead, w // RW_N, RW_N, t), -1, -3)


def _from_scan(a):
    t, n, bh = a.shape
    return jnp.swapaxes(a, 0, 2).reshape(bh * n, t)


def _head_tile(p, b):
    return jnp.tile(p.reshape(RW_HEADS, RW_N).T, (1, b))


def kernel(x, c, ctx, c_ctx, norm1_w, norm2_w, w_mod, b_mod, w_in, rw_mu, rw_w0, rw_w_up, rw_a0, rw_a_up, rw_g_up, rw_k_k, rw_k_a, rw_r_k, rw_ln_w, rw_ln_b, hg_lb, hg_norm_w, p_a, p_b, w_out, w_fc1, w_fc2, final_norm_w):
    b, t, d = x.shape
    n_ctx = ctx.shape[1]
    assert w_mod.shape[0] == 1, "single-layer block"
    assert b * RW_HEADS == LANES and n_ctx == ROW_TILE and t % ROW_TILE == 0
    assert d == RW_HEADS * RW_N == HG_HEADS * HG_N
    n_r = rw_mu.shape[2]
    rank_w, rank_a = rw_w_up.shape[2], rw_a_up.shape[2]
    assert rank_w + rank_a == LANES and n_r == 3 * d + 2 * LANES and w_in.shape[2] == n_r + 7 * d

    c_rows = jnp.zeros((2 * SUBLANES, d), F32).at[:b].set(c).at[b].set(c_ctx)
    mod = _mod(c_rows, w_mod[0], b_mod)
    modcat = jnp.stack([jnp.broadcast_to(mod[b], (b, N_MOD * d)), mod[:b]], axis=1).reshape(2 * b, 1, N_MOD * d)

    zr, zqi, zh, zg = _in_proj(ctx, x, norm1_w, modcat, w_in[0].astype(BF16), n_r)

    wup = jnp.pad(rw_w_up[0], ((0, 0), (0, rank_a), (0, 0))).astype(BF16)
    aup = jnp.pad(rw_a_up[0], ((0, 0), (rank_w, 0), (0, 0))).astype(BF16)
    *scan_ops, g = _rw_prep(zr, rw_mu[0], rw_w0[0], wup, rw_a0[0], aup, rw_g_up[0].astype(BF16), d)
    r_t, k_t, v_t, wf_t, wb_t, af_t, ab_t = (_to_scan(a) for a in scan_ops)
    o_hg = _hg(zqi, zh, hg_lb, n_ctx)
    kap = _head_tile(rw_k_a[0], b)
    tiles = (_head_tile(rw_k_k[0], b), kap, n_ctx // SCAN_TILE)
    y_fwd = _rw_scan(r_t, k_t, v_t, wf_t, af_t, *tiles, run_after=o_hg)
    y_rw_t = _rw_scan(r_t, k_t, v_t, wb_t, ab_t, *tiles,
                      readout=(y_fwd, af_t, _head_tile(rw_r_k[0].reshape(-1), b),
                               _head_tile(rw_ln_w[0], b), _head_tile(rw_ln_b[0], b)))
    y_rw = _from_scan(y_rw_t)

    return _merge(y_rw, g, o_hg, zh, zg, x, modcat, hg_norm_w, p_a[0].astype(BF16), p_b[0].astype(BF16),
                  w_out[0].astype(BF16), norm2_w, w_fc1[0].astype(BF16), w_fc2[0].astype(BF16),
                  final_norm_w.reshape(1, d), n_ctx)
```

```python
import functools

import jax
import jax.numpy as jnp
from jax import lax
from jax.experimental import pallas as pl
from jax.experimental.pallas import tpu as pltpu

F32 = jnp.float32
BF16 = jnp.bfloat16

GRID_W = 64
RW_HEADS = 16
RW_N = 64
HG_HEADS = 8
HG_N = 128
HG_CHUNK = 64
N_MOD = 6
RMS_EPS = 1e-6
RW_GN_EPS = 64e-5
L2_EPS = 1e-12
DECAY_SCALE = 0.6065306597126334

SUBLANES = 8
LANES = 128
ROW_TILE = 256
SCAN_TILE = 64
VMEM_LIMIT = 56 * 1024 * 1024


def _params(*sem):
    return pltpu.CompilerParams(dimension_semantics=sem, vmem_limit_bytes=VMEM_LIMIT)


def _bdot(a, b):
    return jnp.dot(a, b, preferred_element_type=F32)


def _sigmoid(x):
    return 1.0 / (1.0 + jnp.exp(-x))


def _mod_kernel(c_ref, w_ref, b_ref, o_ref):
    c = c_ref[...]
    act = c * _sigmoid(c)
    o_ref[...] = _bdot(act.astype(BF16), w_ref[...].astype(BF16)) + b_ref[...]


def _mod(c_rows, w_mod, b_mod):
    rows, d = c_rows.shape
    n = w_mod.shape[1]
    return pl.pallas_call(
        _mod_kernel,
        grid=(n // d,),
        in_specs=[pl.BlockSpec((rows, d), lambda j: (0, 0)),
                  pl.BlockSpec((d, d), lambda j: (0, j)),
                  pl.BlockSpec((1, d), lambda j: (0, j))],
        out_specs=pl.BlockSpec((rows, d), lambda j: (0, j)),
        out_shape=jax.ShapeDtypeStruct((rows, n), F32),
        compiler_params=_params("arbitrary"),
        name="mod",
    )(c_rows, w_mod, b_mod)


def _modulated_norm(x, nw, sh, sc):
    ms = jnp.mean(x * x, axis=-1, keepdims=True)
    return (x * lax.rsqrt(ms + RMS_EPS) * nw) * (1.0 + sc) + sh


def _in_proj_kernel(ctx_ref, x_ref, nw_ref, sh_ref, sc_ref, w_ref, zr_ref, zqi_ref, zh_ref, zg_ref):
    tokens = jnp.where(pl.program_id(1) == 0, ctx_ref[0], x_ref[0])
    hb = _modulated_norm(tokens, nw_ref[...], sh_ref[0], sc_ref[0]).astype(BF16)
    d = x_ref.shape[2]
    n_r = zr_ref.shape[2]
    col = lambda j0, j1: _bdot(hb, w_ref[:, n_r + j0 * d:n_r + j1 * d])
    zr_ref[0] = _bdot(hb, w_ref[:, :n_r])
    zqi_ref[0, :, :d] = col(0, 1).astype(BF16)
    zqi_ref[0, :, d:] = col(3, 4).astype(BF16)
    zh_ref[0, :, :2 * d] = col(1, 3)
    zh_ref[0, :, 2 * d:] = col(4, 5)
    zg_ref[0] = col(5, 7)


def _in_proj(ctx, x, nw, modcat, w_bf, n_r):
    b, t, d = x.shape
    tt = t + ctx.shape[1]
    nblk = tt // ROW_TILE
    mod_idx = lambda col: (lambda bi, i: (2 * bi + jnp.minimum(i, 1), 0, col))
    outs = ((n_r, F32), (2 * d, BF16), (3 * d, F32), (2 * d, F32))
    return pl.pallas_call(
        _in_proj_kernel,
        grid=(b, nblk),
        in_specs=[pl.BlockSpec((1, ROW_TILE, d), lambda bi, i: (bi, 0, 0)),
                  pl.BlockSpec((1, ROW_TILE, d), lambda bi, i: (bi, jnp.maximum(i - 1, 0), 0)),
                  pl.BlockSpec((1, d), lambda bi, i: (0, 0)),
                  pl.BlockSpec((1, 1, d), mod_idx(0)),
                  pl.BlockSpec((1, 1, d), mod_idx(1)),
                  pl.BlockSpec(w_bf.shape, lambda bi, i: (0, 0), pipeline_mode=pl.Buffered(1))],
        out_specs=[pl.BlockSpec((1, ROW_TILE, n), lambda bi, i: (bi, i, 0)) for n, _ in outs],
        out_shape=[jax.ShapeDtypeStruct((b, tt, n), dt) for n, dt in outs],
        compiler_params=_params("arbitrary", "arbitrary"),
        name="in_proj",
    )(ctx, x, nw, modcat, modcat, w_bf)


def _rw_prep_kernel(z_ref, zp_ref, zn_ref, mu_ref, w0_ref, wup_ref, a0_ref, aup_ref, gup_ref,
                    r_ref, k_ref, v_ref, wf_ref, wb_ref, af_ref, ab_ref, g_ref):
    i = pl.program_id(1)
    nblk = pl.num_programs(1)
    tt = z_ref.shape[1]
    d = r_ref.shape[0]
    is_lat = i > 0
    row = lax.broadcasted_iota(jnp.int32, (tt, 1), 0)
    col = row % GRID_W
    lmask = jnp.where(is_lat, col, row) == 0
    rmask = jnp.where(is_lat, col, row - (tt - GRID_W)) == GRID_W - 1
    latf = is_lat.astype(F32)
    up_ok = (i > 1).astype(F32)
    dn_ok = (i < nblk - 1).astype(F32)

    def shifted(c0, c1):
        z = z_ref[0, :, c0:c1]
        mu = mu_ref[:, c0:c1]
        left = jnp.where(lmask, 0.0, pltpu.roll(z, 1, 0))
        right = jnp.where(rmask, 0.0, pltpu.roll(z, tt - 1, 0))
        up = jnp.concatenate([zp_ref[0, :, c0:c1] * up_ok, z[:tt - GRID_W]], axis=0)
        down = jnp.concatenate([z[GRID_W:], zn_ref[0, :, c0:c1] * dn_ok], axis=0)
        m_up, m_dn = latf * mu[2:3], latf * mu[3:4]
        m_self = 1.0 - mu[0:1] - mu[1:2] - m_up - m_dn
        return z * m_self + left * mu[0:1] + right * mu[1:2] + up * m_up + down * m_dn

    r_ref[...] = shifted(0, d).T
    k_ref[...] = shifted(d, 2 * d).T
    v_ref[...] = shifted(2 * d, 3 * d).T
    rest = shifted(3 * d, z_ref.shape[2])
    xwa = rest[:, :LANES]
    xw_t = jnp.tanh(xwa).astype(BF16)
    xa_b = xwa.astype(BF16)
    for dr, (wd_ref, ad_ref) in enumerate(((wf_ref, af_ref), (wb_ref, ab_ref))):
        wz = w0_ref[dr:dr + 1, :] + _bdot(xw_t, wup_ref[dr])
        wd_ref[...] = jnp.exp(-DECAY_SCALE * _sigmoid(wz)).T
        ad_ref[...] = _sigmoid(a0_ref[dr:dr + 1, :] + _bdot(xa_b, aup_ref[dr])).T
    g_ref[...] = _bdot(_sigmoid(rest[:, LANES:]).astype(BF16), gup_ref[...])


def _rw_prep(zr, mu, w0, wup, a0, aup, gup, d):
    b, tt, nr = zr.shape
    nblk = tt // ROW_TILE
    per = ROW_TILE // GRID_W
    last = tt // GRID_W - 1
    const = lambda shape: pl.BlockSpec(shape, lambda bi, i: (0,) * len(shape))
    row_spec = pl.BlockSpec((d, ROW_TILE), lambda bi, i: (bi, i))
    gate_spec = pl.BlockSpec((ROW_TILE, d), lambda bi, i: (i, bi))
    row_shape = jax.ShapeDtypeStruct((b * d, tt), F32)
    gate_shape = jax.ShapeDtypeStruct((tt, b * d), F32)
    return pl.pallas_call(
        _rw_prep_kernel,
        grid=(b, nblk),
        in_specs=[pl.BlockSpec((1, ROW_TILE, nr), lambda bi, i: (bi, i, 0)),
                  pl.BlockSpec((1, GRID_W, nr), lambda bi, i: (bi, jnp.maximum(i * per - 1, 0), 0)),
                  pl.BlockSpec((1, GRID_W, nr), lambda bi, i: (bi, jnp.minimum(i * per + per, last), 0)),
                  const(mu.shape), const(w0.shape), const(wup.shape), const(a0.shape),
                  const(aup.shape), const(gup.shape)],
        out_specs=[row_spec] * 7 + [gate_spec],
        out_shape=[row_shape] * 7 + [gate_shape],
        compiler_params=_params("arbitrary", "arbitrary"),
        name="rw_prep",
    )(zr, zr, zr, mu, w0, wup, a0, aup, gup)


K_UNROLL = 16


def _seq_block(n_ctx_blocks, n_blocks):
    def blk(d, i):
        bwd = jnp.where(i < n_ctx_blocks, n_ctx_blocks - 1 - i, n_blocks - 1 + n_ctx_blocks - i)
        return jnp.where(d == 0, i, bwd)
    return blk


def _rw_scan_kernel(*refs, backward):
    if backward:
        (r_ref, k_ref, v_ref, w_ref, a_ref, kkp_ref, kap_ref, yf_ref, af_ref, rkp_ref, lnw_ref, lnb_ref,
         y_ref, s_ref, p_s, sa_s, kk0_s, bt_s, kt_s, rd_s, kkd_s) = refs
    else:
        (r_ref, k_ref, v_ref, w_ref, a_ref, kkp_ref, kap_ref, _order_ref,
         y_ref, s_ref, p_s, sa_s, kk0_s, bt_s, kt_s, rd_s, kkd_s) = refs
    i = pl.program_id(0)
    tb = r_ref.shape[0]
    nv = RW_N // SUBLANES

    @pl.when(i == 0)
    def _():
        s_ref[...] = jnp.zeros_like(s_ref)
        p_s[...] = jnp.ones_like(p_s)

    def bcast(ref, *idx):
        k = idx[-1]
        row = ref[(*idx[:-1], pl.ds(k, 1), slice(None))]
        return jnp.broadcast_to(row, (SUBLANES, LANES))

    def time_index(s):
        s = jnp.minimum(s, tb - 1)
        return tb - 1 - s if backward else s

    def bf16_pair(x):
        hi = lax.bitcast_convert_type(x.astype(BF16).astype(F32), jnp.uint32)
        return lax.bitcast_convert_type(hi | (hi >> 16), F32)

    def norm_key(t):
        kkr = k_ref[t] * kkp_ref[...]
        nrm = jnp.sqrt(jnp.sum(kkr * kkr, axis=0, keepdims=True))
        return kkr / jnp.maximum(nrm, L2_EPS)

    def scaled_key(t, a):
        return k_ref[t] * (1.0 + (a - 1.0) * kap_ref[...])

    def prepare(s, slot):
        t = time_index(s)
        a = a_ref[t]
        kk = norm_key(t)
        p_prev = p_s[...]
        kkd_s[1 - slot] = bf16_pair(p_prev * kk)
        p = p_prev * jnp.where(s < tb, w_ref[t], 1.0)
        p_s[...] = p
        inv_p = 1.0 / p
        bt_s[slot] = kk * a * inv_p
        kt_s[slot] = scaled_key(t, a) * inv_p
        rd_s[slot] = bf16_pair(p * r_ref[t])

    def restart():
        kk0_s[...] = norm_key(time_index(0))
        acc = [None] * nv
        for k in range(RW_N):
            pb = bcast(p_s, k)
            kkb = bcast(kk0_s, k)
            for j in range(nv):
                rows = slice(SUBLANES * j, SUBLANES * (j + 1))
                sn = s_ref[k, rows, :] * pb
                s_ref[k, rows, :] = sn
                acc[j] = sn * kkb if acc[j] is None else acc[j] + sn * kkb
        sa_s[...] = -jnp.concatenate(acc, axis=0)
        p_s[...] = jnp.ones_like(p_s)

    def sweep(s, slot):
        t = time_index(s)
        zero = jnp.zeros((SUBLANES, LANES), F32)

        def key_block(kblk, carry):
            yacc, acc = list(carry[0]), list(carry[1])
            yb = [jnp.zeros((2 * SUBLANES, LANES), BF16)] * (nv // 2)
            ab = [jnp.zeros((2 * SUBLANES, LANES), BF16)] * (nv // 2)
            for kk in range(K_UNROLL):
                k = kblk * K_UNROLL + kk
                bb = bcast(bt_s, slot, k)
                kb = bcast(kt_s, slot, k)
                rb = pltpu.bitcast(bcast(rd_s, slot, k), BF16)
                kkn = pltpu.bitcast(bcast(kkd_s, slot, k), BF16)
                for m in range(nv // 2):
                    pair_rows = []
                    for j in (2 * m, 2 * m + 1):
                        rows = slice(SUBLANES * j, SUBLANES * (j + 1))
                        sn = s_ref[k, rows, :] + (sa_s[rows, :] * bb + v_ref[t, rows, :] * kb)
                        s_ref[k, rows, :] = sn
                        pair_rows.append(sn)
                    snp = jnp.concatenate(pair_rows, axis=0).astype(BF16)
                    yb[m] = yb[m] + snp * rb
                    ab[m] = ab[m] + snp * kkn
            for m in range(nv // 2):
                y32 = yb[m].astype(F32)
                a32 = ab[m].astype(F32)
                for h, j in enumerate((2 * m, 2 * m + 1)):
                    yacc[j] = yacc[j] + y32[SUBLANES * h:SUBLANES * (h + 1)]
                    acc[j] = acc[j] + a32[SUBLANES * h:SUBLANES * (h + 1)]
            return yacc, acc

        init = [zero] * nv
        n_kblk = RW_N // K_UNROLL
        carry = lax.fori_loop(0, n_kblk - 1, key_block, (init, init))
        yacc, acc = key_block(n_kblk - 1, carry)
        sa_s[...] = -jnp.concatenate(acc, axis=0)
        y = jnp.concatenate(yacc, axis=0)
        if backward:
            y = y + yf_ref[t]
            yc = y - jnp.mean(y, axis=0, keepdims=True)
            var = jnp.mean(yc * yc, axis=0, keepdims=True)
            y = yc * lax.rsqrt(var + RW_GN_EPS) * lnw_ref[...] + lnb_ref[...]
            k_sum = scaled_key(t, af_ref[t]) + scaled_key(t, a_ref[t])
            y = y + jnp.sum(r_ref[t] * k_sum * rkp_ref[...], axis=0, keepdims=True) * v_ref[t]
        y_ref[t] = y

    restart()
    prepare(0, 0)
    prepare(1, 1)

    def pair(p, carry):
        s = 2 * p
        sweep(s, 0)
        prepare(s + 2, 0)
        sweep(s + 1, 1)
        prepare(s + 3, 1)
        return carry

    lax.fori_loop(0, tb // 2, pair, 0)


def _rw_scan(r_t, k_t, v_t, w_t, a_t, kkp, kap, n_ctx_blocks, *, run_after=None, readout=None):
    backward = readout is not None
    direction = int(backward)
    tt = r_t.shape[0]
    nblk = tt // SCAN_TILE
    seq = _seq_block(n_ctx_blocks, nblk)
    blk = lambda i: seq(direction, i)
    tile = (SCAN_TILE, RW_N, LANES)
    shared = pl.BlockSpec(tile, lambda i: (blk(i), 0, 0))
    const = pl.BlockSpec((RW_N, LANES), lambda i: (0, 0))
    y_blk = lambda i: jnp.where(i < n_ctx_blocks, blk(n_ctx_blocks), blk(i)) - n_ctx_blocks
    y_spec = pl.BlockSpec(tile, lambda i: (y_blk(i), 0, 0))
    in_specs = [shared, shared, shared, shared, shared, const, const]
    args = [r_t, k_t, v_t, w_t, a_t, kkp, kap]
    if backward:
        y_fwd, a_fwd, rkp, lnw, lnb = readout
        in_specs += [y_spec, shared, const, const, const]
        args += [y_fwd, a_fwd, rkp, lnw, lnb]
    else:
        in_specs += [pl.BlockSpec(memory_space=pl.ANY)]
        args += [run_after]
    return pl.pallas_call(
        functools.partial(_rw_scan_kernel, backward=backward),
        grid=(nblk,),
        in_specs=in_specs,
        out_specs=y_spec,
        out_shape=jax.ShapeDtypeStruct((tt - n_ctx_blocks * SCAN_TILE, RW_N, LANES), F32),
        scratch_shapes=[pltpu.VMEM((RW_N, RW_N, LANES), F32)] + [pltpu.VMEM((RW_N, LANES), F32)] * 3
                       + [pltpu.VMEM((2, RW_N, LANES), F32)] * 4,
        compiler_params=_params("arbitrary"),
        name="rw_scan_bwd" if backward else "rw_scan_fwd",
    )(*args)


def _hg_kernel(q_ref, f_ref, i_ref, lbp_ref, tri_ref, o_ref, st_ref, qd_scr, kv_scr, dec_scr):
    d = pl.program_id(1)
    i = pl.program_id(2)
    n_chunks = q_ref.shape[1] // HG_CHUNK

    @pl.when(i == 0)
    def _():
        st_ref[...] = jnp.zeros_like(st_ref)

    lbp = lbp_ref[...]
    e = jnp.exp(lbp - jnp.max(lbp, axis=0, keepdims=True))
    lb = e[0:1] / jnp.sum(e, axis=0, keepdims=True)
    t_idx = lax.broadcasted_iota(jnp.int32, (HG_CHUNK, HG_CHUNK), 0)
    s_idx = lax.broadcasted_iota(jnp.int32, (HG_CHUNK, HG_CHUNK), 1)
    nt = (((1,), (1,)), ((), ()))
    tn = (((0,), (0,)), ((), ()))

    def run(fwd):
        mask = (s_idx <= t_idx) if fwd else (s_idx >= t_idx)
        mid_row = HG_CHUNK // 2 - 1 if fwd else HG_CHUNK // 2
        last_row = HG_CHUNK - 1 if fwd else 0
        order = list(range(n_chunks)) if fwd else list(reversed(range(n_chunks)))

        f = lb + (1.0 - lb) * _sigmoid(f_ref[0])
        lf = jnp.log(f)
        hi = lf.astype(BF16)
        lo = (lf - hi.astype(F32)).astype(BF16)
        tri = tri_ref[0]
        b = _bdot(tri, hi) + _bdot(tri, lo)
        for c in order:
            rows = slice(c * HG_CHUNK, (c + 1) * HG_CHUNK)
            bc = b[rows]
            b_mid = bc[mid_row:mid_row + 1]
            b_last = bc[last_row:last_row + 1]
            q_in = q_ref[0, rows, :] * jnp.exp(bc - b_mid)
            k_in = (1.0 - f[rows]) * jnp.exp(b_mid - bc)
            qd_scr[rows, :] = (q_in * jnp.exp(b_mid)).astype(BF16)
            k_dec = (k_in * jnp.exp(b_last - b_mid)).astype(BF16)
            dec_scr[c:c + 1, :] = jnp.exp(b_last)
            q_in = q_in.astype(BF16)
            k_in = k_in.astype(BF16)
            vb = i_ref[0, rows, :].astype(BF16)
            heads = [slice(h * HG_N, (h + 1) * HG_N) for h in range(HG_HEADS)]
            scores = [lax.dot_general(q_in[:, cols], k_in[:, cols], nt, preferred_element_type=F32)
                      for cols in heads]
            for h, cols in enumerate(heads):
                kv_scr[c, h] = lax.dot_general(vb[:, cols], k_dec[:, cols], tn, preferred_element_type=F32)
            for h, cols in enumerate(heads):
                o_ref[0, 0, rows, cols] = _bdot(jnp.where(mask, scores[h], 0.0).astype(BF16), vb[:, cols])
        for c in order:
            rows = slice(c * HG_CHUNK, (c + 1) * HG_CHUNK)
            for h in range(HG_HEADS):
                cols = slice(h * HG_N, (h + 1) * HG_N)
                st = st_ref[h]
                o_ref[0, 0, rows, cols] += lax.dot_general(qd_scr[rows, cols], st.astype(BF16), nt,
                                                           preferred_element_type=F32)
                st_ref[h] = dec_scr[c:c + 1, cols] * st + kv_scr[c, h]

    @pl.when(d == 0)
    def _():
        run(True)

    @pl.when(d == 1)
    def _():
        run(False)


def _hg_tri(n_rows):
    t = jnp.arange(n_rows)[:, None]
    s = jnp.arange(n_rows)[None, :]
    same = (t // HG_CHUNK) == (s // HG_CHUNK)
    return jnp.stack([same & (s <= t), same & (s >= t)]).astype(BF16)


def _hg(zqi, zh, hg_lb, n_ctx):
    b, tt, _ = zh.shape
    w = HG_HEADS * HG_N
    nblk = tt // ROW_TILE
    n_chunks = ROW_TILE // HG_CHUNK
    blk = _seq_block(n_ctx // ROW_TILE, nblk)
    return pl.pallas_call(
        _hg_kernel,
        grid=(b, 2, nblk),
        in_specs=[pl.BlockSpec((1, ROW_TILE, w), lambda bi, d, i: (bi, blk(d, i), 0)),
                  pl.BlockSpec((1, ROW_TILE, w), lambda bi, d, i: (bi, blk(d, i), d)),
                  pl.BlockSpec((1, ROW_TILE, w), lambda bi, d, i: (bi, blk(d, i), 1)),
                  pl.BlockSpec(hg_lb.shape, lambda bi, d, i: (0, 0)),
                  pl.BlockSpec((1, ROW_TILE, ROW_TILE), lambda bi, d, i: (d, 0, 0))],
        out_specs=pl.BlockSpec((1, 1, ROW_TILE, w), lambda bi, d, i: (d, bi, blk(d, i), 0)),
        out_shape=jax.ShapeDtypeStruct((2, b, tt, w), F32),
        scratch_shapes=[pltpu.VMEM((HG_HEADS, HG_N, HG_N), F32),
                        pltpu.VMEM((ROW_TILE, w), BF16),
                        pltpu.VMEM((n_chunks, HG_HEADS, HG_N, HG_N), F32),
                        pltpu.VMEM((SUBLANES, w), F32)],
        compiler_params=_params("arbitrary", "arbitrary", "arbitrary"),
        name="hg",
    )(zqi, zh, zqi, hg_lb, _hg_tri(ROW_TILE))


def _merge_kernel(yrw_ref, g_ref, o_ref, gz_ref, zg_ref, x_ref, g1_ref, sh2_ref, sc2_ref, g2_ref, hnw_ref,
                  pa_ref, pb_ref, wo_ref, nw2_ref, w1_ref, w2_ref, fw_ref, out_ref):
    d = x_ref.shape[2]
    y_rw = (yrw_ref[...].T * g_ref[...]).astype(BF16)
    o = o_ref[0, 0] + o_ref[1, 0]
    gz = gz_ref[0]
    hnw = hnw_ref[...]
    parts = []
    for h in range(HG_HEADS):
        cols = slice(h * HG_N, (h + 1) * HG_N)
        oh = o[:, cols]
        ms = jnp.mean(oh * oh, axis=-1, keepdims=True)
        parts.append(oh * lax.rsqrt(ms + RMS_EPS) * hnw[:, cols])
    y_hg = (jnp.concatenate(parts, axis=1) * (gz * _sigmoid(gz))).astype(BF16)
    zg = zg_ref[0]
    m = _sigmoid(zg[:, :d]) * _bdot(y_rw, pa_ref[...]) + _sigmoid(zg[:, d:]) * _bdot(y_hg, pb_ref[...])
    x1 = x_ref[0] + g1_ref[0] * _bdot(m.astype(BF16), wo_ref[...])
    hb = _modulated_norm(x1, nw2_ref[...], sh2_ref[0], sc2_ref[0]).astype(BF16)
    u = jnp.maximum(_bdot(hb, w1_ref[...]), 0.0)
    y = x1 + g2_ref[0] * _bdot((u * u).astype(BF16), w2_ref[...])
    ms = jnp.mean(y * y, axis=-1, keepdims=True)
    out_ref[0] = y * lax.rsqrt(ms + RMS_EPS) * fw_ref[...]


def _merge(y_rw, g, o_hg, zh, zg, x, modcat, hnw, pa, pb, wo, nw2, w1, w2, fw, n_ctx):
    b, t, d = x.shape
    off = n_ctx // ROW_TILE
    lat = lambda bi, i: (bi, i, 0)
    cat = lambda bi, i: (bi, i + off, 0)
    mod = lambda col: pl.BlockSpec((1, 1, d), lambda bi, i: (2 * bi + 1, 0, col))
    const = lambda a: pl.BlockSpec(a.shape, lambda bi, i: (0,) * a.ndim, pipeline_mode=pl.Buffered(1))
    return pl.pallas_call(
        _merge_kernel,
        grid=(b, t // ROW_TILE),
        in_specs=[pl.BlockSpec((d, ROW_TILE), lambda bi, i: (bi, i)),
                  pl.BlockSpec((ROW_TILE, d), lambda bi, i: (i + off, bi)),
                  pl.BlockSpec((2, 1, ROW_TILE, d), lambda bi, i: (0, bi, i + off, 0)),
                  pl.BlockSpec((1, ROW_TILE, d), lambda bi, i: (bi, i + off, 2)),
                  pl.BlockSpec((1, ROW_TILE, 2 * d), cat),
                  pl.BlockSpec((1, ROW_TILE, d), lat),
                  mod(2), mod(3), mod(4), mod(5),
                  const(hnw), const(pa), const(pb), const(wo), const(nw2), const(w1), const(w2), const(fw)],
        out_specs=pl.BlockSpec((1, ROW_TILE, d), lat),
        out_shape=jax.ShapeDtypeStruct((b, t, d), F32),
        compiler_params=_params("arbitrary", "arbitrary"),
        name="merge_mlp",
    )(y_rw, g, o_hg, zh, zg, x, modcat, modcat, modcat, modcat, hnw, pa, pb, wo, nw2, w1, w2, fw)


def _to_scan(a):
    *lead, w, t = a.shape
    return jnp.swapaxes(a.reshape(*lead, w // RW_N, RW_N, t), -1, -3)


def _from_scan(a):
    t, n, bh = a.shape
    return jnp.swapaxes(a, 0, 2).reshape(bh * n, t)


def _head_tile(p, b):
    return jnp.tile(p.reshape(RW_HEADS, RW_N).T, (1, b))


def kernel(x, c, ctx, c_ctx, norm1_w, norm2_w, w_mod, b_mod, w_in, rw_mu, rw_w0, rw_w_up, rw_a0, rw_a_up, rw_g_up, rw_k_k, rw_k_a, rw_r_k, rw_ln_w, rw_ln_b, hg_lb, hg_norm_w, p_a, p_b, w_out, w_fc1, w_fc2, final_norm_w):
    b, t, d = x.shape
    n_ctx = ctx.shape[1]
    assert w_mod.shape[0] == 1, "single-layer block"
    assert b * RW_HEADS == LANES and n_ctx == ROW_TILE and t % ROW_TILE == 0
    assert d == RW_HEADS * RW_N == HG_HEADS * HG_N
    n_r = rw_mu.shape[2]
    rank_w, rank_a = rw_w_up.shape[2], rw_a_up.shape[2]
    assert rank_w + rank_a == LANES and n_r == 3 * d + 2 * LANES and w_in.shape[2] == n_r + 7 * d

    c_rows = jnp.zeros((2 * SUBLANES, d), F32).at[:b].set(c).at[b].set(c_ctx)
    mod = _mod(c_rows, w_mod[0], b_mod)
    modcat = jnp.stack([jnp.broadcast_to(mod[b], (b, N_MOD * d)), mod[:b]], axis=1).reshape(2 * b, 1, N_MOD * d)

    zr, zqi, zh, zg = _in_proj(ctx, x, norm1_w, modcat, w_in[0].astype(BF16), n_r)

    wup = jnp.pad(rw_w_up[0], ((0, 0), (0, rank_a), (0, 0))).astype(BF16)
    aup = jnp.pad(rw_a_up[0], ((0, 0), (rank_w, 0), (0, 0))).astype(BF16)
    *scan_ops, g = _rw_prep(zr, rw_mu[0], rw_w0[0], wup, rw_a0[0], aup, rw_g_up[0].astype(BF16), d)
    r_t, k_t, v_t, wf_t, wb_t, af_t, ab_t = (_to_scan(a) for a in scan_ops)
    o_hg = _hg(zqi, zh, hg_lb, n_ctx)
    kap = _head_tile(rw_k_a[0], b)
    tiles = (_head_tile(rw_k_k[0], b), kap, n_ctx // SCAN_TILE)
    y_fwd = _rw_scan(r_t, k_t, v_t, wf_t, af_t, *tiles, run_after=o_hg)
    y_rw_t = _rw_scan(r_t, k_t, v_t, wb_t, ab_t, *tiles,
                      readout=(y_fwd, af_t, _head_tile(rw_r_k[0].reshape(-1), b),
                               _head_tile(rw_ln_w[0], b), _head_tile(rw_ln_b[0], b)))
    y_rw = _from_scan(y_rw_t)

    return _merge(y_rw, g, o_hg, zh, zg, x, modcat, hg_norm_w, p_a[0].astype(BF16), p_b[0].astype(BF16),
                  w_out[0].astype(BF16), norm2_w, w_fc1[0].astype(BF16), w_fc2[0].astype(BF16),
                  final_norm_w.reshape(1, d), n_ctx)
```

```python
import functools

import jax
import jax.numpy as jnp
from jax import lax
from jax.experimental import pallas as pl
from jax.experimental.pallas import tpu as pltpu

F32 = jnp.float32
BF16 = jnp.bfloat16

GRID_W = 64
RW_HEADS = 16
RW_N = 64
HG_HEADS = 8
HG_N = 128
HG_CHUNK = 64
N_MOD = 6
RMS_EPS = 1e-6
RW_GN_EPS = 64e-5
L2_EPS = 1e-12
DECAY_SCALE = 0.6065306597126334

SUBLANES = 8
LANES = 128
ROW_TILE = 256
SCAN_TILE = 64
VMEM_LIMIT = 56 * 1024 * 1024


def _params(*sem):
    return pltpu.CompilerParams(dimension_semantics=sem, vmem_limit_bytes=VMEM_LIMIT)


def _bdot(a, b):
    return jnp.dot(a, b, preferred_element_type=F32)


def _sigmoid(x):
    return 1.0 / (1.0 + jnp.exp(-x))


def _mod_kernel(c_ref, w_ref, b_ref, o_ref):
    c = c_ref[...]
    act = c * _sigmoid(c)
    o_ref[...] = _bdot(act.astype(BF16), w_ref[...].astype(BF16)) + b_ref[...]


def _mod(c_rows, w_mod, b_mod):
    rows, d = c_rows.shape
    n = w_mod.shape[1]
    return pl.pallas_call(
        _mod_kernel,
        grid=(n // d,),
        in_specs=[pl.BlockSpec((rows, d), lambda j: (0, 0)),
                  pl.BlockSpec((d, d), lambda j: (0, j)),
                  pl.BlockSpec((1, d), lambda j: (0, j))],
        out_specs=pl.BlockSpec((rows, d), lambda j: (0, j)),
        out_shape=jax.ShapeDtypeStruct((rows, n), F32),
        compiler_params=_params("arbitrary"),
        name="mod",
    )(c_rows, w_mod, b_mod)


def _modulated_norm(x, nw, sh, sc):
    ms = jnp.mean(x * x, axis=-1, keepdims=True)
    return (x * lax.rsqrt(ms + RMS_EPS) * nw) * (1.0 + sc) + sh


def _in_proj_kernel(ctx_ref, x_ref, nw_ref, sh_ref, sc_ref, w_ref, zr_ref, zqi_ref, zh_ref, zg_ref):
    tokens = jnp.where(pl.program_id(1) == 0, ctx_ref[0], x_ref[0])
    hb = _modulated_norm(tokens, nw_ref[...], sh_ref[0], sc_ref[0]).astype(BF16)
    d = x_ref.shape[2]
    n_r = zr_ref.shape[2]
    col = lambda j0, j1: _bdot(hb, w_ref[:, n_r + j0 * d:n_r + j1 * d])
    zr_ref[0] = _bdot(hb, w_ref[:, :n_r])
    zqi_ref[0, :, :d] = col(0, 1).astype(BF16)
    zqi_ref[0, :, d:] = col(3, 4).astype(BF16)
    zh_ref[0, :, :2 * d] = col(1, 3)
    zh_ref[0, :, 2 * d:] = col(4, 5)
    zg_ref[0] = col(5, 7)


def _in_proj(ctx, x, nw, modcat, w_bf, n_r):
    b, t, d = x.shape
    tt = t + ctx.shape[1]
    nblk = tt // ROW_TILE
    mod_idx = lambda col: (lambda bi, i: (2 * bi + jnp.minimum(i, 1), 0, col))
    outs = ((n_r, F32), (2 * d, BF16), (3 * d, F32), (2 * d, F32))
    return pl.pallas_call(
        _in_proj_kernel,
        grid=(b, nblk),
        in_specs=[pl.BlockSpec((1, ROW_TILE, d), lambda bi, i: (bi, 0, 0)),
                  pl.BlockSpec((1, ROW_TILE, d), lambda bi, i: (bi, jnp.maximum(i - 1, 0), 0)),
                  pl.BlockSpec((1, d), lambda bi, i: (0, 0)),
                  pl.BlockSpec((1, 1, d), mod_idx(0)),
                  pl.BlockSpec((1, 1, d), mod_idx(1)),
                  pl.BlockSpec(w_bf.shape, lambda bi, i: (0, 0), pipeline_mode=pl.Buffered(1))],
        out_specs=[pl.BlockSpec((1, ROW_TILE, n), lambda bi, i: (bi, i, 0)) for n, _ in outs],
        out_shape=[jax.ShapeDtypeStruct((b, tt, n), dt) for n, dt in outs],
        compiler_params=_params("arbitrary", "arbitrary"),
        name="in_proj",
    )(ctx, x, nw, modcat, modcat, w_bf)


def _rw_prep_kernel(z_ref, zp_ref, zn_ref, mu_ref, w0_ref, wup_ref, a0_ref, aup_ref, gup_ref,
                    r_ref, k_ref, v_ref, wf_ref, wb_ref, af_ref, ab_ref, g_ref):
    i = pl.program_id(1)
    nblk = pl.num_programs(1)
    tt = z_ref.shape[1]
    d = r_ref.shape[0]
    is_lat = i > 0
    row = lax.broadcasted_iota(jnp.int32, (tt, 1), 0)
    col = row % GRID_W
    lmask = jnp.where(is_lat, col, row) == 0
    rmask = jnp.where(is_lat, col, row - (tt - GRID_W)) == GRID_W - 1
    latf = is_lat.astype(F32)
    up_ok = (i > 1).astype(F32)
    dn_ok = (i < nblk - 1).astype(F32)

    def shifted(c0, c1):
        z = z_ref[0, :, c0:c1]
        mu = mu_ref[:, c0:c1]
        left = jnp.where(lmask, 0.0, pltpu.roll(z, 1, 0))
        right = jnp.where(rmask, 0.0, pltpu.roll(z, tt - 1, 0))
        up = jnp.concatenate([zp_ref[0, :, c0:c1] * up_ok, z[:tt - GRID_W]], axis=0)
        down = jnp.concatenate([z[GRID_W:], zn_ref[0, :, c0:c1] * dn_ok], axis=0)
        m_up, m_dn = latf * mu[2:3], latf * mu[3:4]
        m_self = 1.0 - mu[0:1] - mu[1:2] - m_up - m_dn
        return z * m_self + left * mu[0:1] + right * mu[1:2] + up * m_up + down * m_dn

    r_ref[...] = shifted(0, d).T
    k_ref[...] = shifted(d, 2 * d).T
    v_ref[...] = shifted(2 * d, 3 * d).T
    rest = shifted(3 * d, z_ref.shape[2])
    xwa = rest[:, :LANES]
    xw_t = jnp.tanh(xwa).astype(BF16)
    xa_b = xwa.astype(BF16)
    for dr, (wd_ref, ad_ref) in enumerate(((wf_ref, af_ref), (wb_ref, ab_ref))):
        wz = w0_ref[dr:dr + 1, :] + _bdot(xw_t, wup_ref[dr])
        wd_ref[...] = jnp.exp(-DECAY_SCALE * _sigmoid(wz)).T
        ad_ref[...] = _sigmoid(a0_ref[dr:dr + 1, :] + _bdot(xa_b, aup_ref[dr])).T
    g_ref[...] = _bdot(_sigmoid(rest[:, LANES:]).astype(BF16), gup_ref[...])


def _rw_prep(zr, mu, w0, wup, a0, aup, gup, d):
    b, tt, nr = zr.shape
    nblk = tt // ROW_TILE
    per = ROW_TILE // GRID_W
    last = tt // GRID_W - 1
    const = lambda shape: pl.BlockSpec(shape, lambda bi, i: (0,) * len(shape))
    row_spec = pl.BlockSpec((d, ROW_TILE), lambda bi, i: (bi, i))
    gate_spec = pl.BlockSpec((ROW_TILE, d), lambda bi, i: (i, bi))
    row_shape = jax.ShapeDtypeStruct((b * d, tt), F32)
    gate_shape = jax.ShapeDtypeStruct((tt, b * d), F32)
    return pl.pallas_call(
        _rw_prep_kernel,
        grid=(b, nblk),
        in_specs=[pl.BlockSpec((1, ROW_TILE, nr), lambda bi, i: (bi, i, 0)),
                  pl.BlockSpec((1, GRID_W, nr), lambda bi, i: (bi, jnp.maximum(i * per - 1, 0), 0)),
                  pl.BlockSpec((1, GRID_W, nr), lambda bi, i: (bi, jnp.minimum(i * per + per, last), 0)),
                  const(mu.shape), const(w0.shape), const(wup.shape), const(a0.shape),
                  const(aup.shape), const(gup.shape)],
        out_specs=[row_spec] * 7 + [gate_spec],
        out_shape=[row_shape] * 7 + [gate_shape],
        compiler_params=_params("arbitrary", "arbitrary"),
        name="rw_prep",
    )(zr, zr, zr, mu, w0, wup, a0, aup, gup)


K_UNROLL = 16


def _seq_block(n_ctx_blocks, n_blocks):
    def blk(d, i):
        bwd = jnp.where(i < n_ctx_blocks, n_ctx_blocks - 1 - i, n_blocks - 1 + n_ctx_blocks - i)
        return jnp.where(d == 0, i, bwd)
    return blk


def _rw_scan_kernel(*refs, backward):
    if backward:
        (r_ref, k_ref, v_ref, w_ref, a_ref, kkp_ref, kap_ref, yf_ref, af_ref, rkp_ref, lnw_ref, lnb_ref,
         y_ref, s_ref, p_s, sa_s, kk0_s, bt_s, kt_s, rd_s, kkd_s) = refs
    else:
        (r_ref, k_ref, v_ref, w_ref, a_ref, kkp_ref, kap_ref, _order_ref,
         y_ref, s_ref, p_s, sa_s, kk0_s, bt_s, kt_s, rd_s, kkd_s) = refs
    i = pl.program_id(0)
    tb = r_ref.shape[0]
    nv = RW_N // SUBLANES

    @pl.when(i == 0)
    def _():
        s_ref[...] = jnp.zeros_like(s_ref)
        p_s[...] = jnp.ones_like(p_s)

    def bcast(ref, *idx):
        k = idx[-1]
        row = ref[(*idx[:-1], pl.ds(k, 1), slice(None))]
        return jnp.broadcast_to(row, (SUBLANES, LANES))

    def time_index(s):
        s = jnp.minimum(s, tb - 1)
        return tb - 1 - s if backward else s

    def bf16_pair(x):
        hi = lax.bitcast_convert_type(x.astype(BF16).astype(F32), jnp.uint32)
        return lax.bitcast_convert_type(hi | (hi >> 16), F32)

    def norm_key(t):
        kkr = k_ref[t] * kkp_ref[...]
        nrm = jnp.sqrt(jnp.sum(kkr * kkr, axis=0, keepdims=True))
        return kkr / jnp.maximum(nrm, L2_EPS)

    def scaled_key(t, a):
        return k_ref[t] * (1.0 + (a - 1.0) * kap_ref[...])

    def prepare(s, slot):
        t = time_index(s)
        a = a_ref[t]
        kk = norm_key(t)
        p_prev = p_s[...]
        kkd_s[1 - slot] = bf16_pair(p_prev * kk)
        p = p_prev * jnp.where(s < tb, w_ref[t], 1.0)
        p_s[...] = p
        inv_p = 1.0 / p
        bt_s[slot] = kk * a * inv_p
        kt_s[slot] = scaled_key(t, a) * inv_p
        rd_s[slot] = bf16_pair(p * r_ref[t])

    def restart():
        kk0_s[...] = norm_key(time_index(0))
        acc = [None] * nv
        for k in range(RW_N):
            pb = bcast(p_s, k)
            kkb = bcast(kk0_s, k)
            for j in range(nv):
                rows = slice(SUBLANES * j, SUBLANES * (j + 1))
                sn = s_ref[k, rows, :] * pb
                s_ref[k, rows, :] = sn
                acc[j] = sn * kkb if acc[j] is None else acc[j] + sn * kkb
        sa_s[...] = -jnp.concatenate(acc, axis=0)
        p_s[...] = jnp.ones_like(p_s)

    def sweep(s, slot):
        t = time_index(s)
        zero = jnp.zeros((SUBLANES, LANES), F32)

        def key_block(kblk, carry):
            yacc, acc = list(carry[0]), list(carry[1])
            yb = [jnp.zeros((2 * SUBLANES, LANES), BF16)] * (nv // 2)
            ab = [jnp.zeros((2 * SUBLANES, LANES), BF16)] * (nv // 2)
            for kk in range(K_UNROLL):
                k = kblk * K_UNROLL + kk
                bb = bcast(bt_s, slot, k)
                kb = bcast(kt_s, slot, k)
                rb = pltpu.bitcast(bcast(rd_s, slot, k), BF16)
                kkn = pltpu.bitcast(bcast(kkd_s, slot, k), BF16)
                for m in range(nv // 2):
                    pair_rows = []
                    for j in (2 * m, 2 * m + 1):
                        rows = slice(SUBLANES * j, SUBLANES * (j + 1))
                        sn = s_ref[k, rows, :] + (sa_s[rows, :] * bb + v_ref[t, rows, :] * kb)
                        s_ref[k, rows, :] = sn
                        pair_rows.append(sn)
                    snp = jnp.concatenate(pair_rows, axis=0).astype(BF16)
                    yb[m] = yb[m] + snp * rb
                    ab[m] = ab[m] + snp * kkn
            for m in range(nv // 2):
                y32 = yb[m].astype(F32)
                a32 = ab[m].astype(F32)
                for h, j in enumerate((2 * m, 2 * m + 1)):
                    yacc[j] = yacc[j] + y32[SUBLANES * h:SUBLANES * (h + 1)]
                    acc[j] = acc[j] + a32[SUBLANES * h:SUBLANES * (h + 1)]
            return yacc, acc

        init = [zero] * nv
        n_kblk = RW_N // K_UNROLL
        carry = lax.fori_loop(0, n_kblk - 1, key_block, (init, init))
        yacc, acc = key_block(n_kblk - 1, carry)
        sa_s[...] = -jnp.concatenate(acc, axis=0)
        y = jnp.concatenate(yacc, axis=0)
        if backward:
            y = y + yf_ref[t]
            yc = y - jnp.mean(y, axis=0, keepdims=True)
            var = jnp.mean(yc * yc, axis=0, keepdims=True)
            y = yc * lax.rsqrt(var + RW_GN_EPS) * lnw_ref[...] + lnb_ref[...]
            k_sum = k_ref[t] * (2.0 + (af_ref[t] + a_ref[t] - 2.0) * kap_ref[...])
            y = y + jnp.sum(r_ref[t] * k_sum * rkp_ref[...], axis=0, keepdims=True) * v_ref[t]
        y_ref[t] = y

    restart()
    prepare(0, 0)
    prepare(1, 1)

    def pair(p, carry):
        s = 2 * p
        sweep(s, 0)
        prepare(s + 2, 0)
        sweep(s + 1, 1)
        prepare(s + 3, 1)
        return carry

    lax.fori_loop(0, tb // 2, pair, 0)


def _rw_scan(r_t, k_t, v_t, w_t, a_t, kkp, kap, n_ctx_blocks, *, run_after=None, readout=None):
    backward = readout is not None
    direction = int(backward)
    tt = r_t.shape[0]
    nblk = tt // SCAN_TILE
    seq = _seq_block(n_ctx_blocks, nblk)
    blk = lambda i: seq(direction, i)
    tile = (SCAN_TILE, RW_N, LANES)
    shared = pl.BlockSpec(tile, lambda i: (blk(i), 0, 0))
    const = pl.BlockSpec((RW_N, LANES), lambda i: (0, 0))
    y_blk = lambda i: jnp.where(i < n_ctx_blocks, blk(n_ctx_blocks), blk(i)) - n_ctx_blocks
    y_spec = pl.BlockSpec(tile, lambda i: (y_blk(i), 0, 0))
    in_specs = [shared, shared, shared, shared, shared, const, const]
    args = [r_t, k_t, v_t, w_t, a_t, kkp, kap]
    if backward:
        y_fwd, a_fwd, rkp, lnw, lnb = readout
        in_specs += [y_spec, shared, const, const, const]
        args += [y_fwd, a_fwd, rkp, lnw, lnb]
    else:
        in_specs += [pl.BlockSpec(memory_space=pl.ANY)]
        args += [run_after]
    return pl.pallas_call(
        functools.partial(_rw_scan_kernel, backward=backward),
        grid=(nblk,),
        in_specs=in_specs,
        out_specs=y_spec,
        out_shape=jax.ShapeDtypeStruct((tt - n_ctx_blocks * SCAN_TILE, RW_N, LANES), F32),
        scratch_shapes=[pltpu.VMEM((RW_N, RW_N, LANES), F32)] + [pltpu.VMEM((RW_N, LANES), F32)] * 3
                       + [pltpu.VMEM((2, RW_N, LANES), F32)] * 4,
        compiler_params=_params("arbitrary"),
        name="rw_scan_bwd" if backward else "rw_scan_fwd",
    )(*args)


def _hg_kernel(q_ref, f_ref, i_ref, lbp_ref, tri_ref, o_ref, st_ref, qd_scr, kv_scr, dec_scr):
    d = pl.program_id(1)
    i = pl.program_id(2)
    n_chunks = q_ref.shape[1] // HG_CHUNK

    @pl.when(i == 0)
    def _():
        st_ref[...] = jnp.zeros_like(st_ref)

    lbp = lbp_ref[...]
    e = jnp.exp(lbp - jnp.max(lbp, axis=0, keepdims=True))
    lb = e[0:1] / jnp.sum(e, axis=0, keepdims=True)
    t_idx = lax.broadcasted_iota(jnp.int32, (HG_CHUNK, HG_CHUNK), 0)
    s_idx = lax.broadcasted_iota(jnp.int32, (HG_CHUNK, HG_CHUNK), 1)
    nt = (((1,), (1,)), ((), ()))
    tn = (((0,), (0,)), ((), ()))

    def run(fwd):
        mask = (s_idx <= t_idx) if fwd else (s_idx >= t_idx)
        mid_row = HG_CHUNK // 2 - 1 if fwd else HG_CHUNK // 2
        last_row = HG_CHUNK - 1 if fwd else 0
        order = list(range(n_chunks)) if fwd else list(reversed(range(n_chunks)))

        f = lb + (1.0 - lb) * _sigmoid(f_ref[0])
        lf = jnp.log(f)
        hi = lf.astype(BF16)
        lo = (lf - hi.astype(F32)).astype(BF16)
        tri = tri_ref[0]
        b = _bdot(tri, hi) + _bdot(tri, lo)
        for c in order:
            rows = slice(c * HG_CHUNK, (c + 1) * HG_CHUNK)
            bc = b[rows]
            b_mid = bc[mid_row:mid_row + 1]
            b_last = bc[last_row:last_row + 1]
            q_in = q_ref[0, rows, :] * jnp.exp(bc - b_mid)
            k_in = (1.0 - f[rows]) * jnp.exp(b_mid - bc)
            qd_scr[rows, :] = (q_in * jnp.exp(b_mid)).astype(BF16)
            k_dec = (k_in * jnp.exp(b_last - b_mid)).astype(BF16)
            dec_scr[c:c + 1, :] = jnp.exp(b_last)
            q_in = q_in.astype(BF16)
            k_in = k_in.astype(BF16)
            vb = i_ref[0, rows, :].astype(BF16)
            heads = [slice(h * HG_N, (h + 1) * HG_N) for h in range(HG_HEADS)]
            scores = [lax.dot_general(q_in[:, cols], k_in[:, cols], nt, preferred_element_type=F32)
                      for cols in heads]
            for h, cols in enumerate(heads):
                kv_scr[c, h] = lax.dot_general(vb[:, cols], k_dec[:, cols], tn, preferred_element_type=F32)
            for h, cols in enumerate(heads):
                o_ref[0, 0, rows, cols] = _bdot(jnp.where(mask, scores[h], 0.0).astype(BF16), vb[:, cols])
        for c in order:
            rows = slice(c * HG_CHUNK, (c + 1) * HG_CHUNK)
            for h in range(HG_HEADS):
                cols = slice(h * HG_N, (h + 1) * HG_N)
                st = st_ref[h]
                o_ref[0, 0, rows, cols] += lax.dot_general(qd_scr[rows, cols], st.astype(BF16), nt,
                                                           preferred_element_type=F32)
                st_ref[h] = dec_scr[c:c + 1, cols] * st + kv_scr[c, h]

    @pl.when(d == 0)
    def _():
        run(True)

    @pl.when(d == 1)
    def _():
        run(False)


def _hg_tri(n_rows):
    t = jnp.arange(n_rows)[:, None]
    s = jnp.arange(n_rows)[None, :]
    same = (t // HG_CHUNK) == (s // HG_CHUNK)
    return jnp.stack([same & (s <= t), same & (s >= t)]).astype(BF16)


def _hg(zqi, zh, hg_lb, n_ctx):
    b, tt, _ = zh.shape
    w = HG_HEADS * HG_N
    nblk = tt // ROW_TILE
    n_chunks = ROW_TILE // HG_CHUNK
    blk = _seq_block(n_ctx // ROW_TILE, nblk)
    return pl.pallas_call(
        _hg_kernel,
        grid=(b, 2, nblk),
        in_specs=[pl.BlockSpec((1, ROW_TILE, w), lambda bi, d, i: (bi, blk(d, i), 0)),
                  pl.BlockSpec((1, ROW_TILE, w), lambda bi, d, i: (bi, blk(d, i), d)),
                  pl.BlockSpec((1, ROW_TILE, w), lambda bi, d, i: (bi, blk(d, i), 1)),
                  pl.BlockSpec(hg_lb.shape, lambda bi, d, i: (0, 0)),
                  pl.BlockSpec((1, ROW_TILE, ROW_TILE), lambda bi, d, i: (d, 0, 0))],
        out_specs=pl.BlockSpec((1, 1, ROW_TILE, w), lambda bi, d, i: (d, bi, blk(d, i), 0)),
        out_shape=jax.ShapeDtypeStruct((2, b, tt, w), F32),
        scratch_shapes=[pltpu.VMEM((HG_HEADS, HG_N, HG_N), F32),
                        pltpu.VMEM((ROW_TILE, w), BF16),
                        pltpu.VMEM((n_chunks, HG_HEADS, HG_N, HG_N), F32),
                        pltpu.VMEM((SUBLANES, w), F32)],
        compiler_params=_params("arbitrary", "arbitrary", "arbitrary"),
        name="hg",
    )(zqi, zh, zqi, hg_lb, _hg_tri(ROW_TILE))


def _merge_kernel(yrw_ref, g_ref, o_ref, gz_ref, zg_ref, x_ref, g1_ref, sh2_ref, sc2_ref, g2_ref, hnw_ref,
                  pa_ref, pb_ref, wo_ref, nw2_ref, w1_ref, w2_ref, fw_ref, out_ref):
    d = x_ref.shape[2]
    y_rw = (yrw_ref[...].T * g_ref[...]).astype(BF16)
    o = o_ref[0, 0] + o_ref[1, 0]
    gz = gz_ref[0]
    hnw = hnw_ref[...]
    parts = []
    for h in range(HG_HEADS):
        cols = slice(h * HG_N, (h + 1) * HG_N)
        oh = o[:, cols]
        ms = jnp.mean(oh * oh, axis=-1, keepdims=True)
        parts.append(oh * lax.rsqrt(ms + RMS_EPS) * hnw[:, cols])
    y_hg = (jnp.concatenate(parts, axis=1) * (gz * _sigmoid(gz))).astype(BF16)
    zg = zg_ref[0]
    m = _sigmoid(zg[:, :d]) * _bdot(y_rw, pa_ref[...]) + _sigmoid(zg[:, d:]) * _bdot(y_hg, pb_ref[...])
    x1 = x_ref[0] + g1_ref[0] * _bdot(m.astype(BF16), wo_ref[...])
    hb = _modulated_norm(x1, nw2_ref[...], sh2_ref[0], sc2_ref[0]).astype(BF16)
    u = jnp.maximum(_bdot(hb, w1_ref[...]), 0.0)
    y = x1 + g2_ref[0] * _bdot((u * u).astype(BF16), w2_ref[...])
    ms = jnp.mean(y * y, axis=-1, keepdims=True)
    out_ref[0] = y * lax.rsqrt(ms + RMS_EPS) * fw_ref[...]


def _merge(y_rw, g, o_hg, zh, zg, x, modcat, hnw, pa, pb, wo, nw2, w1, w2, fw, n_ctx):
    b, t, d = x.shape
    off = n_ctx // ROW_TILE
    lat = lambda bi, i: (bi, i, 0)
    cat = lambda bi, i: (bi, i + off, 0)
    mod = lambda col: pl.BlockSpec((1, 1, d), lambda bi, i: (2 * bi + 1, 0, col))
    const = lambda a: pl.BlockSpec(a.shape, lambda bi, i: (0,) * a.ndim, pipeline_mode=pl.Buffered(1))
    return pl.pallas_call(
        _merge_kernel,
        grid=(b, t // ROW_TILE),
        in_specs=[pl.BlockSpec((d, ROW_TILE), lambda bi, i: (bi, i)),
                  pl.BlockSpec((ROW_TILE, d), lambda bi, i: (i + off, bi)),
                  pl.BlockSpec((2, 1, ROW_TILE, d), lambda bi, i: (0, bi, i + off, 0)),
                  pl.BlockSpec((1, ROW_TILE, d), lambda bi, i: (bi, i + off, 2)),
                  pl.BlockSpec((1, ROW_TILE, 2 * d), cat),
                  pl.BlockSpec((1, ROW_TILE, d), lat),
                  mod(2), mod(3), mod(4), mod(5),
                  const(hnw), const(pa), const(pb), const(wo), const(nw2), const(w1), const(w2), const(fw)],
        out_specs=pl.BlockSpec((1, ROW_TILE, d), lat),
        out_shape=jax.ShapeDtypeStruct((b, t, d), F32),
        compiler_params=_params("arbitrary", "arbitrary"),
        name="merge_mlp",
    )(y_rw, g, o_hg, zh, zg, x, modcat, modcat, modcat, modcat, hnw, pa, pb, wo, nw2, w1, w2, fw)


def _to_scan(a):
    *lead, w, t = a.shape
    return jnp.swapaxes(a.reshape(*lead, w // RW_N, RW_N, t), -1, -3)


def _from_scan(a):
    t, n, bh = a.shape
    return jnp.swapaxes(a, 0, 2).reshape(bh * n, t)


def _head_tile(p, b):
    return jnp.tile(p.reshape(RW_HEADS, RW_N).T, (1, b))


def kernel(x, c, ctx, c_ctx, norm1_w, norm2_w, w_mod, b_mod, w_in, rw_mu, rw_w0, rw_w_up, rw_a0, rw_a_up, rw_g_up, rw_k_k, rw_k_a, rw_r_k, rw_ln_w, rw_ln_b, hg_lb, hg_norm_w, p_a, p_b, w_out, w_fc1, w_fc2, final_norm_w):
    b, t, d = x.shape
    n_ctx = ctx.shape[1]
    assert w_mod.shape[0] == 1, "single-layer block"
    assert b * RW_HEADS == LANES and n_ctx == ROW_TILE and t % ROW_TILE == 0
    assert d == RW_HEADS * RW_N == HG_HEADS * HG_N
    n_r = rw_mu.shape[2]
    rank_w, rank_a = rw_w_up.shape[2], rw_a_up.shape[2]
    assert rank_w + rank_a == LANES and n_r == 3 * d + 2 * LANES and w_in.shape[2] == n_r + 7 * d

    c_rows = jnp.zeros((2 * SUBLANES, d), F32).at[:b].set(c).at[b].set(c_ctx)
    mod = _mod(c_rows, w_mod[0], b_mod)
    modcat = jnp.stack([jnp.broadcast_to(mod[b], (b, N_MOD * d)), mod[:b]], axis=1).reshape(2 * b, 1, N_MOD * d)

    zr, zqi, zh, zg = _in_proj(ctx, x, norm1_w, modcat, w_in[0].astype(BF16), n_r)

    wup = jnp.pad(rw_w_up[0], ((0, 0), (0, rank_a), (0, 0))).astype(BF16)
    aup = jnp.pad(rw_a_up[0], ((0, 0), (rank_w, 0), (0, 0))).astype(BF16)
    *scan_ops, g = _rw_prep(zr, rw_mu[0], rw_w0[0], wup, rw_a0[0], aup, rw_g_up[0].astype(BF16), d)
    r_t, k_t, v_t, wf_t, wb_t, af_t, ab_t = (_to_scan(a) for a in scan_ops)
    o_hg = _hg(zqi, zh, hg_lb, n_ctx)
    kap = _head_tile(rw_k_a[0], b)
    tiles = (_head_tile(rw_k_k[0], b), kap, n_ctx // SCAN_TILE)
    y_fwd = _rw_scan(r_t, k_t, v_t, wf_t, af_t, *tiles, run_after=o_hg)
    y_rw_t = _rw_scan(r_t, k_t, v_t, wb_t, ab_t, *tiles,
                      readout=(y_fwd, af_t, _head_tile(rw_r_k[0].reshape(-1), b),
                               _head_tile(rw_ln_w[0], b), _head_tile(rw_ln_b[0], b)))
    y_rw = _from_scan(y_rw_t)

    return _merge(y_rw, g, o_hg, zh, zg, x, modcat, hg_norm_w, p_a[0].astype(BF16), p_b[0].astype(BF16),
                  w_out[0].astype(BF16), norm2_w, w_fc1[0].astype(BF16), w_fc2[0].astype(BF16),
                  final_norm_w.reshape(1, d), n_ctx)
```

```python
import functools

import jax
import jax.numpy as jnp
from jax import lax
from jax.experimental import pallas as pl
from jax.experimental.pallas import tpu as pltpu

F32 = jnp.float32
BF16 = jnp.bfloat16

GRID_W = 64
RW_HEADS = 16
RW_N = 64
HG_HEADS = 8
HG_N = 128
HG_CHUNK = 64
N_MOD = 6
RMS_EPS = 1e-6
RW_GN_EPS = 64e-5
L2_EPS = 1e-12
DECAY_SCALE = 0.6065306597126334

SUBLANES = 8
LANES = 128
ROW_TILE = 256
SCAN_TILE = 64
VMEM_LIMIT = 56 * 1024 * 1024


def _params(*sem):
    return pltpu.CompilerParams(dimension_semantics=sem, vmem_limit_bytes=VMEM_LIMIT)


def _bdot(a, b):
    return jnp.dot(a, b, preferred_element_type=F32)


def _sigmoid(x):
    return 1.0 / (1.0 + jnp.exp(-x))


def _mod_kernel(c_ref, w_ref, b_ref, o_ref):
    c = c_ref[...]
    act = c * _sigmoid(c)
    o_ref[...] = _bdot(act.astype(BF16), w_ref[...].astype(BF16)) + b_ref[...]


def _mod(c_rows, w_mod, b_mod):
    rows, d = c_rows.shape
    n = w_mod.shape[1]
    return pl.pallas_call(
        _mod_kernel,
        grid=(n // d,),
        in_specs=[pl.BlockSpec((rows, d), lambda j: (0, 0)),
                  pl.BlockSpec((d, d), lambda j: (0, j)),
                  pl.BlockSpec((1, d), lambda j: (0, j))],
        out_specs=pl.BlockSpec((rows, d), lambda j: (0, j)),
        out_shape=jax.ShapeDtypeStruct((rows, n), F32),
        compiler_params=_params("arbitrary"),
        name="mod",
    )(c_rows, w_mod, b_mod)


def _modulated_norm(x, nw, sh, sc):
    ms = jnp.mean(x * x, axis=-1, keepdims=True)
    return (x * lax.rsqrt(ms + RMS_EPS) * nw) * (1.0 + sc) + sh


def _in_proj_kernel(ctx_ref, x_ref, nw_ref, sh_ref, sc_ref, w_ref, zr_ref, zqi_ref, zh_ref, zg_ref):
    tokens = jnp.where(pl.program_id(1) == 0, ctx_ref[0], x_ref[0])
    hb = _modulated_norm(tokens, nw_ref[...], sh_ref[0], sc_ref[0]).astype(BF16)
    d = x_ref.shape[2]
    n_r = zr_ref.shape[2]
    col = lambda j0, j1: _bdot(hb, w_ref[:, n_r + j0 * d:n_r + j1 * d])
    zr_ref[0] = _bdot(hb, w_ref[:, :n_r])
    zqi_ref[0, :, :d] = col(0, 1).astype(BF16)
    zqi_ref[0, :, d:] = col(3, 4).astype(BF16)
    zh_ref[0, :, :2 * d] = col(1, 3)
    zh_ref[0, :, 2 * d:] = col(4, 5)
    zg_ref[0] = col(5, 7)


def _in_proj(ctx, x, nw, modcat, w_bf, n_r):
    b, t, d = x.shape
    tt = t + ctx.shape[1]
    nblk = tt // ROW_TILE
    mod_idx = lambda col: (lambda bi, i: (2 * bi + jnp.minimum(i, 1), 0, col))
    outs = ((n_r, F32), (2 * d, BF16), (3 * d, F32), (2 * d, F32))
    return pl.pallas_call(
        _in_proj_kernel,
        grid=(b, nblk),
        in_specs=[pl.BlockSpec((1, ROW_TILE, d), lambda bi, i: (bi, 0, 0)),
                  pl.BlockSpec((1, ROW_TILE, d), lambda bi, i: (bi, jnp.maximum(i - 1, 0), 0)),
                  pl.BlockSpec((1, d), lambda bi, i: (0, 0)),
                  pl.BlockSpec((1, 1, d), mod_idx(0)),
                  pl.BlockSpec((1, 1, d), mod_idx(1)),
                  pl.BlockSpec(w_bf.shape, lambda bi, i: (0, 0), pipeline_mode=pl.Buffered(1))],
        out_specs=[pl.BlockSpec((1, ROW_TILE, n), lambda bi, i: (bi, i, 0)) for n, _ in outs],
        out_shape=[jax.ShapeDtypeStruct((b, tt, n), dt) for n, dt in outs],
        compiler_params=_params("arbitrary", "arbitrary"),
        name="in_proj",
    )(ctx, x, nw, modcat, modcat, w_bf)


def _rw_prep_kernel(z_ref, zn_ref, mu_ref, w0_ref, wup_ref, a0_ref, aup_ref, gup_ref,
                    r_ref, k_ref, v_ref, wf_ref, wb_ref, af_ref, ab_ref, g_ref, zp_scr):
    i = pl.program_id(1)
    nblk = pl.num_programs(1)
    tt = z_ref.shape[1]
    d = r_ref.shape[0]
    is_lat = i > 0
    row = lax.broadcasted_iota(jnp.int32, (tt, 1), 0)
    col = row % GRID_W
    lmask = jnp.where(is_lat, col, row) == 0
    rmask = jnp.where(is_lat, col, row - (tt - GRID_W)) == GRID_W - 1
    latf = is_lat.astype(F32)
    up_ok = (i > 1).astype(F32)
    dn_ok = (i < nblk - 1).astype(F32)

    @pl.when(i == 0)
    def _():
        zp_scr[...] = jnp.zeros_like(zp_scr)

    def shifted(c0, c1):
        z = z_ref[0, :, c0:c1]
        mu = mu_ref[:, c0:c1]
        left = jnp.where(lmask, 0.0, pltpu.roll(z, 1, 0))
        right = jnp.where(rmask, 0.0, pltpu.roll(z, tt - 1, 0))
        up = jnp.concatenate([zp_scr[:, c0:c1] * up_ok, z[:tt - GRID_W]], axis=0)
        down = jnp.concatenate([z[GRID_W:], zn_ref[0, :, c0:c1] * dn_ok], axis=0)
        m_up, m_dn = latf * mu[2:3], latf * mu[3:4]
        m_self = 1.0 - mu[0:1] - mu[1:2] - m_up - m_dn
        return z * m_self + left * mu[0:1] + right * mu[1:2] + up * m_up + down * m_dn

    r_ref[...] = shifted(0, d).T
    k_ref[...] = shifted(d, 2 * d).T
    v_ref[...] = shifted(2 * d, 3 * d).T
    rest = shifted(3 * d, z_ref.shape[2])
    xwa = rest[:, :LANES]
    xw_t = jnp.tanh(xwa).astype(BF16)
    xa_b = xwa.astype(BF16)
    for dr, (wd_ref, ad_ref) in enumerate(((wf_ref, af_ref), (wb_ref, ab_ref))):
        wz = w0_ref[dr:dr + 1, :] + _bdot(xw_t, wup_ref[dr])
        wd_ref[...] = jnp.exp(-DECAY_SCALE * _sigmoid(wz)).T
        ad_ref[...] = _sigmoid(a0_ref[dr:dr + 1, :] + _bdot(xa_b, aup_ref[dr])).T
    g_ref[...] = _bdot(_sigmoid(rest[:, LANES:]).astype(BF16), gup_ref[...])
    zp_scr[...] = z_ref[0, tt - GRID_W:, :]


def _rw_prep(zr, mu, w0, wup, a0, aup, gup, d):
    b, tt, nr = zr.shape
    nblk = tt // ROW_TILE
    per = ROW_TILE // GRID_W
    last = tt // GRID_W - 1
    const = lambda shape: pl.BlockSpec(shape, lambda bi, i: (0,) * len(shape))
    row_spec = pl.BlockSpec((d, ROW_TILE), lambda bi, i: (bi, i))
    gate_spec = pl.BlockSpec((ROW_TILE, d), lambda bi, i: (i, bi))
    row_shape = jax.ShapeDtypeStruct((b * d, tt), F32)
    gate_shape = jax.ShapeDtypeStruct((tt, b * d), F32)
    return pl.pallas_call(
        _rw_prep_kernel,
        grid=(b, nblk),
        in_specs=[pl.BlockSpec((1, ROW_TILE, nr), lambda bi, i: (bi, i, 0)),
                  pl.BlockSpec((1, GRID_W, nr), lambda bi, i: (bi, jnp.minimum(i * per + per, last), 0)),
                  const(mu.shape), const(w0.shape), const(wup.shape), const(a0.shape),
                  const(aup.shape), const(gup.shape)],
        out_specs=[row_spec] * 7 + [gate_spec],
        out_shape=[row_shape] * 7 + [gate_shape],
        scratch_shapes=[pltpu.VMEM((GRID_W, nr), F32)],
        compiler_params=_params("arbitrary", "arbitrary"),
        name="rw_prep",
    )(zr, zr, mu, w0, wup, a0, aup, gup)


K_UNROLL = 16


def _seq_block(n_ctx_blocks, n_blocks):
    def blk(d, i):
        bwd = jnp.where(i < n_ctx_blocks, n_ctx_blocks - 1 - i, n_blocks - 1 + n_ctx_blocks - i)
        return jnp.where(d == 0, i, bwd)
    return blk


def _rw_scan_kernel(*refs, backward):
    if backward:
        (r_ref, k_ref, v_ref, w_ref, a_ref, kkp_ref, kap_ref, yf_ref, af_ref, rkp_ref, lnw_ref, lnb_ref,
         y_ref, s_ref, p_s, sa_s, kk0_s, bt_s, kt_s, rd_s, kkd_s) = refs
    else:
        (r_ref, k_ref, v_ref, w_ref, a_ref, kkp_ref, kap_ref, _order_ref,
         y_ref, s_ref, p_s, sa_s, kk0_s, bt_s, kt_s, rd_s, kkd_s) = refs
    i = pl.program_id(0)
    tb = r_ref.shape[0]
    nv = RW_N // SUBLANES

    @pl.when(i == 0)
    def _():
        s_ref[...] = jnp.zeros_like(s_ref)
        p_s[...] = jnp.ones_like(p_s)

    def bcast(ref, *idx):
        k = idx[-1]
        row = ref[(*idx[:-1], pl.ds(k, 1), slice(None))]
        return jnp.broadcast_to(row, (SUBLANES, LANES))

    def time_index(s):
        s = jnp.minimum(s, tb - 1)
        return tb - 1 - s if backward else s

    def bf16_pair(x):
        hi = lax.bitcast_convert_type(x.astype(BF16).astype(F32), jnp.uint32)
        return lax.bitcast_convert_type(hi | (hi >> 16), F32)

    def norm_key(t):
        kkr = k_ref[t] * kkp_ref[...]
        nrm = jnp.sqrt(jnp.sum(kkr * kkr, axis=0, keepdims=True))
        return kkr / jnp.maximum(nrm, L2_EPS)

    def scaled_key(t, a):
        return k_ref[t] * (1.0 + (a - 1.0) * kap_ref[...])

    def prepare(s, slot):
        t = time_index(s)
        a = a_ref[t]
        kk = norm_key(t)
        p_prev = p_s[...]
        kkd_s[1 - slot] = bf16_pair(p_prev * kk)
        p = p_prev * jnp.where(s < tb, w_ref[t], 1.0)
        p_s[...] = p
        inv_p = 1.0 / p
        bt_s[slot] = kk * a * inv_p
        kt_s[slot] = scaled_key(t, a) * inv_p
        rd_s[slot] = bf16_pair(p * r_ref[t])

    def restart():
        kk0_s[...] = norm_key(time_index(0))
        acc = [None] * nv
        for k in range(RW_N):
            pb = bcast(p_s, k)
            kkb = bcast(kk0_s, k)
            for j in range(nv):
                rows = slice(SUBLANES * j, SUBLANES * (j + 1))
                sn = s_ref[k, rows, :] * pb
                s_ref[k, rows, :] = sn
                acc[j] = sn * kkb if acc[j] is None else acc[j] + sn * kkb
        sa_s[...] = -jnp.concatenate(acc, axis=0)
        p_s[...] = jnp.ones_like(p_s)

    def sweep(s, slot):
        t = time_index(s)
        zero = jnp.zeros((SUBLANES, LANES), F32)

        def key_block(kblk, carry):
            yacc, acc = list(carry[0]), list(carry[1])
            yb = [jnp.zeros((2 * SUBLANES, LANES), BF16)] * (nv // 2)
            ab = [jnp.zeros((2 * SUBLANES, LANES), BF16)] * (nv // 2)
            for kk in range(K_UNROLL):
                k = kblk * K_UNROLL + kk
                bb = bcast(bt_s, slot, k)
                kb = bcast(kt_s, slot, k)
                rb = pltpu.bitcast(bcast(rd_s, slot, k), BF16)
                kkn = pltpu.bitcast(bcast(kkd_s, slot, k), BF16)
                for m in range(nv // 2):
                    pair_rows = []
                    for j in (2 * m, 2 * m + 1):
                        rows = slice(SUBLANES * j, SUBLANES * (j + 1))
                        sn = s_ref[k, rows, :] + (sa_s[rows, :] * bb + v_ref[t, rows, :] * kb)
                        s_ref[k, rows, :] = sn
                        pair_rows.append(sn)
                    snp = jnp.concatenate(pair_rows, axis=0).astype(BF16)
                    yb[m] = yb[m] + snp * rb
                    ab[m] = ab[m] + snp * kkn
            for m in range(nv // 2):
                y32 = yb[m].astype(F32)
                a32 = ab[m].astype(F32)
                for h, j in enumerate((2 * m, 2 * m + 1)):
                    yacc[j] = yacc[j] + y32[SUBLANES * h:SUBLANES * (h + 1)]
                    acc[j] = acc[j] + a32[SUBLANES * h:SUBLANES * (h + 1)]
            return yacc, acc

        init = [zero] * nv
        n_kblk = RW_N // K_UNROLL
        carry = lax.fori_loop(0, n_kblk - 1, key_block, (init, init))
        yacc, acc = key_block(n_kblk - 1, carry)
        sa_s[...] = -jnp.concatenate(acc, axis=0)
        y = jnp.concatenate(yacc, axis=0)
        if backward:
            y = y + yf_ref[t]
            yc = y - jnp.mean(y, axis=0, keepdims=True)
            var = jnp.mean(yc * yc, axis=0, keepdims=True)
            y = yc * lax.rsqrt(var + RW_GN_EPS) * lnw_ref[...] + lnb_ref[...]
            k_sum = k_ref[t] * (2.0 + (af_ref[t] + a_ref[t] - 2.0) * kap_ref[...])
            y = y + jnp.sum(r_ref[t] * k_sum * rkp_ref[...], axis=0, keepdims=True) * v_ref[t]
        y_ref[t] = y

    restart()
    prepare(0, 0)
    prepare(1, 1)

    def pair(p, carry):
        s = 2 * p
        sweep(s, 0)
        prepare(s + 2, 0)
        sweep(s + 1, 1)
        prepare(s + 3, 1)
        return carry

    lax.fori_loop(0, tb // 2, pair, 0)


def _rw_scan(r_t, k_t, v_t, w_t, a_t, kkp, kap, n_ctx_blocks, *, run_after=None, readout=None):
    backward = readout is not None
    direction = int(backward)
    tt = r_t.shape[0]
    nblk = tt // SCAN_TILE
    seq = _seq_block(n_ctx_blocks, nblk)
    blk = lambda i: seq(direction, i)
    tile = (SCAN_TILE, RW_N, LANES)
    shared = pl.BlockSpec(tile, lambda i: (blk(i), 0, 0))
    const = pl.BlockSpec((RW_N, LANES), lambda i: (0, 0))
    y_blk = lambda i: jnp.where(i < n_ctx_blocks, blk(n_ctx_blocks), blk(i)) - n_ctx_blocks
    y_spec = pl.BlockSpec(tile, lambda i: (y_blk(i), 0, 0))
    in_specs = [shared, shared, shared, shared, shared, const, const]
    args = [r_t, k_t, v_t, w_t, a_t, kkp, kap]
    if backward:
        y_fwd, a_fwd, rkp, lnw, lnb = readout
        in_specs += [y_spec, shared, const, const, const]
        args += [y_fwd, a_fwd, rkp, lnw, lnb]
    else:
        in_specs += [pl.BlockSpec(memory_space=pl.ANY)]
        args += [run_after]
    return pl.pallas_call(
        functools.partial(_rw_scan_kernel, backward=backward),
        grid=(nblk,),
        in_specs=in_specs,
        out_specs=y_spec,
        out_shape=jax.ShapeDtypeStruct((tt - n_ctx_blocks * SCAN_TILE, RW_N, LANES), F32),
        scratch_shapes=[pltpu.VMEM((RW_N, RW_N, LANES), F32)] + [pltpu.VMEM((RW_N, LANES), F32)] * 3
                       + [pltpu.VMEM((2, RW_N, LANES), F32)] * 4,
        compiler_params=_params("arbitrary"),
        name="rw_scan_bwd" if backward else "rw_scan_fwd",
    )(*args)


def _hg_kernel(q_ref, f_ref, i_ref, lbp_ref, tri_ref, o_ref, st_ref, qd_scr, kv_scr, dec_scr):
    d = pl.program_id(1)
    i = pl.program_id(2)
    n_chunks = q_ref.shape[1] // HG_CHUNK

    @pl.when(i == 0)
    def _():
        st_ref[...] = jnp.zeros_like(st_ref)

    lbp = lbp_ref[...]
    e = jnp.exp(lbp - jnp.max(lbp, axis=0, keepdims=True))
    lb = e[0:1] / jnp.sum(e, axis=0, keepdims=True)
    t_idx = lax.broadcasted_iota(jnp.int32, (HG_CHUNK, HG_CHUNK), 0)
    s_idx = lax.broadcasted_iota(jnp.int32, (HG_CHUNK, HG_CHUNK), 1)
    nt = (((1,), (1,)), ((), ()))
    tn = (((0,), (0,)), ((), ()))

    def run(fwd):
        mask = (s_idx <= t_idx) if fwd else (s_idx >= t_idx)
        mid_row = HG_CHUNK // 2 - 1 if fwd else HG_CHUNK // 2
        last_row = HG_CHUNK - 1 if fwd else 0
        order = list(range(n_chunks)) if fwd else list(reversed(range(n_chunks)))

        f = lb + (1.0 - lb) * _sigmoid(f_ref[0])
        lf = jnp.log(f)
        hi = lf.astype(BF16)
        lo = (lf - hi.astype(F32)).astype(BF16)
        tri = tri_ref[0]
        b = _bdot(tri, hi) + _bdot(tri, lo)
        for c in order:
            rows = slice(c * HG_CHUNK, (c + 1) * HG_CHUNK)
            bc = b[rows]
            b_mid = bc[mid_row:mid_row + 1]
            b_last = bc[last_row:last_row + 1]
            q_in = q_ref[0, rows, :] * jnp.exp(bc - b_mid)
            k_in = (1.0 - f[rows]) * jnp.exp(b_mid - bc)
            qd_scr[rows, :] = (q_in * jnp.exp(b_mid)).astype(BF16)
            k_dec = (k_in * jnp.exp(b_last - b_mid)).astype(BF16)
            dec_scr[c:c + 1, :] = jnp.exp(b_last)
            q_in = q_in.astype(BF16)
            k_in = k_in.astype(BF16)
            vb = i_ref[0, rows, :].astype(BF16)
            heads = [slice(h * HG_N, (h + 1) * HG_N) for h in range(HG_HEADS)]
            scores = [lax.dot_general(q_in[:, cols], k_in[:, cols], nt, preferred_element_type=F32)
                      for cols in heads]
            for h, cols in enumerate(heads):
                kv_scr[c, h] = lax.dot_general(vb[:, cols], k_dec[:, cols], tn, preferred_element_type=F32)
            for h, cols in enumerate(heads):
                o_ref[0, 0, rows, cols] = _bdot(jnp.where(mask, scores[h], 0.0).astype(BF16), vb[:, cols])
        for c in order:
            rows = slice(c * HG_CHUNK, (c + 1) * HG_CHUNK)
            for h in range(HG_HEADS):
                cols = slice(h * HG_N, (h + 1) * HG_N)
                st = st_ref[h]
                o_ref[0, 0, rows, cols] += lax.dot_general(qd_scr[rows, cols], st.astype(BF16), nt,
                                                           preferred_element_type=F32)
                st_ref[h] = dec_scr[c:c + 1, cols] * st + kv_scr[c, h]

    @pl.when(d == 0)
    def _():
        run(True)

    @pl.when(d == 1)
    def _():
        run(False)


def _hg_tri(n_rows):
    t = jnp.arange(n_rows)[:, None]
    s = jnp.arange(n_rows)[None, :]
    same = (t // HG_CHUNK) == (s // HG_CHUNK)
    return jnp.stack([same & (s <= t), same & (s >= t)]).astype(BF16)


def _hg(zqi, zh, hg_lb, n_ctx):
    b, tt, _ = zh.shape
    w = HG_HEADS * HG_N
    nblk = tt // ROW_TILE
    n_chunks = ROW_TILE // HG_CHUNK
    blk = _seq_block(n_ctx // ROW_TILE, nblk)
    return pl.pallas_call(
        _hg_kernel,
        grid=(b, 2, nblk),
        in_specs=[pl.BlockSpec((1, ROW_TILE, w), lambda bi, d, i: (bi, blk(d, i), 0)),
                  pl.BlockSpec((1, ROW_TILE, w), lambda bi, d, i: (bi, blk(d, i), d)),
                  pl.BlockSpec((1, ROW_TILE, w), lambda bi, d, i: (bi, blk(d, i), 1)),
                  pl.BlockSpec(hg_lb.shape, lambda bi, d, i: (0, 0)),
                  pl.BlockSpec((1, ROW_TILE, ROW_TILE), lambda bi, d, i: (d, 0, 0))],
        out_specs=pl.BlockSpec((1, 1, ROW_TILE, w), lambda bi, d, i: (d, bi, blk(d, i), 0)),
        out_shape=jax.ShapeDtypeStruct((2, b, tt, w), F32),
        scratch_shapes=[pltpu.VMEM((HG_HEADS, HG_N, HG_N), F32),
                        pltpu.VMEM((ROW_TILE, w), BF16),
                        pltpu.VMEM((n_chunks, HG_HEADS, HG_N, HG_N), F32),
                        pltpu.VMEM((SUBLANES, w), F32)],
        compiler_params=_params("arbitrary", "arbitrary", "arbitrary"),
        name="hg",
    )(zqi, zh, zqi, hg_lb, _hg_tri(ROW_TILE))


def _merge_kernel(yrw_ref, g_ref, o_ref, gz_ref, zg_ref, x_ref, g1_ref, sh2_ref, sc2_ref, g2_ref, hnw_ref,
                  pa_ref, pb_ref, wo_ref, nw2_ref, w1_ref, w2_ref, fw_ref, out_ref):
    d = x_ref.shape[2]
    y_rw = (yrw_ref[...].T * g_ref[...]).astype(BF16)
    o = o_ref[0, 0] + o_ref[1, 0]
    gz = gz_ref[0]
    hnw = hnw_ref[...]
    parts = []
    for h in range(HG_HEADS):
        cols = slice(h * HG_N, (h + 1) * HG_N)
        oh = o[:, cols]
        ms = jnp.mean(oh * oh, axis=-1, keepdims=True)
        parts.append(oh * lax.rsqrt(ms + RMS_EPS) * hnw[:, cols])
    y_hg = (jnp.concatenate(parts, axis=1) * (gz * _sigmoid(gz))).astype(BF16)
    zg = zg_ref[0]
    m = _sigmoid(zg[:, :d]) * _bdot(y_rw, pa_ref[...]) + _sigmoid(zg[:, d:]) * _bdot(y_hg, pb_ref[...])
    x1 = x_ref[0] + g1_ref[0] * _bdot(m.astype(BF16), wo_ref[...])
    hb = _modulated_norm(x1, nw2_ref[...], sh2_ref[0], sc2_ref[0]).astype(BF16)
    u = jnp.maximum(_bdot(hb, w1_ref[...]), 0.0)
    y = x1 + g2_ref[0] * _bdot((u * u).astype(BF16), w2_ref[...])
    ms = jnp.mean(y * y, axis=-1, keepdims=True)
    out_ref[0] = y * lax.rsqrt(ms + RMS_EPS) * fw_ref[...]


def _merge(y_rw, g, o_hg, zh, zg, x, modcat, hnw, pa, pb, wo, nw2, w1, w2, fw, n_ctx):
    b, t, d = x.shape
    off = n_ctx // ROW_TILE
    lat = lambda bi, i: (bi, i, 0)
    cat = lambda bi, i: (bi, i + off, 0)
    mod = lambda col: pl.BlockSpec((1, 1, d), lambda bi, i: (2 * bi + 1, 0, col))
    const = lambda a: pl.BlockSpec(a.shape, lambda bi, i: (0,) * a.ndim, pipeline_mode=pl.Buffered(1))
    return pl.pallas_call(
        _merge_kernel,
        grid=(b, t // ROW_TILE),
        in_specs=[pl.BlockSpec((d, ROW_TILE), lambda bi, i: (bi, i)),
                  pl.BlockSpec((ROW_TILE, d), lambda bi, i: (i + off, bi)),
                  pl.BlockSpec((2, 1, ROW_TILE, d), lambda bi, i: (0, bi, i + off, 0)),
                  pl.BlockSpec((1, ROW_TILE, d), lambda bi, i: (bi, i + off, 2)),
                  pl.BlockSpec((1, ROW_TILE, 2 * d), cat),
                  pl.BlockSpec((1, ROW_TILE, d), lat),
                  mod(2), mod(3), mod(4), mod(5),
                  const(hnw), const(pa), const(pb), const(wo), const(nw2), const(w1), const(w2), const(fw)],
        out_specs=pl.BlockSpec((1, ROW_TILE, d), lat),
        out_shape=jax.ShapeDtypeStruct((b, t, d), F32),
        compiler_params=_params("arbitrary", "arbitrary"),
        name="merge_mlp",
    )(y_rw, g, o_hg, zh, zg, x, modcat, modcat, modcat, modcat, hnw, pa, pb, wo, nw2, w1, w2, fw)


def _to_scan(a):
    *lead, w, t = a.shape
    return jnp.swapaxes(a.reshape(*lead, w // RW_N, RW_N, t), -1, -3)


def _from_scan(a):
    t, n, bh = a.shape
    return jnp.swapaxes(a, 0, 2).reshape(bh * n, t)


def _head_tile(p, b):
    return jnp.tile(p.reshape(RW_HEADS, RW_N).T, (1, b))


def kernel(x, c, ctx, c_ctx, norm1_w, norm2_w, w_mod, b_mod, w_in, rw_mu, rw_w0, rw_w_up, rw_a0, rw_a_up, rw_g_up, rw_k_k, rw_k_a, rw_r_k, rw_ln_w, rw_ln_b, hg_lb, hg_norm_w, p_a, p_b, w_out, w_fc1, w_fc2, final_norm_w):
    b, t, d = x.shape
    n_ctx = ctx.shape[1]
    assert w_mod.shape[0] == 1, "single-layer block"
    assert b * RW_HEADS == LANES and n_ctx == ROW_TILE and t % ROW_TILE == 0
    assert d == RW_HEADS * RW_N == HG_HEADS * HG_N
    n_r = rw_mu.shape[2]
    rank_w, rank_a = rw_w_up.shape[2], rw_a_up.shape[2]
    assert rank_w + rank_a == LANES and n_r == 3 * d + 2 * LANES and w_in.shape[2] == n_r + 7 * d

    c_rows = jnp.zeros((2 * SUBLANES, d), F32).at[:b].set(c).at[b].set(c_ctx)
    mod = _mod(c_rows, w_mod[0], b_mod)
    modcat = jnp.stack([jnp.broadcast_to(mod[b], (b, N_MOD * d)), mod[:b]], axis=1).reshape(2 * b, 1, N_MOD * d)

    zr, zqi, zh, zg = _in_proj(ctx, x, norm1_w, modcat, w_in[0].astype(BF16), n_r)

    wup = jnp.pad(rw_w_up[0], ((0, 0), (0, rank_a), (0, 0))).astype(BF16)
    aup = jnp.pad(rw_a_up[0], ((0, 0), (rank_w, 0), (0, 0))).astype(BF16)
    *scan_ops, g = _rw_prep(zr, rw_mu[0], rw_w0[0], wup, rw_a0[0], aup, rw_g_up[0].astype(BF16), d)
    r_t, k_t, v_t, wf_t, wb_t, af_t, ab_t = (_to_scan(a) for a in scan_ops)
    o_hg = _hg(zqi, zh, hg_lb, n_ctx)
    kap = _head_tile(rw_k_a[0], b)
    tiles = (_head_tile(rw_k_k[0], b), kap, n_ctx // SCAN_TILE)
    y_fwd = _rw_scan(r_t, k_t, v_t, wf_t, af_t, *tiles, run_after=o_hg)
    y_rw_t = _rw_scan(r_t, k_t, v_t, wb_t, ab_t, *tiles,
                      readout=(y_fwd, af_t, _head_tile(rw_r_k[0].reshape(-1), b),
                               _head_tile(rw_ln_w[0], b), _head_tile(rw_ln_b[0], b)))
    y_rw = _from_scan(y_rw_t)

    return _merge(y_rw, g, o_hg, zh, zg, x, modcat, hg_norm_w, p_a[0].astype(BF16), p_b[0].astype(BF16),
                  w_out[0].astype(BF16), norm2_w, w_fc1[0].astype(BF16), w_fc2[0].astype(BF16),
                  final_norm_w.reshape(1, d), n_ctx)
```

```python
import functools

import jax
import jax.numpy as jnp
from jax import lax
from jax.experimental import pallas as pl
from jax.experimental.pallas import tpu as pltpu

F32 = jnp.float32
BF16 = jnp.bfloat16

GRID_W = 64
RW_HEADS = 16
RW_N = 64
HG_HEADS = 8
HG_N = 128
HG_CHUNK = 64
N_MOD = 6
RMS_EPS = 1e-6
RW_GN_EPS = 64e-5
L2_EPS = 1e-12
DECAY_SCALE = 0.6065306597126334

SUBLANES = 8
LANES = 128
ROW_TILE = 256
SCAN_TILE = 64
VMEM_LIMIT = 56 * 1024 * 1024


def _params(*sem):
    return pltpu.CompilerParams(dimension_semantics=sem, vmem_limit_bytes=VMEM_LIMIT)


def _bdot(a, b):
    return jnp.dot(a, b, preferred_element_type=F32)


def _sigmoid(x):
    return 1.0 / (1.0 + jnp.exp(-x))


def _mod_kernel(c_ref, w_ref, b_ref, o_ref):
    c = c_ref[...]
    act = c * _sigmoid(c)
    o_ref[...] = _bdot(act.astype(BF16), w_ref[...].astype(BF16)) + b_ref[...]


def _mod(c_rows, w_mod, b_mod):
    rows, d = c_rows.shape
    n = w_mod.shape[1]
    return pl.pallas_call(
        _mod_kernel,
        grid=(n // d,),
        in_specs=[pl.BlockSpec((rows, d), lambda j: (0, 0)),
                  pl.BlockSpec((d, d), lambda j: (0, j)),
                  pl.BlockSpec((1, d), lambda j: (0, j))],
        out_specs=pl.BlockSpec((rows, d), lambda j: (0, j)),
        out_shape=jax.ShapeDtypeStruct((rows, n), F32),
        compiler_params=_params("arbitrary"),
        name="mod",
    )(c_rows, w_mod, b_mod)


def _modulated_norm(x, nw, sh, sc):
    ms = jnp.mean(x * x, axis=-1, keepdims=True)
    return (x * lax.rsqrt(ms + RMS_EPS) * nw) * (1.0 + sc) + sh


def _in_proj_kernel(ctx_ref, x_ref, nw_ref, sh_ref, sc_ref, w_ref, zr_ref, zqi_ref, zh_ref, zg_ref):
    tokens = jnp.where(pl.program_id(1) == 0, ctx_ref[0], x_ref[0])
    hb = _modulated_norm(tokens, nw_ref[...], sh_ref[0], sc_ref[0]).astype(BF16)
    d = x_ref.shape[2]
    n_r = zr_ref.shape[2]
    col = lambda j0, j1: _bdot(hb, w_ref[:, n_r + j0 * d:n_r + j1 * d])
    zr_ref[0] = _bdot(hb, w_ref[:, :n_r])
    zqi_ref[0, :, :d] = col(0, 1).astype(BF16)
    zqi_ref[0, :, d:] = col(3, 4).astype(BF16)
    zh_ref[0, :, :2 * d] = col(1, 3)
    zh_ref[0, :, 2 * d:] = col(4, 5)
    zg_ref[0] = col(5, 7)


def _in_proj(ctx, x, nw, modcat, w_bf, n_r):
    b, t, d = x.shape
    tt = t + ctx.shape[1]
    nblk = tt // ROW_TILE
    mod_idx = lambda col: (lambda bi, i: (2 * bi + jnp.minimum(i, 1), 0, col))
    outs = ((n_r, F32), (2 * d, BF16), (3 * d, F32), (2 * d, F32))
    return pl.pallas_call(
        _in_proj_kernel,
        grid=(b, nblk),
        in_specs=[pl.BlockSpec((1, ROW_TILE, d), lambda bi, i: (bi, 0, 0)),
                  pl.BlockSpec((1, ROW_TILE, d), lambda bi, i: (bi, jnp.maximum(i - 1, 0), 0)),
                  pl.BlockSpec((1, d), lambda bi, i: (0, 0)),
                  pl.BlockSpec((1, 1, d), mod_idx(0)),
                  pl.BlockSpec((1, 1, d), mod_idx(1)),
                  pl.BlockSpec(w_bf.shape, lambda bi, i: (0, 0), pipeline_mode=pl.Buffered(1))],
        out_specs=[pl.BlockSpec((1, ROW_TILE, n), lambda bi, i: (bi, i, 0)) for n, _ in outs],
        out_shape=[jax.ShapeDtypeStruct((b, tt, n), dt) for n, dt in outs],
        compiler_params=_params("arbitrary", "arbitrary"),
        name="in_proj",
    )(ctx, x, nw, modcat, modcat, w_bf)


def _rw_prep_kernel(z_ref, zp_ref, zn_ref, mu_ref, w0_ref, wup_ref, a0_ref, aup_ref, gup_ref,
                    r_ref, k_ref, v_ref, wf_ref, wb_ref, af_ref, ab_ref, g_ref):
    i = pl.program_id(1)
    nblk = pl.num_programs(1)
    tt = z_ref.shape[1]
    d = r_ref.shape[0]
    is_lat = i > 0
    row = lax.broadcasted_iota(jnp.int32, (tt, 1), 0)
    col = row % GRID_W
    lmask = jnp.where(is_lat, col, row) == 0
    rmask = jnp.where(is_lat, col, row - (tt - GRID_W)) == GRID_W - 1
    latf = is_lat.astype(F32)
    up_ok = (i > 1).astype(F32)
    dn_ok = (i < nblk - 1).astype(F32)

    def shifted(c0, c1):
        z = z_ref[0, :, c0:c1]
        mu = mu_ref[:, c0:c1]
        left = jnp.where(lmask, 0.0, pltpu.roll(z, 1, 0))
        right = jnp.where(rmask, 0.0, pltpu.roll(z, tt - 1, 0))
        up = jnp.concatenate([zp_ref[0, :, c0:c1] * up_ok, z[:tt - GRID_W]], axis=0)
        down = jnp.concatenate([z[GRID_W:], zn_ref[0, :, c0:c1] * dn_ok], axis=0)
        m_up, m_dn = latf * mu[2:3], latf * mu[3:4]
        m_self = 1.0 - mu[0:1] - mu[1:2] - m_up - m_dn
        return z * m_self + left * mu[0:1] + right * mu[1:2] + up * m_up + down * m_dn

    r_ref[...] = shifted(0, d).T
    k_ref[...] = shifted(d, 2 * d).T
    v_ref[...] = shifted(2 * d, 3 * d).T
    rest = shifted(3 * d, z_ref.shape[2])
    xwa = rest[:, :LANES]
    xw_t = jnp.tanh(xwa).astype(BF16)
    xa_b = xwa.astype(BF16)
    for dr, (wd_ref, ad_ref) in enumerate(((wf_ref, af_ref), (wb_ref, ab_ref))):
        wz = w0_ref[dr:dr + 1, :] + _bdot(xw_t, wup_ref[dr])
        wd_ref[...] = jnp.exp(-DECAY_SCALE * _sigmoid(wz)).T
        ad_ref[...] = _sigmoid(a0_ref[dr:dr + 1, :] + _bdot(xa_b, aup_ref[dr])).T
    g_ref[...] = _bdot(_sigmoid(rest[:, LANES:]).astype(BF16), gup_ref[...])


def _rw_prep(zr, mu, w0, wup, a0, aup, gup, d):
    b, tt, nr = zr.shape
    nblk = tt // ROW_TILE
    per = ROW_TILE // GRID_W
    last = tt // GRID_W - 1
    const = lambda shape: pl.BlockSpec(shape, lambda bi, i: (0,) * len(shape))
    row_spec = pl.BlockSpec((d, ROW_TILE), lambda bi, i: (bi, i))
    gate_spec = pl.BlockSpec((ROW_TILE, d), lambda bi, i: (i, bi))
    row_shape = jax.ShapeDtypeStruct((b * d, tt), F32)
    gate_shape = jax.ShapeDtypeStruct((tt, b * d), F32)
    return pl.pallas_call(
        _rw_prep_kernel,
        grid=(b, nblk),
        in_specs=[pl.BlockSpec((1, ROW_TILE, nr), lambda bi, i: (bi, i, 0)),
                  pl.BlockSpec((1, GRID_W, nr), lambda bi, i: (bi, jnp.maximum(i * per - 1, 0), 0)),
                  pl.BlockSpec((1, GRID_W, nr), lambda bi, i: (bi, jnp.minimum(i * per + per, last), 0)),
                  const(mu.shape), const(w0.shape), const(wup.shape), const(a0.shape),
                  const(aup.shape), const(gup.shape)],
        out_specs=[row_spec] * 7 + [gate_spec],
        out_shape=[row_shape] * 7 + [gate_shape],
        compiler_params=_params("arbitrary", "arbitrary"),
        name="rw_prep",
    )(zr, zr, zr, mu, w0, wup, a0, aup, gup)


K_UNROLL = 16


def _seq_block(n_ctx_blocks, n_blocks):
    def blk(d, i):
        bwd = jnp.where(i < n_ctx_blocks, n_ctx_blocks - 1 - i, n_blocks - 1 + n_ctx_blocks - i)
        return jnp.where(d == 0, i, bwd)
    return blk


def _rw_scan_kernel(*refs, backward):
    if backward:
        (r_ref, k_ref, v_ref, w_ref, a_ref, kkp_ref, kap_ref, yf_ref, af_ref, rkp_ref, lnw_ref, lnb_ref,
         y_ref, s_ref, p_s, sa_s, kk0_s, bt_s, kt_s, rd_s, kkd_s) = refs
    else:
        (r_ref, k_ref, v_ref, w_ref, a_ref, kkp_ref, kap_ref, _order_ref,
         y_ref, s_ref, p_s, sa_s, kk0_s, bt_s, kt_s, rd_s, kkd_s) = refs
    i = pl.program_id(0)
    tb = r_ref.shape[0]
    nv = RW_N // SUBLANES

    @pl.when(i == 0)
    def _():
        s_ref[...] = jnp.zeros_like(s_ref)
        p_s[...] = jnp.ones_like(p_s)

    def bcast(ref, *idx):
        k = idx[-1]
        row = ref[(*idx[:-1], pl.ds(k, 1), slice(None))]
        return jnp.broadcast_to(row, (SUBLANES, LANES))

    def time_index(s):
        s = jnp.minimum(s, tb - 1)
        return tb - 1 - s if backward else s

    def bf16_pair(x):
        hi = lax.bitcast_convert_type(x.astype(BF16).astype(F32), jnp.uint32)
        return lax.bitcast_convert_type(hi | (hi >> 16), F32)

    def norm_key(t):
        kkr = k_ref[t] * kkp_ref[...]
        nrm = jnp.sqrt(jnp.sum(kkr * kkr, axis=0, keepdims=True))
        return kkr / jnp.maximum(nrm, L2_EPS)

    def scaled_key(t, a):
        return k_ref[t] * (1.0 + (a - 1.0) * kap_ref[...])

    def prepare(s, slot):
        t = time_index(s)
        a = a_ref[t]
        kk = norm_key(t)
        p_prev = p_s[...]
        kkd_s[1 - slot] = bf16_pair(p_prev * kk)
        p = p_prev * jnp.where(s < tb, w_ref[t], 1.0)
        p_s[...] = p
        inv_p = 1.0 / p
        bt_s[slot] = bf16_pair(kk * a * inv_p)
        kt_s[slot] = bf16_pair(scaled_key(t, a) * inv_p)
        rd_s[slot] = bf16_pair(p * r_ref[t])

    def restart():
        kk0_s[...] = norm_key(time_index(0))
        acc = [None] * nv
        for k in range(RW_N):
            pb = bcast(p_s, k)
            kkb = bcast(kk0_s, k)
            for j in range(nv):
                rows = slice(SUBLANES * j, SUBLANES * (j + 1))
                sn = s_ref[k, rows, :] * pb
                s_ref[k, rows, :] = sn
                acc[j] = sn * kkb if acc[j] is None else acc[j] + sn * kkb
        sa_s[...] = -jnp.concatenate(acc, axis=0)
        p_s[...] = jnp.ones_like(p_s)

    def sweep(s, slot):
        t = time_index(s)
        zero = jnp.zeros((SUBLANES, LANES), F32)

        sa_p = [sa_s[2 * SUBLANES * m:2 * SUBLANES * (m + 1), :].astype(BF16) for m in range(nv // 2)]
        v_p = [v_ref[t, 2 * SUBLANES * m:2 * SUBLANES * (m + 1), :].astype(BF16) for m in range(nv // 2)]

        def key_block(kblk, carry):
            yacc, acc = list(carry[0]), list(carry[1])
            yb = [jnp.zeros((2 * SUBLANES, LANES), BF16)] * (nv // 2)
            ab = [jnp.zeros((2 * SUBLANES, LANES), BF16)] * (nv // 2)
            for kk in range(K_UNROLL):
                k = kblk * K_UNROLL + kk
                bb = pltpu.bitcast(bcast(bt_s, slot, k), BF16)
                kb = pltpu.bitcast(bcast(kt_s, slot, k), BF16)
                rb = pltpu.bitcast(bcast(rd_s, slot, k), BF16)
                kkn = pltpu.bitcast(bcast(kkd_s, slot, k), BF16)
                for m in range(nv // 2):
                    rows = slice(2 * SUBLANES * m, 2 * SUBLANES * (m + 1))
                    sn = s_ref[k, rows, :] + (sa_p[m] * bb + v_p[m] * kb).astype(F32)
                    s_ref[k, rows, :] = sn
                    snp = sn.astype(BF16)
                    yb[m] = yb[m] + snp * rb
                    ab[m] = ab[m] + snp * kkn
            for m in range(nv // 2):
                y32 = yb[m].astype(F32)
                a32 = ab[m].astype(F32)
                for h, j in enumerate((2 * m, 2 * m + 1)):
                    yacc[j] = yacc[j] + y32[SUBLANES * h:SUBLANES * (h + 1)]
                    acc[j] = acc[j] + a32[SUBLANES * h:SUBLANES * (h + 1)]
            return yacc, acc

        init = [zero] * nv
        n_kblk = RW_N // K_UNROLL
        carry = lax.fori_loop(0, n_kblk - 1, key_block, (init, init))
        yacc, acc = key_block(n_kblk - 1, carry)
        sa_s[...] = -jnp.concatenate(acc, axis=0)
        y = jnp.concatenate(yacc, axis=0)
        if backward:
            y = y + yf_ref[t]
            yc = y - jnp.mean(y, axis=0, keepdims=True)
            var = jnp.mean(yc * yc, axis=0, keepdims=True)
            y = yc * lax.rsqrt(var + RW_GN_EPS) * lnw_ref[...] + lnb_ref[...]
            k_sum = k_ref[t] * (2.0 + (af_ref[t] + a_ref[t] - 2.0) * kap_ref[...])
            y = y + jnp.sum(r_ref[t] * k_sum * rkp_ref[...], axis=0, keepdims=True) * v_ref[t]
        y_ref[t] = y

    restart()
    prepare(0, 0)
    prepare(1, 1)

    def pair(p, carry):
        s = 2 * p
        sweep(s, 0)
        prepare(s + 2, 0)
        sweep(s + 1, 1)
        prepare(s + 3, 1)
        return carry

    lax.fori_loop(0, tb // 2, pair, 0)


def _rw_scan(r_t, k_t, v_t, w_t, a_t, kkp, kap, n_ctx_blocks, *, run_after=None, readout=None):
    backward = readout is not None
    direction = int(backward)
    tt = r_t.shape[0]
    nblk = tt // SCAN_TILE
    seq = _seq_block(n_ctx_blocks, nblk)
    blk = lambda i: seq(direction, i)
    tile = (SCAN_TILE, RW_N, LANES)
    shared = pl.BlockSpec(tile, lambda i: (blk(i), 0, 0))
    const = pl.BlockSpec((RW_N, LANES), lambda i: (0, 0))
    y_blk = lambda i: jnp.where(i < n_ctx_blocks, blk(n_ctx_blocks), blk(i)) - n_ctx_blocks
    y_spec = pl.BlockSpec(tile, lambda i: (y_blk(i), 0, 0))
    in_specs = [shared, shared, shared, shared, shared, const, const]
    args = [r_t, k_t, v_t, w_t, a_t, kkp, kap]
    if backward:
        y_fwd, a_fwd, rkp, lnw, lnb = readout
        in_specs += [y_spec, shared, const, const, const]
        args += [y_fwd, a_fwd, rkp, lnw, lnb]
    else:
        in_specs += [pl.BlockSpec(memory_space=pl.ANY)]
        args += [run_after]
    return pl.pallas_call(
        functools.partial(_rw_scan_kernel, backward=backward),
        grid=(nblk,),
        in_specs=in_specs,
        out_specs=y_spec,
        out_shape=jax.ShapeDtypeStruct((tt - n_ctx_blocks * SCAN_TILE, RW_N, LANES), F32),
        scratch_shapes=[pltpu.VMEM((RW_N, RW_N, LANES), F32)] + [pltpu.VMEM((RW_N, LANES), F32)] * 3
                       + [pltpu.VMEM((2, RW_N, LANES), F32)] * 4,
        compiler_params=_params("arbitrary"),
        name="rw_scan_bwd" if backward else "rw_scan_fwd",
    )(*args)


def _hg_kernel(q_ref, f_ref, i_ref, lbp_ref, tri_ref, o_ref, st_ref, qd_scr, kv_scr, dec_scr):
    d = pl.program_id(1)
    i = pl.program_id(2)
    n_chunks = q_ref.shape[1] // HG_CHUNK

    @pl.when(i == 0)
    def _():
        st_ref[...] = jnp.zeros_like(st_ref)

    lbp = lbp_ref[...]
    e = jnp.exp(lbp - jnp.max(lbp, axis=0, keepdims=True))
    lb = e[0:1] / jnp.sum(e, axis=0, keepdims=True)
    t_idx = lax.broadcasted_iota(jnp.int32, (HG_CHUNK, HG_CHUNK), 0)
    s_idx = lax.broadcasted_iota(jnp.int32, (HG_CHUNK, HG_CHUNK), 1)
    nt = (((1,), (1,)), ((), ()))
    tn = (((0,), (0,)), ((), ()))

    def run(fwd):
        mask = (s_idx <= t_idx) if fwd else (s_idx >= t_idx)
        mid_row = HG_CHUNK // 2 - 1 if fwd else HG_CHUNK // 2
        last_row = HG_CHUNK - 1 if fwd else 0
        order = list(range(n_chunks)) if fwd else list(reversed(range(n_chunks)))

        f = lb + (1.0 - lb) * _sigmoid(f_ref[0])
        lf = jnp.log(f)
        hi = lf.astype(BF16)
        lo = (lf - hi.astype(F32)).astype(BF16)
        tri = tri_ref[0]
        b = _bdot(tri, hi) + _bdot(tri, lo)
        for c in order:
            rows = slice(c * HG_CHUNK, (c + 1) * HG_CHUNK)
            bc = b[rows]
            b_mid = bc[mid_row:mid_row + 1]
            b_last = bc[last_row:last_row + 1]
            q_in = q_ref[0, rows, :] * jnp.exp(bc - b_mid)
            k_in = (1.0 - f[rows]) * jnp.exp(b_mid - bc)
            qd_scr[rows, :] = (q_in * jnp.exp(b_mid)).astype(BF16)
            k_dec = (k_in * jnp.exp(b_last - b_mid)).astype(BF16)
            dec_scr[c:c + 1, :] = jnp.exp(b_last)
            q_in = q_in.astype(BF16)
            k_in = k_in.astype(BF16)
            vb = i_ref[0, rows, :].astype(BF16)
            heads = [slice(h * HG_N, (h + 1) * HG_N) for h in range(HG_HEADS)]
            scores = [lax.dot_general(q_in[:, cols], k_in[:, cols], nt, preferred_element_type=F32)
                      for cols in heads]
            for h, cols in enumerate(heads):
                kv_scr[c, h] = lax.dot_general(vb[:, cols], k_dec[:, cols], tn, preferred_element_type=F32)
            for h, cols in enumerate(heads):
                o_ref[0, 0, rows, cols] = _bdot(jnp.where(mask, scores[h], 0.0).astype(BF16), vb[:, cols])
        for c in order:
            rows = slice(c * HG_CHUNK, (c + 1) * HG_CHUNK)
            for h in range(HG_HEADS):
                cols = slice(h * HG_N, (h + 1) * HG_N)
                st = st_ref[h]
                o_ref[0, 0, rows, cols] += lax.dot_general(qd_scr[rows, cols], st.astype(BF16), nt,
                                                           preferred_element_type=F32)
                st_ref[h] = dec_scr[c:c + 1, cols] * st + kv_scr[c, h]

    @pl.when(d == 0)
    def _():
        run(True)

    @pl.when(d == 1)
    def _():
        run(False)


def _hg_tri(n_rows):
    t = jnp.arange(n_rows)[:, None]
    s = jnp.arange(n_rows)[None, :]
    same = (t // HG_CHUNK) == (s // HG_CHUNK)
    return jnp.stack([same & (s <= t), same & (s >= t)]).astype(BF16)


def _hg(zqi, zh, hg_lb, n_ctx):
    b, tt, _ = zh.shape
    w = HG_HEADS * HG_N
    nblk = tt // ROW_TILE
    n_chunks = ROW_TILE // HG_CHUNK
    blk = _seq_block(n_ctx // ROW_TILE, nblk)
    return pl.pallas_call(
        _hg_kernel,
        grid=(b, 2, nblk),
        in_specs=[pl.BlockSpec((1, ROW_TILE, w), lambda bi, d, i: (bi, blk(d, i), 0)),
                  pl.BlockSpec((1, ROW_TILE, w), lambda bi, d, i: (bi, blk(d, i), d)),
                  pl.BlockSpec((1, ROW_TILE, w), lambda bi, d, i: (bi, blk(d, i), 1)),
                  pl.BlockSpec(hg_lb.shape, lambda bi, d, i: (0, 0)),
                  pl.BlockSpec((1, ROW_TILE, ROW_TILE), lambda bi, d, i: (d, 0, 0))],
        out_specs=pl.BlockSpec((1, 1, ROW_TILE, w), lambda bi, d, i: (d, bi, blk(d, i), 0)),
        out_shape=jax.ShapeDtypeStruct((2, b, tt, w), F32),
        scratch_shapes=[pltpu.VMEM((HG_HEADS, HG_N, HG_N), F32),
                        pltpu.VMEM((ROW_TILE, w), BF16),
                        pltpu.VMEM((n_chunks, HG_HEADS, HG_N, HG_N), F32),
                        pltpu.VMEM((SUBLANES, w), F32)],
        compiler_params=_params("arbitrary", "arbitrary", "arbitrary"),
        name="hg",
    )(zqi, zh, zqi, hg_lb, _hg_tri(ROW_TILE))


def _merge_kernel(yrw_ref, g_ref, o_ref, gz_ref, zg_ref, x_ref, g1_ref, sh2_ref, sc2_ref, g2_ref, hnw_ref,
                  pa_ref, pb_ref, wo_ref, nw2_ref, w1_ref, w2_ref, fw_ref, out_ref):
    d = x_ref.shape[2]
    y_rw = (yrw_ref[...].T * g_ref[...]).astype(BF16)
    o = o_ref[0, 0] + o_ref[1, 0]
    gz = gz_ref[0]
    hnw = hnw_ref[...]
    parts = []
    for h in range(HG_HEADS):
        cols = slice(h * HG_N, (h + 1) * HG_N)
        oh = o[:, cols]
        ms = jnp.mean(oh * oh, axis=-1, keepdims=True)
        parts.append(oh * lax.rsqrt(ms + RMS_EPS) * hnw[:, cols])
    y_hg = (jnp.concatenate(parts, axis=1) * (gz * _sigmoid(gz))).astype(BF16)
    zg = zg_ref[0]
    m = _sigmoid(zg[:, :d]) * _bdot(y_rw, pa_ref[...]) + _sigmoid(zg[:, d:]) * _bdot(y_hg, pb_ref[...])
    x1 = x_ref[0] + g1_ref[0] * _bdot(m.astype(BF16), wo_ref[...])
    hb = _modulated_norm(x1, nw2_ref[...], sh2_ref[0], sc2_ref[0]).astype(BF16)
    u = jnp.maximum(_bdot(hb, w1_ref[...]), 0.0)
    y = x1 + g2_ref[0] * _bdot((u * u).astype(BF16), w2_ref[...])
    ms = jnp.mean(y * y, axis=-1, keepdims=True)
    out_ref[0] = y * lax.rsqrt(ms + RMS_EPS) * fw_ref[...]


def _merge(y_rw, g, o_hg, zh, zg, x, modcat, hnw, pa, pb, wo, nw2, w1, w2, fw, n_ctx):
    b, t, d = x.shape
    off = n_ctx // ROW_TILE
    lat = lambda bi, i: (bi, i, 0)
    cat = lambda bi, i: (bi, i + off, 0)
    mod = lambda col: pl.BlockSpec((1, 1, d), lambda bi, i: (2 * bi + 1, 0, col))
    const = lambda a: pl.BlockSpec(a.shape, lambda bi, i: (0,) * a.ndim, pipeline_mode=pl.Buffered(1))
    return pl.pallas_call(
        _merge_kernel,
        grid=(b, t // ROW_TILE),
        in_specs=[pl.BlockSpec((d, ROW_TILE), lambda bi, i: (bi, i)),
                  pl.BlockSpec((ROW_TILE, d), lambda bi, i: (i + off, bi)),
                  pl.BlockSpec((2, 1, ROW_TILE, d), lambda bi, i: (0, bi, i + off, 0)),
                  pl.BlockSpec((1, ROW_TILE, d), lambda bi, i: (bi, i + off, 2)),
                  pl.BlockSpec((1, ROW_TILE, 2 * d), cat),
                  pl.BlockSpec((1, ROW_TILE, d), lat),
                  mod(2), mod(3), mod(4), mod(5),
                  const(hnw), const(pa), const(pb), const(wo), const(nw2), const(w1), const(w2), const(fw)],
        out_specs=pl.BlockSpec((1, ROW_TILE, d), lat),
        out_shape=jax.ShapeDtypeStruct((b, t, d), F32),
        compiler_params=_params("arbitrary", "arbitrary"),
        name="merge_mlp",
    )(y_rw, g, o_hg, zh, zg, x, modcat, modcat, modcat, modcat, hnw, pa, pb, wo, nw2, w1, w2, fw)


def _to_scan(a):
    *lead, w, t = a.shape
    return jnp.swapaxes(a.reshape(*lead, w // RW_N, RW_N, t), -1, -3)


def _from_scan(a):
    t, n, bh = a.shape
    return jnp.swapaxes(a, 0, 2).reshape(bh * n, t)


def _head_tile(p, b):
    return jnp.tile(p.reshape(RW_HEADS, RW_N).T, (1, b))


def kernel(x, c, ctx, c_ctx, norm1_w, norm2_w, w_mod, b_mod, w_in, rw_mu, rw_w0, rw_w_up, rw_a0, rw_a_up, rw_g_up, rw_k_k, rw_k_a, rw_r_k, rw_ln_w, rw_ln_b, hg_lb, hg_norm_w, p_a, p_b, w_out, w_fc1, w_fc2, final_norm_w):
    b, t, d = x.shape
    n_ctx = ctx.shape[1]
    assert w_mod.shape[0] == 1, "single-layer block"
    assert b * RW_HEADS == LANES and n_ctx == ROW_TILE and t % ROW_TILE == 0
    assert d == RW_HEADS * RW_N == HG_HEADS * HG_N
    n_r = rw_mu.shape[2]
    rank_w, rank_a = rw_w_up.shape[2], rw_a_up.shape[2]
    assert rank_w + rank_a == LANES and n_r == 3 * d + 2 * LANES and w_in.shape[2] == n_r + 7 * d

    c_rows = jnp.zeros((2 * SUBLANES, d), F32).at[:b].set(c).at[b].set(c_ctx)
    mod = _mod(c_rows, w_mod[0], b_mod)
    modcat = jnp.stack([jnp.broadcast_to(mod[b], (b, N_MOD * d)), mod[:b]], axis=1).reshape(2 * b, 1, N_MOD * d)

    zr, zqi, zh, zg = _in_proj(ctx, x, norm1_w, modcat, w_in[0].astype(BF16), n_r)

    wup = jnp.pad(rw_w_up[0], ((0, 0), (0, rank_a), (0, 0))).astype(BF16)
    aup = jnp.pad(rw_a_up[0], ((0, 0), (rank_w, 0), (0, 0))).astype(BF16)
    *scan_ops, g = _rw_prep(zr, rw_mu[0], rw_w0[0], wup, rw_a0[0], aup, rw_g_up[0].astype(BF16), d)
    r_t, k_t, v_t, wf_t, wb_t, af_t, ab_t = (_to_scan(a) for a in scan_ops)
    o_hg = _hg(zqi, zh, hg_lb, n_ctx)
    kap = _head_tile(rw_k_a[0], b)
    tiles = (_head_tile(rw_k_k[0], b), kap, n_ctx // SCAN_TILE)
    y_fwd = _rw_scan(r_t, k_t, v_t, wf_t, af_t, *tiles, run_after=o_hg)
    y_rw_t = _rw_scan(r_t, k_t, v_t, wb_t, ab_t, *tiles,
                      readout=(y_fwd, af_t, _head_tile(rw_r_k[0].reshape(-1), b),
                               _head_tile(rw_ln_w[0], b), _head_tile(rw_ln_b[0], b)))
    y_rw = _from_scan(y_rw_t)

    return _merge(y_rw, g, o_hg, zh, zg, x, modcat, hg_norm_w, p_a[0].astype(BF16), p_b[0].astype(BF16),
                  w_out[0].astype(BF16), norm2_w, w_fc1[0].astype(BF16), w_fc2[0].astype(BF16),
                  final_norm_w.reshape(1, d), n_ctx)
```

```python
import functools

import jax
import jax.numpy as jnp
from jax import lax
from jax.experimental import pallas as pl
from jax.experimental.pallas import tpu as pltpu

F32 = jnp.float32
BF16 = jnp.bfloat16

GRID_W = 64
RW_HEADS = 16
RW_N = 64
HG_HEADS = 8
HG_N = 128
HG_CHUNK = 64
N_MOD = 6
RMS_EPS = 1e-6
RW_GN_EPS = 64e-5
L2_EPS = 1e-12
DECAY_SCALE = 0.6065306597126334

SUBLANES = 8
LANES = 128
ROW_TILE = 256
SCAN_TILE = 64
VMEM_LIMIT = 56 * 1024 * 1024


def _params(*sem):
    return pltpu.CompilerParams(dimension_semantics=sem, vmem_limit_bytes=VMEM_LIMIT)


def _bdot(a, b):
    return jnp.dot(a, b, preferred_element_type=F32)


def _sigmoid(x):
    return 1.0 / (1.0 + jnp.exp(-x))


def _mod_kernel(c_ref, w_ref, b_ref, o_ref):
    c = c_ref[...]
    act = c * _sigmoid(c)
    o_ref[...] = _bdot(act.astype(BF16), w_ref[...].astype(BF16)) + b_ref[...]


def _mod(c_rows, w_mod, b_mod):
    rows, d = c_rows.shape
    n = w_mod.shape[1]
    return pl.pallas_call(
        _mod_kernel,
        grid=(n // d,),
        in_specs=[pl.BlockSpec((rows, d), lambda j: (0, 0)),
                  pl.BlockSpec((d, d), lambda j: (0, j)),
                  pl.BlockSpec((1, d), lambda j: (0, j))],
        out_specs=pl.BlockSpec((rows, d), lambda j: (0, j)),
        out_shape=jax.ShapeDtypeStruct((rows, n), F32),
        compiler_params=_params("arbitrary"),
        name="mod",
    )(c_rows, w_mod, b_mod)


def _modulated_norm(x, nw, sh, sc):
    ms = jnp.mean(x * x, axis=-1, keepdims=True)
    return (x * lax.rsqrt(ms + RMS_EPS) * nw) * (1.0 + sc) + sh


def _in_proj_kernel(ctx_ref, x_ref, nw_ref, sh_ref, sc_ref, w_ref, zr_ref, zqi_ref, zh_ref, zg_ref):
    tokens = jnp.where(pl.program_id(1) == 0, ctx_ref[0], x_ref[0])
    hb = _modulated_norm(tokens, nw_ref[...], sh_ref[0], sc_ref[0]).astype(BF16)
    d = x_ref.shape[2]
    n_r = zr_ref.shape[2]
    col = lambda j0, j1: _bdot(hb, w_ref[:, n_r + j0 * d:n_r + j1 * d])
    zr_ref[0] = _bdot(hb, w_ref[:, :n_r])
    zqi_ref[0, :, :d] = col(0, 1).astype(BF16)
    zqi_ref[0, :, d:] = col(3, 4).astype(BF16)
    zh_ref[0, :, :2 * d] = col(1, 3)
    zh_ref[0, :, 2 * d:] = col(4, 5)
    zg_ref[0] = col(5, 7)


def _in_proj(ctx, x, nw, modcat, w_bf, n_r):
    b, t, d = x.shape
    tt = t + ctx.shape[1]
    nblk = tt // ROW_TILE
    mod_idx = lambda col: (lambda bi, i: (2 * bi + jnp.minimum(i, 1), 0, col))
    outs = ((n_r, F32), (2 * d, BF16), (3 * d, F32), (2 * d, F32))
    return pl.pallas_call(
        _in_proj_kernel,
        grid=(b, nblk),
        in_specs=[pl.BlockSpec((1, ROW_TILE, d), lambda bi, i: (bi, 0, 0)),
                  pl.BlockSpec((1, ROW_TILE, d), lambda bi, i: (bi, jnp.maximum(i - 1, 0), 0)),
                  pl.BlockSpec((1, d), lambda bi, i: (0, 0)),
                  pl.BlockSpec((1, 1, d), mod_idx(0)),
                  pl.BlockSpec((1, 1, d), mod_idx(1)),
                  pl.BlockSpec(w_bf.shape, lambda bi, i: (0, 0), pipeline_mode=pl.Buffered(1))],
        out_specs=[pl.BlockSpec((1, ROW_TILE, n), lambda bi, i: (bi, i, 0)) for n, _ in outs],
        out_shape=[jax.ShapeDtypeStruct((b, tt, n), dt) for n, dt in outs],
        compiler_params=_params("arbitrary", "arbitrary"),
        name="in_proj",
    )(ctx, x, nw, modcat, modcat, w_bf)


def _rw_prep_kernel(z_ref, zp_ref, zn_ref, mu_ref, w0_ref, wup_ref, a0_ref, aup_ref, gup_ref,
                    r_ref, k_ref, v_ref, wf_ref, wb_ref, af_ref, ab_ref, g_ref):
    i = pl.program_id(1)
    nblk = pl.num_programs(1)
    tt = z_ref.shape[1]
    d = r_ref.shape[0]
    is_lat = i > 0
    row = lax.broadcasted_iota(jnp.int32, (tt, 1), 0)
    col = row % GRID_W
    lmask = jnp.where(is_lat, col, row) == 0
    rmask = jnp.where(is_lat, col, row - (tt - GRID_W)) == GRID_W - 1
    latf = is_lat.astype(F32)
    up_ok = (i > 1).astype(F32)
    dn_ok = (i < nblk - 1).astype(F32)

    def shifted(c0, c1):
        z = z_ref[0, :, c0:c1]
        mu = mu_ref[:, c0:c1]
        left = jnp.where(lmask, 0.0, pltpu.roll(z, 1, 0))
        right = jnp.where(rmask, 0.0, pltpu.roll(z, tt - 1, 0))
        up = jnp.concatenate([zp_ref[0, :, c0:c1] * up_ok, z[:tt - GRID_W]], axis=0)
        down = jnp.concatenate([z[GRID_W:], zn_ref[0, :, c0:c1] * dn_ok], axis=0)
        m_up, m_dn = latf * mu[2:3], latf * mu[3:4]
        m_self = 1.0 - mu[0:1] - mu[1:2] - m_up - m_dn
        return z * m_self + left * mu[0:1] + right * mu[1:2] + up * m_up + down * m_dn

    r_ref[...] = shifted(0, d).T
    k_ref[...] = shifted(d, 2 * d).T
    v_ref[...] = shifted(2 * d, 3 * d).T
    rest = shifted(3 * d, z_ref.shape[2])
    xwa = rest[:, :LANES]
    xw_t = jnp.tanh(xwa).astype(BF16)
    xa_b = xwa.astype(BF16)
    for dr, (wd_ref, ad_ref) in enumerate(((wf_ref, af_ref), (wb_ref, ab_ref))):
        wz = w0_ref[dr:dr + 1, :] + _bdot(xw_t, wup_ref[dr])
        wd_ref[...] = jnp.exp(-DECAY_SCALE * _sigmoid(wz)).T
        ad_ref[...] = _sigmoid(a0_ref[dr:dr + 1, :] + _bdot(xa_b, aup_ref[dr])).T
    g_ref[...] = _bdot(_sigmoid(rest[:, LANES:]).astype(BF16), gup_ref[...])


def _rw_prep(zr, mu, w0, wup, a0, aup, gup, d):
    b, tt, nr = zr.shape
    nblk = tt // ROW_TILE
    per = ROW_TILE // GRID_W
    last = tt // GRID_W - 1
    const = lambda shape: pl.BlockSpec(shape, lambda bi, i: (0,) * len(shape))
    row_spec = pl.BlockSpec((d, ROW_TILE), lambda bi, i: (bi, i))
    gate_spec = pl.BlockSpec((ROW_TILE, d), lambda bi, i: (i, bi))
    row_shape = jax.ShapeDtypeStruct((b * d, tt), F32)
    gate_shape = jax.ShapeDtypeStruct((tt, b * d), F32)
    return pl.pallas_call(
        _rw_prep_kernel,
        grid=(b, nblk),
        in_specs=[pl.BlockSpec((1, ROW_TILE, nr), lambda bi, i: (bi, i, 0)),
                  pl.BlockSpec((1, GRID_W, nr), lambda bi, i: (bi, jnp.maximum(i * per - 1, 0), 0)),
                  pl.BlockSpec((1, GRID_W, nr), lambda bi, i: (bi, jnp.minimum(i * per + per, last), 0)),
                  const(mu.shape), const(w0.shape), const(wup.shape), const(a0.shape),
                  const(aup.shape), const(gup.shape)],
        out_specs=[row_spec] * 7 + [gate_spec],
        out_shape=[row_shape] * 7 + [gate_shape],
        compiler_params=_params("arbitrary", "arbitrary"),
        name="rw_prep",
    )(zr, zr, zr, mu, w0, wup, a0, aup, gup)


K_UNROLL = 32


def _seq_block(n_ctx_blocks, n_blocks):
    def blk(d, i):
        bwd = jnp.where(i < n_ctx_blocks, n_ctx_blocks - 1 - i, n_blocks - 1 + n_ctx_blocks - i)
        return jnp.where(d == 0, i, bwd)
    return blk


def _rw_scan_kernel(*refs, backward):
    if backward:
        (r_ref, k_ref, v_ref, w_ref, a_ref, kkp_ref, kap_ref, yf_ref, af_ref, rkp_ref, lnw_ref, lnb_ref,
         y_ref, s_ref, p_s, sa_s, kk0_s, bt_s, kt_s, rd_s, kkd_s) = refs
    else:
        (r_ref, k_ref, v_ref, w_ref, a_ref, kkp_ref, kap_ref, _order_ref,
         y_ref, s_ref, p_s, sa_s, kk0_s, bt_s, kt_s, rd_s, kkd_s) = refs
    i = pl.program_id(0)
    tb = r_ref.shape[0]
    nv = RW_N // SUBLANES

    @pl.when(i == 0)
    def _():
        s_ref[...] = jnp.zeros_like(s_ref)
        p_s[...] = jnp.ones_like(p_s)

    def bcast(ref, *idx):
        k = idx[-1]
        row = ref[(*idx[:-1], pl.ds(k, 1), slice(None))]
        return jnp.broadcast_to(row, (SUBLANES, LANES))

    def time_index(s):
        s = jnp.minimum(s, tb - 1)
        return tb - 1 - s if backward else s

    def bf16_pair(x):
        hi = lax.bitcast_convert_type(x.astype(BF16).astype(F32), jnp.uint32)
        return lax.bitcast_convert_type(hi | (hi >> 16), F32)

    def norm_key(t):
        kkr = k_ref[t] * kkp_ref[...]
        nrm = jnp.sqrt(jnp.sum(kkr * kkr, axis=0, keepdims=True))
        return kkr / jnp.maximum(nrm, L2_EPS)

    def scaled_key(t, a):
        return k_ref[t] * (1.0 + (a - 1.0) * kap_ref[...])

    def prepare(s, slot):
        t = time_index(s)
        a = a_ref[t]
        kk = norm_key(t)
        p_prev = p_s[...]
        kkd_s[1 - slot] = bf16_pair(p_prev * kk)
        p = p_prev * jnp.where(s < tb, w_ref[t], 1.0)
        p_s[...] = p
        inv_p = 1.0 / p
        bt_s[slot] = bf16_pair(kk * a * inv_p)
        kt_s[slot] = bf16_pair(scaled_key(t, a) * inv_p)
        rd_s[slot] = bf16_pair(p * r_ref[t])

    def restart():
        kk0_s[...] = norm_key(time_index(0))
        acc = [None] * nv
        for k in range(RW_N):
            pb = bcast(p_s, k)
            kkb = bcast(kk0_s, k)
            for j in range(nv):
                rows = slice(SUBLANES * j, SUBLANES * (j + 1))
                sn = s_ref[k, rows, :] * pb
                s_ref[k, rows, :] = sn
                acc[j] = sn * kkb if acc[j] is None else acc[j] + sn * kkb
        sa_s[...] = -jnp.concatenate(acc, axis=0)
        p_s[...] = jnp.ones_like(p_s)

    def sweep(s, slot):
        t = time_index(s)
        zero = jnp.zeros((SUBLANES, LANES), F32)

        sa_p = [sa_s[2 * SUBLANES * m:2 * SUBLANES * (m + 1), :].astype(BF16) for m in range(nv // 2)]
        v_p = [v_ref[t, 2 * SUBLANES * m:2 * SUBLANES * (m + 1), :].astype(BF16) for m in range(nv // 2)]

        def key_block(kblk, carry):
            yacc, acc = list(carry[0]), list(carry[1])
            yb = [jnp.zeros((2 * SUBLANES, LANES), BF16)] * (nv // 2)
            ab = [jnp.zeros((2 * SUBLANES, LANES), BF16)] * (nv // 2)
            for kk in range(K_UNROLL):
                k = kblk * K_UNROLL + kk
                bb = pltpu.bitcast(bcast(bt_s, slot, k), BF16)
                kb = pltpu.bitcast(bcast(kt_s, slot, k), BF16)
                rb = pltpu.bitcast(bcast(rd_s, slot, k), BF16)
                kkn = pltpu.bitcast(bcast(kkd_s, slot, k), BF16)
                for m in range(nv // 2):
                    rows = slice(2 * SUBLANES * m, 2 * SUBLANES * (m + 1))
                    sn = s_ref[k, rows, :] + (sa_p[m] * bb + v_p[m] * kb).astype(F32)
                    s_ref[k, rows, :] = sn
                    snp = sn.astype(BF16)
                    yb[m] = yb[m] + snp * rb
                    ab[m] = ab[m] + snp * kkn
            for m in range(nv // 2):
                y32 = yb[m].astype(F32)
                a32 = ab[m].astype(F32)
                for h, j in enumerate((2 * m, 2 * m + 1)):
                    yacc[j] = yacc[j] + y32[SUBLANES * h:SUBLANES * (h + 1)]
                    acc[j] = acc[j] + a32[SUBLANES * h:SUBLANES * (h + 1)]
            return yacc, acc

        init = [zero] * nv
        n_kblk = RW_N // K_UNROLL
        carry = lax.fori_loop(0, n_kblk - 1, key_block, (init, init))
        yacc, acc = key_block(n_kblk - 1, carry)
        sa_s[...] = -jnp.concatenate(acc, axis=0)
        y = jnp.concatenate(yacc, axis=0)
        if backward:
            y = y + yf_ref[t]
            yc = y - jnp.mean(y, axis=0, keepdims=True)
            var = jnp.mean(yc * yc, axis=0, keepdims=True)
            y = yc * lax.rsqrt(var + RW_GN_EPS) * lnw_ref[...] + lnb_ref[...]
            k_sum = k_ref[t] * (2.0 + (af_ref[t] + a_ref[t] - 2.0) * kap_ref[...])
            y = y + jnp.sum(r_ref[t] * k_sum * rkp_ref[...], axis=0, keepdims=True) * v_ref[t]
        y_ref[t] = y

    restart()
    prepare(0, 0)
    prepare(1, 1)

    def pair(p, carry):
        s = 2 * p
        sweep(s, 0)
        prepare(s + 2, 0)
        sweep(s + 1, 1)
        prepare(s + 3, 1)
        return carry

    lax.fori_loop(0, tb // 2, pair, 0)


def _rw_scan(r_t, k_t, v_t, w_t, a_t, kkp, kap, n_ctx_blocks, *, run_after=None, readout=None):
    backward = readout is not None
    direction = int(backward)
    tt = r_t.shape[0]
    nblk = tt // SCAN_TILE
    seq = _seq_block(n_ctx_blocks, nblk)
    blk = lambda i: seq(direction, i)
    tile = (SCAN_TILE, RW_N, LANES)
    shared = pl.BlockSpec(tile, lambda i: (blk(i), 0, 0))
    const = pl.BlockSpec((RW_N, LANES), lambda i: (0, 0))
    y_blk = lambda i: jnp.where(i < n_ctx_blocks, blk(n_ctx_blocks), blk(i)) - n_ctx_blocks
    y_spec = pl.BlockSpec(tile, lambda i: (y_blk(i), 0, 0))
    in_specs = [shared, shared, shared, shared, shared, const, const]
    args = [r_t, k_t, v_t, w_t, a_t, kkp, kap]
    if backward:
        y_fwd, a_fwd, rkp, lnw, lnb = readout
        in_specs += [y_spec, shared, const, const, const]
        args += [y_fwd, a_fwd, rkp, lnw, lnb]
    else:
        in_specs += [pl.BlockSpec(memory_space=pl.ANY)]
        args += [run_after]
    return pl.pallas_call(
        functools.partial(_rw_scan_kernel, backward=backward),
        grid=(nblk,),
        in_specs=in_specs,
        out_specs=y_spec,
        out_shape=jax.ShapeDtypeStruct((tt - n_ctx_blocks * SCAN_TILE, RW_N, LANES), F32),
        scratch_shapes=[pltpu.VMEM((RW_N, RW_N, LANES), F32)] + [pltpu.VMEM((RW_N, LANES), F32)] * 3
                       + [pltpu.VMEM((2, RW_N, LANES), F32)] * 4,
        compiler_params=_params("arbitrary"),
        name="rw_scan_bwd" if backward else "rw_scan_fwd",
    )(*args)


def _hg_kernel(q_ref, f_ref, i_ref, lbp_ref, tri_ref, o_ref, st_ref, qd_scr, kv_scr, dec_scr):
    d = pl.program_id(1)
    i = pl.program_id(2)
    n_chunks = q_ref.shape[1] // HG_CHUNK

    @pl.when(i == 0)
    def _():
        st_ref[...] = jnp.zeros_like(st_ref)

    lbp = lbp_ref[...]
    e = jnp.exp(lbp - jnp.max(lbp, axis=0, keepdims=True))
    lb = e[0:1] / jnp.sum(e, axis=0, keepdims=True)
    t_idx = lax.broadcasted_iota(jnp.int32, (HG_CHUNK, HG_CHUNK), 0)
    s_idx = lax.broadcasted_iota(jnp.int32, (HG_CHUNK, HG_CHUNK), 1)
    nt = (((1,), (1,)), ((), ()))
    tn = (((0,), (0,)), ((), ()))

    def run(fwd):
        mask = (s_idx <= t_idx) if fwd else (s_idx >= t_idx)
        mid_row = HG_CHUNK // 2 - 1 if fwd else HG_CHUNK // 2
        last_row = HG_CHUNK - 1 if fwd else 0
        order = list(range(n_chunks)) if fwd else list(reversed(range(n_chunks)))

        f = lb + (1.0 - lb) * _sigmoid(f_ref[0])
        lf = jnp.log(f)
        hi = lf.astype(BF16)
        lo = (lf - hi.astype(F32)).astype(BF16)
        tri = tri_ref[0]
        b = _bdot(tri, hi) + _bdot(tri, lo)
        for c in order:
            rows = slice(c * HG_CHUNK, (c + 1) * HG_CHUNK)
            bc = b[rows]
            b_mid = bc[mid_row:mid_row + 1]
            b_last = bc[last_row:last_row + 1]
            q_in = q_ref[0, rows, :] * jnp.exp(bc - b_mid)
            k_in = (1.0 - f[rows]) * jnp.exp(b_mid - bc)
            qd_scr[rows, :] = (q_in * jnp.exp(b_mid)).astype(BF16)
            k_dec = (k_in * jnp.exp(b_last - b_mid)).astype(BF16)
            dec_scr[c:c + 1, :] = jnp.exp(b_last)
            q_in = q_in.astype(BF16)
            k_in = k_in.astype(BF16)
            vb = i_ref[0, rows, :].astype(BF16)
            heads = [slice(h * HG_N, (h + 1) * HG_N) for h in range(HG_HEADS)]
            scores = [lax.dot_general(q_in[:, cols], k_in[:, cols], nt, preferred_element_type=F32)
                      for cols in heads]
            for h, cols in enumerate(heads):
                kv_scr[c, h] = lax.dot_general(vb[:, cols], k_dec[:, cols], tn, preferred_element_type=F32)
            for h, cols in enumerate(heads):
                o_ref[0, 0, rows, cols] = _bdot(jnp.where(mask, scores[h], 0.0).astype(BF16), vb[:, cols])
        for c in order:
            rows = slice(c * HG_CHUNK, (c + 1) * HG_CHUNK)
            for h in range(HG_HEADS):
                cols = slice(h * HG_N, (h + 1) * HG_N)
                st = st_ref[h]
                o_ref[0, 0, rows, cols] += lax.dot_general(qd_scr[rows, cols], st.astype(BF16), nt,
                                                           preferred_element_type=F32)
                st_ref[h] = dec_scr[c:c + 1, cols] * st + kv_scr[c, h]

    @pl.when(d == 0)
    def _():
        run(True)

    @pl.when(d == 1)
    def _():
        run(False)


def _hg_tri(n_rows):
    t = jnp.arange(n_rows)[:, None]
    s = jnp.arange(n_rows)[None, :]
    same = (t // HG_CHUNK) == (s // HG_CHUNK)
    return jnp.stack([same & (s <= t), same & (s >= t)]).astype(BF16)


def _hg(zqi, zh, hg_lb, n_ctx):
    b, tt, _ = zh.shape
    w = HG_HEADS * HG_N
    nblk = tt // ROW_TILE
    n_chunks = ROW_TILE // HG_CHUNK
    blk = _seq_block(n_ctx // ROW_TILE, nblk)
    return pl.pallas_call(
        _hg_kernel,
        grid=(b, 2, nblk),
        in_specs=[pl.BlockSpec((1, ROW_TILE, w), lambda bi, d, i: (bi, blk(d, i), 0)),
                  pl.BlockSpec((1, ROW_TILE, w), lambda bi, d, i: (bi, blk(d, i), d)),
                  pl.BlockSpec((1, ROW_TILE, w), lambda bi, d, i: (bi, blk(d, i), 1)),
                  pl.BlockSpec(hg_lb.shape, lambda bi, d, i: (0, 0)),
                  pl.BlockSpec((1, ROW_TILE, ROW_TILE), lambda bi, d, i: (d, 0, 0))],
        out_specs=pl.BlockSpec((1, 1, ROW_TILE, w), lambda bi, d, i: (d, bi, blk(d, i), 0)),
        out_shape=jax.ShapeDtypeStruct((2, b, tt, w), F32),
        scratch_shapes=[pltpu.VMEM((HG_HEADS, HG_N, HG_N), F32),
                        pltpu.VMEM((ROW_TILE, w), BF16),
                        pltpu.VMEM((n_chunks, HG_HEADS, HG_N, HG_N), F32),
                        pltpu.VMEM((SUBLANES, w), F32)],
        compiler_params=_params("arbitrary", "arbitrary", "arbitrary"),
        name="hg",
    )(zqi, zh, zqi, hg_lb, _hg_tri(ROW_TILE))


def _merge_kernel(yrw_ref, g_ref, o_ref, gz_ref, zg_ref, x_ref, g1_ref, sh2_ref, sc2_ref, g2_ref, hnw_ref,
                  pa_ref, pb_ref, wo_ref, nw2_ref, w1_ref, w2_ref, fw_ref, out_ref):
    d = x_ref.shape[2]
    y_rw = (yrw_ref[...].T * g_ref[...]).astype(BF16)
    o = o_ref[0, 0] + o_ref[1, 0]
    gz = gz_ref[0]
    hnw = hnw_ref[...]
    parts = []
    for h in range(HG_HEADS):
        cols = slice(h * HG_N, (h + 1) * HG_N)
        oh = o[:, cols]
        ms = jnp.mean(oh * oh, axis=-1, keepdims=True)
        parts.append(oh * lax.rsqrt(ms + RMS_EPS) * hnw[:, cols])
    y_hg = (jnp.concatenate(parts, axis=1) * (gz * _sigmoid(gz))).astype(BF16)
    zg = zg_ref[0]
    m = _sigmoid(zg[:, :d]) * _bdot(y_rw, pa_ref[...]) + _sigmoid(zg[:, d:]) * _bdot(y_hg, pb_ref[...])
    x1 = x_ref[0] + g1_ref[0] * _bdot(m.astype(BF16), wo_ref[...])
    hb = _modulated_norm(x1, nw2_ref[...], sh2_ref[0], sc2_ref[0]).astype(BF16)
    u = jnp.maximum(_bdot(hb, w1_ref[...]), 0.0)
    y = x1 + g2_ref[0] * _bdot((u * u).astype(BF16), w2_ref[...])
    ms = jnp.mean(y * y, axis=-1, keepdims=True)
    out_ref[0] = y * lax.rsqrt(ms + RMS_EPS) * fw_ref[...]


def _merge(y_rw, g, o_hg, zh, zg, x, modcat, hnw, pa, pb, wo, nw2, w1, w2, fw, n_ctx):
    b, t, d = x.shape
    off = n_ctx // ROW_TILE
    lat = lambda bi, i: (bi, i, 0)
    cat = lambda bi, i: (bi, i + off, 0)
    mod = lambda col: pl.BlockSpec((1, 1, d), lambda bi, i: (2 * bi + 1, 0, col))
    const = lambda a: pl.BlockSpec(a.shape, lambda bi, i: (0,) * a.ndim, pipeline_mode=pl.Buffered(1))
    return pl.pallas_call(
        _merge_kernel,
        grid=(b, t // ROW_TILE),
        in_specs=[pl.BlockSpec((d, ROW_TILE), lambda bi, i: (bi, i)),
                  pl.BlockSpec((ROW_TILE, d), lambda bi, i: (i + off, bi)),
                  pl.BlockSpec((2, 1, ROW_TILE, d), lambda bi, i: (0, bi, i + off, 0)),
                  pl.BlockSpec((1, ROW_TILE, d), lambda bi, i: (bi, i + off, 2)),
                  pl.BlockSpec((1, ROW_TILE, 2 * d), cat),
                  pl.BlockSpec((1, ROW_TILE, d), lat),
                  mod(2), mod(3), mod(4), mod(5),
                  const(hnw), const(pa), const(pb), const(wo), const(nw2), const(w1), const(w2), const(fw)],
        out_specs=pl.BlockSpec((1, ROW_TILE, d), lat),
        out_shape=jax.ShapeDtypeStruct((b, t, d), F32),
        compiler_params=_params("arbitrary", "arbitrary"),
        name="merge_mlp",
    )(y_rw, g, o_hg, zh, zg, x, modcat, modcat, modcat, modcat, hnw, pa, pb, wo, nw2, w1, w2, fw)


def _to_scan(a):
    *lead, w, t = a.shape
    return jnp.swapaxes(a.reshape(*lead, w // RW_N, RW_N, t), -1, -3)


def _from_scan(a):
    t, n, bh = a.shape
    return jnp.swapaxes(a, 0, 2).reshape(bh * n, t)


def _head_tile(p, b):
    return jnp.tile(p.reshape(RW_HEADS, RW_N).T, (1, b))


def kernel(x, c, ctx, c_ctx, norm1_w, norm2_w, w_mod, b_mod, w_in, rw_mu, rw_w0, rw_w_up, rw_a0, rw_a_up, rw_g_up, rw_k_k, rw_k_a, rw_r_k, rw_ln_w, rw_ln_b, hg_lb, hg_norm_w, p_a, p_b, w_out, w_fc1, w_fc2, final_norm_w):
    b, t, d = x.shape
    n_ctx = ctx.shape[1]
    assert w_mod.shape[0] == 1, "single-layer block"
    assert b * RW_HEADS == LANES and n_ctx == ROW_TILE and t % ROW_TILE == 0
    assert d == RW_HEADS * RW_N == HG_HEADS * HG_N
    n_r = rw_mu.shape[2]
    rank_w, rank_a = rw_w_up.shape[2], rw_a_up.shape[2]
    assert rank_w + rank_a == LANES and n_r == 3 * d + 2 * LANES and w_in.shape[2] == n_r + 7 * d

    c_rows = jnp.zeros((2 * SUBLANES, d), F32).at[:b].set(c).at[b].set(c_ctx)
    mod = _mod(c_rows, w_mod[0], b_mod)
    modcat = jnp.stack([jnp.broadcast_to(mod[b], (b, N_MOD * d)), mod[:b]], axis=1).reshape(2 * b, 1, N_MOD * d)

    zr, zqi, zh, zg = _in_proj(ctx, x, norm1_w, modcat, w_in[0].astype(BF16), n_r)

    wup = jnp.pad(rw_w_up[0], ((0, 0), (0, rank_a), (0, 0))).astype(BF16)
    aup = jnp.pad(rw_a_up[0], ((0, 0), (rank_w, 0), (0, 0))).astype(BF16)
    *scan_ops, g = _rw_prep(zr, rw_mu[0], rw_w0[0], wup, rw_a0[0], aup, rw_g_up[0].astype(BF16), d)
    r_t, k_t, v_t, wf_t, wb_t, af_t, ab_t = (_to_scan(a) for a in scan_ops)
    o_hg = _hg(zqi, zh, hg_lb, n_ctx)
    kap = _head_tile(rw_k_a[0], b)
    tiles = (_head_tile(rw_k_k[0], b), kap, n_ctx // SCAN_TILE)
    y_fwd = _rw_scan(r_t, k_t, v_t, wf_t, af_t, *tiles, run_after=o_hg)
    y_rw_t = _rw_scan(r_t, k_t, v_t, wb_t, ab_t, *tiles,
                      readout=(y_fwd, af_t, _head_tile(rw_r_k[0].reshape(-1), b),
                               _head_tile(rw_ln_w[0], b), _head_tile(rw_ln_b[0], b)))
    y_rw = _from_scan(y_rw_t)

    return _merge(y_rw, g, o_hg, zh, zg, x, modcat, hg_norm_w, p_a[0].astype(BF16), p_b[0].astype(BF16),
                  w_out[0].astype(BF16), norm2_w, w_fc1[0].astype(BF16), w_fc2[0].astype(BF16),
                  final_norm_w.reshape(1, d), n_ctx)
```

```python
import functools

import jax
import jax.numpy as jnp
from jax import lax
from jax.experimental import pallas as pl
from jax.experimental.pallas import tpu as pltpu

F32 = jnp.float32
BF16 = jnp.bfloat16

GRID_W = 64
RW_HEADS = 16
RW_N = 64
HG_HEADS = 8
HG_N = 128
HG_CHUNK = 64
N_MOD = 6
RMS_EPS = 1e-6
RW_GN_EPS = 64e-5
L2_EPS = 1e-12
DECAY_SCALE = 0.6065306597126334

SUBLANES = 8
LANES = 128
ROW_TILE = 256
SCAN_TILE = 64
VMEM_LIMIT = 56 * 1024 * 1024


def _params(*sem):
    return pltpu.CompilerParams(dimension_semantics=sem, vmem_limit_bytes=VMEM_LIMIT)


def _bdot(a, b):
    return jnp.dot(a, b, preferred_element_type=F32)


def _sigmoid(x):
    return 1.0 / (1.0 + jnp.exp(-x))


def _mod_kernel(c_ref, w_ref, b_ref, o_ref):
    c = c_ref[...]
    act = c * _sigmoid(c)
    o_ref[...] = _bdot(act.astype(BF16), w_ref[...].astype(BF16)) + b_ref[...]


def _mod(c_rows, w_mod, b_mod):
    rows, d = c_rows.shape
    n = w_mod.shape[1]
    return pl.pallas_call(
        _mod_kernel,
        grid=(n // d,),
        in_specs=[pl.BlockSpec((rows, d), lambda j: (0, 0)),
                  pl.BlockSpec((d, d), lambda j: (0, j)),
                  pl.BlockSpec((1, d), lambda j: (0, j))],
        out_specs=pl.BlockSpec((rows, d), lambda j: (0, j)),
        out_shape=jax.ShapeDtypeStruct((rows, n), F32),
        compiler_params=_params("arbitrary"),
        name="mod",
    )(c_rows, w_mod, b_mod)


def _modulated_norm(x, nw, sh, sc):
    ms = jnp.mean(x * x, axis=-1, keepdims=True)
    return (x * lax.rsqrt(ms + RMS_EPS) * nw) * (1.0 + sc) + sh


def _in_proj_kernel(ctx_ref, x_ref, nw_ref, sh_ref, sc_ref, w_ref, zr_ref, zqi_ref, zh_ref, zg_ref):
    tokens = jnp.where(pl.program_id(1) == 0, ctx_ref[0], x_ref[0])
    hb = _modulated_norm(tokens, nw_ref[...], sh_ref[0], sc_ref[0]).astype(BF16)
    d = x_ref.shape[2]
    n_r = zr_ref.shape[2]
    col = lambda j0, j1: _bdot(hb, w_ref[:, n_r + j0 * d:n_r + j1 * d])
    zr_ref[0] = _bdot(hb, w_ref[:, :n_r])
    zqi_ref[0, :, :d] = col(0, 1).astype(BF16)
    zqi_ref[0, :, d:] = col(3, 4).astype(BF16)
    zh_ref[0, :, :2 * d] = col(1, 3)
    zh_ref[0, :, 2 * d:] = col(4, 5)
    zg_ref[0] = col(5, 7)


def _in_proj(ctx, x, nw, modcat, w_bf, n_r):
    b, t, d = x.shape
    tt = t + ctx.shape[1]
    nblk = tt // ROW_TILE
    mod_idx = lambda col: (lambda bi, i: (2 * bi + jnp.minimum(i, 1), 0, col))
    outs = ((n_r, F32), (2 * d, BF16), (3 * d, F32), (2 * d, F32))
    return pl.pallas_call(
        _in_proj_kernel,
        grid=(b, nblk),
        in_specs=[pl.BlockSpec((1, ROW_TILE, d), lambda bi, i: (bi, 0, 0)),
                  pl.BlockSpec((1, ROW_TILE, d), lambda bi, i: (bi, jnp.maximum(i - 1, 0), 0)),
                  pl.BlockSpec((1, d), lambda bi, i: (0, 0)),
                  pl.BlockSpec((1, 1, d), mod_idx(0)),
                  pl.BlockSpec((1, 1, d), mod_idx(1)),
                  pl.BlockSpec(w_bf.shape, lambda bi, i: (0, 0), pipeline_mode=pl.Buffered(1))],
        out_specs=[pl.BlockSpec((1, ROW_TILE, n), lambda bi, i: (bi, i, 0)) for n, _ in outs],
        out_shape=[jax.ShapeDtypeStruct((b, tt, n), dt) for n, dt in outs],
        compiler_params=_params("arbitrary", "arbitrary"),
        name="in_proj",
    )(ctx, x, nw, modcat, modcat, w_bf)


def _rw_prep_kernel(z_ref, zn_ref, mu_ref, w0_ref, wup_ref, a0_ref, aup_ref, gup_ref,
                    r_ref, k_ref, v_ref, wf_ref, wb_ref, af_ref, ab_ref, g_ref, zp_scr):
    i = pl.program_id(1)
    nblk = pl.num_programs(1)
    tt = z_ref.shape[1]
    d = r_ref.shape[0]
    is_lat = i > 0
    row = lax.broadcasted_iota(jnp.int32, (tt, 1), 0)
    col = row % GRID_W
    lmask = jnp.where(is_lat, col, row) == 0
    rmask = jnp.where(is_lat, col, row - (tt - GRID_W)) == GRID_W - 1
    latf = is_lat.astype(F32)
    up_ok = (i > 1).astype(F32)
    dn_ok = (i < nblk - 1).astype(F32)

    @pl.when(i == 0)
    def _():
        zp_scr[...] = jnp.zeros_like(zp_scr)

    def shifted(c0, c1):
        z = z_ref[0, :, c0:c1]
        mu = mu_ref[:, c0:c1]
        left = jnp.where(lmask, 0.0, pltpu.roll(z, 1, 0))
        right = jnp.where(rmask, 0.0, pltpu.roll(z, tt - 1, 0))
        up = jnp.concatenate([zp_scr[:, c0:c1] * up_ok, z[:tt - GRID_W]], axis=0)
        down = jnp.concatenate([z[GRID_W:], zn_ref[0, :, c0:c1] * dn_ok], axis=0)
        m_up, m_dn = latf * mu[2:3], latf * mu[3:4]
        m_self = 1.0 - mu[0:1] - mu[1:2] - m_up - m_dn
        return z * m_self + left * mu[0:1] + right * mu[1:2] + up * m_up + down * m_dn

    r_ref[...] = shifted(0, d).T.astype(BF16)
    k_ref[...] = shifted(d, 2 * d).T.astype(BF16)
    v_ref[...] = shifted(2 * d, 3 * d).T.astype(BF16)
    rest = shifted(3 * d, z_ref.shape[2])
    xwa = rest[:, :LANES]
    xw_t = jnp.tanh(xwa).astype(BF16)
    xa_b = xwa.astype(BF16)
    for dr, (wd_ref, ad_ref) in enumerate(((wf_ref, af_ref), (wb_ref, ab_ref))):
        wz = w0_ref[dr:dr + 1, :] + _bdot(xw_t, wup_ref[dr])
        wd_ref[...] = jnp.exp(-DECAY_SCALE * _sigmoid(wz)).T
        ad_ref[...] = _sigmoid(a0_ref[dr:dr + 1, :] + _bdot(xa_b, aup_ref[dr])).T.astype(BF16)
    g_ref[...] = _bdot(_sigmoid(rest[:, LANES:]).astype(BF16), gup_ref[...])
    zp_scr[...] = z_ref[0, tt - GRID_W:, :]


def _rw_prep(zr, mu, w0, wup, a0, aup, gup, d):
    b, tt, nr = zr.shape
    nblk = tt // ROW_TILE
    per = ROW_TILE // GRID_W
    last = tt // GRID_W - 1
    const = lambda shape: pl.BlockSpec(shape, lambda bi, i: (0,) * len(shape))
    row_spec = pl.BlockSpec((d, ROW_TILE), lambda bi, i: (bi, i))
    gate_spec = pl.BlockSpec((ROW_TILE, d), lambda bi, i: (i, bi))
    row_shape = jax.ShapeDtypeStruct((b * d, tt), F32)
    half_shape = jax.ShapeDtypeStruct((b * d, tt), BF16)
    gate_shape = jax.ShapeDtypeStruct((tt, b * d), F32)
    return pl.pallas_call(
        _rw_prep_kernel,
        grid=(b, nblk),
        in_specs=[pl.BlockSpec((1, ROW_TILE, nr), lambda bi, i: (bi, i, 0)),
                  pl.BlockSpec((1, GRID_W, nr), lambda bi, i: (bi, jnp.minimum(i * per + per, last), 0)),
                  const(mu.shape), const(w0.shape), const(wup.shape), const(a0.shape),
                  const(aup.shape), const(gup.shape)],
        out_specs=[row_spec] * 7 + [gate_spec],
        out_shape=[half_shape] * 3 + [row_shape] * 2 + [half_shape] * 2 + [gate_shape],
        scratch_shapes=[pltpu.VMEM((GRID_W, nr), F32)],
        compiler_params=_params("arbitrary", "arbitrary"),
        name="rw_prep",
    )(zr, zr, mu, w0, wup, a0, aup, gup)


K_UNROLL = 32


def _seq_block(n_ctx_blocks, n_blocks):
    def blk(d, i):
        bwd = jnp.where(i < n_ctx_blocks, n_ctx_blocks - 1 - i, n_blocks - 1 + n_ctx_blocks - i)
        return jnp.where(d == 0, i, bwd)
    return blk


def _rw_scan_kernel(*refs, backward):
    if backward:
        (r_ref, k_ref, v_ref, w_ref, a_ref, kkp_ref, kap_ref, yf_ref, af_ref, rkp_ref, lnw_ref, lnb_ref,
         y_ref, s_ref, p_s, sa_s, kk0_s, bt_s, kt_s, rd_s, kkd_s) = refs
    else:
        (r_ref, k_ref, v_ref, w_ref, a_ref, kkp_ref, kap_ref, _order_ref,
         y_ref, s_ref, p_s, sa_s, kk0_s, bt_s, kt_s, rd_s, kkd_s) = refs
    i = pl.program_id(0)
    tb = r_ref.shape[0]
    nv = RW_N // SUBLANES

    @pl.when(i == 0)
    def _():
        s_ref[...] = jnp.zeros_like(s_ref)
        p_s[...] = jnp.ones_like(p_s)

    def bcast(ref, *idx):
        k = idx[-1]
        row = ref[(*idx[:-1], pl.ds(k, 1), slice(None))]
        return jnp.broadcast_to(row, (SUBLANES, LANES))

    def time_index(s):
        s = jnp.minimum(s, tb - 1)
        return tb - 1 - s if backward else s

    def bf16_pair(x):
        hi = lax.bitcast_convert_type(x.astype(BF16).astype(F32), jnp.uint32)
        return lax.bitcast_convert_type(hi | (hi >> 16), F32)

    def tile(ref, t):
        return ref[t].astype(F32)

    def norm_key(t):
        kkr = tile(k_ref, t) * kkp_ref[...]
        nrm = jnp.sqrt(jnp.sum(kkr * kkr, axis=0, keepdims=True))
        return kkr / jnp.maximum(nrm, L2_EPS)

    def scaled_key(t, a):
        return tile(k_ref, t) * (1.0 + (a - 1.0) * kap_ref[...])

    def prepare(s, slot):
        t = time_index(s)
        a = tile(a_ref, t)
        kk = norm_key(t)
        p_prev = p_s[...]
        kkd_s[1 - slot] = bf16_pair(p_prev * kk)
        p = p_prev * jnp.where(s < tb, w_ref[t], 1.0)
        p_s[...] = p
        inv_p = 1.0 / p
        bt_s[slot] = bf16_pair(kk * a * inv_p)
        kt_s[slot] = bf16_pair(scaled_key(t, a) * inv_p)
        rd_s[slot] = bf16_pair(p * tile(r_ref, t))

    def restart():
        kk0_s[...] = norm_key(time_index(0))
        acc = [None] * nv
        for k in range(RW_N):
            pb = bcast(p_s, k)
            kkb = bcast(kk0_s, k)
            for j in range(nv):
                rows = slice(SUBLANES * j, SUBLANES * (j + 1))
                sn = s_ref[k, rows, :] * pb
                s_ref[k, rows, :] = sn
                acc[j] = sn * kkb if acc[j] is None else acc[j] + sn * kkb
        sa_s[...] = -jnp.concatenate(acc, axis=0)
        p_s[...] = jnp.ones_like(p_s)

    def sweep(s, slot):
        t = time_index(s)
        zero = jnp.zeros((SUBLANES, LANES), F32)

        sa_p = [sa_s[2 * SUBLANES * m:2 * SUBLANES * (m + 1), :].astype(BF16) for m in range(nv // 2)]
        v_p = [v_ref[t, 2 * SUBLANES * m:2 * SUBLANES * (m + 1), :] for m in range(nv // 2)]

        def key_block(kblk, carry):
            yacc, acc = list(carry[0]), list(carry[1])
            yb = [jnp.zeros((2 * SUBLANES, LANES), BF16)] * (nv // 2)
            ab = [jnp.zeros((2 * SUBLANES, LANES), BF16)] * (nv // 2)
            for kk in range(K_UNROLL):
                k = kblk * K_UNROLL + kk
                bb = pltpu.bitcast(bcast(bt_s, slot, k), BF16)
                kb = pltpu.bitcast(bcast(kt_s, slot, k), BF16)
                rb = pltpu.bitcast(bcast(rd_s, slot, k), BF16)
                kkn = pltpu.bitcast(bcast(kkd_s, slot, k), BF16)
                for m in range(nv // 2):
                    rows = slice(2 * SUBLANES * m, 2 * SUBLANES * (m + 1))
                    sn = s_ref[k, rows, :] + (sa_p[m] * bb + v_p[m] * kb).astype(F32)
                    s_ref[k, rows, :] = sn
                    snp = sn.astype(BF16)
                    yb[m] = yb[m] + snp * rb
                    ab[m] = ab[m] + snp * kkn
            for m in range(nv // 2):
                y32 = yb[m].astype(F32)
                a32 = ab[m].astype(F32)
                for h, j in enumerate((2 * m, 2 * m + 1)):
                    yacc[j] = yacc[j] + y32[SUBLANES * h:SUBLANES * (h + 1)]
                    acc[j] = acc[j] + a32[SUBLANES * h:SUBLANES * (h + 1)]
            return yacc, acc

        init = [zero] * nv
        n_kblk = RW_N // K_UNROLL
        carry = lax.fori_loop(0, n_kblk - 1, key_block, (init, init))
        yacc, acc = key_block(n_kblk - 1, carry)
        sa_s[...] = -jnp.concatenate(acc, axis=0)
        y = jnp.concatenate(yacc, axis=0)
        if backward:
            y = y + yf_ref[t]
            yc = y - jnp.mean(y, axis=0, keepdims=True)
            var = jnp.mean(yc * yc, axis=0, keepdims=True)
            y = yc * lax.rsqrt(var + RW_GN_EPS) * lnw_ref[...] + lnb_ref[...]
            k_sum = tile(k_ref, t) * (2.0 + (tile(af_ref, t) + tile(a_ref, t) - 2.0) * kap_ref[...])
            y = y + jnp.sum(tile(r_ref, t) * k_sum * rkp_ref[...], axis=0, keepdims=True) * tile(v_ref, t)
        y_ref[t] = y

    restart()
    prepare(0, 0)
    prepare(1, 1)

    def pair(p, carry):
        s = 2 * p
        sweep(s, 0)
        prepare(s + 2, 0)
        sweep(s + 1, 1)
        prepare(s + 3, 1)
        return carry

    lax.fori_loop(0, tb // 2, pair, 0)


def _rw_scan(r_t, k_t, v_t, w_t, a_t, kkp, kap, n_ctx_blocks, *, run_after=None, readout=None):
    backward = readout is not None
    direction = int(backward)
    tt = r_t.shape[0]
    nblk = tt // SCAN_TILE
    seq = _seq_block(n_ctx_blocks, nblk)
    blk = lambda i: seq(direction, i)
    tile = (SCAN_TILE, RW_N, LANES)
    shared = pl.BlockSpec(tile, lambda i: (blk(i), 0, 0))
    const = pl.BlockSpec((RW_N, LANES), lambda i: (0, 0))
    y_blk = lambda i: jnp.where(i < n_ctx_blocks, blk(n_ctx_blocks), blk(i)) - n_ctx_blocks
    y_spec = pl.BlockSpec(tile, lambda i: (y_blk(i), 0, 0))
    in_specs = [shared, shared, shared, shared, shared, const, const]
    args = [r_t, k_t, v_t, w_t, a_t, kkp, kap]
    if backward:
        y_fwd, a_fwd, rkp, lnw, lnb = readout
        in_specs += [y_spec, shared, const, const, const]
        args += [y_fwd, a_fwd, rkp, lnw, lnb]
    else:
        in_specs += [pl.BlockSpec(memory_space=pl.ANY)]
        args += [run_after]
    return pl.pallas_call(
        functools.partial(_rw_scan_kernel, backward=backward),
        grid=(nblk,),
        in_specs=in_specs,
        out_specs=y_spec,
        out_shape=jax.ShapeDtypeStruct((tt - n_ctx_blocks * SCAN_TILE, RW_N, LANES), F32),
        scratch_shapes=[pltpu.VMEM((RW_N, RW_N, LANES), F32)] + [pltpu.VMEM((RW_N, LANES), F32)] * 3
                       + [pltpu.VMEM((2, RW_N, LANES), F32)] * 4,
        compiler_params=_params("arbitrary"),
        name="rw_scan_bwd" if backward else "rw_scan_fwd",
    )(*args)


def _hg_kernel(q_ref, f_ref, i_ref, lbp_ref, tri_ref, o_ref, st_ref, qd_scr, kv_scr, dec_scr):
    d = pl.program_id(1)
    i = pl.program_id(2)
    n_chunks = q_ref.shape[1] // HG_CHUNK

    @pl.when(i == 0)
    def _():
        st_ref[...] = jnp.zeros_like(st_ref)

    lbp = lbp_ref[...]
    e = jnp.exp(lbp - jnp.max(lbp, axis=0, keepdims=True))
    lb = e[0:1] / jnp.sum(e, axis=0, keepdims=True)
    t_idx = lax.broadcasted_iota(jnp.int32, (HG_CHUNK, HG_CHUNK), 0)
    s_idx = lax.broadcasted_iota(jnp.int32, (HG_CHUNK, HG_CHUNK), 1)
    nt = (((1,), (1,)), ((), ()))
    tn = (((0,), (0,)), ((), ()))

    def run(fwd):
        mask = (s_idx <= t_idx) if fwd else (s_idx >= t_idx)
        mid_row = HG_CHUNK // 2 - 1 if fwd else HG_CHUNK // 2
        last_row = HG_CHUNK - 1 if fwd else 0
        order = list(range(n_chunks)) if fwd else list(reversed(range(n_chunks)))

        f = lb + (1.0 - lb) * _sigmoid(f_ref[0])
        lf = jnp.log(f)
        hi = lf.astype(BF16)
        lo = (lf - hi.astype(F32)).astype(BF16)
        tri = tri_ref[0]
        b = _bdot(tri, hi) + _bdot(tri, lo)
        for c in order:
            rows = slice(c * HG_CHUNK, (c + 1) * HG_CHUNK)
            bc = b[rows]
            b_mid = bc[mid_row:mid_row + 1]
            b_last = bc[last_row:last_row + 1]
            q_in = q_ref[0, rows, :] * jnp.exp(bc - b_mid)
            k_in = (1.0 - f[rows]) * jnp.exp(b_mid - bc)
            qd_scr[rows, :] = (q_in * jnp.exp(b_mid)).astype(BF16)
            k_dec = (k_in * jnp.exp(b_last - b_mid)).astype(BF16)
            dec_scr[c:c + 1, :] = jnp.exp(b_last)
            q_in = q_in.astype(BF16)
            k_in = k_in.astype(BF16)
            vb = i_ref[0, rows, :].astype(BF16)
            heads = [slice(h * HG_N, (h + 1) * HG_N) for h in range(HG_HEADS)]
            scores = [lax.dot_general(q_in[:, cols], k_in[:, cols], nt, preferred_element_type=F32)
                      for cols in heads]
            for h, cols in enumerate(heads):
                kv_scr[c, h] = lax.dot_general(vb[:, cols], k_dec[:, cols], tn, preferred_element_type=F32)
            for h, cols in enumerate(heads):
                o_ref[0, 0, rows, cols] = _bdot(jnp.where(mask, scores[h], 0.0).astype(BF16), vb[:, cols])
        for c in order:
            rows = slice(c * HG_CHUNK, (c + 1) * HG_CHUNK)
            for h in range(HG_HEADS):
                cols = slice(h * HG_N, (h + 1) * HG_N)
                st = st_ref[h]
                o_ref[0, 0, rows, cols] += lax.dot_general(qd_scr[rows, cols], st.astype(BF16), nt,
                                                           preferred_element_type=F32)
                st_ref[h] = dec_scr[c:c + 1, cols] * st + kv_scr[c, h]

    @pl.when(d == 0)
    def _():
        run(True)

    @pl.when(d == 1)
    def _():
        run(False)


def _hg_tri(n_rows):
    t = jnp.arange(n_rows)[:, None]
    s = jnp.arange(n_rows)[None, :]
    same = (t // HG_CHUNK) == (s // HG_CHUNK)
    return jnp.stack([same & (s <= t), same & (s >= t)]).astype(BF16)


def _hg(zqi, zh, hg_lb, n_ctx):
    b, tt, _ = zh.shape
    w = HG_HEADS * HG_N
    nblk = tt // ROW_TILE
    n_chunks = ROW_TILE // HG_CHUNK
    blk = _seq_block(n_ctx // ROW_TILE, nblk)
    return pl.pallas_call(
        _hg_kernel,
        grid=(b, 2, nblk),
        in_specs=[pl.BlockSpec((1, ROW_TILE, w), lambda bi, d, i: (bi, blk(d, i), 0)),
                  pl.BlockSpec((1, ROW_TILE, w), lambda bi, d, i: (bi, blk(d, i), d)),
                  pl.BlockSpec((1, ROW_TILE, w), lambda bi, d, i: (bi, blk(d, i), 1)),
                  pl.BlockSpec(hg_lb.shape, lambda bi, d, i: (0, 0)),
                  pl.BlockSpec((1, ROW_TILE, ROW_TILE), lambda bi, d, i: (d, 0, 0))],
        out_specs=pl.BlockSpec((1, 1, ROW_TILE, w), lambda bi, d, i: (d, bi, blk(d, i), 0)),
        out_shape=jax.ShapeDtypeStruct((2, b, tt, w), F32),
        scratch_shapes=[pltpu.VMEM((HG_HEADS, HG_N, HG_N), F32),
                        pltpu.VMEM((ROW_TILE, w), BF16),
                        pltpu.VMEM((n_chunks, HG_HEADS, HG_N, HG_N), F32),
                        pltpu.VMEM((SUBLANES, w), F32)],
        compiler_params=_params("arbitrary", "arbitrary", "arbitrary"),
        name="hg",
    )(zqi, zh, zqi, hg_lb, _hg_tri(ROW_TILE))


def _merge_kernel(yrw_ref, g_ref, o_ref, gz_ref, zg_ref, x_ref, g1_ref, sh2_ref, sc2_ref, g2_ref, hnw_ref,
                  pa_ref, pb_ref, wo_ref, nw2_ref, w1_ref, w2_ref, fw_ref, out_ref):
    d = x_ref.shape[2]
    y_rw = (yrw_ref[...].T * g_ref[...]).astype(BF16)
    o = o_ref[0, 0] + o_ref[1, 0]
    gz = gz_ref[0]
    hnw = hnw_ref[...]
    parts = []
    for h in range(HG_HEADS):
        cols = slice(h * HG_N, (h + 1) * HG_N)
        oh = o[:, cols]
        ms = jnp.mean(oh * oh, axis=-1, keepdims=True)
        parts.append(oh * lax.rsqrt(ms + RMS_EPS) * hnw[:, cols])
    y_hg = (jnp.concatenate(parts, axis=1) * (gz * _sigmoid(gz))).astype(BF16)
    zg = zg_ref[0]
    m = _sigmoid(zg[:, :d]) * _bdot(y_rw, pa_ref[...]) + _sigmoid(zg[:, d:]) * _bdot(y_hg, pb_ref[...])
    x1 = x_ref[0] + g1_ref[0] * _bdot(m.astype(BF16), wo_ref[...])
    hb = _modulated_norm(x1, nw2_ref[...], sh2_ref[0], sc2_ref[0]).astype(BF16)
    u = jnp.maximum(_bdot(hb, w1_ref[...]), 0.0)
    y = x1 + g2_ref[0] * _bdot((u * u).astype(BF16), w2_ref[...])
    ms = jnp.mean(y * y, axis=-1, keepdims=True)
    out_ref[0] = y * lax.rsqrt(ms + RMS_EPS) * fw_ref[...]


def _merge(y_rw, g, o_hg, zh, zg, x, modcat, hnw, pa, pb, wo, nw2, w1, w2, fw, n_ctx):
    b, t, d = x.shape
    off = n_ctx // ROW_TILE
    lat = lambda bi, i: (bi, i, 0)
    cat = lambda bi, i: (bi, i + off, 0)
    mod = lambda col: pl.BlockSpec((1, 1, d), lambda bi, i: (2 * bi + 1, 0, col))
    const = lambda a: pl.BlockSpec(a.shape, lambda bi, i: (0,) * a.ndim, pipeline_mode=pl.Buffered(1))
    return pl.pallas_call(
        _merge_kernel,
        grid=(b, t // ROW_TILE),
        in_specs=[pl.BlockSpec((d, ROW_TILE), lambda bi, i: (bi, i)),
                  pl.BlockSpec((ROW_TILE, d), lambda bi, i: (i + off, bi)),
                  pl.BlockSpec((2, 1, ROW_TILE, d), lambda bi, i: (0, bi, i + off, 0)),
                  pl.BlockSpec((1, ROW_TILE, d), lambda bi, i: (bi, i + off, 2)),
                  pl.BlockSpec((1, ROW_TILE, 2 * d), cat),
                  pl.BlockSpec((1, ROW_TILE, d), lat),
                  mod(2), mod(3), mod(4), mod(5),
                  const(hnw), const(pa), const(pb), const(wo), const(nw2), const(w1), const(w2), const(fw)],
        out_specs=pl.BlockSpec((1, ROW_TILE, d), lat),
        out_shape=jax.ShapeDtypeStruct((b, t, d), F32),
        compiler_params=_params("arbitrary", "arbitrary"),
        name="merge_mlp",
    )(y_rw, g, o_hg, zh, zg, x, modcat, modcat, modcat, modcat, hnw, pa, pb, wo, nw2, w1, w2, fw)


def _to_scan(a):
    *lead, w, t = a.shape
    return jnp.swapaxes(a.reshape(*lead, w // RW_N, RW_N, t), -1, -3)


def _from_scan(a):
    t, n, bh = a.shape
    return jnp.swapaxes(a, 0, 2).reshape(bh * n, t)


def _head_tile(p, b):
    return jnp.tile(p.reshape(RW_HEADS, RW_N).T, (1, b))


def kernel(x, c, ctx, c_ctx, norm1_w, norm2_w, w_mod, b_mod, w_in, rw_mu, rw_w0, rw_w_up, rw_a0, rw_a_up, rw_g_up, rw_k_k, rw_k_a, rw_r_k, rw_ln_w, rw_ln_b, hg_lb, hg_norm_w, p_a, p_b, w_out, w_fc1, w_fc2, final_norm_w):
    b, t, d = x.shape
    n_ctx = ctx.shape[1]
    assert w_mod.shape[0] == 1, "single-layer block"
    assert b * RW_HEADS == LANES and n_ctx == ROW_TILE and t % ROW_TILE == 0
    assert d == RW_HEADS * RW_N == HG_HEADS * HG_N
    n_r = rw_mu.shape[2]
    rank_w, rank_a = rw_w_up.shape[2], rw_a_up.shape[2]
    assert rank_w + rank_a == LANES and n_r == 3 * d + 2 * LANES and w_in.shape[2] == n_r + 7 * d

    c_rows = jnp.zeros((2 * SUBLANES, d), F32).at[:b].set(c).at[b].set(c_ctx)
    mod = _mod(c_rows, w_mod[0], b_mod)
    modcat = jnp.stack([jnp.broadcast_to(mod[b], (b, N_MOD * d)), mod[:b]], axis=1).reshape(2 * b, 1, N_MOD * d)

    zr, zqi, zh, zg = _in_proj(ctx, x, norm1_w, modcat, w_in[0].astype(BF16), n_r)

    wup = jnp.pad(rw_w_up[0], ((0, 0), (0, rank_a), (0, 0))).astype(BF16)
    aup = jnp.pad(rw_a_up[0], ((0, 0), (rank_w, 0), (0, 0))).astype(BF16)
    *scan_ops, g = _rw_prep(zr, rw_mu[0], rw_w0[0], wup, rw_a0[0], aup, rw_g_up[0].astype(BF16), d)
    r_t, k_t, v_t, wf_t, wb_t, af_t, ab_t = (_to_scan(a) for a in scan_ops)
    o_hg = _hg(zqi, zh, hg_lb, n_ctx)
    kap = _head_tile(rw_k_a[0], b)
    tiles = (_head_tile(rw_k_k[0], b), kap, n_ctx // SCAN_TILE)
    y_fwd = _rw_scan(r_t, k_t, v_t, wf_t, af_t, *tiles, run_after=o_hg)
    y_rw_t = _rw_scan(r_t, k_t, v_t, wb_t, ab_t, *tiles,
                      readout=(y_fwd, af_t, _head_tile(rw_r_k[0].reshape(-1), b),
                               _head_tile(rw_ln_w[0], b), _head_tile(rw_ln_b[0], b)))
    y_rw = _from_scan(y_rw_t)

    return _merge(y_rw, g, o_hg, zh, zg, x, modcat, hg_norm_w, p_a[0].astype(BF16), p_b[0].astype(BF16),
                  w_out[0].astype(BF16), norm2_w, w_fc1[0].astype(BF16), w_fc2[0].astype(BF16),
                  final_norm_w.reshape(1, d), n_ctx)
```

```python
import functools

import jax
import jax.numpy as jnp
from jax import lax
from jax.experimental import pallas as pl
from jax.experimental.pallas import tpu as pltpu

F32 = jnp.float32
BF16 = jnp.bfloat16

GRID_W = 64
RW_HEADS = 16
RW_N = 64
HG_HEADS = 8
HG_N = 128
HG_CHUNK = 64
N_MOD = 6
RMS_EPS = 1e-6
RW_GN_EPS = 64e-5
L2_EPS = 1e-12
DECAY_SCALE = 0.6065306597126334

SUBLANES = 8
LANES = 128
ROW_TILE = 256
SCAN_TILE = 64
VMEM_LIMIT = 56 * 1024 * 1024


def _params(*sem):
    return pltpu.CompilerParams(dimension_semantics=sem, vmem_limit_bytes=VMEM_LIMIT)


def _bdot(a, b):
    return jnp.dot(a, b, preferred_element_type=F32)


def _sigmoid(x):
    return 1.0 / (1.0 + jnp.exp(-x))


def _mod_kernel(c_ref, w_ref, b_ref, o_ref):
    c = c_ref[...]
    act = c * _sigmoid(c)
    o_ref[...] = _bdot(act.astype(BF16), w_ref[...].astype(BF16)) + b_ref[...]


def _mod(c_rows, w_mod, b_mod):
    rows, d = c_rows.shape
    n = w_mod.shape[1]
    return pl.pallas_call(
        _mod_kernel,
        grid=(n // d,),
        in_specs=[pl.BlockSpec((rows, d), lambda j: (0, 0)),
                  pl.BlockSpec((d, d), lambda j: (0, j)),
                  pl.BlockSpec((1, d), lambda j: (0, j))],
        out_specs=pl.BlockSpec((rows, d), lambda j: (0, j)),
        out_shape=jax.ShapeDtypeStruct((rows, n), F32),
        compiler_params=_params("arbitrary"),
        name="mod",
    )(c_rows, w_mod, b_mod)


def _modulated_norm(x, nw, sh, sc):
    ms = jnp.mean(x * x, axis=-1, keepdims=True)
    return (x * lax.rsqrt(ms + RMS_EPS) * nw) * (1.0 + sc) + sh


def _in_proj_kernel(ctx_ref, x_ref, nw_ref, sh_ref, sc_ref, w_ref, zr_ref, zqi_ref, zh_ref, zg_ref):
    tokens = jnp.where(pl.program_id(1) == 0, ctx_ref[0], x_ref[0])
    hb = _modulated_norm(tokens, nw_ref[...], sh_ref[0], sc_ref[0]).astype(BF16)
    d = x_ref.shape[2]
    n_r = zr_ref.shape[2]
    col = lambda j0, j1: _bdot(hb, w_ref[:, n_r + j0 * d:n_r + j1 * d])
    zr_ref[0] = _bdot(hb, w_ref[:, :n_r])
    zqi_ref[0, :, :d] = col(0, 1).astype(BF16)
    zqi_ref[0, :, d:] = col(3, 4).astype(BF16)
    zh_ref[0, :, :2 * d] = col(1, 3)
    zh_ref[0, :, 2 * d:] = col(4, 5)
    zg_ref[0] = col(5, 7)


def _in_proj(ctx, x, nw, modcat, w_bf, n_r):
    b, t, d = x.shape
    tt = t + ctx.shape[1]
    nblk = tt // ROW_TILE
    mod_idx = lambda col: (lambda bi, i: (2 * bi + jnp.minimum(i, 1), 0, col))
    outs = ((n_r, F32), (2 * d, BF16), (3 * d, F32), (2 * d, F32))
    return pl.pallas_call(
        _in_proj_kernel,
        grid=(b, nblk),
        in_specs=[pl.BlockSpec((1, ROW_TILE, d), lambda bi, i: (bi, 0, 0)),
                  pl.BlockSpec((1, ROW_TILE, d), lambda bi, i: (bi, jnp.maximum(i - 1, 0), 0)),
                  pl.BlockSpec((1, d), lambda bi, i: (0, 0)),
                  pl.BlockSpec((1, 1, d), mod_idx(0)),
                  pl.BlockSpec((1, 1, d), mod_idx(1)),
                  pl.BlockSpec(w_bf.shape, lambda bi, i: (0, 0), pipeline_mode=pl.Buffered(1))],
        out_specs=[pl.BlockSpec((1, ROW_TILE, n), lambda bi, i: (bi, i, 0)) for n, _ in outs],
        out_shape=[jax.ShapeDtypeStruct((b, tt, n), dt) for n, dt in outs],
        compiler_params=_params("arbitrary", "arbitrary"),
        name="in_proj",
    )(ctx, x, nw, modcat, modcat, w_bf)


def _rw_prep_kernel(z_ref, zn_ref, mu_ref, w0_ref, wup_ref, a0_ref, aup_ref, gup_ref,
                    r_ref, k_ref, v_ref, wf_ref, wb_ref, af_ref, ab_ref, g_ref, zp_scr):
    i = pl.program_id(1)
    nblk = pl.num_programs(1)
    tt = z_ref.shape[1]
    d = r_ref.shape[0]
    is_lat = i > 0
    row = lax.broadcasted_iota(jnp.int32, (tt, 1), 0)
    col = row % GRID_W
    lmask = jnp.where(is_lat, col, row) == 0
    rmask = jnp.where(is_lat, col, row - (tt - GRID_W)) == GRID_W - 1
    latf = is_lat.astype(F32)
    up_ok = (i > 1).astype(F32)
    dn_ok = (i < nblk - 1).astype(F32)

    @pl.when(i == 0)
    def _():
        zp_scr[...] = jnp.zeros_like(zp_scr)

    def shifted(c0, c1):
        z = z_ref[0, :, c0:c1]
        mu = mu_ref[:, c0:c1]
        left = jnp.where(lmask, 0.0, pltpu.roll(z, 1, 0))
        right = jnp.where(rmask, 0.0, pltpu.roll(z, tt - 1, 0))
        up = jnp.concatenate([zp_scr[:, c0:c1] * up_ok, z[:tt - GRID_W]], axis=0)
        down = jnp.concatenate([z[GRID_W:], zn_ref[0, :, c0:c1] * dn_ok], axis=0)
        m_up, m_dn = latf * mu[2:3], latf * mu[3:4]
        m_self = 1.0 - mu[0:1] - mu[1:2] - m_up - m_dn
        return z * m_self + left * mu[0:1] + right * mu[1:2] + up * m_up + down * m_dn

    r_ref[...] = shifted(0, d).T
    k_ref[...] = shifted(d, 2 * d).T
    v_ref[...] = shifted(2 * d, 3 * d).T
    rest = shifted(3 * d, z_ref.shape[2])
    xwa = rest[:, :LANES]
    xw_t = jnp.tanh(xwa).astype(BF16)
    xa_b = xwa.astype(BF16)
    for dr, (wd_ref, ad_ref) in enumerate(((wf_ref, af_ref), (wb_ref, ab_ref))):
        wz = w0_ref[dr:dr + 1, :] + _bdot(xw_t, wup_ref[dr])
        wd_ref[...] = jnp.exp(-DECAY_SCALE * _sigmoid(wz)).T
        ad_ref[...] = _sigmoid(a0_ref[dr:dr + 1, :] + _bdot(xa_b, aup_ref[dr])).T
    g_ref[...] = _bdot(_sigmoid(rest[:, LANES:]).astype(BF16), gup_ref[...])
    zp_scr[...] = z_ref[0, tt - GRID_W:, :]


def _rw_prep(zr, mu, w0, wup, a0, aup, gup, d):
    b, tt, nr = zr.shape
    nblk = tt // ROW_TILE
    per = ROW_TILE // GRID_W
    last = tt // GRID_W - 1
    const = lambda shape: pl.BlockSpec(shape, lambda bi, i: (0,) * len(shape))
    row_spec = pl.BlockSpec((d, ROW_TILE), lambda bi, i: (bi, i))
    gate_spec = pl.BlockSpec((ROW_TILE, d), lambda bi, i: (i, bi))
    row_shape = jax.ShapeDtypeStruct((b * d, tt), F32)
    gate_shape = jax.ShapeDtypeStruct((tt, b * d), F32)
    return pl.pallas_call(
        _rw_prep_kernel,
        grid=(b, nblk),
        in_specs=[pl.BlockSpec((1, ROW_TILE, nr), lambda bi, i: (bi, i, 0)),
                  pl.BlockSpec((1, GRID_W, nr), lambda bi, i: (bi, jnp.minimum(i * per + per, last), 0)),
                  const(mu.shape), const(w0.shape), const(wup.shape), const(a0.shape),
                  const(aup.shape), const(gup.shape)],
        out_specs=[row_spec] * 7 + [gate_spec],
        out_shape=[row_shape] * 7 + [gate_shape],
        scratch_shapes=[pltpu.VMEM((GRID_W, nr), F32)],
        compiler_params=_params("arbitrary", "arbitrary"),
        name="rw_prep",
    )(zr, zr, mu, w0, wup, a0, aup, gup)


K_UNROLL = 32


def _seq_block(n_ctx_blocks, n_blocks):
    def blk(d, i):
        bwd = jnp.where(i < n_ctx_blocks, n_ctx_blocks - 1 - i, n_blocks - 1 + n_ctx_blocks - i)
        return jnp.where(d == 0, i, bwd)
    return blk


def _rw_scan_kernel(*refs, backward):
    if backward:
        (r_ref, k_ref, v_ref, w_ref, a_ref, kkp_ref, kap_ref, yf_ref, af_ref, rkp_ref, lnw_ref, lnb_ref,
         y_ref, s_ref, p_s, sa_s, kk0_s, bt_s, kt_s, rd_s, kkd_s) = refs
    else:
        (r_ref, k_ref, v_ref, w_ref, a_ref, kkp_ref, kap_ref, _order_ref,
         y_ref, s_ref, p_s, sa_s, kk0_s, bt_s, kt_s, rd_s, kkd_s) = refs
    i = pl.program_id(0)
    tb = r_ref.shape[0]
    nv = RW_N // SUBLANES

    @pl.when(i == 0)
    def _():
        s_ref[...] = jnp.zeros_like(s_ref)
        p_s[...] = jnp.ones_like(p_s)

    def bcast(ref, *idx):
        k = idx[-1]
        row = ref[(*idx[:-1], pl.ds(k, 1), slice(None))]
        return jnp.broadcast_to(row, (SUBLANES, LANES))

    def time_index(s):
        s = jnp.minimum(s, tb - 1)
        return tb - 1 - s if backward else s

    def bf16_pair(x):
        hi = lax.bitcast_convert_type(x.astype(BF16).astype(F32), jnp.uint32)
        return lax.bitcast_convert_type(hi | (hi >> 16), F32)

    def norm_key(t):
        kkr = k_ref[t] * kkp_ref[...]
        nrm = jnp.sqrt(jnp.sum(kkr * kkr, axis=0, keepdims=True))
        return kkr / jnp.maximum(nrm, L2_EPS)

    def scaled_key(t, a):
        return k_ref[t] * (1.0 + (a - 1.0) * kap_ref[...])

    def prepare(s, slot):
        t = time_index(s)
        a = a_ref[t]
        kk = norm_key(t)
        p_prev = p_s[...]
        kkd_s[1 - slot] = bf16_pair(p_prev * kk)
        p = p_prev * jnp.where(s < tb, w_ref[t], 1.0)
        p_s[...] = p
        inv_p = 1.0 / p
        bt_s[slot] = bf16_pair(kk * a * inv_p)
        kt_s[slot] = bf16_pair(scaled_key(t, a) * inv_p)
        rd_s[slot] = bf16_pair(p * r_ref[t])

    def restart():
        kk0_s[...] = norm_key(time_index(0))
        acc = [None] * nv
        for k in range(RW_N):
            pb = bcast(p_s, k)
            kkb = bcast(kk0_s, k)
            for j in range(nv):
                rows = slice(SUBLANES * j, SUBLANES * (j + 1))
                sn = s_ref[k, rows, :] * pb
                s_ref[k, rows, :] = sn
                acc[j] = sn * kkb if acc[j] is None else acc[j] + sn * kkb
        sa_s[...] = -jnp.concatenate(acc, axis=0)
        p_s[...] = jnp.ones_like(p_s)

    def sweep(s, slot):
        t = time_index(s)
        zero = jnp.zeros((SUBLANES, LANES), F32)

        sa_p = [sa_s[2 * SUBLANES * m:2 * SUBLANES * (m + 1), :].astype(BF16) for m in range(nv // 2)]
        v_p = [v_ref[t, 2 * SUBLANES * m:2 * SUBLANES * (m + 1), :].astype(BF16) for m in range(nv // 2)]

        def key_block(kblk, carry):
            yacc, acc = list(carry[0]), list(carry[1])
            yb = [jnp.zeros((2 * SUBLANES, LANES), BF16)] * (nv // 2)
            ab = [jnp.zeros((2 * SUBLANES, LANES), BF16)] * (nv // 2)
            for kk in range(K_UNROLL):
                k = kblk * K_UNROLL + kk
                bb = pltpu.bitcast(bcast(bt_s, slot, k), BF16)
                kb = pltpu.bitcast(bcast(kt_s, slot, k), BF16)
                rb = pltpu.bitcast(bcast(rd_s, slot, k), BF16)
                kkn = pltpu.bitcast(bcast(kkd_s, slot, k), BF16)
                for m in range(nv // 2):
                    rows = slice(2 * SUBLANES * m, 2 * SUBLANES * (m + 1))
                    sn = s_ref[k, rows, :] + (sa_p[m] * bb + v_p[m] * kb).astype(F32)
                    s_ref[k, rows, :] = sn
                    snp = sn.astype(BF16)
                    yb[m] = yb[m] + snp * rb
                    ab[m] = ab[m] + snp * kkn
            for m in range(nv // 2):
                y32 = yb[m].astype(F32)
                a32 = ab[m].astype(F32)
                for h, j in enumerate((2 * m, 2 * m + 1)):
                    yacc[j] = yacc[j] + y32[SUBLANES * h:SUBLANES * (h + 1)]
                    acc[j] = acc[j] + a32[SUBLANES * h:SUBLANES * (h + 1)]
            return yacc, acc

        init = [zero] * nv
        n_kblk = RW_N // K_UNROLL
        carry = lax.fori_loop(0, n_kblk - 1, key_block, (init, init))
        yacc, acc = key_block(n_kblk - 1, carry)
        sa_s[...] = -jnp.concatenate(acc, axis=0)
        y = jnp.concatenate(yacc, axis=0)
        if backward:
            y = y + yf_ref[t]
            yc = y - jnp.mean(y, axis=0, keepdims=True)
            var = jnp.mean(yc * yc, axis=0, keepdims=True)
            y = yc * lax.rsqrt(var + RW_GN_EPS) * lnw_ref[...] + lnb_ref[...]
            k_sum = k_ref[t] * (2.0 + (af_ref[t] + a_ref[t] - 2.0) * kap_ref[...])
            y = y + jnp.sum(r_ref[t] * k_sum * rkp_ref[...], axis=0, keepdims=True) * v_ref[t]
        y_ref[t] = y

    restart()
    prepare(0, 0)
    prepare(1, 1)

    def pair(p, carry):
        s = 2 * p
        sweep(s, 0)
        prepare(s + 2, 0)
        sweep(s + 1, 1)
        prepare(s + 3, 1)
        return carry

    lax.fori_loop(0, tb // 2, pair, 0)


def _rw_scan(r_t, k_t, v_t, w_t, a_t, kkp, kap, n_ctx_blocks, *, run_after=None, readout=None):
    backward = readout is not None
    direction = int(backward)
    tt = r_t.shape[0]
    nblk = tt // SCAN_TILE
    seq = _seq_block(n_ctx_blocks, nblk)
    blk = lambda i: seq(direction, i)
    tile = (SCAN_TILE, RW_N, LANES)
    shared = pl.BlockSpec(tile, lambda i: (blk(i), 0, 0))
    const = pl.BlockSpec((RW_N, LANES), lambda i: (0, 0))
    y_blk = lambda i: jnp.where(i < n_ctx_blocks, blk(n_ctx_blocks), blk(i)) - n_ctx_blocks
    y_spec = pl.BlockSpec(tile, lambda i: (y_blk(i), 0, 0))
    in_specs = [shared, shared, shared, shared, shared, const, const]
    args = [r_t, k_t, v_t, w_t, a_t, kkp, kap]
    if backward:
        y_fwd, a_fwd, rkp, lnw, lnb = readout
        in_specs += [y_spec, shared, const, const, const]
        args += [y_fwd, a_fwd, rkp, lnw, lnb]
    else:
        in_specs += [pl.BlockSpec(memory_space=pl.ANY)]
        args += [run_after]
    return pl.pallas_call(
        functools.partial(_rw_scan_kernel, backward=backward),
        grid=(nblk,),
        in_specs=in_specs,
        out_specs=y_spec,
        out_shape=jax.ShapeDtypeStruct((tt - n_ctx_blocks * SCAN_TILE, RW_N, LANES), F32),
        scratch_shapes=[pltpu.VMEM((RW_N, RW_N, LANES), F32)] + [pltpu.VMEM((RW_N, LANES), F32)] * 3
                       + [pltpu.VMEM((2, RW_N, LANES), F32)] * 4,
        compiler_params=_params("arbitrary"),
        name="rw_scan_bwd" if backward else "rw_scan_fwd",
    )(*args)


def _hg_kernel(q_ref, f_ref, i_ref, lbp_ref, tri_ref, o_ref, st_ref, qd_scr, kv_scr, dec_scr):
    d = pl.program_id(1)
    i = pl.program_id(2)
    n_chunks = q_ref.shape[1] // HG_CHUNK

    @pl.when(i == 0)
    def _():
        st_ref[...] = jnp.zeros_like(st_ref)

    lbp = lbp_ref[...]
    e = jnp.exp(lbp - jnp.max(lbp, axis=0, keepdims=True))
    lb = e[0:1] / jnp.sum(e, axis=0, keepdims=True)
    t_idx = lax.broadcasted_iota(jnp.int32, (HG_CHUNK, HG_CHUNK), 0)
    s_idx = lax.broadcasted_iota(jnp.int32, (HG_CHUNK, HG_CHUNK), 1)
    nt = (((1,), (1,)), ((), ()))
    tn = (((0,), (0,)), ((), ()))

    def run(fwd):
        mask = (s_idx <= t_idx) if fwd else (s_idx >= t_idx)
        mid_row = HG_CHUNK // 2 - 1 if fwd else HG_CHUNK // 2
        last_row = HG_CHUNK - 1 if fwd else 0
        order = list(range(n_chunks)) if fwd else list(reversed(range(n_chunks)))

        f = lb + (1.0 - lb) * _sigmoid(f_ref[0])
        lf = jnp.log(f)
        hi = lf.astype(BF16)
        lo = (lf - hi.astype(F32)).astype(BF16)
        tri = tri_ref[0]
        b = _bdot(tri, hi) + _bdot(tri, lo)
        for c in order:
            rows = slice(c * HG_CHUNK, (c + 1) * HG_CHUNK)
            bc = b[rows]
            b_mid = bc[mid_row:mid_row + 1]
            b_last = bc[last_row:last_row + 1]
            q_in = q_ref[0, rows, :] * jnp.exp(bc - b_mid)
            k_in = (1.0 - f[rows]) * jnp.exp(b_mid - bc)
            qd_scr[rows, :] = (q_in * jnp.exp(b_mid)).astype(BF16)
            k_dec = (k_in * jnp.exp(b_last - b_mid)).astype(BF16)
            dec_scr[c:c + 1, :] = jnp.exp(b_last)
            q_in = q_in.astype(BF16)
            k_in = k_in.astype(BF16)
            vb = i_ref[0, rows, :].astype(BF16)
            heads = [slice(h * HG_N, (h + 1) * HG_N) for h in range(HG_HEADS)]
            scores = [lax.dot_general(q_in[:, cols], k_in[:, cols], nt, preferred_element_type=F32)
                      for cols in heads]
            for h, cols in enumerate(heads):
                kv_scr[c, h] = lax.dot_general(vb[:, cols], k_dec[:, cols], tn, preferred_element_type=F32)
            for h, cols in enumerate(heads):
                o_ref[0, 0, rows, cols] = _bdot(jnp.where(mask, scores[h], 0.0).astype(BF16), vb[:, cols])
        for c in order:
            rows = slice(c * HG_CHUNK, (c + 1) * HG_CHUNK)
            for h in range(HG_HEADS):
                cols = slice(h * HG_N, (h + 1) * HG_N)
                st = st_ref[h]
                o_ref[0, 0, rows, cols] += lax.dot_general(qd_scr[rows, cols], st.astype(BF16), nt,
                                                           preferred_element_type=F32)
                st_ref[h] = dec_scr[c:c + 1, cols] * st + kv_scr[c, h]

    @pl.when(d == 0)
    def _():
        run(True)

    @pl.when(d == 1)
    def _():
        run(False)


def _hg_tri(n_rows):
    t = jnp.arange(n_rows)[:, None]
    s = jnp.arange(n_rows)[None, :]
    same = (t // HG_CHUNK) == (s // HG_CHUNK)
    return jnp.stack([same & (s <= t), same & (s >= t)]).astype(BF16)


def _hg(zqi, zh, hg_lb, n_ctx):
    b, tt, _ = zh.shape
    w = HG_HEADS * HG_N
    nblk = tt // ROW_TILE
    n_chunks = ROW_TILE // HG_CHUNK
    blk = _seq_block(n_ctx // ROW_TILE, nblk)
    return pl.pallas_call(
        _hg_kernel,
        grid=(b, 2, nblk),
        in_specs=[pl.BlockSpec((1, ROW_TILE, w), lambda bi, d, i: (bi, blk(d, i), 0)),
                  pl.BlockSpec((1, ROW_TILE, w), lambda bi, d, i: (bi, blk(d, i), d)),
                  pl.BlockSpec((1, ROW_TILE, w), lambda bi, d, i: (bi, blk(d, i), 1)),
                  pl.BlockSpec(hg_lb.shape, lambda bi, d, i: (0, 0)),
                  pl.BlockSpec((1, ROW_TILE, ROW_TILE), lambda bi, d, i: (d, 0, 0))],
        out_specs=pl.BlockSpec((1, 1, ROW_TILE, w), lambda bi, d, i: (d, bi, blk(d, i), 0)),
        out_shape=jax.ShapeDtypeStruct((2, b, tt, w), F32),
        scratch_shapes=[pltpu.VMEM((HG_HEADS, HG_N, HG_N), F32),
                        pltpu.VMEM((ROW_TILE, w), BF16),
                        pltpu.VMEM((n_chunks, HG_HEADS, HG_N, HG_N), F32),
                        pltpu.VMEM((SUBLANES, w), F32)],
        compiler_params=_params("arbitrary", "arbitrary", "arbitrary"),
        name="hg",
    )(zqi, zh, zqi, hg_lb, _hg_tri(ROW_TILE))


def _merge_kernel(yrw_ref, g_ref, o_ref, gz_ref, zg_ref, x_ref, g1_ref, sh2_ref, sc2_ref, g2_ref, hnw_ref,
                  pa_ref, pb_ref, wo_ref, nw2_ref, w1_ref, w2_ref, fw_ref, out_ref):
    d = x_ref.shape[2]
    y_rw = (yrw_ref[...].T * g_ref[...]).astype(BF16)
    o = o_ref[0, 0] + o_ref[1, 0]
    gz = gz_ref[0]
    hnw = hnw_ref[...]
    parts = []
    for h in range(HG_HEADS):
        cols = slice(h * HG_N, (h + 1) * HG_N)
        oh = o[:, cols]
        ms = jnp.mean(oh * oh, axis=-1, keepdims=True)
        parts.append(oh * lax.rsqrt(ms + RMS_EPS) * hnw[:, cols])
    y_hg = (jnp.concatenate(parts, axis=1) * (gz * _sigmoid(gz))).astype(BF16)
    zg = zg_ref[0]
    m = _sigmoid(zg[:, :d]) * _bdot(y_rw, pa_ref[...]) + _sigmoid(zg[:, d:]) * _bdot(y_hg, pb_ref[...])
    x1 = x_ref[0] + g1_ref[0] * _bdot(m.astype(BF16), wo_ref[...])
    hb = _modulated_norm(x1, nw2_ref[...], sh2_ref[0], sc2_ref[0]).astype(BF16)
    u = jnp.maximum(_bdot(hb, w1_ref[...]), 0.0)
    y = x1 + g2_ref[0] * _bdot((u * u).astype(BF16), w2_ref[...])
    ms = jnp.mean(y * y, axis=-1, keepdims=True)
    out_ref[0] = y * lax.rsqrt(ms + RMS_EPS) * fw_ref[...]


def _merge(y_rw, g, o_hg, zh, zg, x, modcat, hnw, pa, pb, wo, nw2, w1, w2, fw, n_ctx):
    b, t, d = x.shape
    off = n_ctx // ROW_TILE
    lat = lambda bi, i: (bi, i, 0)
    cat = lambda bi, i: (bi, i + off, 0)
    mod = lambda col: pl.BlockSpec((1, 1, d), lambda bi, i: (2 * bi + 1, 0, col))
    const = lambda a: pl.BlockSpec(a.shape, lambda bi, i: (0,) * a.ndim, pipeline_mode=pl.Buffered(1))
    return pl.pallas_call(
        _merge_kernel,
        grid=(b, t // ROW_TILE),
        in_specs=[pl.BlockSpec((d, ROW_TILE), lambda bi, i: (bi, i)),
                  pl.BlockSpec((ROW_TILE, d), lambda bi, i: (i + off, bi)),
                  pl.BlockSpec((2, 1, ROW_TILE, d), lambda bi, i: (0, bi, i + off, 0)),
                  pl.BlockSpec((1, ROW_TILE, d), lambda bi, i: (bi, i + off, 2)),
                  pl.BlockSpec((1, ROW_TILE, 2 * d), cat),
                  pl.BlockSpec((1, ROW_TILE, d), lat),
                  mod(2), mod(3), mod(4), mod(5),
                  const(hnw), const(pa), const(pb), const(wo), const(nw2), const(w1), const(w2), const(fw)],
        out_specs=pl.BlockSpec((1, ROW_TILE, d), lat),
        out_shape=jax.ShapeDtypeStruct((b, t, d), F32),
        compiler_params=_params("arbitrary", "arbitrary"),
        name="merge_mlp",
    )(y_rw, g, o_hg, zh, zg, x, modcat, modcat, modcat, modcat, hnw, pa, pb, wo, nw2, w1, w2, fw)


def _to_scan(a):
    *lead, w, t = a.shape
    return jnp.swapaxes(a.reshape(*lead, w // RW_N, RW_N, t), -1, -3)


def _from_scan(a):
    t, n, bh = a.shape
    return jnp.swapaxes(a, 0, 2).reshape(bh * n, t)


def _head_tile(p, b):
    return jnp.tile(p.reshape(RW_HEADS, RW_N).T, (1, b))


def kernel(x, c, ctx, c_ctx, norm1_w, norm2_w, w_mod, b_mod, w_in, rw_mu, rw_w0, rw_w_up, rw_a0, rw_a_up, rw_g_up, rw_k_k, rw_k_a, rw_r_k, rw_ln_w, rw_ln_b, hg_lb, hg_norm_w, p_a, p_b, w_out, w_fc1, w_fc2, final_norm_w):
    b, t, d = x.shape
    n_ctx = ctx.shape[1]
    assert w_mod.shape[0] == 1, "single-layer block"
    assert b * RW_HEADS == LANES and n_ctx == ROW_TILE and t % ROW_TILE == 0
    assert d == RW_HEADS * RW_N == HG_HEADS * HG_N
    n_r = rw_mu.shape[2]
    rank_w, rank_a = rw_w_up.shape[2], rw_a_up.shape[2]
    assert rank_w + rank_a == LANES and n_r == 3 * d + 2 * LANES and w_in.shape[2] == n_r + 7 * d

    c_rows = jnp.zeros((2 * SUBLANES, d), F32).at[:b].set(c).at[b].set(c_ctx)
    mod = _mod(c_rows, w_mod[0], b_mod)
    modcat = jnp.stack([jnp.broadcast_to(mod[b], (b, N_MOD * d)), mod[:b]], axis=1).reshape(2 * b, 1, N_MOD * d)

    zr, zqi, zh, zg = _in_proj(ctx, x, norm1_w, modcat, w_in[0].astype(BF16), n_r)

    wup = jnp.pad(rw_w_up[0], ((0, 0), (0, rank_a), (0, 0))).astype(BF16)
    aup = jnp.pad(rw_a_up[0], ((0, 0), (rank_w, 0), (0, 0))).astype(BF16)
    *scan_ops, g = _rw_prep(zr, rw_mu[0], rw_w0[0], wup, rw_a0[0], aup, rw_g_up[0].astype(BF16), d)
    r_t, k_t, v_t, wf_t, wb_t, af_t, ab_t = (_to_scan(a) for a in scan_ops)
    o_hg = _hg(zqi, zh, hg_lb, n_ctx)
    kap = _head_tile(rw_k_a[0], b)
    tiles = (_head_tile(rw_k_k[0], b), kap, n_ctx // SCAN_TILE)
    y_fwd = _rw_scan(r_t, k_t, v_t, wf_t, af_t, *tiles, run_after=o_hg)
    y_rw_t = _rw_scan(r_t, k_t, v_t, wb_t, ab_t, *tiles,
                      readout=(y_fwd, af_t, _head_tile(rw_r_k[0].reshape(-1), b),
                               _head_tile(rw_ln_w[0], b), _head_tile(rw_ln_b[0], b)))
    y_rw = _from_scan(y_rw_t)

    return _merge(y_rw, g, o_hg, zh, zg, x, modcat, hg_norm_w, p_a[0].astype(BF16), p_b[0].astype(BF16),
                  w_out[0].astype(BF16), norm2_w, w_fc1[0].astype(BF16), w_fc2[0].astype(BF16),
                  final_norm_w.reshape(1, d), n_ctx)
```

```python
import functools

import jax
import jax.numpy as jnp
from jax import lax
from jax.experimental import pallas as pl
from jax.experimental.pallas import tpu as pltpu

F32 = jnp.float32
BF16 = jnp.bfloat16

GRID_W = 64
RW_HEADS = 16
RW_N = 64
HG_HEADS = 8
HG_N = 128
HG_CHUNK = 64
N_MOD = 6
RMS_EPS = 1e-6
RW_GN_EPS = 64e-5
L2_EPS = 1e-12
DECAY_SCALE = 0.6065306597126334

SUBLANES = 8
LANES = 128
ROW_TILE = 256
SCAN_TILE = 64
VMEM_LIMIT = 56 * 1024 * 1024


def _params(*sem):
    return pltpu.CompilerParams(dimension_semantics=sem, vmem_limit_bytes=VMEM_LIMIT)


def _bdot(a, b):
    return jnp.dot(a, b, preferred_element_type=F32)


def _sigmoid(x):
    return 1.0 / (1.0 + jnp.exp(-x))


def _mod_kernel(c_ref, w_ref, b_ref, o_ref):
    c = c_ref[...]
    act = c * _sigmoid(c)
    o_ref[...] = _bdot(act.astype(BF16), w_ref[...].astype(BF16)) + b_ref[...]


def _mod(c_rows, w_mod, b_mod):
    rows, d = c_rows.shape
    n = w_mod.shape[1]
    return pl.pallas_call(
        _mod_kernel,
        grid=(n // d,),
        in_specs=[pl.BlockSpec((rows, d), lambda j: (0, 0)),
                  pl.BlockSpec((d, d), lambda j: (0, j)),
                  pl.BlockSpec((1, d), lambda j: (0, j))],
        out_specs=pl.BlockSpec((rows, d), lambda j: (0, j)),
        out_shape=jax.ShapeDtypeStruct((rows, n), F32),
        compiler_params=_params("arbitrary"),
        name="mod",
    )(c_rows, w_mod, b_mod)


def _modulated_norm(x, nw, sh, sc):
    ms = jnp.mean(x * x, axis=-1, keepdims=True)
    return (x * lax.rsqrt(ms + RMS_EPS) * nw) * (1.0 + sc) + sh


def _in_proj_kernel(ctx_ref, x_ref, nw_ref, sh_ref, sc_ref, w_ref, zr_ref, zqi_ref, zh_ref, zg_ref):
    tokens = jnp.where(pl.program_id(1) == 0, ctx_ref[0], x_ref[0])
    hb = _modulated_norm(tokens, nw_ref[...], sh_ref[0], sc_ref[0]).astype(BF16)
    d = x_ref.shape[2]
    n_r = zr_ref.shape[2]
    col = lambda j0, j1: _bdot(hb, w_ref[:, n_r + j0 * d:n_r + j1 * d])
    zr_ref[0] = _bdot(hb, w_ref[:, :n_r])
    zqi_ref[0, :, :d] = col(0, 1).astype(BF16)
    zqi_ref[0, :, d:] = col(3, 4).astype(BF16)
    zh_ref[0, :, :2 * d] = col(1, 3)
    zh_ref[0, :, 2 * d:] = col(4, 5)
    zg_ref[0] = col(5, 7)


def _in_proj(ctx, x, nw, modcat, w_bf, n_r):
    b, t, d = x.shape
    tt = t + ctx.shape[1]
    nblk = tt // ROW_TILE
    mod_idx = lambda col: (lambda bi, i: (2 * bi + jnp.minimum(i, 1), 0, col))
    outs = ((n_r, F32), (2 * d, BF16), (3 * d, F32), (2 * d, F32))
    return pl.pallas_call(
        _in_proj_kernel,
        grid=(b, nblk),
        in_specs=[pl.BlockSpec((1, ROW_TILE, d), lambda bi, i: (bi, 0, 0)),
                  pl.BlockSpec((1, ROW_TILE, d), lambda bi, i: (bi, jnp.maximum(i - 1, 0), 0)),
                  pl.BlockSpec((1, d), lambda bi, i: (0, 0)),
                  pl.BlockSpec((1, 1, d), mod_idx(0)),
                  pl.BlockSpec((1, 1, d), mod_idx(1)),
                  pl.BlockSpec(w_bf.shape, lambda bi, i: (0, 0), pipeline_mode=pl.Buffered(1))],
        out_specs=[pl.BlockSpec((1, ROW_TILE, n), lambda bi, i: (bi, i, 0)) for n, _ in outs],
        out_shape=[jax.ShapeDtypeStruct((b, tt, n), dt) for n, dt in outs],
        compiler_params=_params("arbitrary", "arbitrary"),
        name="in_proj",
    )(ctx, x, nw, modcat, modcat, w_bf)


def _rw_prep_kernel(z_ref, zn_ref, mu_ref, w0_ref, wup_ref, a0_ref, aup_ref, gup_ref,
                    r_ref, k_ref, v_ref, wf_ref, wb_ref, af_ref, ab_ref, g_ref, zp_scr):
    i = pl.program_id(1)
    nblk = pl.num_programs(1)
    tt = z_ref.shape[1]
    d = r_ref.shape[0]
    is_lat = i > 0
    row = lax.broadcasted_iota(jnp.int32, (tt, 1), 0)
    col = row % GRID_W
    lmask = jnp.where(is_lat, col, row) == 0
    rmask = jnp.where(is_lat, col, row - (tt - GRID_W)) == GRID_W - 1
    latf = is_lat.astype(F32)
    up_ok = (i > 1).astype(F32)
    dn_ok = (i < nblk - 1).astype(F32)

    @pl.when(i == 0)
    def _():
        zp_scr[...] = jnp.zeros_like(zp_scr)

    def shifted(c0, c1):
        z = z_ref[0, :, c0:c1]
        mu = mu_ref[:, c0:c1]
        left = jnp.where(lmask, 0.0, pltpu.roll(z, 1, 0))
        right = jnp.where(rmask, 0.0, pltpu.roll(z, tt - 1, 0))
        up = jnp.concatenate([zp_scr[:, c0:c1] * up_ok, z[:tt - GRID_W]], axis=0)
        down = jnp.concatenate([z[GRID_W:], zn_ref[0, :, c0:c1] * dn_ok], axis=0)
        m_up, m_dn = latf * mu[2:3], latf * mu[3:4]
        m_self = 1.0 - mu[0:1] - mu[1:2] - m_up - m_dn
        return z * m_self + left * mu[0:1] + right * mu[1:2] + up * m_up + down * m_dn

    r_ref[...] = shifted(0, d).T
    k_ref[...] = shifted(d, 2 * d).T
    v_ref[...] = shifted(2 * d, 3 * d).T
    rest = shifted(3 * d, z_ref.shape[2])
    xwa = rest[:, :LANES]
    xw_t = jnp.tanh(xwa).astype(BF16)
    xa_b = xwa.astype(BF16)
    for dr, (wd_ref, ad_ref) in enumerate(((wf_ref, af_ref), (wb_ref, ab_ref))):
        wz = w0_ref[dr:dr + 1, :] + _bdot(xw_t, wup_ref[dr])
        wd_ref[...] = jnp.exp(-DECAY_SCALE * _sigmoid(wz)).T
        ad_ref[...] = _sigmoid(a0_ref[dr:dr + 1, :] + _bdot(xa_b, aup_ref[dr])).T
    g_ref[...] = _bdot(_sigmoid(rest[:, LANES:]).astype(BF16), gup_ref[...])
    zp_scr[...] = z_ref[0, tt - GRID_W:, :]


def _rw_prep(zr, mu, w0, wup, a0, aup, gup, d):
    b, tt, nr = zr.shape
    nblk = tt // ROW_TILE
    per = ROW_TILE // GRID_W
    last = tt // GRID_W - 1
    const = lambda shape: pl.BlockSpec(shape, lambda bi, i: (0,) * len(shape))
    row_spec = pl.BlockSpec((d, ROW_TILE), lambda bi, i: (bi, i))
    gate_spec = pl.BlockSpec((ROW_TILE, d), lambda bi, i: (i, bi))
    row_shape = jax.ShapeDtypeStruct((b * d, tt), F32)
    gate_shape = jax.ShapeDtypeStruct((tt, b * d), F32)
    return pl.pallas_call(
        _rw_prep_kernel,
        grid=(b, nblk),
        in_specs=[pl.BlockSpec((1, ROW_TILE, nr), lambda bi, i: (bi, i, 0)),
                  pl.BlockSpec((1, GRID_W, nr), lambda bi, i: (bi, jnp.minimum(i * per + per, last), 0)),
                  const(mu.shape), const(w0.shape), const(wup.shape), const(a0.shape),
                  const(aup.shape), const(gup.shape)],
        out_specs=[row_spec] * 7 + [gate_spec],
        out_shape=[row_shape] * 7 + [gate_shape],
        scratch_shapes=[pltpu.VMEM((GRID_W, nr), F32)],
        compiler_params=_params("arbitrary", "arbitrary"),
        name="rw_prep",
    )(zr, zr, mu, w0, wup, a0, aup, gup)


K_UNROLL = 32


def _seq_block(n_ctx_blocks, n_blocks):
    def blk(d, i):
        bwd = jnp.where(i < n_ctx_blocks, n_ctx_blocks - 1 - i, n_blocks - 1 + n_ctx_blocks - i)
        return jnp.where(d == 0, i, bwd)
    return blk


def _rw_scan_kernel(*refs, backward):
    if backward:
        (r_ref, k_ref, v_ref, w_ref, a_ref, kkp_ref, kap_ref, yf_ref, af_ref, rkp_ref, lnw_ref, lnb_ref,
         y_ref, s_ref, p_s, sa_s, kk0_s, bt_s, kt_s, rd_s, kkd_s) = refs
    else:
        (r_ref, k_ref, v_ref, w_ref, a_ref, kkp_ref, kap_ref, _order_ref,
         y_ref, s_ref, p_s, sa_s, kk0_s, bt_s, kt_s, rd_s, kkd_s) = refs
    i = pl.program_id(0)
    tb = r_ref.shape[0]
    nv = RW_N // SUBLANES

    @pl.when(i == 0)
    def _():
        s_ref[...] = jnp.zeros_like(s_ref)
        p_s[...] = jnp.ones_like(p_s)

    def bcast(ref, *idx):
        k = idx[-1]
        row = ref[(*idx[:-1], pl.ds(k, 1), slice(None))]
        return jnp.broadcast_to(row, (SUBLANES, LANES))

    def time_index(s):
        s = jnp.minimum(s, tb - 1)
        return tb - 1 - s if backward else s

    def bf16_pair(x):
        hi = lax.bitcast_convert_type(x.astype(BF16).astype(F32), jnp.uint32)
        return lax.bitcast_convert_type(hi | (hi >> 16), F32)

    def norm_key(t):
        kkr = k_ref[t] * kkp_ref[...]
        nrm = jnp.sqrt(jnp.sum(kkr * kkr, axis=0, keepdims=True))
        return kkr / jnp.maximum(nrm, L2_EPS)

    def scaled_key(t, a):
        return k_ref[t] * (1.0 + (a - 1.0) * kap_ref[...])

    def prepare(s, slot):
        t = time_index(s)
        a = a_ref[t]
        kk = norm_key(t)
        p_prev = p_s[...]
        kkd_s[1 - slot] = bf16_pair(p_prev * kk)
        p = p_prev * jnp.where(s < tb, w_ref[t], 1.0)
        p_s[...] = p
        inv_p = 1.0 / p
        bt_s[slot] = bf16_pair(kk * a * inv_p)
        kt_s[slot] = bf16_pair(scaled_key(t, a) * inv_p)
        rd_s[slot] = bf16_pair(p * r_ref[t])

    def restart():
        kk0_s[...] = norm_key(time_index(0))
        acc = [None] * nv
        for k in range(RW_N):
            pb = bcast(p_s, k)
            kkb = bcast(kk0_s, k)
            for j in range(nv):
                rows = slice(SUBLANES * j, SUBLANES * (j + 1))
                sn = s_ref[k, rows, :] * pb
                s_ref[k, rows, :] = sn
                acc[j] = sn * kkb if acc[j] is None else acc[j] + sn * kkb
        sa_s[...] = -jnp.concatenate(acc, axis=0)
        p_s[...] = jnp.ones_like(p_s)

    def sweep(s, slot):
        t = time_index(s)
        zero = jnp.zeros((SUBLANES, LANES), F32)

        sa_p = [sa_s[2 * SUBLANES * m:2 * SUBLANES * (m + 1), :].astype(BF16) for m in range(nv // 2)]
        v_p = [v_ref[t, 2 * SUBLANES * m:2 * SUBLANES * (m + 1), :].astype(BF16) for m in range(nv // 2)]

        def key_block(kblk, carry):
            yacc, acc = list(carry[0]), list(carry[1])
            yb = [jnp.zeros((2 * SUBLANES, LANES), BF16)] * (nv // 2)
            ab = [jnp.zeros((2 * SUBLANES, LANES), BF16)] * (nv // 2)
            for kk in range(K_UNROLL):
                k = kblk * K_UNROLL + kk
                bb = pltpu.bitcast(bcast(bt_s, slot, k), BF16)
                kb = pltpu.bitcast(bcast(kt_s, slot, k), BF16)
                rb = pltpu.bitcast(bcast(rd_s, slot, k), BF16)
                kkn = pltpu.bitcast(bcast(kkd_s, slot, k), BF16)
                for m in range(nv // 2):
                    rows = slice(2 * SUBLANES * m, 2 * SUBLANES * (m + 1))
                    sn = s_ref[k, rows, :] + (sa_p[m] * bb + v_p[m] * kb).astype(F32)
                    s_ref[k, rows, :] = sn
                    snp = sn.astype(BF16)
                    yb[m] = yb[m] + snp * rb
                    ab[m] = ab[m] + snp * kkn
            for m in range(nv // 2):
                y32 = yb[m].astype(F32)
                a32 = ab[m].astype(F32)
                for h, j in enumerate((2 * m, 2 * m + 1)):
                    yacc[j] = yacc[j] + y32[SUBLANES * h:SUBLANES * (h + 1)]
                    acc[j] = acc[j] + a32[SUBLANES * h:SUBLANES * (h + 1)]
            return yacc, acc

        init = [zero] * nv
        n_kblk = RW_N // K_UNROLL
        carry = lax.fori_loop(0, n_kblk - 1, key_block, (init, init))
        yacc, acc = key_block(n_kblk - 1, carry)
        sa_s[...] = -jnp.concatenate(acc, axis=0)
        y = jnp.concatenate(yacc, axis=0)
        if backward:
            y = y + yf_ref[t]
            yc = y - jnp.mean(y, axis=0, keepdims=True)
            var = jnp.mean(yc * yc, axis=0, keepdims=True)
            y = yc * lax.rsqrt(var + RW_GN_EPS) * lnw_ref[...] + lnb_ref[...]
            k_sum = k_ref[t] * (2.0 + (af_ref[t] + a_ref[t] - 2.0) * kap_ref[...])
            y = y + jnp.sum(r_ref[t] * k_sum * rkp_ref[...], axis=0, keepdims=True) * v_ref[t]
        y_ref[t] = y

    restart()
    prepare(0, 0)
    prepare(1, 1)

    def pair(p, carry):
        s = 2 * p
        sweep(s, 0)
        prepare(s + 2, 0)
        sweep(s + 1, 1)
        prepare(s + 3, 1)
        return carry

    lax.fori_loop(0, tb // 2, pair, 0)


def _rw_scan(r_t, k_t, v_t, w_t, a_t, kkp, kap, n_ctx_blocks, *, run_after=None, readout=None):
    backward = readout is not None
    direction = int(backward)
    tt = r_t.shape[0]
    nblk = tt // SCAN_TILE
    seq = _seq_block(n_ctx_blocks, nblk)
    blk = lambda i: seq(direction, i)
    tile = (SCAN_TILE, RW_N, LANES)
    shared = pl.BlockSpec(tile, lambda i: (blk(i), 0, 0))
    const = pl.BlockSpec((RW_N, LANES), lambda i: (0, 0))
    y_blk = lambda i: jnp.where(i < n_ctx_blocks, blk(n_ctx_blocks), blk(i)) - n_ctx_blocks
    y_spec = pl.BlockSpec(tile, lambda i: (y_blk(i), 0, 0))
    in_specs = [shared, shared, shared, shared, shared, const, const]
    args = [r_t, k_t, v_t, w_t, a_t, kkp, kap]
    if backward:
        y_fwd, a_fwd, rkp, lnw, lnb = readout
        in_specs += [y_spec, shared, const, const, const]
        args += [y_fwd, a_fwd, rkp, lnw, lnb]
    else:
        in_specs += [pl.BlockSpec(memory_space=pl.ANY)]
        args += [run_after]
    return pl.pallas_call(
        functools.partial(_rw_scan_kernel, backward=backward),
        grid=(nblk,),
        in_specs=in_specs,
        out_specs=y_spec,
        out_shape=jax.ShapeDtypeStruct((tt - n_ctx_blocks * SCAN_TILE, RW_N, LANES), F32),
        scratch_shapes=[pltpu.VMEM((RW_N, RW_N, LANES), F32)] + [pltpu.VMEM((RW_N, LANES), F32)] * 3
                       + [pltpu.VMEM((2, RW_N, LANES), F32)] * 4,
        compiler_params=_params("arbitrary"),
        name="rw_scan_bwd" if backward else "rw_scan_fwd",
    )(*args)


def _hg_kernel(q_ref, f_ref, i_ref, lbp_ref, tri_ref, o_ref, st_ref, qd_scr, kv_scr, dec_scr):
    d = pl.program_id(1)
    i = pl.program_id(2)
    n_chunks = q_ref.shape[1] // HG_CHUNK

    @pl.when(i == 0)
    def _():
        st_ref[...] = jnp.zeros_like(st_ref)

    lbp = lbp_ref[...]
    e = jnp.exp(lbp - jnp.max(lbp, axis=0, keepdims=True))
    lb = e[0:1] / jnp.sum(e, axis=0, keepdims=True)
    t_idx = lax.broadcasted_iota(jnp.int32, (HG_CHUNK, HG_CHUNK), 0)
    s_idx = lax.broadcasted_iota(jnp.int32, (HG_CHUNK, HG_CHUNK), 1)
    nt = (((1,), (1,)), ((), ()))
    tn = (((0,), (0,)), ((), ()))

    def run(fwd):
        mask = (s_idx <= t_idx) if fwd else (s_idx >= t_idx)
        mid_row = HG_CHUNK // 2 - 1 if fwd else HG_CHUNK // 2
        last_row = HG_CHUNK - 1 if fwd else 0
        order = list(range(n_chunks)) if fwd else list(reversed(range(n_chunks)))

        f = lb + (1.0 - lb) * _sigmoid(f_ref[0])
        lf = jnp.log(f)
        hi = lf.astype(BF16)
        lo = (lf - hi.astype(F32)).astype(BF16)
        tri = tri_ref[0]
        b = _bdot(tri, hi) + _bdot(tri, lo)
        for c in order:
            rows = slice(c * HG_CHUNK, (c + 1) * HG_CHUNK)
            bc = b[rows]
            b_mid = bc[mid_row:mid_row + 1]
            b_last = bc[last_row:last_row + 1]
            q_in = q_ref[0, rows, :] * jnp.exp(bc - b_mid)
            k_in = (1.0 - f[rows]) * jnp.exp(b_mid - bc)
            qd_scr[rows, :] = (q_in * jnp.exp(b_mid)).astype(BF16)
            k_dec = (k_in * jnp.exp(b_last - b_mid)).astype(BF16)
            dec_scr[c:c + 1, :] = jnp.exp(b_last)
            q_in = q_in.astype(BF16)
            k_in = k_in.astype(BF16)
            vb = i_ref[0, rows, :].astype(BF16)
            heads = [slice(h * HG_N, (h + 1) * HG_N) for h in range(HG_HEADS)]
            scores = [lax.dot_general(q_in[:, cols], k_in[:, cols], nt, preferred_element_type=F32)
                      for cols in heads]
            for h, cols in enumerate(heads):
                kv_scr[c, h] = lax.dot_general(vb[:, cols], k_dec[:, cols], tn, preferred_element_type=F32)
            for h, cols in enumerate(heads):
                o_ref[0, 0, rows, cols] = _bdot(jnp.where(mask, scores[h], 0.0).astype(BF16), vb[:, cols])
        for c in order:
            rows = slice(c * HG_CHUNK, (c + 1) * HG_CHUNK)
            for h in range(HG_HEADS):
                cols = slice(h * HG_N, (h + 1) * HG_N)
                st = st_ref[h]
                o_ref[0, 0, rows, cols] += lax.dot_general(qd_scr[rows, cols], st.astype(BF16), nt,
                                                           preferred_element_type=F32)
                st_ref[h] = dec_scr[c:c + 1, cols] * st + kv_scr[c, h]

    @pl.when(d == 0)
    def _():
        run(True)

    @pl.when(d == 1)
    def _():
        run(False)


def _hg_tri(n_rows):
    t = jnp.arange(n_rows)[:, None]
    s = jnp.arange(n_rows)[None, :]
    same = (t // HG_CHUNK) == (s // HG_CHUNK)
    return jnp.stack([same & (s <= t), same & (s >= t)]).astype(BF16)


def _hg(zqi, zh, hg_lb, n_ctx):
    b, tt, _ = zh.shape
    w = HG_HEADS * HG_N
    nblk = tt // ROW_TILE
    n_chunks = ROW_TILE // HG_CHUNK
    blk = _seq_block(n_ctx // ROW_TILE, nblk)
    return pl.pallas_call(
        _hg_kernel,
        grid=(b, 2, nblk),
        in_specs=[pl.BlockSpec((1, ROW_TILE, w), lambda bi, d, i: (bi, blk(d, i), 0)),
                  pl.BlockSpec((1, ROW_TILE, w), lambda bi, d, i: (bi, blk(d, i), d)),
                  pl.BlockSpec((1, ROW_TILE, w), lambda bi, d, i: (bi, blk(d, i), 1)),
                  pl.BlockSpec(hg_lb.shape, lambda bi, d, i: (0, 0)),
                  pl.BlockSpec((1, ROW_TILE, ROW_TILE), lambda bi, d, i: (d, 0, 0))],
        out_specs=pl.BlockSpec((1, 1, ROW_TILE, w), lambda bi, d, i: (d, bi, blk(d, i), 0)),
        out_shape=jax.ShapeDtypeStruct((2, b, tt, w), F32),
        scratch_shapes=[pltpu.VMEM((HG_HEADS, HG_N, HG_N), F32),
                        pltpu.VMEM((ROW_TILE, w), BF16),
                        pltpu.VMEM((n_chunks, HG_HEADS, HG_N, HG_N), F32),
                        pltpu.VMEM((SUBLANES, w), F32)],
        compiler_params=_params("arbitrary", "arbitrary", "arbitrary"),
        name="hg",
    )(zqi, zh, zqi, hg_lb, _hg_tri(ROW_TILE))


def _merge_kernel(yrw_ref, g_ref, o_ref, gz_ref, zg_ref, x_ref, g1_ref, sh2_ref, sc2_ref, g2_ref, hnw_ref,
                  pa_ref, pb_ref, wo_ref, nw2_ref, w1_ref, w2_ref, fw_ref, out_ref):
    d = x_ref.shape[2]
    y_rw = (yrw_ref[...].T * g_ref[...]).astype(BF16)
    o = o_ref[0, 0] + o_ref[1, 0]
    gz = gz_ref[0]
    hnw = hnw_ref[...]
    parts = []
    for h in range(HG_HEADS):
        cols = slice(h * HG_N, (h + 1) * HG_N)
        oh = o[:, cols]
        ms = jnp.mean(oh * oh, axis=-1, keepdims=True)
        parts.append(oh * lax.rsqrt(ms + RMS_EPS) * hnw[:, cols])
    y_hg = (jnp.concatenate(parts, axis=1) * (gz * _sigmoid(gz))).astype(BF16)
    zg = zg_ref[0]
    m = _sigmoid(zg[:, :d]) * _bdot(y_rw, pa_ref[...]) + _sigmoid(zg[:, d:]) * _bdot(y_hg, pb_ref[...])
    x1 = x_ref[0] + g1_ref[0] * _bdot(m.astype(BF16), wo_ref[...])
    hb = _modulated_norm(x1, nw2_ref[...], sh2_ref[0], sc2_ref[0]).astype(BF16)
    u = jnp.maximum(_bdot(hb, w1_ref[...]), 0.0)
    y = x1 + g2_ref[0] * _bdot((u * u).astype(BF16), w2_ref[...])
    ms = jnp.mean(y * y, axis=-1, keepdims=True)
    out_ref[0] = y * lax.rsqrt(ms + RMS_EPS) * fw_ref[...]


def _merge(y_rw, g, o_hg, zh, zg, x, modcat, hnw, pa, pb, wo, nw2, w1, w2, fw, n_ctx):
    b, t, d = x.shape
    off = n_ctx // ROW_TILE
    lat = lambda bi, i: (bi, i, 0)
    cat = lambda bi, i: (bi, i + off, 0)
    mod = lambda col: pl.BlockSpec((1, 1, d), lambda bi, i: (2 * bi + 1, 0, col))
    const = lambda a: pl.BlockSpec(a.shape, lambda bi, i: (0,) * a.ndim, pipeline_mode=pl.Buffered(1))
    return pl.pallas_call(
        _merge_kernel,
        grid=(b, t // ROW_TILE),
        in_specs=[pl.BlockSpec((d, ROW_TILE), lambda bi, i: (bi, i)),
                  pl.BlockSpec((ROW_TILE, d), lambda bi, i: (i + off, bi)),
                  pl.BlockSpec((2, 1, ROW_TILE, d), lambda bi, i: (0, bi, i + off, 0)),
                  pl.BlockSpec((1, ROW_TILE, d), lambda bi, i: (bi, i + off, 2)),
                  pl.BlockSpec((1, ROW_TILE, 2 * d), cat),
                  pl.BlockSpec((1, ROW_TILE, d), lat),
                  mod(2), mod(3), mod(4), mod(5),
                  const(hnw), const(pa), const(pb), const(wo), const(nw2), const(w1), const(w2), const(fw)],
        out_specs=pl.BlockSpec((1, ROW_TILE, d), lat),
        out_shape=jax.ShapeDtypeStruct((b, t, d), F32),
        compiler_params=_params("arbitrary", "arbitrary"),
        name="merge_mlp",
    )(y_rw, g, o_hg, zh, zg, x, modcat, modcat, modcat, modcat, hnw, pa, pb, wo, nw2, w1, w2, fw)


def _to_scan(a):
    *lead, w, t = a.shape
    return jnp.swapaxes(a.reshape(*lead, w // RW_N, RW_N, t), -1, -3)


def _from_scan(a):
    t, n, bh = a.shape
    return jnp.swapaxes(a, 0, 2).reshape(bh * n, t)


def _head_tile(p, b):
    return jnp.tile(p.reshape(RW_HEADS, RW_N).T, (1, b))


def kernel(x, c, ctx, c_ctx, norm1_w, norm2_w, w_mod, b_mod, w_in, rw_mu, rw_w0, rw_w_up, rw_a0, rw_a_up, rw_g_up, rw_k_k, rw_k_a, rw_r_k, rw_ln_w, rw_ln_b, hg_lb, hg_norm_w, p_a, p_b, w_out, w_fc1, w_fc2, final_norm_w):
    b, t, d = x.shape
    n_ctx = ctx.shape[1]
    assert w_mod.shape[0] == 1, "single-layer block"
    assert b * RW_HEADS == LANES and n_ctx == ROW_TILE and t % ROW_TILE == 0
    assert d == RW_HEADS * RW_N == HG_HEADS * HG_N
    n_r = rw_mu.shape[2]
    rank_w, rank_a = rw_w_up.shape[2], rw_a_up.shape[2]
    assert rank_w + rank_a == LANES and n_r == 3 * d + 2 * LANES and w_in.shape[2] == n_r + 7 * d

    c_rows = jnp.zeros((2 * SUBLANES, d), F32).at[:b].set(c).at[b].set(c_ctx)
    mod = _mod(c_rows, w_mod[0], b_mod)
    modcat = jnp.stack([jnp.broadcast_to(mod[b], (b, N_MOD * d)), mod[:b]], axis=1).reshape(2 * b, 1, N_MOD * d)

    zr, zqi, zh, zg = _in_proj(ctx, x, norm1_w, modcat, w_in[0].astype(BF16), n_r)

    wup = jnp.pad(rw_w_up[0], ((0, 0), (0, rank_a), (0, 0))).astype(BF16)
    aup = jnp.pad(rw_a_up[0], ((0, 0), (rank_w, 0), (0, 0))).astype(BF16)
    *scan_ops, g = _rw_prep(zr, rw_mu[0], rw_w0[0], wup, rw_a0[0], aup, rw_g_up[0].astype(BF16), d)
    r_t, k_t, v_t, wf_t, wb_t, af_t, ab_t = (_to_scan(a) for a in scan_ops)
    o_hg = _hg(zqi, zh, hg_lb, n_ctx)
    kap = _head_tile(rw_k_a[0], b)
    tiles = (_head_tile(rw_k_k[0], b), kap, n_ctx // SCAN_TILE)
    y_fwd = _rw_scan(r_t, k_t, v_t, wf_t, af_t, *tiles, run_after=o_hg)
    y_rw_t = _rw_scan(r_t, k_t, v_t, wb_t, ab_t, *tiles,
                      readout=(y_fwd, af_t, _head_tile(rw_r_k[0].reshape(-1), b),
                               _head_tile(rw_ln_w[0], b), _head_tile(rw_ln_b[0], b)))
    y_rw_t, late_w = lax.optimization_barrier((y_rw_t, (p_a, p_b, w_out, w_fc1, w_fc2)))
    y_rw = _from_scan(y_rw_t)
    pa, pb, wo, w1, w2 = (w[0].astype(BF16) for w in late_w)
    return _merge(y_rw, g, o_hg, zh, zg, x, modcat, hg_norm_w, pa, pb, wo, norm2_w, w1, w2,
                  final_norm_w.reshape(1, d), n_ctx)
```

```python
import functools

import jax
import jax.numpy as jnp
from jax import lax
from jax.experimental import pallas as pl
from jax.experimental.pallas import tpu as pltpu

F32 = jnp.float32
BF16 = jnp.bfloat16

GRID_W = 64
RW_HEADS = 16
RW_N = 64
HG_HEADS = 8
HG_N = 128
HG_CHUNK = 64
N_MOD = 6
RMS_EPS = 1e-6
RW_GN_EPS = 64e-5
L2_EPS = 1e-12
DECAY_SCALE = 0.6065306597126334

SUBLANES = 8
LANES = 128
ROW_TILE = 256
SCAN_TILE = 64
VMEM_LIMIT = 56 * 1024 * 1024


def _params(*sem):
    return pltpu.CompilerParams(dimension_semantics=sem, vmem_limit_bytes=VMEM_LIMIT)


def _bdot(a, b):
    return jnp.dot(a, b, preferred_element_type=F32)


def _sigmoid(x):
    return 1.0 / (1.0 + jnp.exp(-x))


def _mod_kernel(c_ref, w_ref, b_ref, o_ref):
    c = c_ref[...]
    act = c * _sigmoid(c)
    o_ref[...] = _bdot(act.astype(BF16), w_ref[...].astype(BF16)) + b_ref[...]


def _mod(c_rows, w_mod, b_mod):
    rows, d = c_rows.shape
    n = w_mod.shape[1]
    return pl.pallas_call(
        _mod_kernel,
        grid=(n // d,),
        in_specs=[pl.BlockSpec((rows, d), lambda j: (0, 0)),
                  pl.BlockSpec((d, d), lambda j: (0, j)),
                  pl.BlockSpec((1, d), lambda j: (0, j))],
        out_specs=pl.BlockSpec((rows, d), lambda j: (0, j)),
        out_shape=jax.ShapeDtypeStruct((rows, n), F32),
        compiler_params=_params("arbitrary"),
        name="mod",
    )(c_rows, w_mod, b_mod)


def _modulated_norm(x, nw, sh, sc):
    ms = jnp.mean(x * x, axis=-1, keepdims=True)
    return (x * lax.rsqrt(ms + RMS_EPS) * nw) * (1.0 + sc) + sh


def _in_proj_kernel(ctx_ref, x_ref, nw_ref, sh_ref, sc_ref, w_ref, zr_ref, zqi_ref, zh_ref, zg_ref):
    tokens = jnp.where(pl.program_id(1) == 0, ctx_ref[0], x_ref[0])
    hb = _modulated_norm(tokens, nw_ref[...], sh_ref[0], sc_ref[0]).astype(BF16)
    d = x_ref.shape[2]
    n_r = zr_ref.shape[2]
    col = lambda j0, j1: _bdot(hb, w_ref[:, n_r + j0 * d:n_r + j1 * d])
    zr_ref[0] = _bdot(hb, w_ref[:, :n_r])
    zqi_ref[0, :, :d] = col(0, 1).astype(BF16)
    zqi_ref[0, :, d:] = col(3, 4).astype(BF16)
    zh_ref[0, :, :2 * d] = col(1, 3)
    zh_ref[0, :, 2 * d:] = col(4, 5)
    zg_ref[0] = col(5, 7)


def _in_proj(ctx, x, nw, modcat, w_bf, n_r):
    b, t, d = x.shape
    tt = t + ctx.shape[1]
    nblk = tt // ROW_TILE
    mod_idx = lambda col: (lambda bi, i: (2 * bi + jnp.minimum(i, 1), 0, col))
    outs = ((n_r, F32), (2 * d, BF16), (3 * d, F32), (2 * d, F32))
    return pl.pallas_call(
        _in_proj_kernel,
        grid=(b, nblk),
        in_specs=[pl.BlockSpec((1, ROW_TILE, d), lambda bi, i: (bi, 0, 0)),
                  pl.BlockSpec((1, ROW_TILE, d), lambda bi, i: (bi, jnp.maximum(i - 1, 0), 0)),
                  pl.BlockSpec((1, d), lambda bi, i: (0, 0)),
                  pl.BlockSpec((1, 1, d), mod_idx(0)),
                  pl.BlockSpec((1, 1, d), mod_idx(1)),
                  pl.BlockSpec(w_bf.shape, lambda bi, i: (0, 0), pipeline_mode=pl.Buffered(1))],
        out_specs=[pl.BlockSpec((1, ROW_TILE, n), lambda bi, i: (bi, i, 0)) for n, _ in outs],
        out_shape=[jax.ShapeDtypeStruct((b, tt, n), dt) for n, dt in outs],
        compiler_params=_params("arbitrary", "arbitrary"),
        name="in_proj",
    )(ctx, x, nw, modcat, modcat, w_bf)


def _rw_prep_kernel(z_ref, zn_ref, mu_ref, w0_ref, wup_ref, a0_ref, aup_ref, gup_ref,
                    r_ref, k_ref, v_ref, wf_ref, wb_ref, af_ref, ab_ref, g_ref, zp_scr):
    i = pl.program_id(1)
    nblk = pl.num_programs(1)
    tt = z_ref.shape[1]
    d = r_ref.shape[0]
    is_lat = i > 0
    row = lax.broadcasted_iota(jnp.int32, (tt, 1), 0)
    col = row % GRID_W
    lmask = jnp.where(is_lat, col, row) == 0
    rmask = jnp.where(is_lat, col, row - (tt - GRID_W)) == GRID_W - 1
    latf = is_lat.astype(F32)
    up_ok = (i > 1).astype(F32)
    dn_ok = (i < nblk - 1).astype(F32)

    @pl.when(i == 0)
    def _():
        zp_scr[...] = jnp.zeros_like(zp_scr)

    def shifted(c0, c1):
        z = z_ref[0, :, c0:c1]
        mu = mu_ref[:, c0:c1]
        left = jnp.where(lmask, 0.0, pltpu.roll(z, 1, 0))
        right = jnp.where(rmask, 0.0, pltpu.roll(z, tt - 1, 0))
        up = jnp.concatenate([zp_scr[:, c0:c1] * up_ok, z[:tt - GRID_W]], axis=0)
        down = jnp.concatenate([z[GRID_W:], zn_ref[0, :, c0:c1] * dn_ok], axis=0)
        m_up, m_dn = latf * mu[2:3], latf * mu[3:4]
        m_self = 1.0 - mu[0:1] - mu[1:2] - m_up - m_dn
        return z * m_self + left * mu[0:1] + right * mu[1:2] + up * m_up + down * m_dn

    r_ref[...] = shifted(0, d).T
    k_ref[...] = shifted(d, 2 * d).T
    v_ref[...] = shifted(2 * d, 3 * d).T
    rest = shifted(3 * d, z_ref.shape[2])
    xwa = rest[:, :LANES]
    xw_t = jnp.tanh(xwa).astype(BF16)
    xa_b = xwa.astype(BF16)
    for dr, (wd_ref, ad_ref) in enumerate(((wf_ref, af_ref), (wb_ref, ab_ref))):
        wz = w0_ref[dr:dr + 1, :] + _bdot(xw_t, wup_ref[dr])
        wd_ref[...] = jnp.exp(-DECAY_SCALE * _sigmoid(wz)).T
        ad_ref[...] = _sigmoid(a0_ref[dr:dr + 1, :] + _bdot(xa_b, aup_ref[dr])).T
    g_ref[...] = _bdot(_sigmoid(rest[:, LANES:]).astype(BF16), gup_ref[...])
    zp_scr[...] = z_ref[0, tt - GRID_W:, :]


def _rw_prep(zr, mu, w0, wup, a0, aup, gup, d):
    b, tt, nr = zr.shape
    nblk = tt // ROW_TILE
    per = ROW_TILE // GRID_W
    last = tt // GRID_W - 1
    const = lambda shape: pl.BlockSpec(shape, lambda bi, i: (0,) * len(shape))
    row_spec = pl.BlockSpec((d, ROW_TILE), lambda bi, i: (bi, i))
    gate_spec = pl.BlockSpec((ROW_TILE, d), lambda bi, i: (i, bi))
    row_shape = jax.ShapeDtypeStruct((b * d, tt), F32)
    gate_shape = jax.ShapeDtypeStruct((tt, b * d), F32)
    return pl.pallas_call(
        _rw_prep_kernel,
        grid=(b, nblk),
        in_specs=[pl.BlockSpec((1, ROW_TILE, nr), lambda bi, i: (bi, i, 0)),
                  pl.BlockSpec((1, GRID_W, nr), lambda bi, i: (bi, jnp.minimum(i * per + per, last), 0)),
                  const(mu.shape), const(w0.shape), const(wup.shape), const(a0.shape),
                  const(aup.shape), const(gup.shape)],
        out_specs=[row_spec] * 7 + [gate_spec],
        out_shape=[row_shape] * 7 + [gate_shape],
        scratch_shapes=[pltpu.VMEM((GRID_W, nr), F32)],
        compiler_params=_params("arbitrary", "arbitrary"),
        name="rw_prep",
    )(zr, zr, mu, w0, wup, a0, aup, gup)


K_UNROLL = 32
STEPS_PER_ITER = 4


def _seq_block(n_ctx_blocks, n_blocks):
    def blk(d, i):
        bwd = jnp.where(i < n_ctx_blocks, n_ctx_blocks - 1 - i, n_blocks - 1 + n_ctx_blocks - i)
        return jnp.where(d == 0, i, bwd)
    return blk


def _rw_scan_kernel(*refs, backward):
    if backward:
        (r_ref, k_ref, v_ref, w_ref, a_ref, kkp_ref, kap_ref, yf_ref, af_ref, rkp_ref, lnw_ref, lnb_ref,
         y_ref, s_ref, p_s, sa_s, kk0_s, bt_s, kt_s, rd_s, kkd_s) = refs
    else:
        (r_ref, k_ref, v_ref, w_ref, a_ref, kkp_ref, kap_ref, _order_ref,
         y_ref, s_ref, p_s, sa_s, kk0_s, bt_s, kt_s, rd_s, kkd_s) = refs
    i = pl.program_id(0)
    tb = r_ref.shape[0]
    nv = RW_N // SUBLANES

    @pl.when(i == 0)
    def _():
        s_ref[...] = jnp.zeros_like(s_ref)
        p_s[...] = jnp.ones_like(p_s)

    def bcast(ref, *idx):
        k = idx[-1]
        row = ref[(*idx[:-1], pl.ds(k, 1), slice(None))]
        return jnp.broadcast_to(row, (SUBLANES, LANES))

    def time_index(s):
        s = jnp.minimum(s, tb - 1)
        return tb - 1 - s if backward else s

    def bf16_pair(x):
        hi = lax.bitcast_convert_type(x.astype(BF16).astype(F32), jnp.uint32)
        return lax.bitcast_convert_type(hi | (hi >> 16), F32)

    def norm_key(t):
        kkr = k_ref[t] * kkp_ref[...]
        nrm = jnp.sqrt(jnp.sum(kkr * kkr, axis=0, keepdims=True))
        return kkr / jnp.maximum(nrm, L2_EPS)

    def scaled_key(t, a):
        return k_ref[t] * (1.0 + (a - 1.0) * kap_ref[...])

    def prepare(s, slot):
        t = time_index(s)
        a = a_ref[t]
        kk = norm_key(t)
        p_prev = p_s[...]
        kkd_s[1 - slot] = bf16_pair(p_prev * kk)
        p = p_prev * jnp.where(s < tb, w_ref[t], 1.0)
        p_s[...] = p
        inv_p = 1.0 / p
        bt_s[slot] = bf16_pair(kk * a * inv_p)
        kt_s[slot] = bf16_pair(scaled_key(t, a) * inv_p)
        rd_s[slot] = bf16_pair(p * r_ref[t])

    def restart():
        kk0_s[...] = norm_key(time_index(0))
        acc = [None] * nv
        for k in range(RW_N):
            pb = bcast(p_s, k)
            kkb = bcast(kk0_s, k)
            for j in range(nv):
                rows = slice(SUBLANES * j, SUBLANES * (j + 1))
                sn = s_ref[k, rows, :] * pb
                s_ref[k, rows, :] = sn
                acc[j] = sn * kkb if acc[j] is None else acc[j] + sn * kkb
        sa_s[...] = -jnp.concatenate(acc, axis=0)
        p_s[...] = jnp.ones_like(p_s)

    def sweep(s, slot):
        t = time_index(s)
        zero = jnp.zeros((SUBLANES, LANES), F32)

        sa_p = [sa_s[2 * SUBLANES * m:2 * SUBLANES * (m + 1), :].astype(BF16) for m in range(nv // 2)]
        v_p = [v_ref[t, 2 * SUBLANES * m:2 * SUBLANES * (m + 1), :].astype(BF16) for m in range(nv // 2)]

        def key_block(kblk, carry):
            yacc, acc = list(carry[0]), list(carry[1])
            yb = [jnp.zeros((2 * SUBLANES, LANES), BF16)] * (nv // 2)
            ab = [jnp.zeros((2 * SUBLANES, LANES), BF16)] * (nv // 2)
            for kk in range(K_UNROLL):
                k = kblk * K_UNROLL + kk
                bb = pltpu.bitcast(bcast(bt_s, slot, k), BF16)
                kb = pltpu.bitcast(bcast(kt_s, slot, k), BF16)
                rb = pltpu.bitcast(bcast(rd_s, slot, k), BF16)
                kkn = pltpu.bitcast(bcast(kkd_s, slot, k), BF16)
                for m in range(nv // 2):
                    rows = slice(2 * SUBLANES * m, 2 * SUBLANES * (m + 1))
                    sn = s_ref[k, rows, :] + (sa_p[m] * bb + v_p[m] * kb).astype(F32)
                    s_ref[k, rows, :] = sn
                    snp = sn.astype(BF16)
                    yb[m] = yb[m] + snp * rb
                    ab[m] = ab[m] + snp * kkn
            for m in range(nv // 2):
                y32 = yb[m].astype(F32)
                a32 = ab[m].astype(F32)
                for h, j in enumerate((2 * m, 2 * m + 1)):
                    yacc[j] = yacc[j] + y32[SUBLANES * h:SUBLANES * (h + 1)]
                    acc[j] = acc[j] + a32[SUBLANES * h:SUBLANES * (h + 1)]
            return yacc, acc

        init = [zero] * nv
        n_kblk = RW_N // K_UNROLL
        carry = lax.fori_loop(0, n_kblk - 1, key_block, (init, init))
        yacc, acc = key_block(n_kblk - 1, carry)
        sa_s[...] = -jnp.concatenate(acc, axis=0)
        y = jnp.concatenate(yacc, axis=0)
        if backward:
            y = y + yf_ref[t]
            yc = y - jnp.mean(y, axis=0, keepdims=True)
            var = jnp.mean(yc * yc, axis=0, keepdims=True)
            y = yc * lax.rsqrt(var + RW_GN_EPS) * lnw_ref[...] + lnb_ref[...]
            k_sum = k_ref[t] * (2.0 + (af_ref[t] + a_ref[t] - 2.0) * kap_ref[...])
            y = y + jnp.sum(r_ref[t] * k_sum * rkp_ref[...], axis=0, keepdims=True) * v_ref[t]
        y_ref[t] = y

    restart()
    prepare(0, 0)
    prepare(1, 1)

    def steps(p, carry):
        base = STEPS_PER_ITER * p
        for j in range(STEPS_PER_ITER):
            sweep(base + j, j % 2)
            prepare(base + j + 2, j % 2)
        return carry

    lax.fori_loop(0, tb // STEPS_PER_ITER, steps, 0)


def _rw_scan(r_t, k_t, v_t, w_t, a_t, kkp, kap, n_ctx_blocks, *, run_after=None, readout=None):
    backward = readout is not None
    direction = int(backward)
    tt = r_t.shape[0]
    nblk = tt // SCAN_TILE
    seq = _seq_block(n_ctx_blocks, nblk)
    blk = lambda i: seq(direction, i)
    tile = (SCAN_TILE, RW_N, LANES)
    shared = pl.BlockSpec(tile, lambda i: (blk(i), 0, 0))
    const = pl.BlockSpec((RW_N, LANES), lambda i: (0, 0))
    y_blk = lambda i: jnp.where(i < n_ctx_blocks, blk(n_ctx_blocks), blk(i)) - n_ctx_blocks
    y_spec = pl.BlockSpec(tile, lambda i: (y_blk(i), 0, 0))
    in_specs = [shared, shared, shared, shared, shared, const, const]
    args = [r_t, k_t, v_t, w_t, a_t, kkp, kap]
    if backward:
        y_fwd, a_fwd, rkp, lnw, lnb = readout
        in_specs += [y_spec, shared, const, const, const]
        args += [y_fwd, a_fwd, rkp, lnw, lnb]
    else:
        in_specs += [pl.BlockSpec(memory_space=pl.ANY)]
        args += [run_after]
    return pl.pallas_call(
        functools.partial(_rw_scan_kernel, backward=backward),
        grid=(nblk,),
        in_specs=in_specs,
        out_specs=y_spec,
        out_shape=jax.ShapeDtypeStruct((tt - n_ctx_blocks * SCAN_TILE, RW_N, LANES), F32),
        scratch_shapes=[pltpu.VMEM((RW_N, RW_N, LANES), F32)] + [pltpu.VMEM((RW_N, LANES), F32)] * 3
                       + [pltpu.VMEM((2, RW_N, LANES), F32)] * 4,
        compiler_params=_params("arbitrary"),
        name="rw_scan_bwd" if backward else "rw_scan_fwd",
    )(*args)


def _hg_kernel(q_ref, f_ref, i_ref, lbp_ref, tri_ref, o_ref, st_ref, qd_scr, kv_scr, dec_scr):
    d = pl.program_id(1)
    i = pl.program_id(2)
    n_chunks = q_ref.shape[1] // HG_CHUNK

    @pl.when(i == 0)
    def _():
        st_ref[...] = jnp.zeros_like(st_ref)

    lbp = lbp_ref[...]
    e = jnp.exp(lbp - jnp.max(lbp, axis=0, keepdims=True))
    lb = e[0:1] / jnp.sum(e, axis=0, keepdims=True)
    t_idx = lax.broadcasted_iota(jnp.int32, (HG_CHUNK, HG_CHUNK), 0)
    s_idx = lax.broadcasted_iota(jnp.int32, (HG_CHUNK, HG_CHUNK), 1)
    nt = (((1,), (1,)), ((), ()))
    tn = (((0,), (0,)), ((), ()))

    def run(fwd):
        mask = (s_idx <= t_idx) if fwd else (s_idx >= t_idx)
        mid_row = HG_CHUNK // 2 - 1 if fwd else HG_CHUNK // 2
        last_row = HG_CHUNK - 1 if fwd else 0
        order = list(range(n_chunks)) if fwd else list(reversed(range(n_chunks)))

        f = lb + (1.0 - lb) * _sigmoid(f_ref[0])
        lf = jnp.log(f)
        hi = lf.astype(BF16)
        lo = (lf - hi.astype(F32)).astype(BF16)
        tri = tri_ref[0]
        b = _bdot(tri, hi) + _bdot(tri, lo)
        for c in order:
            rows = slice(c * HG_CHUNK, (c + 1) * HG_CHUNK)
            bc = b[rows]
            b_mid = bc[mid_row:mid_row + 1]
            b_last = bc[last_row:last_row + 1]
            q_in = q_ref[0, rows, :] * jnp.exp(bc - b_mid)
            k_in = (1.0 - f[rows]) * jnp.exp(b_mid - bc)
            qd_scr[rows, :] = (q_in * jnp.exp(b_mid)).astype(BF16)
            k_dec = (k_in * jnp.exp(b_last - b_mid)).astype(BF16)
            dec_scr[c:c + 1, :] = jnp.exp(b_last)
            q_in = q_in.astype(BF16)
            k_in = k_in.astype(BF16)
            vb = i_ref[0, rows, :].astype(BF16)
            heads = [slice(h * HG_N, (h + 1) * HG_N) for h in range(HG_HEADS)]
            scores = [lax.dot_general(q_in[:, cols], k_in[:, cols], nt, preferred_element_type=F32)
                      for cols in heads]
            for h, cols in enumerate(heads):
                kv_scr[c, h] = lax.dot_general(vb[:, cols], k_dec[:, cols], tn, preferred_element_type=F32)
            for h, cols in enumerate(heads):
                o_ref[0, 0, rows, cols] = _bdot(jnp.where(mask, scores[h], 0.0).astype(BF16), vb[:, cols])
        for c in order:
            rows = slice(c * HG_CHUNK, (c + 1) * HG_CHUNK)
            for h in range(HG_HEADS):
                cols = slice(h * HG_N, (h + 1) * HG_N)
                st = st_ref[h]
                o_ref[0, 0, rows, cols] += lax.dot_general(qd_scr[rows, cols], st.astype(BF16), nt,
                                                           preferred_element_type=F32)
                st_ref[h] = dec_scr[c:c + 1, cols] * st + kv_scr[c, h]

    @pl.when(d == 0)
    def _():
        run(True)

    @pl.when(d == 1)
    def _():
        run(False)


def _hg_tri(n_rows):
    t = jnp.arange(n_rows)[:, None]
    s = jnp.arange(n_rows)[None, :]
    same = (t // HG_CHUNK) == (s // HG_CHUNK)
    return jnp.stack([same & (s <= t), same & (s >= t)]).astype(BF16)


def _hg(zqi, zh, hg_lb, n_ctx):
    b, tt, _ = zh.shape
    w = HG_HEADS * HG_N
    nblk = tt // ROW_TILE
    n_chunks = ROW_TILE // HG_CHUNK
    blk = _seq_block(n_ctx // ROW_TILE, nblk)
    return pl.pallas_call(
        _hg_kernel,
        grid=(b, 2, nblk),
        in_specs=[pl.BlockSpec((1, ROW_TILE, w), lambda bi, d, i: (bi, blk(d, i), 0)),
                  pl.BlockSpec((1, ROW_TILE, w), lambda bi, d, i: (bi, blk(d, i), d)),
                  pl.BlockSpec((1, ROW_TILE, w), lambda bi, d, i: (bi, blk(d, i), 1)),
                  pl.BlockSpec(hg_lb.shape, lambda bi, d, i: (0, 0)),
                  pl.BlockSpec((1, ROW_TILE, ROW_TILE), lambda bi, d, i: (d, 0, 0))],
        out_specs=pl.BlockSpec((1, 1, ROW_TILE, w), lambda bi, d, i: (d, bi, blk(d, i), 0)),
        out_shape=jax.ShapeDtypeStruct((2, b, tt, w), F32),
        scratch_shapes=[pltpu.VMEM((HG_HEADS, HG_N, HG_N), F32),
                        pltpu.VMEM((ROW_TILE, w), BF16),
                        pltpu.VMEM((n_chunks, HG_HEADS, HG_N, HG_N), F32),
                        pltpu.VMEM((SUBLANES, w), F32)],
        compiler_params=_params("arbitrary", "arbitrary", "arbitrary"),
        name="hg",
    )(zqi, zh, zqi, hg_lb, _hg_tri(ROW_TILE))


def _merge_kernel(yrw_ref, g_ref, o_ref, gz_ref, zg_ref, x_ref, g1_ref, sh2_ref, sc2_ref, g2_ref, hnw_ref,
                  pa_ref, pb_ref, wo_ref, nw2_ref, w1_ref, w2_ref, fw_ref, out_ref):
    d = x_ref.shape[2]
    y_rw = (yrw_ref[...].T * g_ref[...]).astype(BF16)
    o = o_ref[0, 0] + o_ref[1, 0]
    gz = gz_ref[0]
    hnw = hnw_ref[...]
    parts = []
    for h in range(HG_HEADS):
        cols = slice(h * HG_N, (h + 1) * HG_N)
        oh = o[:, cols]
        ms = jnp.mean(oh * oh, axis=-1, keepdims=True)
        parts.append(oh * lax.rsqrt(ms + RMS_EPS) * hnw[:, cols])
    y_hg = (jnp.concatenate(parts, axis=1) * (gz * _sigmoid(gz))).astype(BF16)
    zg = zg_ref[0]
    m = _sigmoid(zg[:, :d]) * _bdot(y_rw, pa_ref[...]) + _sigmoid(zg[:, d:]) * _bdot(y_hg, pb_ref[...])
    x1 = x_ref[0] + g1_ref[0] * _bdot(m.astype(BF16), wo_ref[...])
    hb = _modulated_norm(x1, nw2_ref[...], sh2_ref[0], sc2_ref[0]).astype(BF16)
    u = jnp.maximum(_bdot(hb, w1_ref[...]), 0.0)
    y = x1 + g2_ref[0] * _bdot((u * u).astype(BF16), w2_ref[...])
    ms = jnp.mean(y * y, axis=-1, keepdims=True)
    out_ref[0] = y * lax.rsqrt(ms + RMS_EPS) * fw_ref[...]


def _merge(y_rw, g, o_hg, zh, zg, x, modcat, hnw, pa, pb, wo, nw2, w1, w2, fw, n_ctx):
    b, t, d = x.shape
    off = n_ctx // ROW_TILE
    lat = lambda bi, i: (bi, i, 0)
    cat = lambda bi, i: (bi, i + off, 0)
    mod = lambda col: pl.BlockSpec((1, 1, d), lambda bi, i: (2 * bi + 1, 0, col))
    const = lambda a: pl.BlockSpec(a.shape, lambda bi, i: (0,) * a.ndim, pipeline_mode=pl.Buffered(1))
    return pl.pallas_call(
        _merge_kernel,
        grid=(b, t // ROW_TILE),
        in_specs=[pl.BlockSpec((d, ROW_TILE), lambda bi, i: (bi, i)),
                  pl.BlockSpec((ROW_TILE, d), lambda bi, i: (i + off, bi)),
                  pl.BlockSpec((2, 1, ROW_TILE, d), lambda bi, i: (0, bi, i + off, 0)),
                  pl.BlockSpec((1, ROW_TILE, d), lambda bi, i: (bi, i + off, 2)),
                  pl.BlockSpec((1, ROW_TILE, 2 * d), cat),
                  pl.BlockSpec((1, ROW_TILE, d), lat),
                  mod(2), mod(3), mod(4), mod(5),
                  const(hnw), const(pa), const(pb), const(wo), const(nw2), const(w1), const(w2), const(fw)],
        out_specs=pl.BlockSpec((1, ROW_TILE, d), lat),
        out_shape=jax.ShapeDtypeStruct((b, t, d), F32),
        compiler_params=_params("arbitrary", "arbitrary"),
        name="merge_mlp",
    )(y_rw, g, o_hg, zh, zg, x, modcat, modcat, modcat, modcat, hnw, pa, pb, wo, nw2, w1, w2, fw)


def _to_scan(a):
    *lead, w, t = a.shape
    return jnp.swapaxes(a.reshape(*lead, w // RW_N, RW_N, t), -1, -3)


def _from_scan(a):
    t, n, bh = a.shape
    return jnp.swapaxes(a, 0, 2).reshape(bh * n, t)


def _head_tile(p, b):
    return jnp.tile(p.reshape(RW_HEADS, RW_N).T, (1, b))


def kernel(x, c, ctx, c_ctx, norm1_w, norm2_w, w_mod, b_mod, w_in, rw_mu, rw_w0, rw_w_up, rw_a0, rw_a_up, rw_g_up, rw_k_k, rw_k_a, rw_r_k, rw_ln_w, rw_ln_b, hg_lb, hg_norm_w, p_a, p_b, w_out, w_fc1, w_fc2, final_norm_w):
    b, t, d = x.shape
    n_ctx = ctx.shape[1]
    assert w_mod.shape[0] == 1, "single-layer block"
    assert b * RW_HEADS == LANES and n_ctx == ROW_TILE and t % ROW_TILE == 0
    assert d == RW_HEADS * RW_N == HG_HEADS * HG_N
    n_r = rw_mu.shape[2]
    rank_w, rank_a = rw_w_up.shape[2], rw_a_up.shape[2]
    assert rank_w + rank_a == LANES and n_r == 3 * d + 2 * LANES and w_in.shape[2] == n_r + 7 * d

    c_rows = jnp.zeros((2 * SUBLANES, d), F32).at[:b].set(c).at[b].set(c_ctx)
    mod = _mod(c_rows, w_mod[0], b_mod)
    modcat = jnp.stack([jnp.broadcast_to(mod[b], (b, N_MOD * d)), mod[:b]], axis=1).reshape(2 * b, 1, N_MOD * d)

    zr, zqi, zh, zg = _in_proj(ctx, x, norm1_w, modcat, w_in[0].astype(BF16), n_r)

    wup = jnp.pad(rw_w_up[0], ((0, 0), (0, rank_a), (0, 0))).astype(BF16)
    aup = jnp.pad(rw_a_up[0], ((0, 0), (rank_w, 0), (0, 0))).astype(BF16)
    *scan_ops, g = _rw_prep(zr, rw_mu[0], rw_w0[0], wup, rw_a0[0], aup, rw_g_up[0].astype(BF16), d)
    r_t, k_t, v_t, wf_t, wb_t, af_t, ab_t = (_to_scan(a) for a in scan_ops)
    o_hg = _hg(zqi, zh, hg_lb, n_ctx)
    kap = _head_tile(rw_k_a[0], b)
    tiles = (_head_tile(rw_k_k[0], b), kap, n_ctx // SCAN_TILE)
    y_fwd = _rw_scan(r_t, k_t, v_t, wf_t, af_t, *tiles, run_after=o_hg)
    y_rw_t = _rw_scan(r_t, k_t, v_t, wb_t, ab_t, *tiles,
                      readout=(y_fwd, af_t, _head_tile(rw_r_k[0].reshape(-1), b),
                               _head_tile(rw_ln_w[0], b), _head_tile(rw_ln_b[0], b)))
    y_rw = _from_scan(y_rw_t)

    return _merge(y_rw, g, o_hg, zh, zg, x, modcat, hg_norm_w, p_a[0].astype(BF16), p_b[0].astype(BF16),
                  w_out[0].astype(BF16), norm2_w, w_fc1[0].astype(BF16), w_fc2[0].astype(BF16),
                  final_norm_w.reshape(1, d), n_ctx)
```

```python
import functools

import jax
import jax.numpy as jnp
from jax import lax
from jax.experimental import pallas as pl
from jax.experimental.pallas import tpu as pltpu

F32 = jnp.float32
BF16 = jnp.bfloat16

GRID_W = 64
RW_HEADS = 16
RW_N = 64
HG_HEADS = 8
HG_N = 128
HG_CHUNK = 64
N_MOD = 6
RMS_EPS = 1e-6
RW_GN_EPS = 64e-5
L2_EPS = 1e-12
DECAY_SCALE = 0.6065306597126334

SUBLANES = 8
LANES = 128
ROW_TILE = 256
SCAN_TILE = 64
VMEM_LIMIT = 56 * 1024 * 1024


def _params(*sem):
    return pltpu.CompilerParams(dimension_semantics=sem, vmem_limit_bytes=VMEM_LIMIT)


def _bdot(a, b):
    return jnp.dot(a, b, preferred_element_type=F32)


def _sigmoid(x):
    return 1.0 / (1.0 + jnp.exp(-x))


def _mod_kernel(c_ref, w_ref, b_ref, o_ref):
    c = c_ref[...]
    act = c * _sigmoid(c)
    o_ref[...] = _bdot(act.astype(BF16), w_ref[...].astype(BF16)) + b_ref[...]


def _mod(c_rows, w_mod, b_mod):
    rows, d = c_rows.shape
    n = w_mod.shape[1]
    return pl.pallas_call(
        _mod_kernel,
        grid=(n // d,),
        in_specs=[pl.BlockSpec((rows, d), lambda j: (0, 0)),
                  pl.BlockSpec((d, d), lambda j: (0, j)),
                  pl.BlockSpec((1, d), lambda j: (0, j))],
        out_specs=pl.BlockSpec((rows, d), lambda j: (0, j)),
        out_shape=jax.ShapeDtypeStruct((rows, n), F32),
        compiler_params=_params("arbitrary"),
        name="mod",
    )(c_rows, w_mod, b_mod)


def _modulated_norm(x, nw, sh, sc):
    ms = jnp.mean(x * x, axis=-1, keepdims=True)
    return (x * lax.rsqrt(ms + RMS_EPS) * nw) * (1.0 + sc) + sh


def _in_proj_kernel(ctx_ref, x_ref, nw_ref, sh_ref, sc_ref, w_ref, zr_ref, zqi_ref, zh_ref, zg_ref):
    tokens = jnp.where(pl.program_id(1) == 0, ctx_ref[0], x_ref[0])
    hb = _modulated_norm(tokens, nw_ref[...], sh_ref[0], sc_ref[0]).astype(BF16)
    d = x_ref.shape[2]
    n_r = zr_ref.shape[2]
    col = lambda j0, j1: _bdot(hb, w_ref[:, n_r + j0 * d:n_r + j1 * d])
    zr_ref[0] = _bdot(hb, w_ref[:, :n_r])
    zqi_ref[0, :, :d] = col(0, 1).astype(BF16)
    zqi_ref[0, :, d:] = col(3, 4).astype(BF16)
    zh_ref[0, :, :2 * d] = col(1, 3)
    zh_ref[0, :, 2 * d:] = col(4, 5)
    zg_ref[0] = col(5, 7)


def _in_proj(ctx, x, nw, modcat, w_bf, n_r):
    b, t, d = x.shape
    tt = t + ctx.shape[1]
    nblk = tt // ROW_TILE
    mod_idx = lambda col: (lambda bi, i: (2 * bi + jnp.minimum(i, 1), 0, col))
    outs = ((n_r, F32), (2 * d, BF16), (3 * d, F32), (2 * d, F32))
    return pl.pallas_call(
        _in_proj_kernel,
        grid=(b, nblk),
        in_specs=[pl.BlockSpec((1, ROW_TILE, d), lambda bi, i: (bi, 0, 0)),
                  pl.BlockSpec((1, ROW_TILE, d), lambda bi, i: (bi, jnp.maximum(i - 1, 0), 0)),
                  pl.BlockSpec((1, d), lambda bi, i: (0, 0)),
                  pl.BlockSpec((1, 1, d), mod_idx(0)),
                  pl.BlockSpec((1, 1, d), mod_idx(1)),
                  pl.BlockSpec(w_bf.shape, lambda bi, i: (0, 0), pipeline_mode=pl.Buffered(1))],
        out_specs=[pl.BlockSpec((1, ROW_TILE, n), lambda bi, i: (bi, i, 0)) for n, _ in outs],
        out_shape=[jax.ShapeDtypeStruct((b, tt, n), dt) for n, dt in outs],
        compiler_params=_params("arbitrary", "arbitrary"),
        name="in_proj",
    )(ctx, x, nw, modcat, modcat, w_bf)


def _rw_prep_kernel(z_ref, zn_ref, mu_ref, w0_ref, wup_ref, a0_ref, aup_ref, gup_ref,
                    r_ref, k_ref, v_ref, wf_ref, wb_ref, af_ref, ab_ref, g_ref, zp_scr):
    i = pl.program_id(1)
    nblk = pl.num_programs(1)
    tt = z_ref.shape[1]
    d = r_ref.shape[0]
    is_lat = i > 0
    row = lax.broadcasted_iota(jnp.int32, (tt, 1), 0)
    col = row % GRID_W
    lmask = jnp.where(is_lat, col, row) == 0
    rmask = jnp.where(is_lat, col, row - (tt - GRID_W)) == GRID_W - 1
    latf = is_lat.astype(F32)
    up_ok = (i > 1).astype(F32)
    dn_ok = (i < nblk - 1).astype(F32)

    @pl.when(i == 0)
    def _():
        zp_scr[...] = jnp.zeros_like(zp_scr)

    def shifted(c0, c1):
        z = z_ref[0, :, c0:c1]
        mu = mu_ref[:, c0:c1]
        left = jnp.where(lmask, 0.0, pltpu.roll(z, 1, 0))
        right = jnp.where(rmask, 0.0, pltpu.roll(z, tt - 1, 0))
        up = jnp.concatenate([zp_scr[:, c0:c1] * up_ok, z[:tt - GRID_W]], axis=0)
        down = jnp.concatenate([z[GRID_W:], zn_ref[0, :, c0:c1] * dn_ok], axis=0)
        m_up, m_dn = latf * mu[2:3], latf * mu[3:4]
        m_self = 1.0 - mu[0:1] - mu[1:2] - m_up - m_dn
        return z * m_self + left * mu[0:1] + right * mu[1:2] + up * m_up + down * m_dn

    r_ref[...] = shifted(0, d).T
    k_ref[...] = shifted(d, 2 * d).T
    v_ref[...] = shifted(2 * d, 3 * d).T
    rest = shifted(3 * d, z_ref.shape[2])
    xwa = rest[:, :LANES]
    xw_t = jnp.tanh(xwa).astype(BF16)
    xa_b = xwa.astype(BF16)
    for dr, (wd_ref, ad_ref) in enumerate(((wf_ref, af_ref), (wb_ref, ab_ref))):
        wz = w0_ref[dr:dr + 1, :] + _bdot(xw_t, wup_ref[dr])
        wd_ref[...] = jnp.exp(-DECAY_SCALE * _sigmoid(wz)).T
        ad_ref[...] = _sigmoid(a0_ref[dr:dr + 1, :] + _bdot(xa_b, aup_ref[dr])).T
    g_ref[...] = _bdot(_sigmoid(rest[:, LANES:]).astype(BF16), gup_ref[...])
    zp_scr[...] = z_ref[0, tt - GRID_W:, :]


def _rw_prep(zr, mu, w0, wup, a0, aup, gup, d):
    b, tt, nr = zr.shape
    nblk = tt // ROW_TILE
    per = ROW_TILE // GRID_W
    last = tt // GRID_W - 1
    const = lambda shape: pl.BlockSpec(shape, lambda bi, i: (0,) * len(shape))
    row_spec = pl.BlockSpec((d, ROW_TILE), lambda bi, i: (bi, i))
    gate_spec = pl.BlockSpec((ROW_TILE, d), lambda bi, i: (i, bi))
    row_shape = jax.ShapeDtypeStruct((b * d, tt), F32)
    gate_shape = jax.ShapeDtypeStruct((tt, b * d), F32)
    return pl.pallas_call(
        _rw_prep_kernel,
        grid=(b, nblk),
        in_specs=[pl.BlockSpec((1, ROW_TILE, nr), lambda bi, i: (bi, i, 0)),
                  pl.BlockSpec((1, GRID_W, nr), lambda bi, i: (bi, jnp.minimum(i * per + per, last), 0)),
                  const(mu.shape), const(w0.shape), const(wup.shape), const(a0.shape),
                  const(aup.shape), const(gup.shape)],
        out_specs=[row_spec] * 7 + [gate_spec],
        out_shape=[row_shape] * 7 + [gate_shape],
        scratch_shapes=[pltpu.VMEM((GRID_W, nr), F32)],
        compiler_params=_params("arbitrary", "arbitrary"),
        name="rw_prep",
    )(zr, zr, mu, w0, wup, a0, aup, gup)


K_UNROLL = 32
STEPS_PER_ITER = (4, 2)


def _seq_block(n_ctx_blocks, n_blocks):
    def blk(d, i):
        bwd = jnp.where(i < n_ctx_blocks, n_ctx_blocks - 1 - i, n_blocks - 1 + n_ctx_blocks - i)
        return jnp.where(d == 0, i, bwd)
    return blk


def _rw_scan_kernel(*refs, backward):
    if backward:
        (r_ref, k_ref, v_ref, w_ref, a_ref, kkp_ref, kap_ref, yf_ref, af_ref, rkp_ref, lnw_ref, lnb_ref,
         y_ref, s_ref, p_s, sa_s, kk0_s, bt_s, kt_s, rd_s, kkd_s) = refs
    else:
        (r_ref, k_ref, v_ref, w_ref, a_ref, kkp_ref, kap_ref, _order_ref,
         y_ref, s_ref, p_s, sa_s, kk0_s, bt_s, kt_s, rd_s, kkd_s) = refs
    i = pl.program_id(0)
    tb = r_ref.shape[0]
    nv = RW_N // SUBLANES

    @pl.when(i == 0)
    def _():
        s_ref[...] = jnp.zeros_like(s_ref)
        p_s[...] = jnp.ones_like(p_s)

    def bcast(ref, *idx):
        k = idx[-1]
        row = ref[(*idx[:-1], pl.ds(k, 1), slice(None))]
        return jnp.broadcast_to(row, (SUBLANES, LANES))

    def time_index(s):
        s = jnp.minimum(s, tb - 1)
        return tb - 1 - s if backward else s

    def bf16_pair(x):
        hi = lax.bitcast_convert_type(x.astype(BF16).astype(F32), jnp.uint32)
        return lax.bitcast_convert_type(hi | (hi >> 16), F32)

    def norm_key(t):
        kkr = k_ref[t] * kkp_ref[...]
        nrm = jnp.sqrt(jnp.sum(kkr * kkr, axis=0, keepdims=True))
        return kkr / jnp.maximum(nrm, L2_EPS)

    def scaled_key(t, a):
        return k_ref[t] * (1.0 + (a - 1.0) * kap_ref[...])

    def prepare(s, slot):
        t = time_index(s)
        a = a_ref[t]
        kk = norm_key(t)
        p_prev = p_s[...]
        kkd_s[1 - slot] = bf16_pair(p_prev * kk)
        p = p_prev * jnp.where(s < tb, w_ref[t], 1.0)
        p_s[...] = p
        inv_p = 1.0 / p
        bt_s[slot] = bf16_pair(kk * a * inv_p)
        kt_s[slot] = bf16_pair(scaled_key(t, a) * inv_p)
        rd_s[slot] = bf16_pair(p * r_ref[t])

    def restart():
        kk0_s[...] = norm_key(time_index(0))
        acc = [None] * nv
        for k in range(RW_N):
            pb = bcast(p_s, k)
            kkb = bcast(kk0_s, k)
            for j in range(nv):
                rows = slice(SUBLANES * j, SUBLANES * (j + 1))
                sn = s_ref[k, rows, :] * pb
                s_ref[k, rows, :] = sn
                acc[j] = sn * kkb if acc[j] is None else acc[j] + sn * kkb
        sa_s[...] = -jnp.concatenate(acc, axis=0)
        p_s[...] = jnp.ones_like(p_s)

    def sweep(s, slot):
        t = time_index(s)
        zero = jnp.zeros((SUBLANES, LANES), F32)

        sa_p = [sa_s[2 * SUBLANES * m:2 * SUBLANES * (m + 1), :].astype(BF16) for m in range(nv // 2)]
        v_p = [v_ref[t, 2 * SUBLANES * m:2 * SUBLANES * (m + 1), :].astype(BF16) for m in range(nv // 2)]

        def key_block(kblk, carry):
            yacc, acc = list(carry[0]), list(carry[1])
            yb = [jnp.zeros((2 * SUBLANES, LANES), BF16)] * (nv // 2)
            ab = [jnp.zeros((2 * SUBLANES, LANES), BF16)] * (nv // 2)
            for kk in range(K_UNROLL):
                k = kblk * K_UNROLL + kk
                bb = pltpu.bitcast(bcast(bt_s, slot, k), BF16)
                kb = pltpu.bitcast(bcast(kt_s, slot, k), BF16)
                rb = pltpu.bitcast(bcast(rd_s, slot, k), BF16)
                kkn = pltpu.bitcast(bcast(kkd_s, slot, k), BF16)
                for m in range(nv // 2):
                    rows = slice(2 * SUBLANES * m, 2 * SUBLANES * (m + 1))
                    sn = s_ref[k, rows, :] + (sa_p[m] * bb + v_p[m] * kb).astype(F32)
                    s_ref[k, rows, :] = sn
                    snp = sn.astype(BF16)
                    yb[m] = yb[m] + snp * rb
                    ab[m] = ab[m] + snp * kkn
            for m in range(nv // 2):
                y32 = yb[m].astype(F32)
                a32 = ab[m].astype(F32)
                for h, j in enumerate((2 * m, 2 * m + 1)):
                    yacc[j] = yacc[j] + y32[SUBLANES * h:SUBLANES * (h + 1)]
                    acc[j] = acc[j] + a32[SUBLANES * h:SUBLANES * (h + 1)]
            return yacc, acc

        init = [zero] * nv
        n_kblk = RW_N // K_UNROLL
        carry = lax.fori_loop(0, n_kblk - 1, key_block, (init, init))
        yacc, acc = key_block(n_kblk - 1, carry)
        sa_s[...] = -jnp.concatenate(acc, axis=0)
        y = jnp.concatenate(yacc, axis=0)
        if backward:
            y = y + yf_ref[t]
            yc = y - jnp.mean(y, axis=0, keepdims=True)
            var = jnp.mean(yc * yc, axis=0, keepdims=True)
            y = yc * lax.rsqrt(var + RW_GN_EPS) * lnw_ref[...] + lnb_ref[...]
            k_sum = k_ref[t] * (2.0 + (af_ref[t] + a_ref[t] - 2.0) * kap_ref[...])
            y = y + jnp.sum(r_ref[t] * k_sum * rkp_ref[...], axis=0, keepdims=True) * v_ref[t]
        y_ref[t] = y

    restart()
    prepare(0, 0)
    prepare(1, 1)

    n_steps = STEPS_PER_ITER[backward]

    def steps(p, carry):
        base = n_steps * p
        for j in range(n_steps):
            sweep(base + j, j % 2)
            prepare(base + j + 2, j % 2)
        return carry

    lax.fori_loop(0, tb // n_steps, steps, 0)


def _rw_scan(r_t, k_t, v_t, w_t, a_t, kkp, kap, n_ctx_blocks, *, run_after=None, readout=None):
    backward = readout is not None
    direction = int(backward)
    tt = r_t.shape[0]
    nblk = tt // SCAN_TILE
    seq = _seq_block(n_ctx_blocks, nblk)
    blk = lambda i: seq(direction, i)
    tile = (SCAN_TILE, RW_N, LANES)
    shared = pl.BlockSpec(tile, lambda i: (blk(i), 0, 0))
    const = pl.BlockSpec((RW_N, LANES), lambda i: (0, 0))
    y_blk = lambda i: jnp.where(i < n_ctx_blocks, blk(n_ctx_blocks), blk(i)) - n_ctx_blocks
    y_spec = pl.BlockSpec(tile, lambda i: (y_blk(i), 0, 0))
    in_specs = [shared, shared, shared, shared, shared, const, const]
    args = [r_t, k_t, v_t, w_t, a_t, kkp, kap]
    if backward:
        y_fwd, a_fwd, rkp, lnw, lnb = readout
        in_specs += [y_spec, shared, const, const, const]
        args += [y_fwd, a_fwd, rkp, lnw, lnb]
    else:
        in_specs += [pl.BlockSpec(memory_space=pl.ANY)]
        args += [run_after]
    return pl.pallas_call(
        functools.partial(_rw_scan_kernel, backward=backward),
        grid=(nblk,),
        in_specs=in_specs,
        out_specs=y_spec,
        out_shape=jax.ShapeDtypeStruct((tt - n_ctx_blocks * SCAN_TILE, RW_N, LANES), F32),
        scratch_shapes=[pltpu.VMEM((RW_N, RW_N, LANES), F32)] + [pltpu.VMEM((RW_N, LANES), F32)] * 3
                       + [pltpu.VMEM((2, RW_N, LANES), F32)] * 4,
        compiler_params=_params("arbitrary"),
        name="rw_scan_bwd" if backward else "rw_scan_fwd",
    )(*args)


def _hg_kernel(q_ref, f_ref, i_ref, lbp_ref, tri_ref, o_ref, st_ref, qd_scr, kv_scr, dec_scr):
    d = pl.program_id(1)
    i = pl.program_id(2)
    n_chunks = q_ref.shape[1] // HG_CHUNK

    @pl.when(i == 0)
    def _():
        st_ref[...] = jnp.zeros_like(st_ref)

    lbp = lbp_ref[...]
    e = jnp.exp(lbp - jnp.max(lbp, axis=0, keepdims=True))
    lb = e[0:1] / jnp.sum(e, axis=0, keepdims=True)
    t_idx = lax.broadcasted_iota(jnp.int32, (HG_CHUNK, HG_CHUNK), 0)
    s_idx = lax.broadcasted_iota(jnp.int32, (HG_CHUNK, HG_CHUNK), 1)
    nt = (((1,), (1,)), ((), ()))
    tn = (((0,), (0,)), ((), ()))

    def run(fwd):
        mask = (s_idx <= t_idx) if fwd else (s_idx >= t_idx)
        mid_row = HG_CHUNK // 2 - 1 if fwd else HG_CHUNK // 2
        last_row = HG_CHUNK - 1 if fwd else 0
        order = list(range(n_chunks)) if fwd else list(reversed(range(n_chunks)))

        f = lb + (1.0 - lb) * _sigmoid(f_ref[0])
        lf = jnp.log(f)
        hi = lf.astype(BF16)
        lo = (lf - hi.astype(F32)).astype(BF16)
        tri = tri_ref[0]
        b = _bdot(tri, hi) + _bdot(tri, lo)
        for c in order:
            rows = slice(c * HG_CHUNK, (c + 1) * HG_CHUNK)
            bc = b[rows]
            b_mid = bc[mid_row:mid_row + 1]
            b_last = bc[last_row:last_row + 1]
            q_in = q_ref[0, rows, :] * jnp.exp(bc - b_mid)
            k_in = (1.0 - f[rows]) * jnp.exp(b_mid - bc)
            qd_scr[rows, :] = (q_in * jnp.exp(b_mid)).astype(BF16)
            k_dec = (k_in * jnp.exp(b_last - b_mid)).astype(BF16)
            dec_scr[c:c + 1, :] = jnp.exp(b_last)
            q_in = q_in.astype(BF16)
            k_in = k_in.astype(BF16)
            vb = i_ref[0, rows, :].astype(BF16)
            heads = [slice(h * HG_N, (h + 1) * HG_N) for h in range(HG_HEADS)]
            scores = [lax.dot_general(q_in[:, cols], k_in[:, cols], nt, preferred_element_type=F32)
                      for cols in heads]
            for h, cols in enumerate(heads):
                kv_scr[c, h] = lax.dot_general(vb[:, cols], k_dec[:, cols], tn, preferred_element_type=F32)
            for h, cols in enumerate(heads):
                o_ref[0, 0, rows, cols] = _bdot(jnp.where(mask, scores[h], 0.0).astype(BF16), vb[:, cols])
        for c in order:
            rows = slice(c * HG_CHUNK, (c + 1) * HG_CHUNK)
            for h in range(HG_HEADS):
                cols = slice(h * HG_N, (h + 1) * HG_N)
                st = st_ref[h]
                o_ref[0, 0, rows, cols] += lax.dot_general(qd_scr[rows, cols], st.astype(BF16), nt,
                                                           preferred_element_type=F32)
                st_ref[h] = dec_scr[c:c + 1, cols] * st + kv_scr[c, h]

    @pl.when(d == 0)
    def _():
        run(True)

    @pl.when(d == 1)
    def _():
        run(False)


def _hg_tri(n_rows):
    t = jnp.arange(n_rows)[:, None]
    s = jnp.arange(n_rows)[None, :]
    same = (t // HG_CHUNK) == (s // HG_CHUNK)
    return jnp.stack([same & (s <= t), same & (s >= t)]).astype(BF16)


def _hg(zqi, zh, hg_lb, n_ctx):
    b, tt, _ = zh.shape
    w = HG_HEADS * HG_N
    nblk = tt // ROW_TILE
    n_chunks = ROW_TILE // HG_CHUNK
    blk = _seq_block(n_ctx // ROW_TILE, nblk)
    return pl.pallas_call(
        _hg_kernel,
        grid=(b, 2, nblk),
        in_specs=[pl.BlockSpec((1, ROW_TILE, w), lambda bi, d, i: (bi, blk(d, i), 0)),
                  pl.BlockSpec((1, ROW_TILE, w), lambda bi, d, i: (bi, blk(d, i), d)),
                  pl.BlockSpec((1, ROW_TILE, w), lambda bi, d, i: (bi, blk(d, i), 1)),
                  pl.BlockSpec(hg_lb.shape, lambda bi, d, i: (0, 0)),
                  pl.BlockSpec((1, ROW_TILE, ROW_TILE), lambda bi, d, i: (d, 0, 0))],
        out_specs=pl.BlockSpec((1, 1, ROW_TILE, w), lambda bi, d, i: (d, bi, blk(d, i), 0)),
        out_shape=jax.ShapeDtypeStruct((2, b, tt, w), F32),
        scratch_shapes=[pltpu.VMEM((HG_HEADS, HG_N, HG_N), F32),
                        pltpu.VMEM((ROW_TILE, w), BF16),
                        pltpu.VMEM((n_chunks, HG_HEADS, HG_N, HG_N), F32),
                        pltpu.VMEM((SUBLANES, w), F32)],
        compiler_params=_params("arbitrary", "arbitrary", "arbitrary"),
        name="hg",
    )(zqi, zh, zqi, hg_lb, _hg_tri(ROW_TILE))


def _merge_kernel(yrw_ref, g_ref, o_ref, gz_ref, zg_ref, x_ref, g1_ref, sh2_ref, sc2_ref, g2_ref, hnw_ref,
                  pa_ref, pb_ref, wo_ref, nw2_ref, w1_ref, w2_ref, fw_ref, out_ref):
    d = x_ref.shape[2]
    y_rw = (yrw_ref[...].T * g_ref[...]).astype(BF16)
    o = o_ref[0, 0] + o_ref[1, 0]
    gz = gz_ref[0]
    hnw = hnw_ref[...]
    parts = []
    for h in range(HG_HEADS):
        cols = slice(h * HG_N, (h + 1) * HG_N)
        oh = o[:, cols]
        ms = jnp.mean(oh * oh, axis=-1, keepdims=True)
        parts.append(oh * lax.rsqrt(ms + RMS_EPS) * hnw[:, cols])
    y_hg = (jnp.concatenate(parts, axis=1) * (gz * _sigmoid(gz))).astype(BF16)
    zg = zg_ref[0]
    m = _sigmoid(zg[:, :d]) * _bdot(y_rw, pa_ref[...]) + _sigmoid(zg[:, d:]) * _bdot(y_hg, pb_ref[...])
    x1 = x_ref[0] + g1_ref[0] * _bdot(m.astype(BF16), wo_ref[...])
    hb = _modulated_norm(x1, nw2_ref[...], sh2_ref[0], sc2_ref[0]).astype(BF16)
    u = jnp.maximum(_bdot(hb, w1_ref[...]), 0.0)
    y = x1 + g2_ref[0] * _bdot((u * u).astype(BF16), w2_ref[...])
    ms = jnp.mean(y * y, axis=-1, keepdims=True)
    out_ref[0] = y * lax.rsqrt(ms + RMS_EPS) * fw_ref[...]


def _merge(y_rw, g, o_hg, zh, zg, x, modcat, hnw, pa, pb, wo, nw2, w1, w2, fw, n_ctx):
    b, t, d = x.shape
    off = n_ctx // ROW_TILE
    lat = lambda bi, i: (bi, i, 0)
    cat = lambda bi, i: (bi, i + off, 0)
    mod = lambda col: pl.BlockSpec((1, 1, d), lambda bi, i: (2 * bi + 1, 0, col))
    const = lambda a: pl.BlockSpec(a.shape, lambda bi, i: (0,) * a.ndim, pipeline_mode=pl.Buffered(1))
    return pl.pallas_call(
        _merge_kernel,
        grid=(b, t // ROW_TILE),
        in_specs=[pl.BlockSpec((d, ROW_TILE), lambda bi, i: (bi, i)),
                  pl.BlockSpec((ROW_TILE, d), lambda bi, i: (i + off, bi)),
                  pl.BlockSpec((2, 1, ROW_TILE, d), lambda bi, i: (0, bi, i + off, 0)),
                  pl.BlockSpec((1, ROW_TILE, d), lambda bi, i: (bi, i + off, 2)),
                  pl.BlockSpec((1, ROW_TILE, 2 * d), cat),
                  pl.BlockSpec((1, ROW_TILE, d), lat),
                  mod(2), mod(3), mod(4), mod(5),
                  const(hnw), const(pa), const(pb), const(wo), const(nw2), const(w1), const(w2), const(fw)],
        out_specs=pl.BlockSpec((1, ROW_TILE, d), lat),
        out_shape=jax.ShapeDtypeStruct((b, t, d), F32),
        compiler_params=_params("arbitrary", "arbitrary"),
        name="merge_mlp",
    )(y_rw, g, o_hg, zh, zg, x, modcat, modcat, modcat, modcat, hnw, pa, pb, wo, nw2, w1, w2, fw)


def _to_scan(a):
    *lead, w, t = a.shape
    return jnp.swapaxes(a.reshape(*lead, w // RW_N, RW_N, t), -1, -3)


def _from_scan(a):
    t, n, bh = a.shape
    return jnp.swapaxes(a, 0, 2).reshape(bh * n, t)


def _head_tile(p, b):
    return jnp.tile(p.reshape(RW_HEADS, RW_N).T, (1, b))


def kernel(x, c, ctx, c_ctx, norm1_w, norm2_w, w_mod, b_mod, w_in, rw_mu, rw_w0, rw_w_up, rw_a0, rw_a_up, rw_g_up, rw_k_k, rw_k_a, rw_r_k, rw_ln_w, rw_ln_b, hg_lb, hg_norm_w, p_a, p_b, w_out, w_fc1, w_fc2, final_norm_w):
    b, t, d = x.shape
    n_ctx = ctx.shape[1]
    assert w_mod.shape[0] == 1, "single-layer block"
    assert b * RW_HEADS == LANES and n_ctx == ROW_TILE and t % ROW_TILE == 0
    assert d == RW_HEADS * RW_N == HG_HEADS * HG_N
    n_r = rw_mu.shape[2]
    rank_w, rank_a = rw_w_up.shape[2], rw_a_up.shape[2]
    assert rank_w + rank_a == LANES and n_r == 3 * d + 2 * LANES and w_in.shape[2] == n_r + 7 * d

    c_rows = jnp.zeros((2 * SUBLANES, d), F32).at[:b].set(c).at[b].set(c_ctx)
    mod = _mod(c_rows, w_mod[0], b_mod)
    modcat = jnp.stack([jnp.broadcast_to(mod[b], (b, N_MOD * d)), mod[:b]], axis=1).reshape(2 * b, 1, N_MOD * d)

    zr, zqi, zh, zg = _in_proj(ctx, x, norm1_w, modcat, w_in[0].astype(BF16), n_r)

    wup = jnp.pad(rw_w_up[0], ((0, 0), (0, rank_a), (0, 0))).astype(BF16)
    aup = jnp.pad(rw_a_up[0], ((0, 0), (rank_w, 0), (0, 0))).astype(BF16)
    *scan_ops, g = _rw_prep(zr, rw_mu[0], rw_w0[0], wup, rw_a0[0], aup, rw_g_up[0].astype(BF16), d)
    r_t, k_t, v_t, wf_t, wb_t, af_t, ab_t = (_to_scan(a) for a in scan_ops)
    o_hg = _hg(zqi, zh, hg_lb, n_ctx)
    kap = _head_tile(rw_k_a[0], b)
    tiles = (_head_tile(rw_k_k[0], b), kap, n_ctx // SCAN_TILE)
    y_fwd = _rw_scan(r_t, k_t, v_t, wf_t, af_t, *tiles, run_after=o_hg)
    y_rw_t = _rw_scan(r_t, k_t, v_t, wb_t, ab_t, *tiles,
                      readout=(y_fwd, af_t, _head_tile(rw_r_k[0].reshape(-1), b),
                               _head_tile(rw_ln_w[0], b), _head_tile(rw_ln_b[0], b)))
    y_rw = _from_scan(y_rw_t)

    return _merge(y_rw, g, o_hg, zh, zg, x, modcat, hg_norm_w, p_a[0].astype(BF16), p_b[0].astype(BF16),
                  w_out[0].astype(BF16), norm2_w, w_fc1[0].astype(BF16), w_fc2[0].astype(BF16),
                  final_norm_w.reshape(1, d), n_ctx)
```
